```python
import math
import jax, jax.numpy as jnp
from jax import lax
import numpy as np

D_MODEL = 2048
BATCH = 8
SEQ = 8192
DEPTH = 1

N_MEM = 256
MEM_HEADS = 4
MEM_HD = 256
BRANCH_WIDTH = 1024
MEM_WIDTH = MEM_HEADS * MEM_HD
HG_HEADS = 8
HG_DK = 128
HG_DV = 128
HG_WIDTH = HG_HEADS * HG_DK
HG_CHUNK = 64
MLA_HEADS = 8
MLA_Q_RANK = 512
MLA_KV_RANK = 512
MLA_NOPE = 128
MLA_ROPE = 64
MLA_V = 128
MLA_QK = MLA_NOPE + MLA_ROPE
MLA_WIDTH = MLA_HEADS * MLA_V
ROPE_THETA = 10000.0
Q_BLOCK = 128
N_BRANCH = 3
D_FF = -(-8 * D_MODEL // (3 * 256)) * 256
ALPHA = (2.0 * DEPTH) ** 0.25
BETA = (8.0 * DEPTH) ** -0.25
LN_EPS = 1e-5
RMS_EPS = 1e-6
IN_SPLITS = (HG_WIDTH, HG_WIDTH, HG_WIDTH, HG_WIDTH, HG_WIDTH,
             MLA_Q_RANK, MLA_KV_RANK, MLA_ROPE, MEM_WIDTH, N_BRANCH * D_MODEL)
IN_WIDTH = sum(IN_SPLITS)

kernel_name = "hybrid_hgrn2_mla_memory_deepnorm_encoder"


def _split(a, sizes):
    idx, acc = [], 0
    for s in sizes[:-1]:
        acc += s
        idx.append(acc)
    return jnp.split(a, idx, axis=-1)


def _layernorm(x, g, b):
    xf = x.astype(jnp.float32)
    mu = jnp.mean(xf, -1, keepdims=True)
    xc = xf - mu
    var = jnp.mean(xc * xc, -1, keepdims=True)
    return (xc * lax.rsqrt(var + LN_EPS) * g.astype(jnp.float32) + b.astype(jnp.float32)).astype(x.dtype)


def _rmsnorm(x, g):
    xf = x.astype(jnp.float32)
    ms = jnp.mean(xf * xf, -1, keepdims=True)
    return (xf * lax.rsqrt(ms + RMS_EPS) * g.astype(jnp.float32)).astype(x.dtype)


def _rope(x, cos, sin):
    xf = x.astype(jnp.float32)
    x1, x2 = jnp.split(xf, 2, axis=-1)
    return jnp.concatenate([x1 * cos - x2 * sin, x2 * cos + x1 * sin], -1).astype(x.dtype)


def _chunk_gla(q, k, v, log_f):
    Z, B, H, T, dk = q.shape
    dv = v.shape[-1]
    n = T // HG_CHUNK

    def to_chunks(a):
        a = a.reshape(Z, B, H, n, HG_CHUNK, a.shape[-1])
        return jnp.moveaxis(a, 3, 0)

    tri = jnp.tril(jnp.ones((HG_CHUNK, HG_CHUNK), dtype=bool))[:, :, None]

    def step(S, inp):
        qc, kc, vc, gc = inp
        b = jnp.cumsum(gc, axis=-2)
        inter = jnp.einsum('zbhtk,zbhkv->zbhtv', qc * jnp.exp(b), S)
        diff = b[..., :, None, :] - b[..., None, :, :]
        decay = jnp.exp(jnp.where(tri, diff, -jnp.inf))
        att = jnp.einsum('zbhtk,zbhsk,zbhtsk->zbhts', qc, kc, decay)
        intra = jnp.einsum('zbhts,zbhsv->zbhtv', att, vc)
        b_last = b[..., -1:, :]
        S = jnp.swapaxes(jnp.exp(b_last), -1, -2) * S + jnp.einsum(
            'zbhsk,zbhsv->zbhkv', kc * jnp.exp(b_last - b), vc)
        return S, inter + intra

    S0 = jnp.zeros((Z, B, H, dk, dv), jnp.float32)
    _, o = lax.scan(step, S0, (to_chunks(q), to_chunks(k), to_chunks(v), to_chunks(log_f)))
    o = jnp.moveaxis(o, 0, 3)
    return o.reshape(Z, B, H, T, dv)


def _hgrn2_branch(q_raw, i_raw, f_fw_raw, f_bw_raw, g_raw, lb, norm_g):
    B, S, _ = q_raw.shape
    f32 = jnp.float32

    def heads(a):
        a = a.reshape(a.shape[:-1] + (HG_HEADS, -1))
        return jnp.swapaxes(a, -2, -3)

    q = heads(jax.nn.silu(q_raw.astype(f32)))
    v = heads(i_raw.astype(f32))
    f_raw = jnp.stack([f_fw_raw, f_bw_raw]).astype(f32)
    lbz = lb[:, None, None, :]
    f = lbz + (1.0 - lbz) * jax.nn.sigmoid(f_raw)
    k = heads(1.0 - f)
    log_f = heads(jnp.log(f))
    flip = lambda a: jnp.flip(a, axis=-2)
    qz = jnp.stack([q, flip(q)])
    vz = jnp.stack([v, flip(v)])
    kz = jnp.stack([k[0], flip(k[1])])
    gz = jnp.stack([log_f[0], flip(log_f[1])])
    o = _chunk_gla(qz, kz, vz, gz)
    o = o[0] + flip(o[1])
    o = _rmsnorm(jnp.swapaxes(o, 1, 2), norm_g)
    o = o.reshape(B, S, HG_WIDTH) * jax.nn.sigmoid(g_raw.astype(f32))
    return o.astype(q_raw.dtype)


def _mla_branch(cq_raw, ckv_raw, krope_raw, cos, sin, g_cq, g_ckv, w_uq, w_ukv):
    B, S, _ = cq_raw.shape
    q = (_rmsnorm(cq_raw, g_cq) @ w_uq).reshape(B, S, MLA_HEADS, MLA_QK)
    q = jnp.concatenate([q[..., :MLA_NOPE],
                         _rope(q[..., MLA_NOPE:], cos[:, :, None, :], sin[:, :, None, :])], -1)
    kv = (_rmsnorm(ckv_raw, g_ckv) @ w_ukv).reshape(B, S, MLA_HEADS, MLA_NOPE + MLA_V)
    k_rope = _rope(krope_raw, cos, sin)
    k = jnp.concatenate([kv[..., :MLA_NOPE],
                         jnp.broadcast_to(k_rope[:, :, None, :], (B, S, MLA_HEADS, MLA_ROPE))], -1)
    v = kv[..., MLA_NOPE:]
    scale = MLA_QK ** -0.5
    qb = jnp.moveaxis(q.reshape(B, S // Q_BLOCK, Q_BLOCK, MLA_HEADS, MLA_QK), 1, 0)

    def attend(qblk):
        s = jnp.einsum('bqhd,bkhd->bhqk', qblk, k).astype(jnp.float32) * scale
        p = jax.nn.softmax(s, axis=-1).astype(v.dtype)
        return jnp.einsum('bhqk,bkhd->bqhd', p, v)

    o = lax.map(attend, qb)
    return jnp.moveaxis(o, 0, 1).reshape(B, S, MLA_WIDTH)


def _memory_branch(q_raw, mem, w_kv):
    B, S, _ = q_raw.shape
    M = mem.shape[1]
    q = q_raw.reshape(B, S, MEM_HEADS, MEM_HD)
    kv = (mem @ w_kv).reshape(B, M, 2, MEM_HEADS, MEM_HD)
    k, v = kv[:, :, 0], kv[:, :, 1]
    s = jnp.einsum('bqhd,bmhd->bhqm', q, k).astype(jnp.float32) * (MEM_HD ** -0.5)
    p = jax.nn.softmax(s, axis=-1).astype(v.dtype)
    return jnp.einsum('bhqm,bmhd->bqhd', p, v).reshape(B, S, MEM_WIDTH)


def _fwd_setup_inputs(seed: int = 0) -> dict:
    key = jax.random.key(seed)
    ks = jax.random.split(key, 24)
    f32 = jnp.float32
    L = DEPTH

    def nrm(k, shape, scale):
        return jax.random.normal(k, shape, f32) * scale

    def gain(k, shape):
        return 1.0 + 0.02 * jax.random.normal(k, shape, f32)

    return {
        "x": nrm(ks[0], (BATCH, SEQ, D_MODEL), 1.0),
        "mem": nrm(ks[1], (BATCH, N_MEM, D_MODEL), 1.0),
        "positions": jnp.tile(jnp.arange(SEQ, dtype=jnp.int32)[None, :], (BATCH, 1)),
        "ln_emb_g": gain(ks[2], (D_MODEL,)),
        "ln_emb_b": nrm(ks[3], (D_MODEL,), 0.02),
        "hgrn_lb_logits": nrm(ks[4], (2, DEPTH + 1, HG_WIDTH), 0.5),
        "w_in": nrm(ks[5], (L, D_MODEL, IN_WIDTH), D_MODEL ** -0.5),
        "hgrn_norm_g": gain(ks[6], (L, HG_DV)),
        "mla_g_cq": gain(ks[7], (L, MLA_Q_RANK)),
        "mla_g_ckv": gain(ks[8], (L, MLA_KV_RANK)),
        "mla_w_uq": nrm(ks[9], (L, MLA_Q_RANK, MLA_HEADS * MLA_QK), MLA_Q_RANK ** -0.5),
        "mla_w_ukv": nrm(ks[10], (L, MLA_KV_RANK, MLA_HEADS * (MLA_NOPE + MLA_V)), MLA_KV_RANK ** -0.5),
        "mem_w_kv": nrm(ks[11], (L, D_MODEL, 2 * MEM_WIDTH), D_MODEL ** -0.5),
        "w_branch": nrm(ks[12], (L, N_BRANCH, BRANCH_WIDTH, D_MODEL), BETA * BRANCH_WIDTH ** -0.5),
        "w_o": nrm(ks[13], (L, D_MODEL, D_MODEL), BETA * D_MODEL ** -0.5),
        "ln1_g": gain(ks[14], (L, D_MODEL)),
        "ln1_b": nrm(ks[15], (L, D_MODEL), 0.02),
        "w_ffn_gate": nrm(ks[16], (L, D_MODEL, D_FF), D_MODEL ** -0.5),
        "w_ffn_up": nrm(ks[17], (L, D_MODEL, D_FF), D_MODEL ** -0.5),
        "w_ffn_down": nrm(ks[18], (L, D_FF, D_MODEL), BETA * D_FF ** -0.5),
        "ln2_g": gain(ks[19], (L, D_MODEL)),
        "ln2_b": nrm(ks[20], (L, D_MODEL), 0.02),
    }


def _fwd_reference(x, mem, positions, ln_emb_g, ln_emb_b, hgrn_lb_logits, w_in, hgrn_norm_g,
              mla_g_cq, mla_g_ckv, mla_w_uq, mla_w_ukv, mem_w_kv, w_branch, w_o,
              ln1_g, ln1_b, w_ffn_gate, w_ffn_up, w_ffn_down, ln2_g, ln2_b):
    B, S, D = x.shape
    f32 = jnp.float32
    half = MLA_ROPE // 2
    inv_freq = jnp.power(ROPE_THETA, -jnp.arange(half, dtype=f32) / half)
    ang = positions.astype(f32)[..., None] * inv_freq
    cos, sin = jnp.cos(ang), jnp.sin(ang)
    lb_all = jnp.cumsum(jax.nn.softmax(hgrn_lb_logits.astype(f32), axis=1), axis=1)

    h = _layernorm(x, ln_emb_g, ln_emb_b)
    for l in range(DEPTH):
        proj = h @ w_in[l]
        (q_hg, i_hg, f_fw, f_bw, g_hg, c_q, c_kv, k_rope, q_mem, gates) = _split(proj, IN_SPLITS)
        y_hg = _hgrn2_branch(q_hg, i_hg, f_fw, f_bw, g_hg, lb_all[:, l], hgrn_norm_g[l])
        y_mla = _mla_branch(c_q, c_kv, k_rope, cos, sin, mla_g_cq[l], mla_g_ckv[l],
                            mla_w_uq[l], mla_w_ukv[l])
        y_mem = _memory_branch(q_mem, mem, mem_w_kv[l])
        gsig = jax.nn.sigmoid(gates.astype(f32)).astype(h.dtype).reshape(B, S, N_BRANCH, D)
        merged = (gsig[:, :, 0] * (y_hg @ w_branch[l, 0])
                  + gsig[:, :, 1] * (y_mla @ w_branch[l, 1])
                  + gsig[:, :, 2] * (y_mem @ w_branch[l, 2]))
        mix = merged @ w_o[l]
        h = _layernorm(ALPHA * h + mix, ln1_g[l], ln1_b[l])
        ff = (jax.nn.silu(h @ w_ffn_gate[l]) * (h @ w_ffn_up[l])) @ w_ffn_down[l]
        h = _layernorm(ALPHA * h + ff, ln2_g[l], ln2_b[l])
    return h


import jax as _jax
import jax.numpy as _jnp

TWIN_FORMAT = 'train_step'
FWD_PARAMS = ['x', 'mem', 'positions', 'ln_emb_g', 'ln_emb_b', 'hgrn_lb_logits', 'w_in', 'hgrn_norm_g', 'mla_g_cq', 'mla_g_ckv', 'mla_w_uq', 'mla_w_ukv', 'mem_w_kv', 'w_branch', 'w_o', 'ln1_g', 'ln1_b', 'w_ffn_gate', 'w_ffn_up', 'w_ffn_down', 'ln2_g', 'ln2_b']
TWIN_WEIGHTS = ['ln_emb_g', 'ln_emb_b', 'hgrn_lb_logits', 'w_in', 'hgrn_norm_g', 'mla_g_cq', 'mla_g_ckv', 'mla_w_uq', 'mla_w_ukv', 'mem_w_kv', 'w_branch', 'w_o', 'ln1_g', 'ln1_b', 'w_ffn_gate', 'w_ffn_up', 'w_ffn_down', 'ln2_g', 'ln2_b']
TWIN_DIFF_INPUT = 'x'
TWIN_INPUTS = ['x', 'mem', 'positions', 'ln_emb_g', 'ln_emb_b', 'hgrn_lb_logits', 'w_in', 'hgrn_norm_g', 'mla_g_cq', 'mla_g_ckv', 'mla_w_uq', 'mla_w_ukv', 'mem_w_kv', 'w_branch', 'w_o', 'ln1_g', 'ln1_b', 'w_ffn_gate', 'w_ffn_up', 'w_ffn_down', 'ln2_g', 'ln2_b', 'loss_target', 'm_ln_emb_g', 'm_ln_emb_b', 'm_hgrn_lb_logits', 'm_w_in', 'm_hgrn_norm_g', 'm_mla_g_cq', 'm_mla_g_ckv', 'm_mla_w_uq', 'm_mla_w_ukv', 'm_mem_w_kv', 'm_w_branch', 'm_w_o', 'm_ln1_g', 'm_ln1_b', 'm_w_ffn_gate', 'm_w_ffn_up', 'm_w_ffn_down', 'm_ln2_g', 'm_ln2_b', 'v_ln_emb_g', 'v_ln_emb_b', 'v_hgrn_lb_logits', 'v_w_in', 'v_hgrn_norm_g', 'v_mla_g_cq', 'v_mla_g_ckv', 'v_mla_w_uq', 'v_mla_w_ukv', 'v_mem_w_kv', 'v_w_branch', 'v_w_o', 'v_ln1_g', 'v_ln1_b', 'v_w_ffn_gate', 'v_w_ffn_up', 'v_w_ffn_down', 'v_ln2_g', 'v_ln2_b']
TWIN_OUTPUTS = ['loss', 'grad_x', 'grad_ln_emb_g', 'grad_ln_emb_b', 'grad_hgrn_lb_logits', 'grad_w_in', 'grad_hgrn_norm_g', 'grad_mla_g_cq', 'grad_mla_g_ckv', 'grad_mla_w_uq', 'grad_mla_w_ukv', 'grad_mem_w_kv', 'grad_w_branch', 'grad_w_o', 'grad_ln1_g', 'grad_ln1_b', 'grad_w_ffn_gate', 'grad_w_ffn_up', 'grad_w_ffn_down', 'grad_ln2_g', 'grad_ln2_b', 'delta_ln_emb_g', 'delta_ln_emb_b', 'delta_hgrn_lb_logits', 'delta_w_in', 'delta_hgrn_norm_g', 'delta_mla_g_cq', 'delta_mla_g_ckv', 'delta_mla_w_uq', 'delta_mla_w_ukv', 'delta_mem_w_kv', 'delta_w_branch', 'delta_w_o', 'delta_ln1_g', 'delta_ln1_b', 'delta_w_ffn_gate', 'delta_w_ffn_up', 'delta_w_ffn_down', 'delta_ln2_g', 'delta_ln2_b', 'new_m_ln_emb_g', 'new_m_ln_emb_b', 'new_m_hgrn_lb_logits', 'new_m_w_in', 'new_m_hgrn_norm_g', 'new_m_mla_g_cq', 'new_m_mla_g_ckv', 'new_m_mla_w_uq', 'new_m_mla_w_ukv', 'new_m_mem_w_kv', 'new_m_w_branch', 'new_m_w_o', 'new_m_ln1_g', 'new_m_ln1_b', 'new_m_w_ffn_gate', 'new_m_w_ffn_up', 'new_m_w_ffn_down', 'new_m_ln2_g', 'new_m_ln2_b', 'new_v_ln_emb_g', 'new_v_ln_emb_b', 'new_v_hgrn_lb_logits', 'new_v_w_in', 'new_v_hgrn_norm_g', 'new_v_mla_g_cq', 'new_v_mla_g_ckv', 'new_v_mla_w_uq', 'new_v_mla_w_ukv', 'new_v_mem_w_kv', 'new_v_w_branch', 'new_v_w_o', 'new_v_ln1_g', 'new_v_ln1_b', 'new_v_w_ffn_gate', 'new_v_w_ffn_up', 'new_v_w_ffn_down', 'new_v_ln2_g', 'new_v_ln2_b']
TWIN_LEAF_KINDS = {'loss': 'loss', 'grad_x': 'grad_x', 'grad_ln_emb_g': 'grad_w', 'grad_ln_emb_b': 'grad_w', 'grad_hgrn_lb_logits': 'grad_w', 'grad_w_in': 'grad_w', 'grad_hgrn_norm_g': 'grad_w', 'grad_mla_g_cq': 'grad_w', 'grad_mla_g_ckv': 'grad_w', 'grad_mla_w_uq': 'grad_w', 'grad_mla_w_ukv': 'grad_w', 'grad_mem_w_kv': 'grad_w', 'grad_w_branch': 'grad_w', 'grad_w_o': 'grad_w', 'grad_ln1_g': 'grad_w', 'grad_ln1_b': 'grad_w', 'grad_w_ffn_gate': 'grad_w', 'grad_w_ffn_up': 'grad_w', 'grad_w_ffn_down': 'grad_w', 'grad_ln2_g': 'grad_w', 'grad_ln2_b': 'grad_w', 'delta_ln_emb_g': 'delta_w', 'delta_ln_emb_b': 'delta_w', 'delta_hgrn_lb_logits': 'delta_w', 'delta_w_in': 'delta_w', 'delta_hgrn_norm_g': 'delta_w', 'delta_mla_g_cq': 'delta_w', 'delta_mla_g_ckv': 'delta_w', 'delta_mla_w_uq': 'delta_w', 'delta_mla_w_ukv': 'delta_w', 'delta_mem_w_kv': 'delta_w', 'delta_w_branch': 'delta_w', 'delta_w_o': 'delta_w', 'delta_ln1_g': 'delta_w', 'delta_ln1_b': 'delta_w', 'delta_w_ffn_gate': 'delta_w', 'delta_w_ffn_up': 'delta_w', 'delta_w_ffn_down': 'delta_w', 'delta_ln2_g': 'delta_w', 'delta_ln2_b': 'delta_w', 'new_m_ln_emb_g': 'new_m', 'new_m_ln_emb_b': 'new_m', 'new_m_hgrn_lb_logits': 'new_m', 'new_m_w_in': 'new_m', 'new_m_hgrn_norm_g': 'new_m', 'new_m_mla_g_cq': 'new_m', 'new_m_mla_g_ckv': 'new_m', 'new_m_mla_w_uq': 'new_m', 'new_m_mla_w_ukv': 'new_m', 'new_m_mem_w_kv': 'new_m', 'new_m_w_branch': 'new_m', 'new_m_w_o': 'new_m', 'new_m_ln1_g': 'new_m', 'new_m_ln1_b': 'new_m', 'new_m_w_ffn_gate': 'new_m', 'new_m_w_ffn_up': 'new_m', 'new_m_w_ffn_down': 'new_m', 'new_m_ln2_g': 'new_m', 'new_m_ln2_b': 'new_m', 'new_v_ln_emb_g': 'new_v', 'new_v_ln_emb_b': 'new_v', 'new_v_hgrn_lb_logits': 'new_v', 'new_v_w_in': 'new_v', 'new_v_hgrn_norm_g': 'new_v', 'new_v_mla_g_cq': 'new_v', 'new_v_mla_g_ckv': 'new_v', 'new_v_mla_w_uq': 'new_v', 'new_v_mla_w_ukv': 'new_v', 'new_v_mem_w_kv': 'new_v', 'new_v_w_branch': 'new_v', 'new_v_w_o': 'new_v', 'new_v_ln1_g': 'new_v', 'new_v_ln1_b': 'new_v', 'new_v_w_ffn_gate': 'new_v', 'new_v_w_ffn_up': 'new_v', 'new_v_w_ffn_down': 'new_v', 'new_v_ln2_g': 'new_v', 'new_v_ln2_b': 'new_v'}


def _forward(args):
    return _fwd_reference(*[args[k] for k in FWD_PARAMS])


def _output_shape():
    def fwd():
        inp = _fwd_setup_inputs(0)
        return _fwd_reference(*[inp[k] for k in FWD_PARAMS])
    out = _jax.eval_shape(fwd)
    return out.shape, out.dtype

N_MICROBATCH = 1
ADAM_LR = 0.001
ADAM_B1 = 0.9
ADAM_B2 = 0.999
ADAM_EPS = 1e-08
ADAM_WD = 0.01
ADAM_STEP = 10
PER_EXAMPLE_BATCH_AXIS = {'x': 0, 'mem': 0, 'positions': 0, 'loss_target': 0}
SHARED_INPUTS = []
_WEIGHT_DTYPES = {'ln_emb_g': _jnp.float32, 'ln_emb_b': _jnp.float32, 'hgrn_lb_logits': _jnp.float32, 'w_in': _jnp.float32, 'hgrn_norm_g': _jnp.float32, 'mla_g_cq': _jnp.float32, 'mla_g_ckv': _jnp.float32, 'mla_w_uq': _jnp.float32, 'mla_w_ukv': _jnp.float32, 'mem_w_kv': _jnp.float32, 'w_branch': _jnp.float32, 'w_o': _jnp.float32, 'ln1_g': _jnp.float32, 'ln1_b': _jnp.float32, 'w_ffn_gate': _jnp.float32, 'w_ffn_up': _jnp.float32, 'w_ffn_down': _jnp.float32, 'ln2_g': _jnp.float32, 'ln2_b': _jnp.float32}
MOMENT_SCALE = {'ln_emb_g': 9.771424e-01, 'ln_emb_b': 4.891607e-01, 'hgrn_lb_logits': 7.740887e-04, 'w_in': 5.439247e-03, 'hgrn_norm_g': 7.345324e-02, 'mla_g_cq': 4.558731e-03, 'mla_g_ckv': 6.528106e-03, 'mla_w_uq': 2.593730e-03, 'mla_w_ukv': 3.050298e-03, 'mem_w_kv': 3.239453e-03, 'w_branch': 1.147717e-02, 'w_o': 1.984033e-02, 'ln1_g': 9.842501e-01, 'ln1_b': 4.694464e-01, 'w_ffn_gate': 2.228352e-02, 'w_ffn_up': 2.163367e-02, 'w_ffn_down': 6.030554e-02, 'ln2_g': 3.201153e+01, 'ln2_b': 9.287659e-01}


def _to_microbatches(a, axis):
    t = _jnp.moveaxis(a, axis, 0)
    t = t.reshape((N_MICROBATCH, t.shape[0] // N_MICROBATCH) + t.shape[1:])
    return _jnp.moveaxis(t, 1, axis + 1)


def setup_inputs(seed: int = 0) -> dict:
    inp = _fwd_setup_inputs(seed)
    key = _jax.random.fold_in(_jax.random.key(seed), 7919)
    shape, _ = _output_shape()
    out = dict(inp)
    out["loss_target"] = _jax.random.normal(_jax.random.fold_in(key, 0), shape, _jnp.float32)
    for i, name in enumerate(TWIN_WEIGHTS):
        w = inp[name].astype(_jnp.float32)
        if MOMENT_SCALE is None:
            s = _jnp.sqrt(_jnp.mean(_jnp.square(w)) + 1e-30)
        else:
            s = MOMENT_SCALE[name]
        km, kv = _jax.random.split(_jax.random.fold_in(key, i + 1))
        out[name] = w
        out["m_" + name] = s * _jax.random.normal(km, w.shape, _jnp.float32)
        out["v_" + name] = (s * s) * _jax.random.uniform(kv, w.shape, _jnp.float32, 0.5, 1.5)
    if N_MICROBATCH > 1:
        for name, axis in PER_EXAMPLE_BATCH_AXIS.items():
            out[name] = _to_microbatches(out[name], axis)
    return {'x': out['x'], 'mem': out['mem'], 'positions': out['positions'], 'ln_emb_g': out['ln_emb_g'], 'ln_emb_b': out['ln_emb_b'], 'hgrn_lb_logits': out['hgrn_lb_logits'], 'w_in': out['w_in'], 'hgrn_norm_g': out['hgrn_norm_g'], 'mla_g_cq': out['mla_g_cq'], 'mla_g_ckv': out['mla_g_ckv'], 'mla_w_uq': out['mla_w_uq'], 'mla_w_ukv': out['mla_w_ukv'], 'mem_w_kv': out['mem_w_kv'], 'w_branch': out['w_branch'], 'w_o': out['w_o'], 'ln1_g': out['ln1_g'], 'ln1_b': out['ln1_b'], 'w_ffn_gate': out['w_ffn_gate'], 'w_ffn_up': out['w_ffn_up'], 'w_ffn_down': out['w_ffn_down'], 'ln2_g': out['ln2_g'], 'ln2_b': out['ln2_b'], 'loss_target': out['loss_target'], 'm_ln_emb_g': out['m_ln_emb_g'], 'm_ln_emb_b': out['m_ln_emb_b'], 'm_hgrn_lb_logits': out['m_hgrn_lb_logits'], 'm_w_in': out['m_w_in'], 'm_hgrn_norm_g': out['m_hgrn_norm_g'], 'm_mla_g_cq': out['m_mla_g_cq'], 'm_mla_g_ckv': out['m_mla_g_ckv'], 'm_mla_w_uq': out['m_mla_w_uq'], 'm_mla_w_ukv': out['m_mla_w_ukv'], 'm_mem_w_kv': out['m_mem_w_kv'], 'm_w_branch': out['m_w_branch'], 'm_w_o': out['m_w_o'], 'm_ln1_g': out['m_ln1_g'], 'm_ln1_b': out['m_ln1_b'], 'm_w_ffn_gate': out['m_w_ffn_gate'], 'm_w_ffn_up': out['m_w_ffn_up'], 'm_w_ffn_down': out['m_w_ffn_down'], 'm_ln2_g': out['m_ln2_g'], 'm_ln2_b': out['m_ln2_b'], 'v_ln_emb_g': out['v_ln_emb_g'], 'v_ln_emb_b': out['v_ln_emb_b'], 'v_hgrn_lb_logits': out['v_hgrn_lb_logits'], 'v_w_in': out['v_w_in'], 'v_hgrn_norm_g': out['v_hgrn_norm_g'], 'v_mla_g_cq': out['v_mla_g_cq'], 'v_mla_g_ckv': out['v_mla_g_ckv'], 'v_mla_w_uq': out['v_mla_w_uq'], 'v_mla_w_ukv': out['v_mla_w_ukv'], 'v_mem_w_kv': out['v_mem_w_kv'], 'v_w_branch': out['v_w_branch'], 'v_w_o': out['v_w_o'], 'v_ln1_g': out['v_ln1_g'], 'v_ln1_b': out['v_ln1_b'], 'v_w_ffn_gate': out['v_w_ffn_gate'], 'v_w_ffn_up': out['v_w_ffn_up'], 'v_w_ffn_down': out['v_w_ffn_down'], 'v_ln2_g': out['v_ln2_g'], 'v_ln2_b': out['v_ln2_b']}


def _loss(weights, diff, rest, loss_target):
    with _jax.named_scope("forward"):
        args = {**rest, TWIN_DIFF_INPUT: diff, **{k: w.astype(_WEIGHT_DTYPES[k]) for k, w in weights.items()}}
        y = _forward(args)
    with _jax.named_scope("loss_head"):
        err = _jnp.square(y.astype(_jnp.float32) - loss_target)
        return 0.5 * _jnp.sum(_jnp.mean(err, axis=-1)) if err.ndim else 0.5 * err


def _adamw(w, g, m, v):
    m = ADAM_B1 * m + (1.0 - ADAM_B1) * g
    v = ADAM_B2 * v + (1.0 - ADAM_B2) * _jnp.square(g)
    m_hat = m / (1.0 - ADAM_B1 ** ADAM_STEP)
    v_hat = v / (1.0 - ADAM_B2 ** ADAM_STEP)
    delta = -ADAM_LR * (m_hat / (_jnp.sqrt(v_hat) + ADAM_EPS) + ADAM_WD * w)
    return delta, m, v


def reference(x, mem, positions, ln_emb_g, ln_emb_b, hgrn_lb_logits, w_in, hgrn_norm_g, mla_g_cq, mla_g_ckv, mla_w_uq, mla_w_ukv, mem_w_kv, w_branch, w_o, ln1_g, ln1_b, w_ffn_gate, w_ffn_up, w_ffn_down, ln2_g, ln2_b, loss_target, m_ln_emb_g, m_ln_emb_b, m_hgrn_lb_logits, m_w_in, m_hgrn_norm_g, m_mla_g_cq, m_mla_g_ckv, m_mla_w_uq, m_mla_w_ukv, m_mem_w_kv, m_w_branch, m_w_o, m_ln1_g, m_ln1_b, m_w_ffn_gate, m_w_ffn_up, m_w_ffn_down, m_ln2_g, m_ln2_b, v_ln_emb_g, v_ln_emb_b, v_hgrn_lb_logits, v_w_in, v_hgrn_norm_g, v_mla_g_cq, v_mla_g_ckv, v_mla_w_uq, v_mla_w_ukv, v_mem_w_kv, v_w_branch, v_w_o, v_ln1_g, v_ln1_b, v_w_ffn_gate, v_w_ffn_up, v_w_ffn_down, v_ln2_g, v_ln2_b):
    given = dict(x=x, mem=mem, positions=positions, ln_emb_g=ln_emb_g, ln_emb_b=ln_emb_b, hgrn_lb_logits=hgrn_lb_logits, w_in=w_in, hgrn_norm_g=hgrn_norm_g, mla_g_cq=mla_g_cq, mla_g_ckv=mla_g_ckv, mla_w_uq=mla_w_uq, mla_w_ukv=mla_w_ukv, mem_w_kv=mem_w_kv, w_branch=w_branch, w_o=w_o, ln1_g=ln1_g, ln1_b=ln1_b, w_ffn_gate=w_ffn_gate, w_ffn_up=w_ffn_up, w_ffn_down=w_ffn_down, ln2_g=ln2_g, ln2_b=ln2_b, loss_target=loss_target, m_ln_emb_g=m_ln_emb_g, m_ln_emb_b=m_ln_emb_b, m_hgrn_lb_logits=m_hgrn_lb_logits, m_w_in=m_w_in, m_hgrn_norm_g=m_hgrn_norm_g, m_mla_g_cq=m_mla_g_cq, m_mla_g_ckv=m_mla_g_ckv, m_mla_w_uq=m_mla_w_uq, m_mla_w_ukv=m_mla_w_ukv, m_mem_w_kv=m_mem_w_kv, m_w_branch=m_w_branch, m_w_o=m_w_o, m_ln1_g=m_ln1_g, m_ln1_b=m_ln1_b, m_w_ffn_gate=m_w_ffn_gate, m_w_ffn_up=m_w_ffn_up, m_w_ffn_down=m_w_ffn_down, m_ln2_g=m_ln2_g, m_ln2_b=m_ln2_b, v_ln_emb_g=v_ln_emb_g, v_ln_emb_b=v_ln_emb_b, v_hgrn_lb_logits=v_hgrn_lb_logits, v_w_in=v_w_in, v_hgrn_norm_g=v_hgrn_norm_g, v_mla_g_cq=v_mla_g_cq, v_mla_g_ckv=v_mla_g_ckv, v_mla_w_uq=v_mla_w_uq, v_mla_w_ukv=v_mla_w_ukv, v_mem_w_kv=v_mem_w_kv, v_w_branch=v_w_branch, v_w_o=v_w_o, v_ln1_g=v_ln1_g, v_ln1_b=v_ln1_b, v_w_ffn_gate=v_w_ffn_gate, v_w_ffn_up=v_w_ffn_up, v_w_ffn_down=v_w_ffn_down, v_ln2_g=v_ln2_g, v_ln2_b=v_ln2_b)
    weights = {n: given[n] for n in TWIN_WEIGHTS}
    shared = {n: given[n] for n in SHARED_INPUTS}
    per_example = {n: given[n] for n in ['x', 'mem', 'positions']}
    grad_fn = _jax.value_and_grad(_loss, argnums=(0, 1))

    def one_microbatch(ex, loss_target):
        ex = dict(ex)
        diff = ex.pop(TWIN_DIFF_INPUT)
        return grad_fn(weights, diff, {**shared, **ex}, loss_target)

    if N_MICROBATCH == 1:
        loss, (grad_w, grad_x) = one_microbatch(per_example, given["loss_target"])
    else:
        def body(carry, xs):
            loss_sum, grad_sum = carry
            l_k, (gw_k, gx_k) = one_microbatch(xs[0], xs[1])
            with _jax.named_scope("update"):
                return (loss_sum + l_k, _jax.tree.map(_jnp.add, grad_sum, gw_k)), gx_k

        init = (_jnp.zeros((), _jnp.float32), _jax.tree.map(_jnp.zeros_like, weights))
        (loss, grad_w), grad_x = _jax.lax.scan(body, init, (per_example, given["loss_target"]))
    with _jax.named_scope("update"):
        delta_w, new_m, new_v = {}, {}, {}
        for n in TWIN_WEIGHTS:
            delta_w[n], new_m[n], new_v[n] = _adamw(weights[n], grad_w[n], given["m_" + n], given["v_" + n])
    return (loss, grad_x, *[grad_w[n] for n in TWIN_WEIGHTS], *[delta_w[n] for n in TWIN_WEIGHTS],
            *[new_m[n] for n in TWIN_WEIGHTS], *[new_v[n] for n in TWIN_WEIGHTS])
```

```python
import functools
import math

import jax
import jax.numpy as jnp
from jax import lax
from jax.experimental import pallas as pl
from jax.experimental.pallas import tpu as pltpu

F32 = jnp.float32
BF16 = jnp.bfloat16

HG_HEADS = 8
HG_D = 128
MLA_HEADS = 8
MLA_NOPE = 128
MLA_ROPE = 64
MLA_V = 128
MEM_HEADS = 4
N_BRANCH = 3
ROPE_THETA = 10000.0
DEPTH = 1
ALPHA = (2.0 * DEPTH) ** 0.25
LN_EPS = 1e-5
RMS_EPS = 1e-6
ADAM_LR = 0.001
ADAM_B1 = 0.9
ADAM_B2 = 0.999
ADAM_EPS = 1e-08
ADAM_WD = 0.01
ADAM_STEP = 10

LANES = 128
SUBLANES = 8
VMEM_LIMIT = 48 * 1024 * 1024

HG_CHUNK = 128
HG_SUB = 16
PACK_W = 1024

MESH = pl.DeviceIdType.MESH
HI = lax.Precision.HIGHEST


def _cparams(sem=None):
    if sem is None:
        return pltpu.CompilerParams(vmem_limit_bytes=VMEM_LIMIT)
    return pltpu.CompilerParams(dimension_semantics=sem, vmem_limit_bytes=VMEM_LIMIT)


def _tile(dim, pref, quantum):
    t = min(pref, dim) // quantum * quantum
    while t >= quantum:
        if dim % t == 0:
            return t
        t -= quantum
    return dim


def _sigmoid(x):
    return 1.0 / (1.0 + jnp.exp(-x))


def _mm(a, b, M, N, K, *, ta=False, tb=False, a_off=(0, 0), b_off=(0, 0), add=None,
        tm=1024, tn=1024, tk=512, name):
    tm = _tile(M, tm, LANES if ta else SUBLANES)
    tn = _tile(N, tn, LANES)
    tk = _tile(K, tk, LANES)
    nk = K // tk
    ar, ac = a_off
    br, bc = b_off

    if ta:
        assert ar % tk == 0 and ac % tm == 0
        a_spec = pl.BlockSpec((tk, tm), lambda i, j, k: (ar // tk + k, ac // tm + i))
    else:
        assert ar % tm == 0 and ac % tk == 0
        a_spec = pl.BlockSpec((tm, tk), lambda i, j, k: (ar // tm + i, ac // tk + k))
    if tb:
        assert br % tn == 0 and bc % tk == 0
        b_spec = pl.BlockSpec((tn, tk), lambda i, j, k: (br // tn + j, bc // tk + k))
    else:
        assert br % tk == 0 and bc % tn == 0
        b_spec = pl.BlockSpec((tk, tn), lambda i, j, k: (br // tk + k, bc // tn + j))
    o_spec = pl.BlockSpec((tm, tn), lambda i, j, k: (i, j))
    mixed = a.dtype != b.dtype

    def body(*refs):
        if add is None:
            a_ref, b_ref, o_ref, acc = refs
        else:
            a_ref, b_ref, add_ref, o_ref, acc = refs
        k = pl.program_id(2)

        @pl.when(k == 0)
        def _():
            acc[...] = jnp.zeros_like(acc)

        av = a_ref[...]
        bv = b_ref[...]
        if ta:
            av = av.astype(F32).T
        if mixed:
            av = av.astype(BF16)
            bv = bv.astype(BF16)
        dims = (((1,), (1 if tb else 0,)), ((), ()))
        acc[...] += lax.dot_general(av, bv, dims, preferred_element_type=F32)

        @pl.when(k == nk - 1)
        def _():
            if add is None:
                o_ref[...] = acc[...]
            else:
                o_ref[...] = acc[...] + add_ref[...]

    in_specs = [a_spec, b_spec]
    args = [a, b]
    if add is not None:
        in_specs.append(o_spec)
        args.append(add)
    return pl.pallas_call(
        body, name=name, grid=(M // tm, N // tn, nk),
        in_specs=in_specs, out_specs=o_spec,
        out_shape=jax.ShapeDtypeStruct((M, N), F32),
        scratch_shapes=[pltpu.VMEM((tm, tn), F32)],
        compiler_params=_cparams(("parallel", "parallel", "arbitrary")),
    )(*args)


class _Rows:
    def __init__(self, arr, width, col0=0, lead=None, dtype=F32):
        self.arr, self.width, self.col0, self.lead, self.dtype = arr, width, col0, lead, dtype


def _rowwise(fn, rows, consts, outs, accs, *, R, tr, ncol=1, name):
    tr = _tile(R, tr, SUBLANES)
    nrow = R // tr

    def spec(r):
        if r.lead is None:
            return pl.BlockSpec((tr, r.width), lambda j, i, c0=r.col0: (i, c0 + j))
        return pl.BlockSpec((None, tr, r.width), lambda j, i, c0=r.col0, l=r.lead: (l, i, c0 + j))

    in_specs = [spec(r) for r in rows]
    for c in consts:
        in_specs.append(pl.BlockSpec(c.shape, lambda j, i, nd=c.ndim: (0,) * nd))
    out_specs = [spec(o) for o in outs]
    out_shape = [jax.ShapeDtypeStruct((R, o.arr), o.dtype) for o in outs]
    for w in accs:
        out_specs.append(pl.BlockSpec((1, w), lambda j, i: (0, j)))
        out_shape.append(jax.ShapeDtypeStruct((1, w * ncol), F32))
    n_in = len(rows) + len(consts)
    n_out = len(outs)

    def body(*refs):
        ins = [r[...] for r in refs[:n_in]]
        res = fn(*ins)
        if not isinstance(res, (tuple, list)):
            res = (res,)
        for k in range(n_out):
            refs[n_in + k][...] = res[k].astype(refs[n_in + k].dtype)
        i = pl.program_id(1)
        for k in range(len(accs)):
            a_ref = refs[n_in + n_out + k]

            @pl.when(i == 0)
            def _(a_ref=a_ref):
                a_ref[...] = jnp.zeros_like(a_ref)

            a_ref[...] += res[n_out + k]

    res = pl.pallas_call(
        body, name=name, grid=(ncol, nrow),
        in_specs=in_specs, out_specs=out_specs, out_shape=out_shape,
        compiler_params=_cparams(("parallel", "arbitrary")),
    )(*[r.arr for r in rows], *consts)
    return res


def _colsum(x):
    return jnp.sum(x, axis=0, keepdims=True)


def _ln_stats(z):
    mu = jnp.mean(z, axis=-1, keepdims=True)
    zc = z - mu
    var = jnp.mean(zc * zc, axis=-1, keepdims=True)
    rstd = lax.rsqrt(var + LN_EPS)
    return zc * rstd, rstd


def _ln_bwd_core(z, g, dy):
    xhat, rstd = _ln_stats(z)
    dxh = dy * g
    m1 = jnp.mean(dxh, axis=-1, keepdims=True)
    m2 = jnp.mean(dxh * xhat, axis=-1, keepdims=True)
    dz = rstd * (dxh - m1 - xhat * m2)
    return dz, _colsum(dy * xhat), _colsum(dy)


def _rms_fwd(x, g, eps):
    r = lax.rsqrt(jnp.mean(x * x, axis=-1, keepdims=True) + eps)
    return x * r * g


def _rms_bwd(x, g, dy, eps):
    r = lax.rsqrt(jnp.mean(x * x, axis=-1, keepdims=True) + eps)
    xr = x * r
    dyg = dy * g
    dx = r * (dyg - xr * jnp.mean(dyg * xr, axis=-1, keepdims=True))
    return dx, dy * xr


def _hg_gate(fr, lb):
    sig = _sigmoid(fr)
    f = lb + (1.0 - lb) * sig
    return sig, f


def _hg_masks(rev):
    C = HG_CHUNK
    t = lax.broadcasted_iota(jnp.int32, (C, C), 0)
    s = lax.broadcasted_iota(jnp.int32, (C, C), 1)
    tri = (s >= t) if rev else (s <= t)
    return tri


def _hg_offdiag(Q, K, b, i, rev):
    C, sb = HG_CHUNK, HG_SUB
    nb = C // sb
    if (not rev and i == 0) or (rev and i == nb - 1):
        return None
    ref = b[sb * i - 1:sb * i] if not rev else b[sb * (i + 1):sb * (i + 1) + 1]
    srow = lax.broadcasted_iota(jnp.int32, (C, 1), 0)
    smask = (srow < sb * i) if not rev else (srow >= sb * (i + 1))
    qscale = jnp.exp(jnp.minimum(b - ref, 0.0))
    kscale = jnp.where(smask, jnp.exp(jnp.minimum(ref - b, 0.0)), 0.0)
    return qscale, kscale


def _hg_att(Q, K, b, rev):
    C, sb = HG_CHUNK, HG_SUB
    lane = lax.broadcasted_iota(jnp.int32, (sb, C), 1)
    rloc = lax.broadcasted_iota(jnp.int32, (sb, 1), 0)
    rows = []
    for i in range(C // sb):
        sl = slice(sb * i, sb * i + sb)
        Qi, Ki, bi = Q[sl], K[sl], b[sl]
        od = _hg_offdiag(Q, K, b, i, rev)
        if od is None:
            acc = jnp.zeros((sb, C), F32)
        else:
            qs, ks = od
            acc = lax.dot_general(Qi * qs[sl], K * ks, (((1,), (1,)), ((), ())),
                                  precision=HI, preferred_element_type=F32)
        for j in range(sb):
            e = jnp.exp(jnp.minimum(bi - bi[j:j + 1], 0.0))
            col = jnp.sum(Qi * Ki[j:j + 1] * e, axis=-1, keepdims=True)
            vis = (rloc <= j) if rev else (rloc >= j)
            acc = jnp.where(lane == sb * i + j, jnp.where(vis, col, 0.0), acc)
        rows.append(acc)
    return jnp.concatenate(rows, axis=0)


def _hg_att_bwd(Q, K, b, dA, rev):
    C, sb = HG_CHUNK, HG_SUB
    rloc = lax.broadcasted_iota(jnp.int32, (sb, 1), 0)
    rrow = lax.broadcasted_iota(jnp.int32, (sb, HG_D), 0)
    trow = lax.broadcasted_iota(jnp.int32, (C, C), 1) // sb
    dAT = dA.T
    dQ_rows, dKd_rows = [], []
    dK = jnp.zeros((C, HG_D), F32)
    for i in range(C // sb):
        sl = slice(sb * i, sb * i + sb)
        Qi, Ki, bi, dAi = Q[sl], K[sl], b[sl], dA[sl]
        od = _hg_offdiag(Q, K, b, i, rev)
        if od is None:
            dQi = jnp.zeros((sb, HG_D), F32)
        else:
            qs, ks = od
            dQi = lax.dot_general(dAi, K * ks, (((1,), (0,)), ((), ())),
                                  precision=HI, preferred_element_type=F32) * qs[sl]
            zt = jnp.where(trow == i, dAT, 0.0)
            dK = dK + lax.dot_general(zt, Q * qs, (((1,), (0,)), ((), ())),
                                      precision=HI, preferred_element_type=F32) * ks
        dKd = jnp.zeros((sb, HG_D), F32)
        for j in range(sb):
            vis = (rloc <= j) if rev else (rloc >= j)
            e = jnp.where(vis, jnp.exp(jnp.minimum(bi - bi[j:j + 1], 0.0)), 0.0)
            dcol = dAi[:, sb * i + j:sb * i + j + 1]
            dQi = dQi + dcol * Ki[j:j + 1] * e
            krow = jnp.sum(dcol * Qi * e, axis=0, keepdims=True)
            dKd = jnp.where(rrow == j, krow, dKd)
        dQ_rows.append(dQi)
        dKd_rows.append(dKd)
    return jnp.concatenate(dQ_rows, axis=0), dK + jnp.concatenate(dKd_rows, axis=0)


def _hg_prep(qr, fr, lb, tri):
    sigq = _sigmoid(qr)
    Q = qr * sigq
    sig, f = _hg_gate(fr, lb)
    K = 1.0 - f
    logf = jnp.log(f)
    b = lax.dot_general(tri.astype(F32), logf, (((1,), (0,)), ((), ())),
                        precision=HI, preferred_element_type=F32)
    return sigq, Q, sig, f, K, b


def _hgrn_scan(P, lb, *, S, rev, name):
    C = HG_CHUNK
    H = HG_HEADS
    NC = S // C
    fcol = (3 if rev else 2) * H

    def cidx(n):
        return NC - 1 - n if rev else n

    def body(q_ref, v_ref, f_ref, lb_ref, o_ref, st_ref, a_ref, state):
        n = pl.program_id(1)

        @pl.when(n == 0)
        def _():
            state[...] = jnp.zeros_like(state)

        tri = _hg_masks(rev)
        _, Q, _, _, K, b = _hg_prep(q_ref[...], f_ref[...], lb_ref[...], tri)
        V = v_ref[...]
        ST0 = state[...]
        st_ref[...] = ST0
        e_b = jnp.exp(b)
        bE = b[0:1] if rev else b[C - 1:C]
        W = jnp.exp(bE - b)
        inter = lax.dot_general(Q * e_b, ST0, (((1,), (1,)), ((), ())), preferred_element_type=F32)
        A = _hg_att(Q, K, b, rev)
        a_ref[...] = A
        o_ref[...] = inter + jnp.dot(A, V, preferred_element_type=F32)
        state[...] = ST0 * jnp.exp(bE) + lax.dot_general(
            V, K * W, (((0,), (0,)), ((), ())), preferred_element_type=F32)

    blk = lambda c0: pl.BlockSpec((C, HG_D), lambda h, n, c0=c0: (cidx(n), c0 + h))
    return pl.pallas_call(
        body, name=name, grid=(H, NC),
        in_specs=[blk(0), blk(H), blk(fcol), pl.BlockSpec((1, HG_D), lambda h, n: (0, h))],
        out_specs=[pl.BlockSpec((C, HG_D), lambda h, n: (cidx(n), h)),
                   pl.BlockSpec((None, None, HG_D, HG_D), lambda h, n: (cidx(n), h, 0, 0)),
                   pl.BlockSpec((None, None, C, C), lambda h, n: (cidx(n), h, 0, 0))],
        out_shape=[jax.ShapeDtypeStruct((S, H * HG_D), F32),
                   jax.ShapeDtypeStruct((NC, H, HG_D, HG_D), F32),
                   jax.ShapeDtypeStruct((NC, H, C, C), F32)],
        scratch_shapes=[pltpu.VMEM((HG_D, HG_D), F32)],
        compiler_params=_cparams(("parallel", "arbitrary")),
    )(P, P, P, lb)


def _hgrn_scan_bwd(P, lb, st, amat, do, *, S, rev, name):
    C = HG_CHUNK
    H = HG_HEADS
    NC = S // C
    fcol = (3 if rev else 2) * H

    def cidx(n):
        return n if rev else NC - 1 - n

    def body(q_ref, v_ref, f_ref, lb_ref, st_ref, a_ref, do_ref, dq_ref, dv_ref, df_ref, dlb_ref, dstate):
        n = pl.program_id(1)

        @pl.when(n == 0)
        def _():
            dstate[...] = jnp.zeros_like(dstate)
            dlb_ref[...] = jnp.zeros_like(dlb_ref)

        tri = _hg_masks(rev)
        lbv = lb_ref[...]
        qr = q_ref[...]
        sigq, Q, sig, f, K, b = _hg_prep(qr, f_ref[...], lbv, tri)
        V = v_ref[...]
        ST0 = st_ref[...]
        A = a_ref[...]
        dO = do_ref[...]
        dST1 = dstate[...]
        e_b = jnp.exp(b)
        bE = b[0:1] if rev else b[C - 1:C]
        eE = jnp.exp(bE)
        W = jnp.exp(bE - b)
        Qe = Q * e_b
        KW = K * W
        dA = jnp.where(tri, lax.dot_general(dO, V, (((1,), (1,)), ((), ())), preferred_element_type=F32), 0.0)
        dV = (lax.dot_general(A, dO, (((0,), (0,)), ((), ())), preferred_element_type=F32)
              + lax.dot_general(KW, dST1, (((1,), (1,)), ((), ())), preferred_element_type=F32))
        dQe = jnp.dot(dO, ST0, preferred_element_type=F32)
        dKW = jnp.dot(V, dST1, preferred_element_type=F32)
        dstate[...] = dST1 * eE + lax.dot_general(dO, Qe, (((0,), (0,)), ((), ())), preferred_element_type=F32)
        dQa, dKa = _hg_att_bwd(Q, K, b, dA, rev)
        dQ = dQe * e_b + dQa
        dK = dKW * W + dKa
        extra = _colsum(KW * dKW) + eE * _colsum(ST0 * dST1)
        trow = lax.broadcasted_iota(jnp.int32, (C, 1), 0)
        db = Q * dQ - K * dK + jnp.where(trow == (0 if rev else C - 1), extra, 0.0)
        dlogf = lax.dot_general(_hg_masks(not rev).astype(F32), db, (((1,), (0,)), ((), ())),
                                precision=HI, preferred_element_type=F32)
        dfv = dlogf / f - dK
        df_ref[...] = dfv * (1.0 - lbv) * sig * (1.0 - sig)
        dlb_ref[...] += _colsum(dfv * (1.0 - sig))
        dq_ref[...] = dQ * (sigq * (1.0 + qr * (1.0 - sigq)))
        dv_ref[...] = dV

    blk = lambda c0: pl.BlockSpec((C, HG_D), lambda h, n, c0=c0: (cidx(n), c0 + h))
    oblk = pl.BlockSpec((C, HG_D), lambda h, n: (cidx(n), h))
    return pl.pallas_call(
        body, name=name, grid=(H, NC),
        in_specs=[blk(0), blk(H), blk(fcol), pl.BlockSpec((1, HG_D), lambda h, n: (0, h)),
                  pl.BlockSpec((None, None, HG_D, HG_D), lambda h, n: (cidx(n), h, 0, 0)),
                  pl.BlockSpec((None, None, C, C), lambda h, n: (cidx(n), h, 0, 0)),
                  oblk],
        out_specs=[oblk, oblk, oblk, pl.BlockSpec((1, HG_D), lambda h, n: (0, h))],
        out_shape=[jax.ShapeDtypeStruct((S, H * HG_D), F32)] * 3 + [jax.ShapeDtypeStruct((1, H * HG_D), F32)],
        scratch_shapes=[pltpu.VMEM((HG_D, HG_D), F32)],
        compiler_params=_cparams(("parallel", "arbitrary")),
    )(P, P, P, lb, st, amat, do)


def _attn_fwd(q, k, v, *, S, T, H, dqk, dv, q_col0, k_col0, v_col0, scale, tq, tk, name):
    tq = _tile(S, tq, SUBLANES)
    tk = _tile(T, tk, LANES)
    nk = T // tk

    def body(q_ref, k_ref, v_ref, o_ref, lse_ref, m_s, l_s, acc):
        j = pl.program_id(2)

        @pl.when(j == 0)
        def _():
            m_s[...] = jnp.full_like(m_s, -jnp.inf)
            l_s[...] = jnp.zeros_like(l_s)
            acc[...] = jnp.zeros_like(acc)

        s = lax.dot_general(q_ref[...], k_ref[...], (((1,), (1,)), ((), ())),
                            preferred_element_type=F32) * scale
        m_new = jnp.maximum(m_s[...], jnp.max(s, axis=-1, keepdims=True))
        corr = jnp.exp(m_s[...] - m_new)
        p = jnp.exp(s - m_new)
        l_s[...] = corr * l_s[...] + jnp.sum(p, axis=-1, keepdims=True)
        acc[...] = corr * acc[...] + jnp.dot(p, v_ref[...], preferred_element_type=F32)
        m_s[...] = m_new

        @pl.when(j == nk - 1)
        def _():
            o_ref[...] = acc[...] / l_s[...]
            lse_ref[...] = m_s[...] + jnp.log(l_s[...])

    return pl.pallas_call(
        body, name=name, grid=(H, S // tq, nk),
        in_specs=[pl.BlockSpec((tq, dqk), lambda h, i, j: (i, q_col0 + h)),
                  pl.BlockSpec((tk, dqk), lambda h, i, j: (j, k_col0 + h)),
                  pl.BlockSpec((tk, dv), lambda h, i, j: (j, v_col0 + h))],
        out_specs=[pl.BlockSpec((tq, dv), lambda h, i, j: (i, h)),
                   pl.BlockSpec((None, tq, 1), lambda h, i, j: (h, i, 0))],
        out_shape=[jax.ShapeDtypeStruct((S, H * dv), F32), jax.ShapeDtypeStruct((H, S, 1), F32)],
        scratch_shapes=[pltpu.VMEM((tq, 1), F32), pltpu.VMEM((tq, 1), F32), pltpu.VMEM((tq, dv), F32)],
        compiler_params=_cparams(("parallel", "parallel", "arbitrary")),
    )(q, k, v)


def _attn_bwd(q, k, v, o, lse, do, *, S, T, H, dqk, dv, q_col0, k_col0, v_col0, scale, tq, tk, name):
    tq = _tile(S, tq, SUBLANES)
    tk = _tile(T, tk, LANES)
    nq = S // tq

    def body(q_ref, k_ref, v_ref, o_ref, lse_ref, do_ref, dq_ref, dk_ref, dv_ref, dk_acc, dv_acc):
        j = pl.program_id(1)
        i = pl.program_id(2)

        @pl.when(jnp.logical_and(i == 0, j == 0))
        def _():
            dq_ref[...] = jnp.zeros_like(dq_ref)

        @pl.when(i == 0)
        def _():
            dk_acc[...] = jnp.zeros_like(dk_acc)
            dv_acc[...] = jnp.zeros_like(dv_acc)

        qv, kv, vv, dov = q_ref[...], k_ref[...], v_ref[...], do_ref[...]
        s = lax.dot_general(qv, kv, (((1,), (1,)), ((), ())), preferred_element_type=F32) * scale
        p = jnp.exp(s - lse_ref[...])
        delta = jnp.sum(dov * o_ref[...], axis=-1, keepdims=True)
        dp = lax.dot_general(dov, vv, (((1,), (1,)), ((), ())), preferred_element_type=F32)
        ds = p * (dp - delta) * scale
        dv_acc[...] += lax.dot_general(p, dov, (((0,), (0,)), ((), ())), preferred_element_type=F32)
        dk_acc[...] += lax.dot_general(ds, qv, (((0,), (0,)), ((), ())), preferred_element_type=F32)
        rows = pl.ds(pl.multiple_of(i * tq, tq), tq)
        dq_ref[rows, :] += jnp.dot(ds, kv, preferred_element_type=F32)

        @pl.when(i == nq - 1)
        def _():
            dk_ref[...] = dk_acc[...]
            dv_ref[...] = dv_acc[...]

    return pl.pallas_call(
        body, name=name, grid=(H, T // tk, nq),
        in_specs=[pl.BlockSpec((tq, dqk), lambda h, j, i: (i, q_col0 + h)),
                  pl.BlockSpec((tk, dqk), lambda h, j, i: (j, k_col0 + h)),
                  pl.BlockSpec((tk, dv), lambda h, j, i: (j, v_col0 + h)),
                  pl.BlockSpec((tq, dv), lambda h, j, i: (i, h)),
                  pl.BlockSpec((None, tq, 1), lambda h, j, i: (h, i, 0)),
                  pl.BlockSpec((tq, dv), lambda h, j, i: (i, h))],
        out_specs=[pl.BlockSpec((S, dqk), lambda h, j, i: (0, h)),
                   pl.BlockSpec((tk, dqk), lambda h, j, i: (j, h)),
                   pl.BlockSpec((tk, dv), lambda h, j, i: (j, h))],
        out_shape=[jax.ShapeDtypeStruct((S, H * dqk), F32), jax.ShapeDtypeStruct((T, H * dqk), F32),
                   jax.ShapeDtypeStruct((T, H * dv), F32)],
        scratch_shapes=[pltpu.VMEM((tk, dqk), F32), pltpu.VMEM((tk, dv), F32)],
        compiler_params=_cparams(("parallel", "arbitrary", "arbitrary")),
    )(q, k, v, o, lse, do)


def _rope_tables(positions):
    half = MLA_ROPE // 2
    inv_freq = jnp.power(ROPE_THETA, -jnp.arange(half, dtype=F32) / half)
    ang = positions.astype(F32)[:, None] * inv_freq
    cos, sin = jnp.cos(ang), jnp.sin(ang)
    z = jnp.zeros_like(cos)
    tc = jnp.concatenate([cos, cos, z, z], axis=1)
    ta = jnp.concatenate([-sin, z, z, z], axis=1)
    tb = jnp.concatenate([z, sin, z, z], axis=1)
    return tc, ta, tb


def _rope_apply(v, tc, ta, tb):
    half = MLA_ROPE // 2
    return v * tc + pltpu.roll(v, LANES - half, 1) * ta + pltpu.roll(v, half, 1) * tb


def _rope_apply_t(d, tc, ta, tb):
    half = MLA_ROPE // 2
    return d * tc + pltpu.roll(d * ta, half, 1) + pltpu.roll(d * tb, LANES - half, 1)


def _local_step(x, mem, positions, loss_target, W, small):
    S, D = x.shape
    M = mem.shape[0]
    HW = HG_HEADS * HG_D
    QR = small["mla_g_cq"].shape[1]
    KR = small["mla_g_ckv"].shape[1]
    MW = W["mem_w_kv"].shape[1] // 2
    MHD = MW // MEM_HEADS
    DFF = W["w_ffn_gu"].shape[1] // 2
    QW = MLA_HEADS * 2 * LANES
    VW = MLA_HEADS * MLA_V
    c_hg, c_cq, c_ckv, c_qm, c_gate = 0, 5 * HW, 5 * HW + QR, 5 * HW + QR + KR, 5 * HW + QR + KR + MW
    c_kr = c_gate + N_BRANCH * D
    PW = c_kr + LANES
    assert W["w_in"].shape == (D, PW)
    TR = 256
    row = lambda a: a.reshape(1, -1)
    ge, be = row(small["ln_emb_g"]), row(small["ln_emb_b"])
    g1, b1, g2, b2 = small["ln1_g"], small["ln1_b"], small["ln2_g"], small["ln2_b"]
    lb = small["lb"]
    tc, ta, tb = _rope_tables(positions)

    (h0,) = _rowwise(lambda z, g, b: _ln_stats(z)[0] * g + b, [_Rows(x, D)], [ge, be],
                     [_Rows(D, D)], [], R=S, tr=TR, name="ln_emb")
    P = _mm(h0, W["w_in"], S, PW, D, tn=896, name="proj_in")

    o_fw, st_fw, a_fw = _hgrn_scan(P, lb[0:1], S=S, rev=False, name="hgrn_fw")
    o_bw, st_bw, a_bw = _hgrn_scan(P, lb[1:2], S=S, rev=True, name="hgrn_bw")

    def hg_post(of, ob, gr, ng):
        o = of + ob
        sg = _sigmoid(gr)
        outs = []
        for h in range(HG_HEADS):
            sl = slice(h * HG_D, (h + 1) * HG_D)
            outs.append(_rms_fwd(o[:, sl], ng, RMS_EPS) * sg[:, sl])
        return jnp.concatenate(outs, axis=1)

    (y_hg,) = _rowwise(hg_post, [_Rows(o_fw, HW), _Rows(o_bw, HW), _Rows(P, HW, 4)], [small["hgrn_norm_g"]],
                       [_Rows(HW, HW)], [], R=S, tr=TR, name="hgrn_post")

    def mla_norm(cq, ckv, gq, gk):
        return _rms_fwd(cq, gq, RMS_EPS), _rms_fwd(ckv, gk, RMS_EPS)

    assert c_cq % QR == 0 and c_ckv % KR == 0
    cqn, ckvn = _rowwise(mla_norm, [_Rows(P, QR, c_cq // QR), _Rows(P, KR, c_ckv // KR)],
                         [small["mla_g_cq"], small["mla_g_ckv"]],
                         [_Rows(QR, QR), _Rows(KR, KR)], [], R=S, tr=TR, name="mla_norm")
    q_raw = _mm(cqn, W["mla_w_uq"], S, QW, QR, name="mla_uq")
    kv = _mm(ckvn, W["mla_w_ukv"], S, 2 * VW, KR, name="mla_ukv")

    def rope_fwd(qb, knb, krb, tcb, tab, tbb):
        kr = _rope_apply(krb, tcb, tab, tbb)
        qo, ko = [], []
        for h in range(MLA_HEADS):
            qo += [qb[:, 2 * h * LANES:(2 * h + 1) * LANES],
                   _rope_apply(qb[:, (2 * h + 1) * LANES:(2 * h + 2) * LANES], tcb, tab, tbb)]
            ko += [knb[:, h * LANES:(h + 1) * LANES], kr]
        return jnp.concatenate(qo, axis=1), jnp.concatenate(ko, axis=1)

    qc, kc = _rowwise(rope_fwd, [_Rows(q_raw, QW), _Rows(kv, VW), _Rows(P, LANES, c_kr // LANES),
                                 _Rows(tc, LANES), _Rows(ta, LANES), _Rows(tb, LANES)], [],
                      [_Rows(QW, QW), _Rows(QW, QW)], [], R=S, tr=TR, name="rope_fwd")
    mla_kw = dict(S=S, T=S, H=MLA_HEADS, dqk=2 * LANES, dv=MLA_V, q_col0=0, k_col0=0, v_col0=MLA_HEADS,
                  scale=(MLA_NOPE + MLA_ROPE) ** -0.5, tq=512, tk=512)
    y_mla, lse_mla = _attn_fwd(qc, kc, kv, name="mla_attn", **mla_kw)

    kvm = _mm(mem, W["mem_w_kv"], M, 2 * MW, D, name="mem_kv")
    mem_kw = dict(S=S, T=M, H=MEM_HEADS, dqk=MHD, dv=MHD, q_col0=c_qm // MHD, k_col0=0, v_col0=MEM_HEADS,
                  scale=MHD ** -0.5, tq=512, tk=M)
    assert c_qm % MHD == 0
    y_mem, lse_mem = _attn_fwd(P, kvm, kvm, name="mem_attn", **mem_kw)

    ys = (y_hg, y_mla, y_mem)
    us = [_mm(ys[b], W["w_branch"][b], S, D, HW, name=f"branch{b}") for b in range(N_BRANCH)]
    TCW = _tile(D, 1024, LANES)
    ncw = D // TCW

    def merge_fwd(g0, g1_, g2_, u0, u1, u2):
        return _sigmoid(g0) * u0 + _sigmoid(g1_) * u1 + _sigmoid(g2_) * u2

    gate_rows = [_Rows(P, TCW, (c_gate + b * D) // TCW) for b in range(N_BRANCH)]
    assert c_gate % TCW == 0
    (merged,) = _rowwise(merge_fwd, gate_rows + [_Rows(u, TCW) for u in us], [],
                         [_Rows(D, TCW)], [], R=S, tr=TR, ncol=ncw, name="merge_fwd")
    mix = _mm(merged, W["w_o"], S, D, D, name="out_proj")

    def ln_res(hp, addv, g, b):
        z = ALPHA * hp + addv
        return z, _ln_stats(z)[0] * g + b

    z1, h1 = _rowwise(ln_res, [_Rows(h0, D), _Rows(mix, D)], [g1, b1],
                      [_Rows(D, D), _Rows(D, D)], [], R=S, tr=TR, name="ln1")

    ab = _mm(h1, W["w_ffn_gu"], S, 2 * DFF, D, name="ffn_gu")
    TF = _tile(DFF, 512, LANES)
    nf = DFF // TF

    def swiglu(abv):
        a, b = abv[:, :TF], abv[:, TF:]
        return a * _sigmoid(a) * b

    (cff,) = _rowwise(swiglu, [_Rows(ab, 2 * TF)], [], [_Rows(DFF, TF)], [],
                      R=S, tr=TR, ncol=nf, name="swiglu")
    ff = _mm(cff, W["w_ffn_down"], S, D, DFF, name="ffn_down")

    def loss_bwd(hp, addv, tgt, g, b):
        z = ALPHA * hp + addv
        xhat, rstd = _ln_stats(z)
        y = xhat * g + b
        err = y - tgt
        dy = err * (1.0 / D)
        dxh = dy * g
        m1 = jnp.mean(dxh, axis=-1, keepdims=True)
        m2 = jnp.mean(dxh * xhat, axis=-1, keepdims=True)
        dz = rstd * (dxh - m1 - xhat * m2)
        lrow = jnp.sum(_colsum(err * err), axis=-1, keepdims=True) * (0.5 / D)
        return dz, _colsum(dy * xhat), _colsum(dy), lrow

    dz2, dg2, db2, loss = _rowwise(loss_bwd, [_Rows(h1, D), _Rows(ff, D), _Rows(loss_target, D)], [g2, b2],
                                   [_Rows(D, D)], [D, D, 1], R=S, tr=TR, name="loss_ln2_bwd")
    dcff = _mm(dz2, W["w_ffn_down"], S, DFF, D, tb=True, name="ffn_down_dx")
    g_ffn_down = _mm(cff, dz2, DFF, D, S, ta=True, name="ffn_down_dw")

    def swiglu_bwd(abv, dc):
        a, b = abv[:, :TF], abv[:, TF:]
        sg = _sigmoid(a)
        return jnp.concatenate([dc * b * sg * (1.0 + a * (1.0 - sg)), dc * a * sg], axis=1)

    (dab,) = _rowwise(swiglu_bwd, [_Rows(ab, 2 * TF), _Rows(dcff, TF)], [], [_Rows(2 * DFF, 2 * TF)], [],
                      R=S, tr=TR, ncol=nf, name="swiglu_bwd")
    dh1 = _mm(dab, W["w_ffn_gu"], S, D, 2 * DFF, tb=True, name="ffn_gu_dx")
    g_ffn_gu = _mm(h1, dab, D, 2 * DFF, S, ta=True, name="ffn_gu_dw")

    def ln1_bwd(z, dmm, dz2v, g):
        return _ln_bwd_core(z, g, ALPHA * dz2v + dmm)

    dz1, dg1, db1 = _rowwise(ln1_bwd, [_Rows(z1, D), _Rows(dh1, D), _Rows(dz2, D)], [g1],
                             [_Rows(D, D)], [D, D], R=S, tr=TR, name="ln1_bwd")
    dmerged = _mm(dz1, W["w_o"], S, D, D, tb=True, name="out_proj_dx")
    g_w_o = _mm(merged, dz1, D, D, S, ta=True, name="out_proj_dw")

    def merge_bwd(g0, g1_, g2_, u0, u1, u2, dm):
        res_g, res_u = [], []
        for gv, uv in ((g0, u0), (g1_, u1), (g2_, u2)):
            sg = _sigmoid(gv)
            res_g.append(dm * uv * sg * (1.0 - sg))
            res_u.append(dm * sg)
        return (*res_g, *res_u)

    mres = _rowwise(merge_bwd, gate_rows + [_Rows(u, TCW) for u in us] + [_Rows(dmerged, TCW)], [],
                    [_Rows(D, TCW)] * (2 * N_BRANCH), [], R=S, tr=TR, ncol=ncw, name="merge_bwd")
    dgates, dus = mres[:N_BRANCH], mres[N_BRANCH:]
    dys = [_mm(dus[b], W["w_branch"][b], S, HW, D, tb=True, name=f"branch{b}_dx") for b in range(N_BRANCH)]
    g_w_branch = [_mm(ys[b], dus[b], HW, D, S, ta=True, name=f"branch{b}_dw") for b in range(N_BRANCH)]

    dq_mem, dk_mem, dv_mem = _attn_bwd(P, kvm, kvm, y_mem, lse_mem, dys[2], name="mem_attn_bwd", **mem_kw)
    dkvm = jnp.concatenate([dk_mem, dv_mem], axis=1)
    g_mem_w_kv = _mm(mem, dkvm, D, 2 * MW, M, ta=True, name="mem_kv_dw")

    dqc, dkc, dvv = _attn_bwd(qc, kc, kv, y_mla, lse_mla, dys[1], name="mla_attn_bwd", **mla_kw)

    def rope_bwd(dqb, dkb, tcb, tab, tbb):
        qo, kn = [], []
        dkr = jnp.zeros_like(tcb)
        for h in range(MLA_HEADS):
            qo += [dqb[:, 2 * h * LANES:(2 * h + 1) * LANES],
                   _rope_apply_t(dqb[:, (2 * h + 1) * LANES:(2 * h + 2) * LANES], tcb, tab, tbb)]
            kn.append(dkb[:, 2 * h * LANES:(2 * h + 1) * LANES])
            dkr = dkr + dkb[:, (2 * h + 1) * LANES:(2 * h + 2) * LANES]
        return jnp.concatenate(qo, axis=1), jnp.concatenate(kn, axis=1), _rope_apply_t(dkr, tcb, tab, tbb)

    dq_raw, dkn, dkr_raw = _rowwise(rope_bwd, [_Rows(dqc, QW), _Rows(dkc, QW), _Rows(tc, LANES),
                                               _Rows(ta, LANES), _Rows(tb, LANES)], [],
                                    [_Rows(QW, QW), _Rows(VW, VW), _Rows(LANES, LANES)], [],
                                    R=S, tr=TR, name="rope_bwd")
    dkv = jnp.concatenate([dkn, dvv], axis=1)
    dcqn = _mm(dq_raw, W["mla_w_uq"], S, QR, QW, tb=True, name="mla_uq_dx")
    g_mla_w_uq = _mm(cqn, dq_raw, QR, QW, S, ta=True, name="mla_uq_dw")
    dckvn = _mm(dkv, W["mla_w_ukv"], S, KR, 2 * VW, tb=True, name="mla_ukv_dx")
    g_mla_w_ukv = _mm(ckvn, dkv, KR, 2 * VW, S, ta=True, name="mla_ukv_dw")

    def mla_norm_bwd(cq, ckv, dq_, dk_, gq, gk):
        dcq, gq_rows = _rms_bwd(cq, gq, dq_, RMS_EPS)
        dck, gk_rows = _rms_bwd(ckv, gk, dk_, RMS_EPS)
        return dcq, dck, _colsum(gq_rows), _colsum(gk_rows)

    dcq, dckv, dg_cq, dg_ckv = _rowwise(
        mla_norm_bwd, [_Rows(P, QR, c_cq // QR), _Rows(P, KR, c_ckv // KR), _Rows(dcqn, QR), _Rows(dckvn, KR)],
        [small["mla_g_cq"], small["mla_g_ckv"]], [_Rows(QR, QR), _Rows(KR, KR)], [QR, KR],
        R=S, tr=TR, name="mla_norm_bwd")

    def hg_post_bwd(of, ob, gr, dy, ng):
        o = of + ob
        sg = _sigmoid(gr)
        do_, dgr = [], []
        dng = jnp.zeros((1, HG_D), F32)
        for h in range(HG_HEADS):
            sl = slice(h * HG_D, (h + 1) * HG_D)
            t = _rms_fwd(o[:, sl], ng, RMS_EPS)
            dgr.append(dy[:, sl] * t * sg[:, sl] * (1.0 - sg[:, sl]))
            dx, grow = _rms_bwd(o[:, sl], ng, dy[:, sl] * sg[:, sl], RMS_EPS)
            do_.append(dx)
            dng = dng + _colsum(grow)
        return jnp.concatenate(do_, axis=1), jnp.concatenate(dgr, axis=1), dng

    do_hg, dg_hg, dng = _rowwise(hg_post_bwd, [_Rows(o_fw, HW), _Rows(o_bw, HW), _Rows(P, HW, 4), _Rows(dys[0], HW)],
                                 [small["hgrn_norm_g"]], [_Rows(HW, HW), _Rows(HW, HW)], [HG_D],
                                 R=S, tr=TR, name="hgrn_post_bwd")
    dq_f, dv_f, dff_fw, dlb_f = _hgrn_scan_bwd(P, lb[0:1], st_fw, a_fw, do_hg, S=S, rev=False, name="hgrn_fw_bwd")
    dq_b, dv_b, dff_bw, dlb_b = _hgrn_scan_bwd(P, lb[1:2], st_bw, a_bw, do_hg, S=S, rev=True, name="hgrn_bw_bwd")
    THW = _tile(HW, 1024, LANES)
    dq_hg, dv_hg = _rowwise(lambda a, b, c, d: (a + b, c + d),
                            [_Rows(dq_f, THW), _Rows(dq_b, THW), _Rows(dv_f, THW), _Rows(dv_b, THW)], [],
                            [_Rows(HW, THW), _Rows(HW, THW)], [], R=S, tr=TR, ncol=HW // THW, name="hgrn_dir_sum")

    dP = jnp.concatenate([dq_hg, dv_hg, dff_fw, dff_bw, dg_hg, dcq, dckv, dq_mem, *dgates, dkr_raw], axis=1)
    dh0 = _mm(dP, W["w_in"], S, D, PW, tb=True, tk=_tile(PW, 640, LANES), name="proj_in_dx")
    g_w_in = _mm(h0, dP, D, PW, S, ta=True, tn=896, name="proj_in_dw")

    def ln0_bwd(z, dmm, dz1v, g):
        return _ln_bwd_core(z, g, ALPHA * dz1v + dmm)

    grad_x, dge, dbe = _rowwise(ln0_bwd, [_Rows(x, D), _Rows(dh0, D), _Rows(dz1, D)], [ge],
                                [_Rows(D, D)], [D, D], R=S, tr=TR, name="ln_emb_bwd")

    big = dict(w_in=g_w_in, mla_w_uq=g_mla_w_uq, mla_w_ukv=g_mla_w_ukv, mem_w_kv=g_mem_w_kv,
               w_branch=jnp.stack(g_w_branch), w_o=g_w_o, w_ffn_gu=g_ffn_gu, w_ffn_down=g_ffn_down)
    sm = dict(ln_emb_g=dge, ln_emb_b=dbe, dlb=jnp.concatenate([dlb_f, dlb_b], axis=0), hgrn_norm_g=dng,
              mla_g_cq=dg_cq, mla_g_ckv=dg_ckv, ln1_g=dg1, ln1_b=db1, ln2_g=dg2, ln2_b=db2)
    return loss, grad_x, big, sm


def _coords():
    return lax.axis_index("x"), lax.axis_index("y"), lax.axis_index("c")


def _exchange(src, n_dst, copies, local_copies, *, name):
    _, r, w = src.shape
    n_rc, n_lc = len(copies), len(local_copies)

    def body(src_ref, dst_ref, send_sems, recv_sems, local_sems):
        x, y, c = _coords()
        remote = []
        for k, (mask, sidx, didx) in enumerate(copies):
            peer = (x ^ mask[0], y ^ mask[1], c ^ mask[2])
            remote.append(pltpu.make_async_remote_copy(
                src_ref=src_ref.at[sidx(x, y, c)], dst_ref=dst_ref.at[didx(x, y, c)],
                send_sem=send_sems.at[k], recv_sem=recv_sems.at[k],
                device_id=peer, device_id_type=MESH))
        local = [pltpu.make_async_copy(src_ref.at[sidx(x, y, c)], dst_ref.at[didx(x, y, c)], local_sems.at[k])
                 for k, (sidx, didx) in enumerate(local_copies)]
        for cp in remote + local:
            cp.start()
        for cp in remote:
            cp.wait_recv()
        for cp in remote:
            cp.wait_send()
        for cp in local:
            cp.wait()

    return pl.pallas_call(
        body, name=name,
        in_specs=[pl.BlockSpec(memory_space=pl.ANY)],
        out_specs=pl.BlockSpec(memory_space=pl.ANY),
        out_shape=jax.ShapeDtypeStruct((n_dst, r, w), src.dtype),
        scratch_shapes=[pltpu.SemaphoreType.DMA((n_rc,)), pltpu.SemaphoreType.DMA((n_rc,)),
                        pltpu.SemaphoreType.DMA((max(n_lc, 1),))],
        compiler_params=pltpu.CompilerParams(has_side_effects=True),
    )(src)


_CHIP_MASKS = ((1, 0, 0), (0, 1, 0), (1, 1, 0))
_SIBLING = (0, 0, 1)


def _chip(x, y, c):
    return 2 * x + y


def _gather_chips(shard, *, name):
    copies = [(m, lambda x, y, c: 0, _chip) for m in _CHIP_MASKS]
    return _exchange(shard[None], 4, copies, [(lambda x, y, c: 0, _chip)], name=name)


def _sum_lead(arr, out_dtype, *, name):
    n, r, w = arr.shape
    tr = _tile(r, 512, 16)

    def body(a_ref, o_ref):
        acc = a_ref[0].astype(F32)
        for k in range(1, n):
            acc = acc + a_ref[k].astype(F32)
        o_ref[...] = acc.astype(out_dtype)

    return pl.pallas_call(
        body, name=name, grid=(r // tr,),
        in_specs=[pl.BlockSpec((n, tr, w), lambda i: (0, i, 0))],
        out_specs=pl.BlockSpec((tr, w), lambda i: (i, 0)),
        out_shape=jax.ShapeDtypeStruct((r, w), out_dtype),
        compiler_params=_cparams(("parallel",)),
    )(arr)


def _reduce_scatter(g):
    _, _, rh, w = g.shape
    pieces = g.reshape(8, rh, w)
    pair = _exchange(
        pieces, 8,
        [(_SIBLING, (lambda x, y, c, j=j: 2 * j + 1 - c), (lambda x, y, c, j=j: 4 + j)) for j in range(4)],
        [((lambda x, y, c, j=j: 2 * j + c), (lambda x, y, c, j=j: j)) for j in range(4)],
        name="rs_pair")
    part = _sum_lead(pair.reshape(2, 4 * rh, w), BF16, name="rs_pair_sum").reshape(4, rh, w)
    copies = [(m, (lambda x, y, c, m=m: _chip(x ^ m[0], y ^ m[1], c)), _chip) for m in _CHIP_MASKS]
    quad = _exchange(part, 4, copies, [(_chip, _chip)], name="rs_chips")
    mine = _sum_lead(quad, F32, name="rs_chip_sum")
    both = _exchange(mine[None], 2, [(_SIBLING, lambda x, y, c: 0, lambda x, y, c: c)],
                     [(lambda x, y, c: 0, lambda x, y, c: c)], name="rs_share")
    return both.reshape(2 * rh, w)


def _allreduce_small(v, *, name):
    r, w = v.shape

    def body(v_ref, o_ref, buf, send_sems, recv_sems):
        x, y, c = _coords()
        me = 4 * x + 2 * y + c
        buf[me] = v_ref[...]
        cps = []
        for k in range(7):
            m = ((k + 1) >> 2 & 1, (k + 1) >> 1 & 1, (k + 1) & 1)
            cp = pltpu.make_async_remote_copy(
                src_ref=v_ref, dst_ref=buf.at[me], send_sem=send_sems.at[k], recv_sem=recv_sems.at[k],
                device_id=(x ^ m[0], y ^ m[1], c ^ m[2]), device_id_type=MESH)
            cp.start()
            cps.append(cp)
        for cp in cps:
            cp.wait_recv()
        for cp in cps:
            cp.wait_send()
        acc = buf[0]
        for k in range(1, 8):
            acc = acc + buf[k]
        o_ref[...] = acc

    return pl.pallas_call(
        body, name=name,
        in_specs=[pl.BlockSpec(memory_space=pltpu.VMEM)],
        out_specs=pl.BlockSpec(memory_space=pltpu.VMEM),
        out_shape=jax.ShapeDtypeStruct((r, w), F32),
        scratch_shapes=[pltpu.VMEM((8, r, w), F32), pltpu.SemaphoreType.DMA((7,)), pltpu.SemaphoreType.DMA((7,))],
        compiler_params=pltpu.CompilerParams(has_side_effects=True),
    )(v)


_BIG = (("w_in", 1), ("mla_w_uq", 1), ("mla_w_ukv", 1), ("mem_w_kv", 0), ("w_branch", 2), ("w_o", 0),
        ("w_ffn_gate", 1), ("w_ffn_up", 1), ("w_ffn_down", 0))
_PACK_QUANTUM = 2 * 16 * PACK_W


def _pack(parts):
    flat = jnp.concatenate([p.reshape(-1) for p in parts])
    n = flat.shape[0]
    total = -(-n // _PACK_QUANTUM) * _PACK_QUANTUM
    return jnp.pad(flat, (0, total - n)).reshape(total // PACK_W, PACK_W)


def _unpack(rows, shapes):
    flat = rows.reshape(-1)
    out, off = [], 0
    for shp in shapes:
        n = math.prod(shp)
        out.append(flat[off:off + n].reshape(shp))
        off += n
    return out


def _pad_cols(a, n):
    return jnp.pad(a, ((0, 0), (0, n - a.shape[1])))


def _to_kernel_layout(full, D):
    w_in = full["w_in"]
    QR, KR = full["mla_w_uq"].shape[0], full["mla_w_ukv"].shape[0]
    HW = HG_HEADS * HG_D
    MW = full["mem_w_kv"].shape[1] // 2
    a = 5 * HW + QR + KR
    w_in_k = jnp.concatenate([w_in[:, :a], w_in[:, a + MLA_ROPE:], _pad_cols(w_in[:, a:a + MLA_ROPE], LANES)], axis=1)
    uq = full["mla_w_uq"].reshape(QR, MLA_HEADS, MLA_NOPE + MLA_ROPE)
    uq_k = jnp.pad(uq, ((0, 0), (0, 0), (0, 2 * LANES - MLA_NOPE - MLA_ROPE))).reshape(QR, MLA_HEADS * 2 * LANES)
    ukv = full["mla_w_ukv"].reshape(KR, MLA_HEADS, MLA_NOPE + MLA_V)
    ukv_k = jnp.concatenate([ukv[:, :, :MLA_NOPE].reshape(KR, -1), ukv[:, :, MLA_NOPE:].reshape(KR, -1)], axis=1)
    DFF = full["w_ffn_gate"].shape[1]
    TF = _tile(DFF, 512, LANES)
    gu = jnp.stack([full["w_ffn_gate"].reshape(D, DFF // TF, TF), full["w_ffn_up"].reshape(D, DFF // TF, TF)], axis=2)
    return dict(w_in=w_in_k, mla_w_uq=uq_k, mla_w_ukv=ukv_k, mem_w_kv=full["mem_w_kv"], w_branch=full["w_branch"],
                w_o=full["w_o"], w_ffn_gu=gu.reshape(D, 2 * DFF), w_ffn_down=full["w_ffn_down"])


def _from_kernel_layout(gk, D, QR, KR, MW):
    HW = HG_HEADS * HG_D
    a = 5 * HW + QR + KR
    g = gk["w_in"]
    rest = g.shape[1] - LANES - a
    w_in = jnp.concatenate([g[:, :a], g[:, a + rest:a + rest + MLA_ROPE], g[:, a:a + rest]], axis=1)
    uq = gk["mla_w_uq"].reshape(QR, MLA_HEADS, 2 * LANES)[:, :, :MLA_NOPE + MLA_ROPE].reshape(QR, -1)
    VW = MLA_HEADS * MLA_V
    ukv = jnp.concatenate([gk["mla_w_ukv"][:, :VW].reshape(KR, MLA_HEADS, MLA_NOPE),
                           gk["mla_w_ukv"][:, VW:].reshape(KR, MLA_HEADS, MLA_V)], axis=2).reshape(KR, -1)
    DFF = gk["w_ffn_gu"].shape[1] // 2
    TF = _tile(DFF, 512, LANES)
    gu = gk["w_ffn_gu"].reshape(D, DFF // TF, 2, TF)
    return dict(w_in=w_in, mla_w_uq=uq, mla_w_ukv=ukv, mem_w_kv=gk["mem_w_kv"], w_branch=gk["w_branch"],
                w_o=gk["w_o"], w_ffn_gate=gu[:, :, 0].reshape(D, DFF), w_ffn_up=gu[:, :, 1].reshape(D, DFF),
                w_ffn_down=gk["w_ffn_down"])


def _adamw(w, g, m, v, *, name):
    r, c = w.shape
    tr = max(SUBLANES, min(512, (1 << 20) // (4 * c)) // SUBLANES * SUBLANES)
    c1 = 1.0 / (1.0 - ADAM_B1 ** ADAM_STEP)
    c2 = 1.0 / (1.0 - ADAM_B2 ** ADAM_STEP)

    def fn(wv, gv, mv, vv):
        mn = ADAM_B1 * mv + (1.0 - ADAM_B1) * gv
        vn = ADAM_B2 * vv + (1.0 - ADAM_B2) * (gv * gv)
        delta = -ADAM_LR * ((mn * c1) / (jnp.sqrt(vn * c2) + ADAM_EPS) + ADAM_WD * wv)
        return delta, mn, vn

    return _rowwise(fn, [_Rows(a, c) for a in (w, g, m, v)], [], [_Rows(c, c)] * 3, [], R=r, tr=tr, name=name)


_SMALL = ("ln_emb_g", "ln_emb_b", "hgrn_lb_logits", "hgrn_norm_g", "mla_g_cq", "mla_g_ckv",
          "ln1_g", "ln1_b", "ln2_g", "ln2_b")


def _lb_from_logits(logits):
    return jnp.cumsum(jax.nn.softmax(logits, axis=1), axis=1)[:, 0]


def _small_rows(parts):
    flat = jnp.concatenate([p.reshape(-1) for p in parts])
    n = flat.shape[0]
    total = -(-n // (SUBLANES * LANES)) * SUBLANES * LANES
    return jnp.pad(flat, (0, total - n)).reshape(total // LANES, LANES)


def kernel(x, mem, positions, ln_emb_g, ln_emb_b, hgrn_lb_logits, w_in, hgrn_norm_g, mla_g_cq, mla_g_ckv, mla_w_uq, mla_w_ukv, mem_w_kv, w_branch, w_o, ln1_g, ln1_b, w_ffn_gate, w_ffn_up, w_ffn_down, ln2_g, ln2_b, loss_target, m_ln_emb_g, m_ln_emb_b, m_hgrn_lb_logits, m_w_in, m_hgrn_norm_g, m_mla_g_cq, m_mla_g_ckv, m_mla_w_uq, m_mla_w_ukv, m_mem_w_kv, m_w_branch, m_w_o, m_ln1_g, m_ln1_b, m_w_ffn_gate, m_w_ffn_up, m_w_ffn_down, m_ln2_g, m_ln2_b, v_ln_emb_g, v_ln_emb_b, v_hgrn_lb_logits, v_w_in, v_hgrn_norm_g, v_mla_g_cq, v_mla_g_ckv, v_mla_w_uq, v_mla_w_ukv, v_mem_w_kv, v_w_branch, v_w_o, v_ln1_g, v_ln1_b, v_w_ffn_gate, v_w_ffn_up, v_w_ffn_down, v_ln2_g, v_ln2_b):
    names = ["ln_emb_g", "ln_emb_b", "hgrn_lb_logits", "w_in", "hgrn_norm_g", "mla_g_cq", "mla_g_ckv", "mla_w_uq",
             "mla_w_ukv", "mem_w_kv", "w_branch", "w_o", "ln1_g", "ln1_b", "w_ffn_gate", "w_ffn_up", "w_ffn_down",
             "ln2_g", "ln2_b"]
    wts = dict(zip(names, [ln_emb_g, ln_emb_b, hgrn_lb_logits, w_in, hgrn_norm_g, mla_g_cq, mla_g_ckv, mla_w_uq,
                           mla_w_ukv, mem_w_kv, w_branch, w_o, ln1_g, ln1_b, w_ffn_gate, w_ffn_up, w_ffn_down,
                           ln2_g, ln2_b]))
    mom = dict(zip(names, [m_ln_emb_g, m_ln_emb_b, m_hgrn_lb_logits, m_w_in, m_hgrn_norm_g, m_mla_g_cq, m_mla_g_ckv,
                           m_mla_w_uq, m_mla_w_ukv, m_mem_w_kv, m_w_branch, m_w_o, m_ln1_g, m_ln1_b, m_w_ffn_gate,
                           m_w_ffn_up, m_w_ffn_down, m_ln2_g, m_ln2_b]))
    var = dict(zip(names, [v_ln_emb_g, v_ln_emb_b, v_hgrn_lb_logits, v_w_in, v_hgrn_norm_g, v_mla_g_cq, v_mla_g_ckv,
                           v_mla_w_uq, v_mla_w_ukv, v_mem_w_kv, v_w_branch, v_w_o, v_ln1_g, v_ln1_b, v_w_ffn_gate,
                           v_w_ffn_up, v_w_ffn_down, v_ln2_g, v_ln2_b]))
    xc, yc, cc = _coords()
    chip = _chip(xc, yc, cc)
    S, D = x.shape[1], x.shape[2]

    shards = [wts[n][0] for n, _ in _BIG]
    shard_shapes = [s.shape for s in shards]
    packed = _pack([s.astype(BF16) for s in shards])
    gathered = _gather_chips(packed, name="gather_weights")
    per_chip = [_unpack(gathered[j], shard_shapes) for j in range(4)]
    full = {n: jnp.concatenate([per_chip[j][k] for j in range(4)], axis=ax) for k, (n, ax) in enumerate(_BIG)}
    QR, KR = full["mla_w_uq"].shape[0], full["mla_w_ukv"].shape[0]
    MW = full["mem_w_kv"].shape[1] // 2
    Wk = _to_kernel_layout(full, D)

    lsh = hgrn_lb_logits.shape
    HW = 4 * lsh[2]
    placed = lax.dynamic_update_slice(jnp.zeros((lsh[0], lsh[1], HW), F32), hgrn_lb_logits, (0, 0, chip * lsh[2]))
    placed = jnp.where(cc == 0, placed, 0.0)
    logits = _allreduce_small(_small_rows([placed]), name="gather_logits").reshape(-1)[:placed.size].reshape(placed.shape)
    lb, lb_vjp = jax.vjp(_lb_from_logits, logits)

    small = dict(ln_emb_g=ln_emb_g, ln_emb_b=ln_emb_b, lb=lb, hgrn_norm_g=hgrn_norm_g, mla_g_cq=mla_g_cq,
                 mla_g_ckv=mla_g_ckv, ln1_g=ln1_g, ln1_b=ln1_b, ln2_g=ln2_g, ln2_b=ln2_b)
    loss_l, grad_x, gk, gs = _local_step(x[0], mem[0], positions[0], loss_target[0], Wk, small)

    (dlogits,) = lb_vjp(gs["dlb"])
    sm_parts = [loss_l, gs["ln_emb_g"], gs["ln_emb_b"], dlogits, gs["hgrn_norm_g"], gs["mla_g_cq"], gs["mla_g_ckv"],
                gs["ln1_g"], gs["ln1_b"], gs["ln2_g"], gs["ln2_b"]]
    red = _allreduce_small(_small_rows(sm_parts), name="allreduce_small").reshape(-1)
    sm_out, off = [], 0
    for p in sm_parts:
        sm_out.append(red[off:off + p.size].reshape(p.shape))
        off += p.size
    loss = sm_out[0].reshape(())
    g_small = dict(zip(_SMALL, sm_out[1:]))
    g_small["hgrn_lb_logits"] = lax.dynamic_slice(g_small["hgrn_lb_logits"], (0, 0, chip * lsh[2]), lsh)
    for n in _SMALL:
        g_small[n] = g_small[n].reshape(wts[n].shape)

    gfull = _from_kernel_layout(gk, D, QR, KR, MW)
    pieces = []
    for j in range(4):
        parts = []
        for (n, ax), shp in zip(_BIG, shard_shapes):
            parts.append(lax.slice_in_dim(gfull[n], j * shp[ax], (j + 1) * shp[ax], axis=ax))
        pieces.append(_pack(parts))
    gp = jnp.stack(pieces)
    R = gp.shape[1]
    reduced = _reduce_scatter(gp.reshape(4, 2, R // 2, PACK_W))
    g_big = dict(zip([n for n, _ in _BIG], [g[None] for g in _unpack(reduced, shard_shapes)]))

    grads = {**g_small, **g_big}
    delta, new_m, new_v = {}, {}, {}
    for n, _ in _BIG:
        shp = wts[n].shape
        two_d = lambda a: a.reshape(-1, shp[-1])
        d_, m_, v_ = _adamw(two_d(wts[n]), two_d(grads[n]), two_d(mom[n]), two_d(var[n]), name="adamw_" + n)
        delta[n], new_m[n], new_v[n] = d_.reshape(shp), m_.reshape(shp), v_.reshape(shp)
    sw, sg_, sm_, sv_ = (_small_rows([d[n] for n in _SMALL]) for d in (wts, grads, mom, var))
    d_, m_, v_ = _adamw(sw, sg_, sm_, sv_, name="adamw_small")
    for res, packed_rows in ((delta, d_), (new_m, m_), (new_v, v_)):
        flat, off = packed_rows.reshape(-1), 0
        for n in _SMALL:
            res[n] = flat[off:off + wts[n].size].reshape(wts[n].shape)
            off += wts[n].size

    return (loss, grad_x[None], *[grads[n] for n in names], *[delta[n] for n in names],
            *[new_m[n] for n in names], *[new_v[n] for n in names])
```

```python
import jax
import jax.numpy as jnp
from jax import lax
from jax.experimental import pallas as pl
from jax.experimental.pallas import tpu as pltpu

F32 = jnp.float32
BF16 = jnp.bfloat16

HG_HEADS = 8
HG_D = 128
MLA_HEADS = 8
MLA_NOPE = 128
MLA_ROPE = 64
MLA_V = 128
MEM_HEADS = 4
N_BRANCH = 3
ROPE_THETA = 10000.0
DEPTH = 1
ALPHA = (2.0 * DEPTH) ** 0.25
LN_EPS = 1e-5
RMS_EPS = 1e-6
ADAM_LR = 0.001
ADAM_B1 = 0.9
ADAM_B2 = 0.999
ADAM_EPS = 1e-08
ADAM_WD = 0.01
ADAM_STEP = 10

LANES = 128
SUBLANES = 8
VMEM_LIMIT = 48 * 1024 * 1024

HG_CHUNK = 128
HG_SUB = 16

MESH = pl.DeviceIdType.MESH
HI = lax.Precision.HIGHEST


def _cparams(sem=None):
    if sem is None:
        return pltpu.CompilerParams(vmem_limit_bytes=VMEM_LIMIT)
    return pltpu.CompilerParams(dimension_semantics=sem, vmem_limit_bytes=VMEM_LIMIT)


def _tile(dim, pref, quantum):
    t = min(pref, dim) // quantum * quantum
    while t >= quantum:
        if dim % t == 0:
            return t
        t -= quantum
    return dim


def _sigmoid(x):
    return 1.0 / (1.0 + jnp.exp(-x))


def _mm(a, b, M, N, K, *, ta=False, tb=False, a_off=(0, 0), b_off=(0, 0), add=None,
        tm=1024, tn=1024, tk=1024, name):
    tm = _tile(M, tm, LANES if ta else SUBLANES)
    tn = _tile(N, tn, LANES)
    tk = _tile(K, tk, LANES)
    nk = K // tk
    ar, ac = a_off
    br, bc = b_off

    if ta:
        assert ar % tk == 0 and ac % tm == 0
        a_spec = pl.BlockSpec((tk, tm), lambda i, j, k: (ar // tk + k, ac // tm + i))
    else:
        assert ar % tm == 0 and ac % tk == 0
        a_spec = pl.BlockSpec((tm, tk), lambda i, j, k: (ar // tm + i, ac // tk + k))
    if tb:
        assert br % tn == 0 and bc % tk == 0
        b_spec = pl.BlockSpec((tn, tk), lambda i, j, k: (br // tn + j, bc // tk + k))
    else:
        assert br % tk == 0 and bc % tn == 0
        b_spec = pl.BlockSpec((tk, tn), lambda i, j, k: (br // tk + k, bc // tn + j))
    o_spec = pl.BlockSpec((tm, tn), lambda i, j, k: (i, j))
    mixed = a.dtype != b.dtype

    def body(*refs):
        if add is None:
            a_ref, b_ref, o_ref, acc = refs
        else:
            a_ref, b_ref, add_ref, o_ref, acc = refs
        k = pl.program_id(2)
        av = a_ref[...]
        bv = b_ref[...]
        if ta:
            av = av.astype(F32).T
        if mixed:
            av = av.astype(BF16)
            bv = bv.astype(BF16)
        dims = (((1,), (1 if tb else 0,)), ((), ()))
        d = lax.dot_general(av, bv, dims, preferred_element_type=F32)

        def finish(total):
            o_ref[...] = total if add is None else total + add_ref[...]

        if nk == 1:
            finish(d)
        else:
            @pl.when(k == 0)
            def _():
                acc[...] = d

            @pl.when(jnp.logical_and(k > 0, k < nk - 1))
            def _():
                acc[...] += d

            @pl.when(k == nk - 1)
            def _():
                finish(acc[...] + d)

    in_specs = [a_spec, b_spec]
    args = [a, b]
    if add is not None:
        in_specs.append(o_spec)
        args.append(add)
    return pl.pallas_call(
        body, name=name, grid=(M // tm, N // tn, nk),
        in_specs=in_specs, out_specs=o_spec,
        out_shape=jax.ShapeDtypeStruct((M, N), F32),
        scratch_shapes=[pltpu.VMEM((tm, tn), F32)],
        compiler_params=_cparams(("parallel", "parallel", "arbitrary")),
    )(*args)


class _Rows:
    def __init__(self, arr, width, col0=0, lead=None, dtype=F32):
        self.arr, self.width, self.col0, self.lead, self.dtype = arr, width, col0, lead, dtype


def _rowwise(fn, rows, consts, outs, accs, *, R, tr, ncol=1, name):
    tr = _tile(R, tr, SUBLANES)
    nrow = R // tr

    def spec(r):
        if r.lead is None:
            return pl.BlockSpec((tr, r.width), lambda j, i, c0=r.col0: (i, c0 + j))
        return pl.BlockSpec((None, tr, r.width), lambda j, i, c0=r.col0, l=r.lead: (l, i, c0 + j))

    in_specs = [spec(r) for r in rows]
    for c in consts:
        in_specs.append(pl.BlockSpec(c.shape, lambda j, i, nd=c.ndim: (0,) * nd))
    out_specs = [spec(o) for o in outs]
    out_shape = [jax.ShapeDtypeStruct((R, o.arr), o.dtype) for o in outs]
    for w in accs:
        out_specs.append(pl.BlockSpec((1, w), lambda j, i: (0, j)))
        out_shape.append(jax.ShapeDtypeStruct((1, w * ncol), F32))
    n_in = len(rows) + len(consts)
    n_out = len(outs)

    def body(*refs):
        ins = [r[...] for r in refs[:n_in]]
        res = fn(*ins)
        if not isinstance(res, (tuple, list)):
            res = (res,)
        for k in range(n_out):
            refs[n_in + k][...] = res[k].astype(refs[n_in + k].dtype)
        i = pl.program_id(1)
        for k in range(len(accs)):
            a_ref = refs[n_in + n_out + k]

            @pl.when(i == 0)
            def _(a_ref=a_ref):
                a_ref[...] = jnp.zeros_like(a_ref)

            a_ref[...] += res[n_out + k]

    res = pl.pallas_call(
        body, name=name, grid=(ncol, nrow),
        in_specs=in_specs, out_specs=out_specs, out_shape=out_shape,
        compiler_params=_cparams(("parallel", "arbitrary")),
    )(*[r.arr for r in rows], *consts)
    return res


def _colsum(x):
    return jnp.sum(x, axis=0, keepdims=True)


def _ln_stats(z):
    mu = jnp.mean(z, axis=-1, keepdims=True)
    zc = z - mu
    var = jnp.mean(zc * zc, axis=-1, keepdims=True)
    rstd = lax.rsqrt(var + LN_EPS)
    return zc * rstd, rstd


def _ln_bwd_core(z, g, dy):
    xhat, rstd = _ln_stats(z)
    dxh = dy * g
    m1 = jnp.mean(dxh, axis=-1, keepdims=True)
    m2 = jnp.mean(dxh * xhat, axis=-1, keepdims=True)
    dz = rstd * (dxh - m1 - xhat * m2)
    return dz, _colsum(dy * xhat), _colsum(dy)


def _rms_fwd(x, g, eps):
    r = lax.rsqrt(jnp.mean(x * x, axis=-1, keepdims=True) + eps)
    return x * r * g


def _rms_bwd(x, g, dy, eps):
    r = lax.rsqrt(jnp.mean(x * x, axis=-1, keepdims=True) + eps)
    xr = x * r
    dyg = dy * g
    dx = r * (dyg - xr * jnp.mean(dyg * xr, axis=-1, keepdims=True))
    return dx, dy * xr


def _hg_gate(fr, lb):
    sig = _sigmoid(fr)
    f = lb + (1.0 - lb) * sig
    return sig, f


def _hg_masks(rev):
    C = HG_CHUNK
    t = lax.broadcasted_iota(jnp.int32, (C, C), 0)
    s = lax.broadcasted_iota(jnp.int32, (C, C), 1)
    tri = (s >= t) if rev else (s <= t)
    return tri


def _hg_offdiag(Q, K, b, i, rev):
    C, sb = HG_CHUNK, HG_SUB
    nb = C // sb
    if (not rev and i == 0) or (rev and i == nb - 1):
        return None
    ref = b[sb * i - 1:sb * i] if not rev else b[sb * (i + 1):sb * (i + 1) + 1]
    srow = lax.broadcasted_iota(jnp.int32, (C, 1), 0)
    smask = (srow < sb * i) if not rev else (srow >= sb * (i + 1))
    qscale = jnp.exp(jnp.minimum(b - ref, 0.0))
    kscale = jnp.where(smask, jnp.exp(jnp.minimum(ref - b, 0.0)), 0.0)
    return qscale, kscale


def _hg_att(Q, K, b, rev):
    C, sb = HG_CHUNK, HG_SUB
    lane = lax.broadcasted_iota(jnp.int32, (sb, C), 1)
    rloc = lax.broadcasted_iota(jnp.int32, (sb, 1), 0)
    rows = []
    for i in range(C // sb):
        sl = slice(sb * i, sb * i + sb)
        Qi, Ki, bi = Q[sl], K[sl], b[sl]
        od = _hg_offdiag(Q, K, b, i, rev)
        if od is None:
            acc = jnp.zeros((sb, C), F32)
        else:
            qs, ks = od
            acc = lax.dot_general(Qi * qs[sl], K * ks, (((1,), (1,)), ((), ())),
                                  precision=HI, preferred_element_type=F32)
        for j in range(sb):
            e = jnp.exp(jnp.minimum(bi - bi[j:j + 1], 0.0))
            col = jnp.sum(Qi * Ki[j:j + 1] * e, axis=-1, keepdims=True)
            vis = (rloc <= j) if rev else (rloc >= j)
            acc = jnp.where(lane == sb * i + j, jnp.where(vis, col, 0.0), acc)
        rows.append(acc)
    return jnp.concatenate(rows, axis=0)


def _hg_att_bwd(Q, K, b, dA, rev):
    C, sb = HG_CHUNK, HG_SUB
    rloc = lax.broadcasted_iota(jnp.int32, (sb, 1), 0)
    rrow = lax.broadcasted_iota(jnp.int32, (sb, HG_D), 0)
    trow = lax.broadcasted_iota(jnp.int32, (C, C), 1) // sb
    dAT = dA.T
    dQ_rows, dKd_rows = [], []
    dK = jnp.zeros((C, HG_D), F32)
    for i in range(C // sb):
        sl = slice(sb * i, sb * i + sb)
        Qi, Ki, bi, dAi = Q[sl], K[sl], b[sl], dA[sl]
        od = _hg_offdiag(Q, K, b, i, rev)
        if od is None:
            dQi = jnp.zeros((sb, HG_D), F32)
        else:
            qs, ks = od
            dQi = lax.dot_general(dAi, K * ks, (((1,), (0,)), ((), ())),
                                  precision=HI, preferred_element_type=F32) * qs[sl]
            zt = jnp.where(trow == i, dAT, 0.0)
            dK = dK + lax.dot_general(zt, Q * qs, (((1,), (0,)), ((), ())),
                                      precision=HI, preferred_element_type=F32) * ks
        dKd = jnp.zeros((sb, HG_D), F32)
        for j in range(sb):
            vis = (rloc <= j) if rev else (rloc >= j)
            e = jnp.where(vis, jnp.exp(jnp.minimum(bi - bi[j:j + 1], 0.0)), 0.0)
            dcol = dAi[:, sb * i + j:sb * i + j + 1]
            dQi = dQi + dcol * Ki[j:j + 1] * e
            krow = jnp.sum(dcol * Qi * e, axis=0, keepdims=True)
            dKd = jnp.where(rrow == j, krow, dKd)
        dQ_rows.append(dQi)
        dKd_rows.append(dKd)
    return jnp.concatenate(dQ_rows, axis=0), dK + jnp.concatenate(dKd_rows, axis=0)


def _hg_prep(qr, fr, lb, tri):
    sigq = _sigmoid(qr)
    Q = qr * sigq
    sig, f = _hg_gate(fr, lb)
    K = 1.0 - f
    logf = jnp.log(f)
    b = lax.dot_general(tri.astype(F32), logf, (((1,), (0,)), ((), ())),
                        precision=HI, preferred_element_type=F32)
    return sigq, Q, sig, f, K, b


def _hgrn_scan(P, lb, *, S, rev, name):
    C = HG_CHUNK
    H = HG_HEADS
    NC = S // C
    fcol = (3 if rev else 2) * H

    def cidx(n):
        return NC - 1 - n if rev else n

    def body(q_ref, v_ref, f_ref, lb_ref, o_ref, st_ref, a_ref, state):
        n = pl.program_id(1)

        @pl.when(n == 0)
        def _():
            state[...] = jnp.zeros_like(state)

        tri = _hg_masks(rev)
        _, Q, _, _, K, b = _hg_prep(q_ref[...], f_ref[...], lb_ref[...], tri)
        V = v_ref[...]
        ST0 = state[...]
        st_ref[...] = ST0
        e_b = jnp.exp(b)
        bE = b[0:1] if rev else b[C - 1:C]
        W = jnp.exp(bE - b)
        inter = lax.dot_general(Q * e_b, ST0, (((1,), (1,)), ((), ())), preferred_element_type=F32)
        A = _hg_att(Q, K, b, rev)
        a_ref[...] = A
        o_ref[...] = inter + jnp.dot(A, V, preferred_element_type=F32)
        state[...] = ST0 * jnp.exp(bE) + lax.dot_general(
            V, K * W, (((0,), (0,)), ((), ())), preferred_element_type=F32)

    blk = lambda c0: pl.BlockSpec((C, HG_D), lambda h, n, c0=c0: (cidx(n), c0 + h))
    return pl.pallas_call(
        body, name=name, grid=(H, NC),
        in_specs=[blk(0), blk(H), blk(fcol), pl.BlockSpec((1, HG_D), lambda h, n: (0, h))],
        out_specs=[pl.BlockSpec((C, HG_D), lambda h, n: (cidx(n), h)),
                   pl.BlockSpec((None, None, HG_D, HG_D), lambda h, n: (cidx(n), h, 0, 0)),
                   pl.BlockSpec((None, None, C, C), lambda h, n: (cidx(n), h, 0, 0))],
        out_shape=[jax.ShapeDtypeStruct((S, H * HG_D), F32),
                   jax.ShapeDtypeStruct((NC, H, HG_D, HG_D), F32),
                   jax.ShapeDtypeStruct((NC, H, C, C), F32)],
        scratch_shapes=[pltpu.VMEM((HG_D, HG_D), F32)],
        compiler_params=_cparams(("parallel", "arbitrary")),
    )(P, P, P, lb)


def _hgrn_scan_bwd(P, lb, st, amat, do, *, S, rev, name):
    C = HG_CHUNK
    H = HG_HEADS
    NC = S // C
    fcol = (3 if rev else 2) * H

    def cidx(n):
        return n if rev else NC - 1 - n

    def body(q_ref, v_ref, f_ref, lb_ref, st_ref, a_ref, do_ref, dq_ref, dv_ref, df_ref, dlb_ref, dstate):
        n = pl.program_id(1)

        @pl.when(n == 0)
        def _():
            dstate[...] = jnp.zeros_like(dstate)
            dlb_ref[...] = jnp.zeros_like(dlb_ref)

        tri = _hg_masks(rev)
        lbv = lb_ref[...]
        qr = q_ref[...]
        sigq, Q, sig, f, K, b = _hg_prep(qr, f_ref[...], lbv, tri)
        V = v_ref[...]
        ST0 = st_ref[...]
        A = a_ref[...]
        dO = do_ref[...]
        dST1 = dstate[...]
        e_b = jnp.exp(b)
        bE = b[0:1] if rev else b[C - 1:C]
        eE = jnp.exp(bE)
        W = jnp.exp(bE - b)
        Qe = Q * e_b
        KW = K * W
        dA = jnp.where(tri, lax.dot_general(dO, V, (((1,), (1,)), ((), ())), preferred_element_type=F32), 0.0)
        dV = (lax.dot_general(A, dO, (((0,), (0,)), ((), ())), preferred_element_type=F32)
              + lax.dot_general(KW, dST1, (((1,), (1,)), ((), ())), preferred_element_type=F32))
        dQe = jnp.dot(dO, ST0, preferred_element_type=F32)
        dKW = jnp.dot(V, dST1, preferred_element_type=F32)
        dstate[...] = dST1 * eE + lax.dot_general(dO, Qe, (((0,), (0,)), ((), ())), preferred_element_type=F32)
        dQa, dKa = _hg_att_bwd(Q, K, b, dA, rev)
        dQ = dQe * e_b + dQa
        dK = dKW * W + dKa
        extra = _colsum(KW * dKW) + eE * _colsum(ST0 * dST1)
        trow = lax.broadcasted_iota(jnp.int32, (C, 1), 0)
        db = Q * dQ - K * dK + jnp.where(trow == (0 if rev else C - 1), extra, 0.0)
        dlogf = lax.dot_general(_hg_masks(not rev).astype(F32), db, (((1,), (0,)), ((), ())),
                                precision=HI, preferred_element_type=F32)
        dfv = dlogf / f - dK
        df_ref[...] = dfv * (1.0 - lbv) * sig * (1.0 - sig)
        dlb_ref[...] += _colsum(dfv * (1.0 - sig))
        dq_ref[...] = dQ * (sigq * (1.0 + qr * (1.0 - sigq)))
        dv_ref[...] = dV

    blk = lambda c0: pl.BlockSpec((C, HG_D), lambda h, n, c0=c0: (cidx(n), c0 + h))
    oblk = pl.BlockSpec((C, HG_D), lambda h, n: (cidx(n), h))
    return pl.pallas_call(
        body, name=name, grid=(H, NC),
        in_specs=[blk(0), blk(H), blk(fcol), pl.BlockSpec((1, HG_D), lambda h, n: (0, h)),
                  pl.BlockSpec((None, None, HG_D, HG_D), lambda h, n: (cidx(n), h, 0, 0)),
                  pl.BlockSpec((None, None, C, C), lambda h, n: (cidx(n), h, 0, 0)),
                  oblk],
        out_specs=[oblk, oblk, oblk, pl.BlockSpec((1, HG_D), lambda h, n: (0, h))],
        out_shape=[jax.ShapeDtypeStruct((S, H * HG_D), F32)] * 3 + [jax.ShapeDtypeStruct((1, H * HG_D), F32)],
        scratch_shapes=[pltpu.VMEM((HG_D, HG_D), F32)],
        compiler_params=_cparams(("parallel", "arbitrary")),
    )(P, P, P, lb, st, amat, do)


MXU = BF16
ATT_SUB = 512


def _mx(x):
    return x if x.dtype == MXU else x.astype(MXU)


def _attn_fwd(q, k, v, *, S, T, H, dqk, dv, q_col0, k_col0, v_col0, scale, tq, tk, name):
    tq = _tile(S, tq, SUBLANES)
    tk = _tile(T, tk, LANES)
    nk = T // tk
    ts = _tile(tq, ATT_SUB, SUBLANES)

    def body(q_ref, k_ref, v_ref, o_ref, lse_ref, m_s, l_s, acc):
        j = pl.program_id(2)

        @pl.when(j == 0)
        def _():
            m_s[...] = jnp.full_like(m_s, -jnp.inf)
            l_s[...] = jnp.zeros_like(l_s)
            acc[...] = jnp.zeros_like(acc)

        kv, vv = _mx(k_ref[...]), _mx(v_ref[...])
        for r0 in range(0, tq, ts):
            rows = slice(r0, r0 + ts)
            s = lax.dot_general(_mx(q_ref[rows, :]), kv, (((1,), (1,)), ((), ())),
                                preferred_element_type=F32) * scale
            m_old = m_s[rows, :]
            m_new = jnp.maximum(m_old, jnp.max(s, axis=-1, keepdims=True))
            corr = jnp.exp(m_old - m_new)
            p = jnp.exp(s - m_new)
            l_s[rows, :] = corr * l_s[rows, :] + jnp.sum(p, axis=-1, keepdims=True)
            acc[rows, :] = corr * acc[rows, :] + jnp.dot(_mx(p), vv, preferred_element_type=F32)
            m_s[rows, :] = m_new

        @pl.when(j == nk - 1)
        def _():
            o_ref[...] = acc[...] / l_s[...]
            lse_ref[...] = m_s[...] + jnp.log(l_s[...])

    return pl.pallas_call(
        body, name=name, grid=(H, S // tq, nk),
        in_specs=[pl.BlockSpec((tq, dqk), lambda h, i, j: (i, q_col0 + h)),
                  pl.BlockSpec((tk, dqk), lambda h, i, j: (j, k_col0 + h)),
                  pl.BlockSpec((tk, dv), lambda h, i, j: (j, v_col0 + h))],
        out_specs=[pl.BlockSpec((tq, dv), lambda h, i, j: (i, h)),
                   pl.BlockSpec((None, tq, 1), lambda h, i, j: (h, i, 0))],
        out_shape=[jax.ShapeDtypeStruct((S, H * dv), F32), jax.ShapeDtypeStruct((H, S, 1), F32)],
        scratch_shapes=[pltpu.VMEM((tq, 1), F32), pltpu.VMEM((tq, 1), F32), pltpu.VMEM((tq, dv), F32)],
        compiler_params=_cparams(("parallel", "parallel", "arbitrary")),
    )(q, k, v)


def _attn_bwd(q, k, v, o, lse, do, *, S, T, H, dqk, dv, q_col0, k_col0, v_col0, scale, tq, tk, name):
    tq = _tile(S, tq, SUBLANES)
    tk = _tile(T, tk, LANES)
    nq = S // tq
    ts = _tile(tq, ATT_SUB, SUBLANES)

    def body(q_ref, k_ref, v_ref, o_ref, lse_ref, do_ref, dq_ref, dk_ref, dv_ref, dk_acc, dv_acc):
        j = pl.program_id(1)
        i = pl.program_id(2)

        @pl.when(jnp.logical_and(i == 0, j == 0))
        def _():
            dq_ref[...] = jnp.zeros_like(dq_ref)

        @pl.when(i == 0)
        def _():
            dk_acc[...] = jnp.zeros_like(dk_acc)
            dv_acc[...] = jnp.zeros_like(dv_acc)

        kv, vv = _mx(k_ref[...]), _mx(v_ref[...])
        dk_new, dv_new = dk_acc[...], dv_acc[...]
        for r0 in range(0, tq, ts):
            rows = slice(r0, r0 + ts)
            qv, dov = _mx(q_ref[rows, :]), do_ref[rows, :]
            s = lax.dot_general(qv, kv, (((1,), (1,)), ((), ())), preferred_element_type=F32) * scale
            p = jnp.exp(s - lse_ref[rows, :])
            delta = jnp.sum(dov * o_ref[rows, :], axis=-1, keepdims=True)
            dob = _mx(dov)
            dp = lax.dot_general(dob, vv, (((1,), (1,)), ((), ())), preferred_element_type=F32)
            ds = _mx(p * (dp - delta) * scale)
            dv_new = dv_new + lax.dot_general(_mx(p), dob, (((0,), (0,)), ((), ())), preferred_element_type=F32)
            dk_new = dk_new + lax.dot_general(ds, qv, (((0,), (0,)), ((), ())), preferred_element_type=F32)
            drows = pl.ds(pl.multiple_of(i * tq + r0, ts), ts)
            dq_ref[drows, :] += jnp.dot(ds, kv, preferred_element_type=F32)
        dk_acc[...] = dk_new
        dv_acc[...] = dv_new

        @pl.when(i == nq - 1)
        def _():
            dk_ref[...] = dk_new
            dv_ref[...] = dv_new

    return pl.pallas_call(
        body, name=name, grid=(H, T // tk, nq),
        in_specs=[pl.BlockSpec((tq, dqk), lambda h, j, i: (i, q_col0 + h)),
                  pl.BlockSpec((tk, dqk), lambda h, j, i: (j, k_col0 + h)),
                  pl.BlockSpec((tk, dv), lambda h, j, i: (j, v_col0 + h)),
                  pl.BlockSpec((tq, dv), lambda h, j, i: (i, h)),
                  pl.BlockSpec((None, tq, 1), lambda h, j, i: (h, i, 0)),
                  pl.BlockSpec((tq, dv), lambda h, j, i: (i, h))],
        out_specs=[pl.BlockSpec((S, dqk), lambda h, j, i: (0, h)),
                   pl.BlockSpec((tk, dqk), lambda h, j, i: (j, h)),
                   pl.BlockSpec((tk, dv), lambda h, j, i: (j, h))],
        out_shape=[jax.ShapeDtypeStruct((S, H * dqk), F32), jax.ShapeDtypeStruct((T, H * dqk), F32),
                   jax.ShapeDtypeStruct((T, H * dv), F32)],
        scratch_shapes=[pltpu.VMEM((tk, dqk), F32), pltpu.VMEM((tk, dv), F32)],
        compiler_params=_cparams(("parallel", "arbitrary", "arbitrary")),
    )(q, k, v, o, lse, do)


def _rope_tables(positions):
    half = MLA_ROPE // 2
    inv_freq = jnp.power(ROPE_THETA, -jnp.arange(half, dtype=F32) / half)
    ang = positions.astype(F32)[:, None] * inv_freq
    cos, sin = jnp.cos(ang), jnp.sin(ang)
    z = jnp.zeros_like(cos)
    tc = jnp.concatenate([cos, cos, z, z], axis=1)
    ta = jnp.concatenate([-sin, z, z, z], axis=1)
    tb = jnp.concatenate([z, sin, z, z], axis=1)
    return tc, ta, tb


def _rope_apply(v, tc, ta, tb):
    half = MLA_ROPE // 2
    return v * tc + pltpu.roll(v, LANES - half, 1) * ta + pltpu.roll(v, half, 1) * tb


def _rope_apply_t(d, tc, ta, tb):
    half = MLA_ROPE // 2
    return d * tc + pltpu.roll(d * ta, half, 1) + pltpu.roll(d * tb, LANES - half, 1)


def _local_step(x, mem, positions, loss_target, W, small):
    S, D = x.shape
    M = mem.shape[0]
    HW = HG_HEADS * HG_D
    QR = small["mla_g_cq"].shape[1]
    KR = small["mla_g_ckv"].shape[1]
    MW = W["mem_w_kv"].shape[1] // 2
    MHD = MW // MEM_HEADS
    DFF = W["w_ffn_gu"].shape[1] // 2
    QW = MLA_HEADS * 2 * LANES
    VW = MLA_HEADS * MLA_V
    c_hg, c_cq, c_ckv, c_qm, c_gate = 0, 5 * HW, 5 * HW + QR, 5 * HW + QR + KR, 5 * HW + QR + KR + MW
    c_kr = c_gate + N_BRANCH * D
    PW = c_kr + LANES
    assert W["w_in"].shape == (D, PW)
    TR = 256
    row = lambda a: a.reshape(1, -1)
    ge, be = row(small["ln_emb_g"]), row(small["ln_emb_b"])
    g1, b1, g2, b2 = small["ln1_g"], small["ln1_b"], small["ln2_g"], small["ln2_b"]
    lb = small["lb"]
    tc, ta, tb = _rope_tables(positions)

    (h0,) = _rowwise(lambda z, g, b: _ln_stats(z)[0] * g + b, [_Rows(x, D)], [ge, be],
                     [_Rows(D, D)], [], R=S, tr=TR, name="ln_emb")
    P = _mm(h0, W["w_in"], S, PW, D, tn=896, name="proj_in")

    o_fw, st_fw, a_fw = _hgrn_scan(P, lb[0:1], S=S, rev=False, name="hgrn_fw")
    o_bw, st_bw, a_bw = _hgrn_scan(P, lb[1:2], S=S, rev=True, name="hgrn_bw")

    def hg_post(of, ob, gr, ng):
        o = of + ob
        sg = _sigmoid(gr)
        outs = []
        for h in range(HG_HEADS):
            sl = slice(h * HG_D, (h + 1) * HG_D)
            outs.append(_rms_fwd(o[:, sl], ng, RMS_EPS) * sg[:, sl])
        return jnp.concatenate(outs, axis=1)

    (y_hg,) = _rowwise(hg_post, [_Rows(o_fw, HW), _Rows(o_bw, HW), _Rows(P, HW, 4)], [small["hgrn_norm_g"]],
                       [_Rows(HW, HW)], [], R=S, tr=TR, name="hgrn_post")

    def mla_norm(cq, ckv, gq, gk):
        return _rms_fwd(cq, gq, RMS_EPS), _rms_fwd(ckv, gk, RMS_EPS)

    assert c_cq % QR == 0 and c_ckv % KR == 0
    cqn, ckvn = _rowwise(mla_norm, [_Rows(P, QR, c_cq // QR), _Rows(P, KR, c_ckv // KR)],
                         [small["mla_g_cq"], small["mla_g_ckv"]],
                         [_Rows(QR, QR), _Rows(KR, KR)], [], R=S, tr=TR, name="mla_norm")
    q_raw = _mm(cqn, W["mla_w_uq"], S, QW, QR, name="mla_uq")
    kv = _mm(ckvn, W["mla_w_ukv"], S, 2 * VW, KR, name="mla_ukv")

    def rope_fwd(qb, knb, vb_, krb, tcb, tab, tbb):
        kr = _rope_apply(krb, tcb, tab, tbb)
        qo, ko = [], []
        for h in range(MLA_HEADS):
            qo += [qb[:, 2 * h * LANES:(2 * h + 1) * LANES],
                   _rope_apply(qb[:, (2 * h + 1) * LANES:(2 * h + 2) * LANES], tcb, tab, tbb)]
            ko += [knb[:, h * LANES:(h + 1) * LANES], kr]
        return jnp.concatenate(qo, axis=1), jnp.concatenate(ko, axis=1), vb_

    qc, kc, vc = _rowwise(rope_fwd, [_Rows(q_raw, QW), _Rows(kv, VW), _Rows(kv, VW, 1), _Rows(P, LANES, c_kr // LANES),
                                     _Rows(tc, LANES), _Rows(ta, LANES), _Rows(tb, LANES)], [],
                          [_Rows(QW, QW, dtype=MXU), _Rows(QW, QW, dtype=MXU), _Rows(VW, VW, dtype=MXU)], [],
                          R=S, tr=TR, name="rope_fwd")
    mla_kw = dict(S=S, T=S, H=MLA_HEADS, dqk=2 * LANES, dv=MLA_V, q_col0=0, k_col0=0, v_col0=0,
                  scale=(MLA_NOPE + MLA_ROPE) ** -0.5, tq=1024, tk=512)
    y_mla, lse_mla = _attn_fwd(qc, kc, vc, name="mla_attn", **mla_kw)

    kvm = _mm(mem, W["mem_w_kv"], M, 2 * MW, D, name="mem_kv")
    mem_kw = dict(S=S, T=M, H=MEM_HEADS, dqk=MHD, dv=MHD, q_col0=c_qm // MHD, k_col0=0, v_col0=MEM_HEADS,
                  scale=MHD ** -0.5, tq=1024, tk=M)
    assert c_qm % MHD == 0
    y_mem, lse_mem = _attn_fwd(P, kvm, kvm, name="mem_attn", **mem_kw)

    ys = (y_hg, y_mla, y_mem)
    us = [_mm(ys[b], W["w_branch"][b], S, D, HW, name=f"branch{b}") for b in range(N_BRANCH)]
    TCW = _tile(D, 1024, LANES)
    ncw = D // TCW

    def merge_fwd(g0, g1_, g2_, u0, u1, u2):
        return _sigmoid(g0) * u0 + _sigmoid(g1_) * u1 + _sigmoid(g2_) * u2

    gate_rows = [_Rows(P, TCW, (c_gate + b * D) // TCW) for b in range(N_BRANCH)]
    assert c_gate % TCW == 0
    (merged,) = _rowwise(merge_fwd, gate_rows + [_Rows(u, TCW) for u in us], [],
                         [_Rows(D, TCW)], [], R=S, tr=TR, ncol=ncw, name="merge_fwd")
    mix = _mm(merged, W["w_o"], S, D, D, name="out_proj")

    def ln_res(hp, addv, g, b):
        z = ALPHA * hp + addv
        return z, _ln_stats(z)[0] * g + b

    z1, h1 = _rowwise(ln_res, [_Rows(h0, D), _Rows(mix, D)], [g1, b1],
                      [_Rows(D, D), _Rows(D, D)], [], R=S, tr=TR, name="ln1")

    ab = _mm(h1, W["w_ffn_gu"], S, 2 * DFF, D, name="ffn_gu")
    TF = _tile(DFF, 512, LANES)
    nf = DFF // TF

    def swiglu(abv):
        a, b = abv[:, :TF], abv[:, TF:]
        return a * _sigmoid(a) * b

    (cff,) = _rowwise(swiglu, [_Rows(ab, 2 * TF)], [], [_Rows(DFF, TF)], [],
                      R=S, tr=TR, ncol=nf, name="swiglu")
    ff = _mm(cff, W["w_ffn_down"], S, D, DFF, name="ffn_down")

    def loss_bwd(hp, addv, tgt, g, b):
        z = ALPHA * hp + addv
        xhat, rstd = _ln_stats(z)
        y = xhat * g + b
        err = y - tgt
        dy = err * (1.0 / D)
        dxh = dy * g
        m1 = jnp.mean(dxh, axis=-1, keepdims=True)
        m2 = jnp.mean(dxh * xhat, axis=-1, keepdims=True)
        dz = rstd * (dxh - m1 - xhat * m2)
        lrow = jnp.sum(_colsum(err * err), axis=-1, keepdims=True) * (0.5 / D)
        return dz, _colsum(dy * xhat), _colsum(dy), lrow

    dz2, dg2, db2, loss = _rowwise(loss_bwd, [_Rows(h1, D), _Rows(ff, D), _Rows(loss_target, D)], [g2, b2],
                                   [_Rows(D, D)], [D, D, 1], R=S, tr=TR, name="loss_ln2_bwd")
    dcff = _mm(dz2, W["w_ffn_down"], S, DFF, D, tb=True, name="ffn_down_dx")
    g_ffn_down = _mm(cff, dz2, DFF, D, S, ta=True, name="ffn_down_dw")

    def swiglu_bwd(abv, dc):
        a, b = abv[:, :TF], abv[:, TF:]
        sg = _sigmoid(a)
        return jnp.concatenate([dc * b * sg * (1.0 + a * (1.0 - sg)), dc * a * sg], axis=1)

    (dab,) = _rowwise(swiglu_bwd, [_Rows(ab, 2 * TF), _Rows(dcff, TF)], [], [_Rows(2 * DFF, 2 * TF)], [],
                      R=S, tr=TR, ncol=nf, name="swiglu_bwd")
    dh1 = _mm(dab, W["w_ffn_gu"], S, D, 2 * DFF, tb=True, name="ffn_gu_dx")
    g_ffn_gu = _mm(h1, dab, D, 2 * DFF, S, ta=True, name="ffn_gu_dw")

    def ln1_bwd(z, dmm, dz2v, g):
        return _ln_bwd_core(z, g, ALPHA * dz2v + dmm)

    dz1, dg1, db1 = _rowwise(ln1_bwd, [_Rows(z1, D), _Rows(dh1, D), _Rows(dz2, D)], [g1],
                             [_Rows(D, D)], [D, D], R=S, tr=TR, name="ln1_bwd")
    dmerged = _mm(dz1, W["w_o"], S, D, D, tb=True, name="out_proj_dx")
    g_w_o = _mm(merged, dz1, D, D, S, ta=True, name="out_proj_dw")

    def merge_bwd(g0, g1_, g2_, u0, u1, u2, dm):
        res_g, res_u = [], []
        for gv, uv in ((g0, u0), (g1_, u1), (g2_, u2)):
            sg = _sigmoid(gv)
            res_g.append(dm * uv * sg * (1.0 - sg))
            res_u.append(dm * sg)
        return (*res_g, *res_u)

    mres = _rowwise(merge_bwd, gate_rows + [_Rows(u, TCW) for u in us] + [_Rows(dmerged, TCW)], [],
                    [_Rows(D, TCW)] * (2 * N_BRANCH), [], R=S, tr=TR, ncol=ncw, name="merge_bwd")
    dgates, dus = mres[:N_BRANCH], mres[N_BRANCH:]
    dys = [_mm(dus[b], W["w_branch"][b], S, HW, D, tb=True, name=f"branch{b}_dx") for b in range(N_BRANCH)]
    g_w_branch = [_mm(ys[b], dus[b], HW, D, S, ta=True, name=f"branch{b}_dw") for b in range(N_BRANCH)]

    dq_mem, dk_mem, dv_mem = _attn_bwd(P, kvm, kvm, y_mem, lse_mem, dys[2], name="mem_attn_bwd", **mem_kw)
    dkvm = jnp.concatenate([dk_mem, dv_mem], axis=1)
    g_mem_w_kv = _mm(mem, dkvm, D, 2 * MW, M, ta=True, name="mem_kv_dw")

    dqc, dkc, dvv = _attn_bwd(qc, kc, vc, y_mla, lse_mla, dys[1], name="mla_attn_bwd", **mla_kw)

    def rope_bwd(dqb, dkb, tcb, tab, tbb):
        qo, kn = [], []
        dkr = jnp.zeros_like(tcb)
        for h in range(MLA_HEADS):
            qo += [dqb[:, 2 * h * LANES:(2 * h + 1) * LANES],
                   _rope_apply_t(dqb[:, (2 * h + 1) * LANES:(2 * h + 2) * LANES], tcb, tab, tbb)]
            kn.append(dkb[:, 2 * h * LANES:(2 * h + 1) * LANES])
            dkr = dkr + dkb[:, (2 * h + 1) * LANES:(2 * h + 2) * LANES]
        return jnp.concatenate(qo, axis=1), jnp.concatenate(kn, axis=1), _rope_apply_t(dkr, tcb, tab, tbb)

    dq_raw, dkn, dkr_raw = _rowwise(rope_bwd, [_Rows(dqc, QW), _Rows(dkc, QW), _Rows(tc, LANES),
                                               _Rows(ta, LANES), _Rows(tb, LANES)], [],
                                    [_Rows(QW, QW), _Rows(VW, VW), _Rows(LANES, LANES)], [],
                                    R=S, tr=TR, name="rope_bwd")
    dkv = jnp.concatenate([dkn, dvv], axis=1)
    dcqn = _mm(dq_raw, W["mla_w_uq"], S, QR, QW, tb=True, name="mla_uq_dx")
    g_mla_w_uq = _mm(cqn, dq_raw, QR, QW, S, ta=True, name="mla_uq_dw")
    dckvn = _mm(dkv, W["mla_w_ukv"], S, KR, 2 * VW, tb=True, name="mla_ukv_dx")
    g_mla_w_ukv = _mm(ckvn, dkv, KR, 2 * VW, S, ta=True, name="mla_ukv_dw")

    def mla_norm_bwd(cq, ckv, dq_, dk_, gq, gk):
        dcq, gq_rows = _rms_bwd(cq, gq, dq_, RMS_EPS)
        dck, gk_rows = _rms_bwd(ckv, gk, dk_, RMS_EPS)
        return dcq, dck, _colsum(gq_rows), _colsum(gk_rows)

    dcq, dckv, dg_cq, dg_ckv = _rowwise(
        mla_norm_bwd, [_Rows(P, QR, c_cq // QR), _Rows(P, KR, c_ckv // KR), _Rows(dcqn, QR), _Rows(dckvn, KR)],
        [small["mla_g_cq"], small["mla_g_ckv"]], [_Rows(QR, QR), _Rows(KR, KR)], [QR, KR],
        R=S, tr=TR, name="mla_norm_bwd")

    def hg_post_bwd(of, ob, gr, dy, ng):
        o = of + ob
        sg = _sigmoid(gr)
        do_, dgr = [], []
        dng = jnp.zeros((1, HG_D), F32)
        for h in range(HG_HEADS):
            sl = slice(h * HG_D, (h + 1) * HG_D)
            t = _rms_fwd(o[:, sl], ng, RMS_EPS)
            dgr.append(dy[:, sl] * t * sg[:, sl] * (1.0 - sg[:, sl]))
            dx, grow = _rms_bwd(o[:, sl], ng, dy[:, sl] * sg[:, sl], RMS_EPS)
            do_.append(dx)
            dng = dng + _colsum(grow)
        return jnp.concatenate(do_, axis=1), jnp.concatenate(dgr, axis=1), dng

    do_hg, dg_hg, dng = _rowwise(hg_post_bwd, [_Rows(o_fw, HW), _Rows(o_bw, HW), _Rows(P, HW, 4), _Rows(dys[0], HW)],
                                 [small["hgrn_norm_g"]], [_Rows(HW, HW), _Rows(HW, HW)], [HG_D],
                                 R=S, tr=TR, name="hgrn_post_bwd")
    dq_f, dv_f, dff_fw, dlb_f = _hgrn_scan_bwd(P, lb[0:1], st_fw, a_fw, do_hg, S=S, rev=False, name="hgrn_fw_bwd")
    dq_b, dv_b, dff_bw, dlb_b = _hgrn_scan_bwd(P, lb[1:2], st_bw, a_bw, do_hg, S=S, rev=True, name="hgrn_bw_bwd")
    THW = _tile(HW, 1024, LANES)
    dq_hg, dv_hg = _rowwise(lambda a, b, c, d: (a + b, c + d),
                            [_Rows(dq_f, THW), _Rows(dq_b, THW), _Rows(dv_f, THW), _Rows(dv_b, THW)], [],
                            [_Rows(HW, THW), _Rows(HW, THW)], [], R=S, tr=TR, ncol=HW // THW, name="hgrn_dir_sum")

    dP = jnp.concatenate([dq_hg, dv_hg, dff_fw, dff_bw, dg_hg, dcq, dckv, dq_mem, *dgates, dkr_raw], axis=1)
    dh0 = _mm(dP, W["w_in"], S, D, PW, tb=True, tk=_tile(PW, 640, LANES), name="proj_in_dx")
    g_w_in = _mm(h0, dP, D, PW, S, ta=True, tn=896, name="proj_in_dw")

    def ln0_bwd(z, dmm, dz1v, g):
        return _ln_bwd_core(z, g, ALPHA * dz1v + dmm)

    grad_x, dge, dbe = _rowwise(ln0_bwd, [_Rows(x, D), _Rows(dh0, D), _Rows(dz1, D)], [ge],
                                [_Rows(D, D)], [D, D], R=S, tr=TR, name="ln_emb_bwd")

    big = dict(w_in=g_w_in, mla_w_uq=g_mla_w_uq, mla_w_ukv=g_mla_w_ukv, mem_w_kv=g_mem_w_kv,
               w_branch=jnp.stack(g_w_branch), w_o=g_w_o, w_ffn_gu=g_ffn_gu, w_ffn_down=g_ffn_down)
    sm = dict(ln_emb_g=dge, ln_emb_b=dbe, dlb=jnp.concatenate([dlb_f, dlb_b], axis=0), hgrn_norm_g=dng,
              mla_g_cq=dg_cq, mla_g_ckv=dg_ckv, ln1_g=dg1, ln1_b=db1, ln2_g=dg2, ln2_b=db2)
    return loss, grad_x, big, sm


def _coords():
    return lax.axis_index("x"), lax.axis_index("y"), lax.axis_index("c")


def _exchange(srcs, n_dst, copies, local_copies, *, name):
    n_a, n_rc, n_lc = len(srcs), len(copies), len(local_copies)

    def body(*refs):
        src_refs, dst_refs = refs[:n_a], refs[n_a:2 * n_a]
        send_sems, recv_sems, local_sems = refs[2 * n_a:]
        x, y, c = _coords()
        remote, local = [], []
        for a in range(n_a):
            for k, (mask, sidx, didx) in enumerate(copies):
                peer = (x ^ mask[0], y ^ mask[1], c ^ mask[2])
                remote.append(pltpu.make_async_remote_copy(
                    src_ref=src_refs[a].at[sidx(x, y, c)], dst_ref=dst_refs[a].at[didx(x, y, c)],
                    send_sem=send_sems.at[a * n_rc + k], recv_sem=recv_sems.at[a * n_rc + k],
                    device_id=peer, device_id_type=MESH))
            for k, (sidx, didx) in enumerate(local_copies):
                local.append(pltpu.make_async_copy(src_refs[a].at[sidx(x, y, c)], dst_refs[a].at[didx(x, y, c)],
                                                   local_sems.at[a * n_lc + k]))
        for cp in remote + local:
            cp.start()
        for cp in remote:
            cp.wait_recv()
        for cp in remote:
            cp.wait_send()
        for cp in local:
            cp.wait()

    return pl.pallas_call(
        body, name=name,
        in_specs=[pl.BlockSpec(memory_space=pl.ANY)] * n_a,
        out_specs=[pl.BlockSpec(memory_space=pl.ANY)] * n_a,
        out_shape=[jax.ShapeDtypeStruct((n_dst,) + s.shape[1:], s.dtype) for s in srcs],
        scratch_shapes=[pltpu.SemaphoreType.DMA((n_a * n_rc,)), pltpu.SemaphoreType.DMA((n_a * n_rc,)),
                        pltpu.SemaphoreType.DMA((n_a * n_lc,))],
        compiler_params=pltpu.CompilerParams(has_side_effects=True),
    )(*srcs)


_CHIP_MASKS = ((1, 0, 0), (0, 1, 0), (1, 1, 0))
_SIBLING = (0, 0, 1)


def _chip(x, y, c):
    return 2 * x + y


def _gather_chips(shards, *, name):
    copies = [(m, lambda x, y, c: 0, _chip) for m in _CHIP_MASKS]
    return _exchange([s[None] for s in shards], 4, copies, [(lambda x, y, c: 0, _chip)], name=name)


def _device(x, y, c):
    return 4 * x + 2 * y + c


_ALL_MASKS = tuple((k >> 2 & 1, k >> 1 & 1, k & 1) for k in range(1, 8))
SHARE_BLOCK_BYTES = 4 << 20


def _sum_share(arr, *, name):
    n, rh, w = arr.shape
    tr = _tile(rh, max(16, SHARE_BLOCK_BYTES // (n * w * arr.dtype.itemsize) // 16 * 16), 16)
    nb = rh // tr

    def body(a_ref, o_ref, slots, send_sems, recv_sem, local_sems):
        i = pl.program_id(0)
        x, y, c = _coords()
        sibling = (x, y, 1 - c)

        def pushes(step, slot):
            rows = pl.ds(pl.multiple_of(c * rh + step * tr, SUBLANES), tr)
            return (pltpu.make_async_copy(slots.at[slot], o_ref.at[rows], local_sems.at[slot]),
                    pltpu.make_async_remote_copy(src_ref=slots.at[slot], dst_ref=o_ref.at[rows],
                                                 send_sem=send_sems.at[slot], recv_sem=recv_sem,
                                                 device_id=sibling, device_id_type=MESH))

        def drain(step, slot):
            loc, rem = pushes(step, slot)
            loc.wait()
            rem.wait_send()

        slot = i % 2

        @pl.when(i >= 2)
        def _():
            drain(i - 2, slot)

        acc = a_ref[0].astype(F32)
        for k in range(1, n):
            acc = acc + a_ref[k].astype(F32)
        slots[slot] = acc
        loc, rem = pushes(i, slot)
        loc.start()
        rem.start()

        @pl.when(i == nb - 1)
        def _():
            if nb >= 2:
                drain(i - 1, 1 - slot)
            drain(i, slot)
            other = o_ref.at[pl.ds(pl.multiple_of((1 - c) * rh, SUBLANES), rh)]
            pltpu.make_async_remote_copy(src_ref=other, dst_ref=other, send_sem=send_sems.at[0], recv_sem=recv_sem,
                                         device_id=sibling, device_id_type=MESH).wait_recv()

    return pl.pallas_call(
        body, name=name, grid=(nb,),
        in_specs=[pl.BlockSpec((n, tr, w), lambda i: (0, i, 0))],
        out_specs=pl.BlockSpec(memory_space=pl.ANY),
        out_shape=jax.ShapeDtypeStruct((2 * rh, w), F32),
        scratch_shapes=[pltpu.VMEM((2, tr, w), F32), pltpu.SemaphoreType.DMA((2,)), pltpu.SemaphoreType.DMA,
                        pltpu.SemaphoreType.DMA((2,))],
        compiler_params=pltpu.CompilerParams(dimension_semantics=("arbitrary",), has_side_effects=True,
                                             vmem_limit_bytes=VMEM_LIMIT),
    )(arr)


def _reduce_scatter(pieces, names):
    copies = [(m, (lambda x, y, c, m=m: 2 * _chip(x ^ m[0], y ^ m[1], c) + (c ^ m[2])), _device) for m in _ALL_MASKS]
    local = [((lambda x, y, c: 2 * _chip(x, y, c) + c), _device)]
    recv = _exchange([p.reshape((8,) + p.shape[2:]) for p in pieces], 8, copies, local, name="rs_exchange")
    return [_sum_share(r, name="rs_sum_" + n) for r, n in zip(recv, names)]


def _allreduce_small(v, *, name):
    r, w = v.shape

    def body(v_ref, o_ref, buf, send_sems, recv_sems):
        x, y, c = _coords()
        me = 4 * x + 2 * y + c
        buf[me] = v_ref[...]
        cps = []
        for k in range(7):
            m = ((k + 1) >> 2 & 1, (k + 1) >> 1 & 1, (k + 1) & 1)
            cp = pltpu.make_async_remote_copy(
                src_ref=v_ref, dst_ref=buf.at[me], send_sem=send_sems.at[k], recv_sem=recv_sems.at[k],
                device_id=(x ^ m[0], y ^ m[1], c ^ m[2]), device_id_type=MESH)
            cp.start()
            cps.append(cp)
        for cp in cps:
            cp.wait_recv()
        for cp in cps:
            cp.wait_send()
        acc = buf[0]
        for k in range(1, 8):
            acc = acc + buf[k]
        o_ref[...] = acc

    return pl.pallas_call(
        body, name=name,
        in_specs=[pl.BlockSpec(memory_space=pltpu.VMEM)],
        out_specs=pl.BlockSpec(memory_space=pltpu.VMEM),
        out_shape=jax.ShapeDtypeStruct((r, w), F32),
        scratch_shapes=[pltpu.VMEM((8, r, w), F32), pltpu.SemaphoreType.DMA((7,)), pltpu.SemaphoreType.DMA((7,))],
        compiler_params=pltpu.CompilerParams(has_side_effects=True),
    )(v)


_BIG = (("w_in", 1), ("mla_w_uq", 1), ("mla_w_ukv", 1), ("mem_w_kv", 0), ("w_branch", 1), ("w_o", 0),
        ("w_ffn_gate", 1), ("w_ffn_up", 1), ("w_ffn_down", 0))


def _assemble(gathered, ax):
    _, r, c = gathered.shape
    if ax == 0:
        return gathered.reshape(4 * r, c)
    return gathered.transpose(1, 0, 2).reshape(r, 4 * c)


def _split_pieces(g, ax):
    r, c = g.shape
    if ax == 0:
        return g.reshape(4, 2, r // 8, c)
    return g.reshape(2, r // 2, 4, c // 4).transpose(2, 0, 1, 3)


def _pad_cols(a, n):
    return jnp.pad(a, ((0, 0), (0, n - a.shape[1])))


def _to_kernel_layout(full, D):
    w_in = full["w_in"]
    QR, KR = full["mla_w_uq"].shape[0], full["mla_w_ukv"].shape[0]
    HW = HG_HEADS * HG_D
    MW = full["mem_w_kv"].shape[1] // 2
    a = 5 * HW + QR + KR
    w_in_k = jnp.concatenate([w_in[:, :a], w_in[:, a + MLA_ROPE:], _pad_cols(w_in[:, a:a + MLA_ROPE], LANES)], axis=1)
    uq = full["mla_w_uq"].reshape(QR, MLA_HEADS, MLA_NOPE + MLA_ROPE)
    uq_k = jnp.pad(uq, ((0, 0), (0, 0), (0, 2 * LANES - MLA_NOPE - MLA_ROPE))).reshape(QR, MLA_HEADS * 2 * LANES)
    ukv = full["mla_w_ukv"].reshape(KR, MLA_HEADS, MLA_NOPE + MLA_V)
    ukv_k = jnp.concatenate([ukv[:, :, :MLA_NOPE].reshape(KR, -1), ukv[:, :, MLA_NOPE:].reshape(KR, -1)], axis=1)
    DFF = full["w_ffn_gate"].shape[1]
    TF = _tile(DFF, 512, LANES)
    gu = jnp.stack([full["w_ffn_gate"].reshape(D, DFF // TF, TF), full["w_ffn_up"].reshape(D, DFF // TF, TF)], axis=2)
    return dict(w_in=w_in_k, mla_w_uq=uq_k, mla_w_ukv=ukv_k, mem_w_kv=full["mem_w_kv"], w_branch=full["w_branch"],
                w_o=full["w_o"], w_ffn_gu=gu.reshape(D, 2 * DFF), w_ffn_down=full["w_ffn_down"])


def _from_kernel_layout(gk, D, QR, KR, MW):
    HW = HG_HEADS * HG_D
    a = 5 * HW + QR + KR
    g = gk["w_in"]
    rest = g.shape[1] - LANES - a
    w_in = jnp.concatenate([g[:, :a], g[:, a + rest:a + rest + MLA_ROPE], g[:, a:a + rest]], axis=1)
    uq = gk["mla_w_uq"].reshape(QR, MLA_HEADS, 2 * LANES)[:, :, :MLA_NOPE + MLA_ROPE].reshape(QR, -1)
    VW = MLA_HEADS * MLA_V
    ukv = jnp.concatenate([gk["mla_w_ukv"][:, :VW].reshape(KR, MLA_HEADS, MLA_NOPE),
                           gk["mla_w_ukv"][:, VW:].reshape(KR, MLA_HEADS, MLA_V)], axis=2).reshape(KR, -1)
    DFF = gk["w_ffn_gu"].shape[1] // 2
    TF = _tile(DFF, 512, LANES)
    gu = gk["w_ffn_gu"].reshape(D, DFF // TF, 2, TF)
    return dict(w_in=w_in, mla_w_uq=uq, mla_w_ukv=ukv, mem_w_kv=gk["mem_w_kv"], w_branch=gk["w_branch"],
                w_o=gk["w_o"], w_ffn_gate=gu[:, :, 0].reshape(D, DFF), w_ffn_up=gu[:, :, 1].reshape(D, DFF),
                w_ffn_down=gk["w_ffn_down"])


def _adamw(w, g, m, v, *, name):
    r, c = w.shape
    tr = max(SUBLANES, min(512, (1 << 20) // (4 * c)) // SUBLANES * SUBLANES)
    c1 = 1.0 / (1.0 - ADAM_B1 ** ADAM_STEP)
    c2 = 1.0 / (1.0 - ADAM_B2 ** ADAM_STEP)

    def fn(wv, gv, mv, vv):
        mn = ADAM_B1 * mv + (1.0 - ADAM_B1) * gv
        vn = ADAM_B2 * vv + (1.0 - ADAM_B2) * (gv * gv)
        delta = -ADAM_LR * ((mn * c1) / (jnp.sqrt(vn * c2) + ADAM_EPS) + ADAM_WD * wv)
        return delta, mn, vn

    return _rowwise(fn, [_Rows(a, c) for a in (w, g, m, v)], [], [_Rows(c, c)] * 3, [], R=r, tr=tr, name=name)


_SMALL = ("ln_emb_g", "ln_emb_b", "hgrn_lb_logits", "hgrn_norm_g", "mla_g_cq", "mla_g_ckv",
          "ln1_g", "ln1_b", "ln2_g", "ln2_b")


def _lb_from_logits(logits):
    return jnp.cumsum(jax.nn.softmax(logits, axis=1), axis=1)[:, 0]


def _small_rows(parts):
    flat = jnp.concatenate([p.reshape(-1) for p in parts])
    n = flat.shape[0]
    total = -(-n // (SUBLANES * LANES)) * SUBLANES * LANES
    return jnp.pad(flat, (0, total - n)).reshape(total // LANES, LANES)


def kernel(x, mem, positions, ln_emb_g, ln_emb_b, hgrn_lb_logits, w_in, hgrn_norm_g, mla_g_cq, mla_g_ckv, mla_w_uq, mla_w_ukv, mem_w_kv, w_branch, w_o, ln1_g, ln1_b, w_ffn_gate, w_ffn_up, w_ffn_down, ln2_g, ln2_b, loss_target, m_ln_emb_g, m_ln_emb_b, m_hgrn_lb_logits, m_w_in, m_hgrn_norm_g, m_mla_g_cq, m_mla_g_ckv, m_mla_w_uq, m_mla_w_ukv, m_mem_w_kv, m_w_branch, m_w_o, m_ln1_g, m_ln1_b, m_w_ffn_gate, m_w_ffn_up, m_w_ffn_down, m_ln2_g, m_ln2_b, v_ln_emb_g, v_ln_emb_b, v_hgrn_lb_logits, v_w_in, v_hgrn_norm_g, v_mla_g_cq, v_mla_g_ckv, v_mla_w_uq, v_mla_w_ukv, v_mem_w_kv, v_w_branch, v_w_o, v_ln1_g, v_ln1_b, v_w_ffn_gate, v_w_ffn_up, v_w_ffn_down, v_ln2_g, v_ln2_b):
    names = ["ln_emb_g", "ln_emb_b", "hgrn_lb_logits", "w_in", "hgrn_norm_g", "mla_g_cq", "mla_g_ckv", "mla_w_uq",
             "mla_w_ukv", "mem_w_kv", "w_branch", "w_o", "ln1_g", "ln1_b", "w_ffn_gate", "w_ffn_up", "w_ffn_down",
             "ln2_g", "ln2_b"]
    wts = dict(zip(names, [ln_emb_g, ln_emb_b, hgrn_lb_logits, w_in, hgrn_norm_g, mla_g_cq, mla_g_ckv, mla_w_uq,
                           mla_w_ukv, mem_w_kv, w_branch, w_o, ln1_g, ln1_b, w_ffn_gate, w_ffn_up, w_ffn_down,
                           ln2_g, ln2_b]))
    mom = dict(zip(names, [m_ln_emb_g, m_ln_emb_b, m_hgrn_lb_logits, m_w_in, m_hgrn_norm_g, m_mla_g_cq, m_mla_g_ckv,
                           m_mla_w_uq, m_mla_w_ukv, m_mem_w_kv, m_w_branch, m_w_o, m_ln1_g, m_ln1_b, m_w_ffn_gate,
                           m_w_ffn_up, m_w_ffn_down, m_ln2_g, m_ln2_b]))
    var = dict(zip(names, [v_ln_emb_g, v_ln_emb_b, v_hgrn_lb_logits, v_w_in, v_hgrn_norm_g, v_mla_g_cq, v_mla_g_ckv,
                           v_mla_w_uq, v_mla_w_ukv, v_mem_w_kv, v_w_branch, v_w_o, v_ln1_g, v_ln1_b, v_w_ffn_gate,
                           v_w_ffn_up, v_w_ffn_down, v_ln2_g, v_ln2_b]))
    xc, yc, cc = _coords()
    chip = _chip(xc, yc, cc)
    S, D = x.shape[1], x.shape[2]

    shards = [wts[n].reshape(-1, wts[n].shape[-1]).astype(BF16) for n, _ in _BIG]
    gathered = _gather_chips(shards, name="gather_weights")
    full = {n: _assemble(g, ax) for (n, ax), g in zip(_BIG, gathered)}
    full["w_branch"] = full["w_branch"].reshape(N_BRANCH, -1, D)
    QR, KR = full["mla_w_uq"].shape[0], full["mla_w_ukv"].shape[0]
    MW = full["mem_w_kv"].shape[1] // 2
    Wk = _to_kernel_layout(full, D)

    lsh = hgrn_lb_logits.shape
    HW = 4 * lsh[2]
    placed = lax.dynamic_update_slice(jnp.zeros((lsh[0], lsh[1], HW), F32), hgrn_lb_logits, (0, 0, chip * lsh[2]))
    placed = jnp.where(cc == 0, placed, 0.0)
    logits = _allreduce_small(_small_rows([placed]), name="gather_logits").reshape(-1)[:placed.size].reshape(placed.shape)
    lb, lb_vjp = jax.vjp(_lb_from_logits, logits)

    small = dict(ln_emb_g=ln_emb_g, ln_emb_b=ln_emb_b, lb=lb, hgrn_norm_g=hgrn_norm_g, mla_g_cq=mla_g_cq,
                 mla_g_ckv=mla_g_ckv, ln1_g=ln1_g, ln1_b=ln1_b, ln2_g=ln2_g, ln2_b=ln2_b)
    loss_l, grad_x, gk, gs = _local_step(x[0], mem[0], positions[0], loss_target[0], Wk, small)

    (dlogits,) = lb_vjp(gs["dlb"])
    sm_parts = [loss_l, gs["ln_emb_g"], gs["ln_emb_b"], dlogits, gs["hgrn_norm_g"], gs["mla_g_cq"], gs["mla_g_ckv"],
                gs["ln1_g"], gs["ln1_b"], gs["ln2_g"], gs["ln2_b"]]
    red = _allreduce_small(_small_rows(sm_parts), name="allreduce_small").reshape(-1)
    sm_out, off = [], 0
    for p in sm_parts:
        sm_out.append(red[off:off + p.size].reshape(p.shape))
        off += p.size
    loss = sm_out[0].reshape(())
    g_small = dict(zip(_SMALL, sm_out[1:]))
    g_small["hgrn_lb_logits"] = lax.dynamic_slice(g_small["hgrn_lb_logits"], (0, 0, chip * lsh[2]), lsh)
    for n in _SMALL:
        g_small[n] = g_small[n].reshape(wts[n].shape)

    gfull = _from_kernel_layout(gk, D, QR, KR, MW)
    big_names = [n for n, _ in _BIG]
    pieces = [_split_pieces(gfull[n].reshape(-1, gfull[n].shape[-1]), ax).astype(BF16) for n, ax in _BIG]
    reduced = _reduce_scatter(pieces, big_names)
    g_big = {n: g.reshape(wts[n].shape) for n, g in zip(big_names, reduced)}

    grads = {**g_small, **g_big}
    delta, new_m, new_v = {}, {}, {}
    for n, _ in _BIG:
        shp = wts[n].shape
        two_d = lambda a: a.reshape(-1, shp[-1])
        d_, m_, v_ = _adamw(two_d(wts[n]), two_d(grads[n]), two_d(mom[n]), two_d(var[n]), name="adamw_" + n)
        delta[n], new_m[n], new_v[n] = d_.reshape(shp), m_.reshape(shp), v_.reshape(shp)
    sw, sg_, sm_, sv_ = (_small_rows([d[n] for n in _SMALL]) for d in (wts, grads, mom, var))
    d_, m_, v_ = _adamw(sw, sg_, sm_, sv_, name="adamw_small")
    for res, packed_rows in ((delta, d_), (new_m, m_), (new_v, v_)):
        flat, off = packed_rows.reshape(-1), 0
        for n in _SMALL:
            res[n] = flat[off:off + wts[n].size].reshape(wts[n].shape)
            off += wts[n].size

    return (loss, grad_x[None], *[grads[n] for n in names], *[delta[n] for n in names],
            *[new_m[n] for n in names], *[new_v[n] for n in names])
```

```python
import jax
import jax.numpy as jnp
from jax import lax
from jax.experimental import pallas as pl
from jax.experimental.pallas import tpu as pltpu

F32 = jnp.float32
BF16 = jnp.bfloat16

HG_HEADS = 8
HG_D = 128
MLA_HEADS = 8
MLA_NOPE = 128
MLA_ROPE = 64
MLA_V = 128
MEM_HEADS = 4
N_BRANCH = 3
ROPE_THETA = 10000.0
DEPTH = 1
ALPHA = (2.0 * DEPTH) ** 0.25
LN_EPS = 1e-5
RMS_EPS = 1e-6
ADAM_LR = 0.001
ADAM_B1 = 0.9
ADAM_B2 = 0.999
ADAM_EPS = 1e-08
ADAM_WD = 0.01
ADAM_STEP = 10

LANES = 128
SUBLANES = 8
VMEM_LIMIT = 48 * 1024 * 1024

HG_CHUNK = 128
HG_SUB = 16

MESH = pl.DeviceIdType.MESH
HI = lax.Precision.HIGHEST


def _cparams(sem=None):
    if sem is None:
        return pltpu.CompilerParams(vmem_limit_bytes=VMEM_LIMIT)
    return pltpu.CompilerParams(dimension_semantics=sem, vmem_limit_bytes=VMEM_LIMIT)


def _tile(dim, pref, quantum):
    t = min(pref, dim) // quantum * quantum
    while t >= quantum:
        if dim % t == 0:
            return t
        t -= quantum
    return dim


def _sigmoid(x):
    return 1.0 / (1.0 + jnp.exp(-x))


def _coords():
    return lax.axis_index("x"), lax.axis_index("y"), lax.axis_index("c")


def _chip(x, y, c):
    return 2 * x + y


def _device(x, y, c):
    return 4 * x + 2 * y + c


_CHIP_MASKS = ((1, 0, 0), (0, 1, 0), (1, 1, 0))
_ALL_MASKS = tuple((k >> 2 & 1, k >> 1 & 1, k & 1) for k in range(1, 8))
_HBM = pl.BlockSpec(memory_space=pl.ANY)


class _Exch:
    def __init__(self, srcs, n_dst, copies, local_copies):
        self.srcs, self.n_dst, self.copies, self.local_copies = list(srcs), n_dst, copies, local_copies
        self.n = len(self.srcs)

    def out_shape(self):
        return [jax.ShapeDtypeStruct((self.n_dst,) + s.shape[1:], s.dtype) for s in self.srcs]

    def scratch(self):
        n_rc, n_lc = self.n * len(self.copies), self.n * len(self.local_copies)
        return [pltpu.SemaphoreType.DMA((n_rc,)), pltpu.SemaphoreType.DMA((n_rc,)),
                pltpu.SemaphoreType.DMA((max(n_lc, 1),))]

    def _descriptors(self, src_refs, dst_refs, sems):
        send_sems, recv_sems, local_sems = sems
        x, y, c = _coords()
        n_rc, n_lc = len(self.copies), len(self.local_copies)
        remote, local = [], []
        for a in range(self.n):
            for k, (mask, sidx, didx) in enumerate(self.copies):
                remote.append(pltpu.make_async_remote_copy(
                    src_ref=src_refs[a].at[sidx(x, y, c)], dst_ref=dst_refs[a].at[didx(x, y, c)],
                    send_sem=send_sems.at[a * n_rc + k], recv_sem=recv_sems.at[a * n_rc + k],
                    device_id=(x ^ mask[0], y ^ mask[1], c ^ mask[2]), device_id_type=MESH))
            for k, (sidx, didx) in enumerate(self.local_copies):
                local.append(pltpu.make_async_copy(src_refs[a].at[sidx(x, y, c)], dst_refs[a].at[didx(x, y, c)],
                                                   local_sems.at[a * n_lc + k]))
        return remote, local

    def start(self, src_refs, dst_refs, sems):
        remote, local = self._descriptors(src_refs, dst_refs, sems)
        for cp in remote + local:
            cp.start()

    def wait(self, src_refs, dst_refs, sems):
        remote, local = self._descriptors(src_refs, dst_refs, sems)
        for cp in remote:
            cp.wait_recv()
        for cp in remote:
            cp.wait_send()
        for cp in local:
            cp.wait()


def _exchange(exch, *, name):
    n = exch.n

    def body(*refs):
        src_refs, dst_refs, sems = refs[:n], refs[n:2 * n], refs[2 * n:]
        exch.start(src_refs, dst_refs, sems)
        exch.wait(src_refs, dst_refs, sems)

    return pl.pallas_call(
        body, name=name, in_specs=[_HBM] * n, out_specs=[_HBM] * n, out_shape=exch.out_shape(),
        scratch_shapes=exch.scratch(), compiler_params=pltpu.CompilerParams(has_side_effects=True),
    )(*exch.srcs)


def _carried(call, exch, grid, in_specs, out_specs, out_shape, scratch_shapes, args, *, name):
    n_in, n_out, n_scr = len(in_specs), len(out_specs), len(scratch_shapes)
    n = 0 if exch is None else exch.n

    def body(*refs):
        o0 = n_in + n
        s0 = o0 + n_out + n
        ins, srcs = refs[:n_in], refs[n_in:o0]
        outs, dsts = refs[o0:o0 + n_out], refs[o0 + n_out:s0]
        scr, sems = refs[s0:s0 + n_scr], refs[s0 + n_scr:]
        if exch is not None:
            ids = [pl.program_id(d) for d in range(len(grid))]
            first = _all([i == 0 for i in ids])
            last = _all([i == g - 1 for i, g in zip(ids, grid)])

            @pl.when(first)
            def _():
                exch.start(srcs, dsts, sems)

        call(*ins, *outs, *scr)
        if exch is not None:
            @pl.when(last)
            def _():
                exch.wait(srcs, dsts, sems)

    if exch is None:
        params = pltpu.CompilerParams(dimension_semantics=("arbitrary",) * len(grid), vmem_limit_bytes=VMEM_LIMIT)
        extra_in, extra_out, extra_shape, extra_scr, extra_args = [], [], [], [], []
    else:
        params = pltpu.CompilerParams(dimension_semantics=("arbitrary",) * len(grid), vmem_limit_bytes=VMEM_LIMIT,
                                      has_side_effects=True)
        extra_in, extra_out, extra_shape = [_HBM] * n, [_HBM] * n, exch.out_shape()
        extra_scr, extra_args = exch.scratch(), exch.srcs
    res = pl.pallas_call(
        body, name=name, grid=grid, in_specs=list(in_specs) + extra_in, out_specs=list(out_specs) + extra_out,
        out_shape=list(out_shape) + extra_shape, scratch_shapes=list(scratch_shapes) + extra_scr,
        compiler_params=params,
    )(*args, *extra_args)
    return res[:n_out], res[n_out:]


def _all(conds):
    out = conds[0]
    for c in conds[1:]:
        out = jnp.logical_and(out, c)
    return out


def _mm(a, b, M, N, K, *, ta=False, tb=False, a_off=(0, 0), b_off=(0, 0), add=None, exch=None,
        tm=1024, tn=1024, tk=1024, name):
    tm = _tile(M, tm, LANES if ta else SUBLANES)
    tn = _tile(N, tn, LANES)
    tk = _tile(K, tk, LANES)
    nk = K // tk
    ar, ac = a_off
    br, bc = b_off

    if ta:
        assert ar % tk == 0 and ac % tm == 0
        a_spec = pl.BlockSpec((tk, tm), lambda i, j, k: (ar // tk + k, ac // tm + i))
    else:
        assert ar % tm == 0 and ac % tk == 0
        a_spec = pl.BlockSpec((tm, tk), lambda i, j, k: (ar // tm + i, ac // tk + k))
    if tb:
        assert br % tn == 0 and bc % tk == 0
        b_spec = pl.BlockSpec((tn, tk), lambda i, j, k: (br // tn + j, bc // tk + k))
    else:
        assert br % tk == 0 and bc % tn == 0
        b_spec = pl.BlockSpec((tk, tn), lambda i, j, k: (br // tk + k, bc // tn + j))
    o_spec = pl.BlockSpec((tm, tn), lambda i, j, k: (i, j))
    mixed = a.dtype != b.dtype

    def body(*refs):
        if add is None:
            a_ref, b_ref, o_ref, acc = refs
        else:
            a_ref, b_ref, add_ref, o_ref, acc = refs
        k = pl.program_id(2)
        av = a_ref[...]
        bv = b_ref[...]
        if ta:
            av = av.astype(F32).T
        if mixed:
            av = av.astype(BF16)
            bv = bv.astype(BF16)
        dims = (((1,), (1 if tb else 0,)), ((), ()))
        d = lax.dot_general(av, bv, dims, preferred_element_type=F32)

        def finish(total):
            o_ref[...] = total if add is None else total + add_ref[...]

        if nk == 1:
            finish(d)
        else:
            @pl.when(k == 0)
            def _():
                acc[...] = d

            @pl.when(jnp.logical_and(k > 0, k < nk - 1))
            def _():
                acc[...] += d

            @pl.when(k == nk - 1)
            def _():
                finish(acc[...] + d)

    in_specs = [a_spec, b_spec]
    args = [a, b]
    if add is not None:
        in_specs.append(o_spec)
        args.append(add)
    outs, received = _carried(body, exch, (M // tm, N // tn, nk), in_specs, [o_spec],
                              [jax.ShapeDtypeStruct((M, N), F32)], [pltpu.VMEM((tm, tn), F32)], args, name=name)
    return outs[0] if exch is None else (outs[0], received)


class _Rows:
    def __init__(self, arr, width, col0=0, lead=None, dtype=F32):
        self.arr, self.width, self.col0, self.lead, self.dtype = arr, width, col0, lead, dtype


def _rowwise(fn, rows, consts, outs, accs, *, R, tr, ncol=1, name):
    tr = _tile(R, tr, SUBLANES)
    nrow = R // tr

    def spec(r):
        if r.lead is None:
            return pl.BlockSpec((tr, r.width), lambda j, i, c0=r.col0: (i, c0 + j))
        return pl.BlockSpec((None, tr, r.width), lambda j, i, c0=r.col0, l=r.lead: (l, i, c0 + j))

    in_specs = [spec(r) for r in rows]
    for c in consts:
        in_specs.append(pl.BlockSpec(c.shape, lambda j, i, nd=c.ndim: (0,) * nd))
    out_specs = [spec(o) for o in outs]
    out_shape = [jax.ShapeDtypeStruct((R, o.arr), o.dtype) for o in outs]
    for w in accs:
        out_specs.append(pl.BlockSpec((1, w), lambda j, i: (0, j)))
        out_shape.append(jax.ShapeDtypeStruct((1, w * ncol), F32))
    n_in = len(rows) + len(consts)
    n_out = len(outs)

    def body(*refs):
        ins = [r[...] for r in refs[:n_in]]
        res = fn(*ins)
        if not isinstance(res, (tuple, list)):
            res = (res,)
        for k in range(n_out):
            refs[n_in + k][...] = res[k].astype(refs[n_in + k].dtype)
        i = pl.program_id(1)
        for k in range(len(accs)):
            a_ref = refs[n_in + n_out + k]

            @pl.when(i == 0)
            def _(a_ref=a_ref):
                a_ref[...] = jnp.zeros_like(a_ref)

            a_ref[...] += res[n_out + k]

    res = pl.pallas_call(
        body, name=name, grid=(ncol, nrow),
        in_specs=in_specs, out_specs=out_specs, out_shape=out_shape,
        compiler_params=_cparams(("parallel", "arbitrary")),
    )(*[r.arr for r in rows], *consts)
    return res


def _colsum(x):
    return jnp.sum(x, axis=0, keepdims=True)


def _ln_stats(z):
    mu = jnp.mean(z, axis=-1, keepdims=True)
    zc = z - mu
    var = jnp.mean(zc * zc, axis=-1, keepdims=True)
    rstd = lax.rsqrt(var + LN_EPS)
    return zc * rstd, rstd


def _ln_bwd_core(z, g, dy):
    xhat, rstd = _ln_stats(z)
    dxh = dy * g
    m1 = jnp.mean(dxh, axis=-1, keepdims=True)
    m2 = jnp.mean(dxh * xhat, axis=-1, keepdims=True)
    dz = rstd * (dxh - m1 - xhat * m2)
    return dz, _colsum(dy * xhat), _colsum(dy)


def _rms_fwd(x, g, eps):
    r = lax.rsqrt(jnp.mean(x * x, axis=-1, keepdims=True) + eps)
    return x * r * g


def _rms_bwd(x, g, dy, eps):
    r = lax.rsqrt(jnp.mean(x * x, axis=-1, keepdims=True) + eps)
    xr = x * r
    dyg = dy * g
    dx = r * (dyg - xr * jnp.mean(dyg * xr, axis=-1, keepdims=True))
    return dx, dy * xr


def _hg_gate(fr, lb):
    sig = _sigmoid(fr)
    f = lb + (1.0 - lb) * sig
    return sig, f


def _hg_masks(rev):
    C = HG_CHUNK
    t = lax.broadcasted_iota(jnp.int32, (C, C), 0)
    s = lax.broadcasted_iota(jnp.int32, (C, C), 1)
    tri = (s >= t) if rev else (s <= t)
    return tri


def _hg_offdiag(Q, K, b, i, rev):
    C, sb = HG_CHUNK, HG_SUB
    nb = C // sb
    if (not rev and i == 0) or (rev and i == nb - 1):
        return None
    ref = b[sb * i - 1:sb * i] if not rev else b[sb * (i + 1):sb * (i + 1) + 1]
    srow = lax.broadcasted_iota(jnp.int32, (C, 1), 0)
    smask = (srow < sb * i) if not rev else (srow >= sb * (i + 1))
    qscale = jnp.exp(jnp.minimum(b - ref, 0.0))
    kscale = jnp.where(smask, jnp.exp(jnp.minimum(ref - b, 0.0)), 0.0)
    return qscale, kscale


def _hg_att(Q, K, b, rev):
    C, sb = HG_CHUNK, HG_SUB
    lane = lax.broadcasted_iota(jnp.int32, (sb, C), 1)
    rloc = lax.broadcasted_iota(jnp.int32, (sb, 1), 0)
    rows = []
    for i in range(C // sb):
        sl = slice(sb * i, sb * i + sb)
        Qi, Ki, bi = Q[sl], K[sl], b[sl]
        od = _hg_offdiag(Q, K, b, i, rev)
        if od is None:
            acc = jnp.zeros((sb, C), F32)
        else:
            qs, ks = od
            acc = lax.dot_general(Qi * qs[sl], K * ks, (((1,), (1,)), ((), ())),
                                  precision=HI, preferred_element_type=F32)
        for j in range(sb):
            e = jnp.exp(jnp.minimum(bi - bi[j:j + 1], 0.0))
            col = jnp.sum(Qi * Ki[j:j + 1] * e, axis=-1, keepdims=True)
            vis = (rloc <= j) if rev else (rloc >= j)
            acc = jnp.where(lane == sb * i + j, jnp.where(vis, col, 0.0), acc)
        rows.append(acc)
    return jnp.concatenate(rows, axis=0)


def _hg_att_bwd(Q, K, b, dA, rev):
    C, sb = HG_CHUNK, HG_SUB
    rloc = lax.broadcasted_iota(jnp.int32, (sb, 1), 0)
    rrow = lax.broadcasted_iota(jnp.int32, (sb, HG_D), 0)
    trow = lax.broadcasted_iota(jnp.int32, (C, C), 1) // sb
    dAT = dA.T
    dQ_rows, dKd_rows = [], []
    dK = jnp.zeros((C, HG_D), F32)
    for i in range(C // sb):
        sl = slice(sb * i, sb * i + sb)
        Qi, Ki, bi, dAi = Q[sl], K[sl], b[sl], dA[sl]
        od = _hg_offdiag(Q, K, b, i, rev)
        if od is None:
            dQi = jnp.zeros((sb, HG_D), F32)
        else:
            qs, ks = od
            dQi = lax.dot_general(dAi, K * ks, (((1,), (0,)), ((), ())),
                                  precision=HI, preferred_element_type=F32) * qs[sl]
            zt = jnp.where(trow == i, dAT, 0.0)
            dK = dK + lax.dot_general(zt, Q * qs, (((1,), (0,)), ((), ())),
                                      precision=HI, preferred_element_type=F32) * ks
        dKd = jnp.zeros((sb, HG_D), F32)
        for j in range(sb):
            vis = (rloc <= j) if rev else (rloc >= j)
            e = jnp.where(vis, jnp.exp(jnp.minimum(bi - bi[j:j + 1], 0.0)), 0.0)
            dcol = dAi[:, sb * i + j:sb * i + j + 1]
            dQi = dQi + dcol * Ki[j:j + 1] * e
            krow = jnp.sum(dcol * Qi * e, axis=0, keepdims=True)
            dKd = jnp.where(rrow == j, krow, dKd)
        dQ_rows.append(dQi)
        dKd_rows.append(dKd)
    return jnp.concatenate(dQ_rows, axis=0), dK + jnp.concatenate(dKd_rows, axis=0)


def _hg_prep(qr, fr, lb, tri):
    sigq = _sigmoid(qr)
    Q = qr * sigq
    sig, f = _hg_gate(fr, lb)
    K = 1.0 - f
    logf = jnp.log(f)
    b = lax.dot_general(tri.astype(F32), logf, (((1,), (0,)), ((), ())),
                        precision=HI, preferred_element_type=F32)
    return sigq, Q, sig, f, K, b


def _hgrn_scan(P, lb, *, S, rev, name):
    C = HG_CHUNK
    H = HG_HEADS
    NC = S // C
    fcol = (3 if rev else 2) * H

    def cidx(n):
        return NC - 1 - n if rev else n

    def body(q_ref, v_ref, f_ref, lb_ref, o_ref, st_ref, a_ref, state):
        n = pl.program_id(1)

        @pl.when(n == 0)
        def _():
            state[...] = jnp.zeros_like(state)

        tri = _hg_masks(rev)
        _, Q, _, _, K, b = _hg_prep(q_ref[...], f_ref[...], lb_ref[...], tri)
        V = v_ref[...]
        ST0 = state[...]
        st_ref[...] = ST0
        e_b = jnp.exp(b)
        bE = b[0:1] if rev else b[C - 1:C]
        W = jnp.exp(bE - b)
        inter = lax.dot_general(Q * e_b, ST0, (((1,), (1,)), ((), ())), preferred_element_type=F32)
        A = _hg_att(Q, K, b, rev)
        a_ref[...] = A
        o_ref[...] = inter + jnp.dot(A, V, preferred_element_type=F32)
        state[...] = ST0 * jnp.exp(bE) + lax.dot_general(
            V, K * W, (((0,), (0,)), ((), ())), preferred_element_type=F32)

    blk = lambda c0: pl.BlockSpec((C, HG_D), lambda h, n, c0=c0: (cidx(n), c0 + h))
    return pl.pallas_call(
        body, name=name, grid=(H, NC),
        in_specs=[blk(0), blk(H), blk(fcol), pl.BlockSpec((1, HG_D), lambda h, n: (0, h))],
        out_specs=[pl.BlockSpec((C, HG_D), lambda h, n: (cidx(n), h)),
                   pl.BlockSpec((None, None, HG_D, HG_D), lambda h, n: (cidx(n), h, 0, 0)),
                   pl.BlockSpec((None, None, C, C), lambda h, n: (cidx(n), h, 0, 0))],
        out_shape=[jax.ShapeDtypeStruct((S, H * HG_D), F32),
                   jax.ShapeDtypeStruct((NC, H, HG_D, HG_D), F32),
                   jax.ShapeDtypeStruct((NC, H, C, C), F32)],
        scratch_shapes=[pltpu.VMEM((HG_D, HG_D), F32)],
        compiler_params=_cparams(("parallel", "arbitrary")),
    )(P, P, P, lb)


def _hgrn_scan_bwd(P, lb, st, amat, do, *, S, rev, name):
    C = HG_CHUNK
    H = HG_HEADS
    NC = S // C
    fcol = (3 if rev else 2) * H

    def cidx(n):
        return n if rev else NC - 1 - n

    def body(q_ref, v_ref, f_ref, lb_ref, st_ref, a_ref, do_ref, dq_ref, dv_ref, df_ref, dlb_ref, dstate):
        n = pl.program_id(1)

        @pl.when(n == 0)
        def _():
            dstate[...] = jnp.zeros_like(dstate)
            dlb_ref[...] = jnp.zeros_like(dlb_ref)

        tri = _hg_masks(rev)
        lbv = lb_ref[...]
        qr = q_ref[...]
        sigq, Q, sig, f, K, b = _hg_prep(qr, f_ref[...], lbv, tri)
        V = v_ref[...]
        ST0 = st_ref[...]
        A = a_ref[...]
        dO = do_ref[...]
        dST1 = dstate[...]
        e_b = jnp.exp(b)
        bE = b[0:1] if rev else b[C - 1:C]
        eE = jnp.exp(bE)
        W = jnp.exp(bE - b)
        Qe = Q * e_b
        KW = K * W
        dA = jnp.where(tri, lax.dot_general(dO, V, (((1,), (1,)), ((), ())), preferred_element_type=F32), 0.0)
        dV = (lax.dot_general(A, dO, (((0,), (0,)), ((), ())), preferred_element_type=F32)
              + lax.dot_general(KW, dST1, (((1,), (1,)), ((), ())), preferred_element_type=F32))
        dQe = jnp.dot(dO, ST0, preferred_element_type=F32)
        dKW = jnp.dot(V, dST1, preferred_element_type=F32)
        dstate[...] = dST1 * eE + lax.dot_general(dO, Qe, (((0,), (0,)), ((), ())), preferred_element_type=F32)
        dQa, dKa = _hg_att_bwd(Q, K, b, dA, rev)
        dQ = dQe * e_b + dQa
        dK = dKW * W + dKa
        extra = _colsum(KW * dKW) + eE * _colsum(ST0 * dST1)
        trow = lax.broadcasted_iota(jnp.int32, (C, 1), 0)
        db = Q * dQ - K * dK + jnp.where(trow == (0 if rev else C - 1), extra, 0.0)
        dlogf = lax.dot_general(_hg_masks(not rev).astype(F32), db, (((1,), (0,)), ((), ())),
                                precision=HI, preferred_element_type=F32)
        dfv = dlogf / f - dK
        df_ref[...] = dfv * (1.0 - lbv) * sig * (1.0 - sig)
        dlb_ref[...] += _colsum(dfv * (1.0 - sig))
        dq_ref[...] = dQ * (sigq * (1.0 + qr * (1.0 - sigq)))
        dv_ref[...] = dV

    blk = lambda c0: pl.BlockSpec((C, HG_D), lambda h, n, c0=c0: (cidx(n), c0 + h))
    oblk = pl.BlockSpec((C, HG_D), lambda h, n: (cidx(n), h))
    return pl.pallas_call(
        body, name=name, grid=(H, NC),
        in_specs=[blk(0), blk(H), blk(fcol), pl.BlockSpec((1, HG_D), lambda h, n: (0, h)),
                  pl.BlockSpec((None, None, HG_D, HG_D), lambda h, n: (cidx(n), h, 0, 0)),
                  pl.BlockSpec((None, None, C, C), lambda h, n: (cidx(n), h, 0, 0)),
                  oblk],
        out_specs=[oblk, oblk, oblk, pl.BlockSpec((1, HG_D), lambda h, n: (0, h))],
        out_shape=[jax.ShapeDtypeStruct((S, H * HG_D), F32)] * 3 + [jax.ShapeDtypeStruct((1, H * HG_D), F32)],
        scratch_shapes=[pltpu.VMEM((HG_D, HG_D), F32)],
        compiler_params=_cparams(("parallel", "arbitrary")),
    )(P, P, P, lb, st, amat, do)


LOG2E = 1.4426950408889634
MXU = BF16
ATT_SUB = 512


def _mx(x):
    return x if x.dtype == MXU else x.astype(MXU)


def _attn_fwd(q, k, v, *, S, T, H, dqk, dv, q_col0, k_col0, v_col0, scale, tq, tk, name):
    tq = _tile(S, tq, SUBLANES)
    tk = _tile(T, tk, LANES)
    nk = T // tk
    ts = _tile(tq, ATT_SUB, SUBLANES)

    def body(q_ref, k_ref, v_ref, o_ref, lse_ref, m_s, l_s, acc):
        j = pl.program_id(2)

        @pl.when(j == 0)
        def _():
            m_s[...] = jnp.full_like(m_s, -jnp.inf)
            l_s[...] = jnp.zeros_like(l_s)
            acc[...] = jnp.zeros_like(acc)

        kv, vv = _mx(k_ref[...]), _mx(v_ref[...])
        m_all, l_all, a_all = m_s[...], l_s[...], acc[...]
        ms, ls, accs = [], [], []
        for r0 in range(0, tq, ts):
            rows = slice(r0, r0 + ts)
            s = lax.dot_general(_mx(q_ref[rows, :]), kv, (((1,), (1,)), ((), ())),
                                preferred_element_type=F32) * (scale * LOG2E)
            m_old = m_all[rows]
            m_new = jnp.maximum(m_old, jnp.max(s, axis=-1, keepdims=True))
            corr = jnp.exp2(m_old - m_new)
            p = jnp.exp2(s - m_new)
            ms.append(m_new)
            ls.append(corr * l_all[rows] + jnp.sum(p, axis=-1, keepdims=True))
            accs.append(corr * a_all[rows] + jnp.dot(_mx(p), vv, preferred_element_type=F32))
        m_s[...] = jnp.concatenate(ms, axis=0)
        l_s[...] = jnp.concatenate(ls, axis=0)
        acc[...] = jnp.concatenate(accs, axis=0)

        @pl.when(j == nk - 1)
        def _():
            o_ref[...] = acc[...] / l_s[...]
            lse_ref[...] = (m_s[...] + jnp.log2(l_s[...])) * (1.0 / LOG2E)

    return pl.pallas_call(
        body, name=name, grid=(H, S // tq, nk),
        in_specs=[pl.BlockSpec((tq, dqk), lambda h, i, j: (i, q_col0 + h)),
                  pl.BlockSpec((tk, dqk), lambda h, i, j: (j, k_col0 + h)),
                  pl.BlockSpec((tk, dv), lambda h, i, j: (j, v_col0 + h))],
        out_specs=[pl.BlockSpec((tq, dv), lambda h, i, j: (i, h)),
                   pl.BlockSpec((None, tq, 1), lambda h, i, j: (h, i, 0))],
        out_shape=[jax.ShapeDtypeStruct((S, H * dv), F32), jax.ShapeDtypeStruct((H, S, 1), F32)],
        scratch_shapes=[pltpu.VMEM((tq, 1), F32), pltpu.VMEM((tq, 1), F32), pltpu.VMEM((tq, dv), F32)],
        compiler_params=_cparams(("parallel", "parallel", "arbitrary")),
    )(q, k, v)


def _attn_bwd(q, k, v, o, lse, do, *, S, T, H, dqk, dv, q_col0, k_col0, v_col0, scale, tq, tk, exch=None, name):
    tq = _tile(S, tq, SUBLANES)
    tk = _tile(T, tk, LANES)
    nq = S // tq
    ts = _tile(tq, ATT_SUB, SUBLANES)

    def body(q_ref, k_ref, v_ref, o_ref, lse_ref, do_ref, dq_ref, dk_ref, dv_ref, dk_acc, dv_acc):
        j = pl.program_id(1)
        i = pl.program_id(2)

        @pl.when(jnp.logical_and(i == 0, j == 0))
        def _():
            dq_ref[...] = jnp.zeros_like(dq_ref)

        @pl.when(i == 0)
        def _():
            dk_acc[...] = jnp.zeros_like(dk_acc)
            dv_acc[...] = jnp.zeros_like(dv_acc)

        kv, vv = _mx(k_ref[...]), _mx(v_ref[...])
        dk_new, dv_new = dk_acc[...], dv_acc[...]
        lse2 = lse_ref[...] * LOG2E
        dqs = []
        for r0 in range(0, tq, ts):
            rows = slice(r0, r0 + ts)
            qv, dov = _mx(q_ref[rows, :]), do_ref[rows, :]
            s = lax.dot_general(qv, kv, (((1,), (1,)), ((), ())), preferred_element_type=F32) * (scale * LOG2E)
            p = jnp.exp2(s - lse2[rows])
            delta = jnp.sum(dov * o_ref[rows, :], axis=-1, keepdims=True)
            dob = _mx(dov)
            dp = lax.dot_general(dob, vv, (((1,), (1,)), ((), ())), preferred_element_type=F32)
            ds = _mx(p * (dp - delta) * scale)
            dv_new = dv_new + lax.dot_general(_mx(p), dob, (((0,), (0,)), ((), ())), preferred_element_type=F32)
            dk_new = dk_new + lax.dot_general(ds, qv, (((0,), (0,)), ((), ())), preferred_element_type=F32)
            dqs.append(jnp.dot(ds, kv, preferred_element_type=F32))
        dq_ref[pl.ds(pl.multiple_of(i * tq, tq), tq), :] += jnp.concatenate(dqs, axis=0)
        dk_acc[...] = dk_new
        dv_acc[...] = dv_new

        @pl.when(i == nq - 1)
        def _():
            dk_ref[...] = dk_new
            dv_ref[...] = dv_new

    outs, received = _carried(
        body, exch, (H, T // tk, nq),
        [pl.BlockSpec((tq, dqk), lambda h, j, i: (i, q_col0 + h)),
         pl.BlockSpec((tk, dqk), lambda h, j, i: (j, k_col0 + h)),
         pl.BlockSpec((tk, dv), lambda h, j, i: (j, v_col0 + h)),
         pl.BlockSpec((tq, dv), lambda h, j, i: (i, h)),
         pl.BlockSpec((None, tq, 1), lambda h, j, i: (h, i, 0)),
         pl.BlockSpec((tq, dv), lambda h, j, i: (i, h))],
        [pl.BlockSpec((S, dqk), lambda h, j, i: (0, h)),
         pl.BlockSpec((tk, dqk), lambda h, j, i: (j, h)),
         pl.BlockSpec((tk, dv), lambda h, j, i: (j, h))],
        [jax.ShapeDtypeStruct((S, H * dqk), F32), jax.ShapeDtypeStruct((T, H * dqk), F32),
         jax.ShapeDtypeStruct((T, H * dv), F32)],
        [pltpu.VMEM((tk, dqk), F32), pltpu.VMEM((tk, dv), F32)], [q, k, v, o, lse, do], name=name)
    return outs if exch is None else (outs, received)


def _rope_tables(positions):
    half = MLA_ROPE // 2
    inv_freq = jnp.power(ROPE_THETA, -jnp.arange(half, dtype=F32) / half)
    ang = positions.astype(F32)[:, None] * inv_freq
    cos, sin = jnp.cos(ang), jnp.sin(ang)
    z = jnp.zeros_like(cos)
    tc = jnp.concatenate([cos, cos, z, z], axis=1)
    ta = jnp.concatenate([-sin, z, z, z], axis=1)
    tb = jnp.concatenate([z, sin, z, z], axis=1)
    return tc, ta, tb


def _rope_apply(v, tc, ta, tb):
    half = MLA_ROPE // 2
    return v * tc + pltpu.roll(v, LANES - half, 1) * ta + pltpu.roll(v, half, 1) * tb


def _rope_apply_t(d, tc, ta, tb):
    half = MLA_ROPE // 2
    return d * tc + pltpu.roll(d * ta, half, 1) + pltpu.roll(d * tb, LANES - half, 1)


def _local_step(x, mem, positions, loss_target, w_in_k, small, W, MW, DFF, comm=None):
    S, D = x.shape
    M = mem.shape[0]
    HW = HG_HEADS * HG_D
    QR = small["mla_g_cq"].shape[1]
    KR = small["mla_g_ckv"].shape[1]
    MHD = MW // MEM_HEADS
    QW = MLA_HEADS * 2 * LANES
    VW = MLA_HEADS * MLA_V
    c_hg, c_cq, c_ckv, c_qm, c_gate = 0, 5 * HW, 5 * HW + QR, 5 * HW + QR + KR, 5 * HW + QR + KR + MW
    c_kr = c_gate + N_BRANCH * D
    PW = c_kr + LANES
    assert w_in_k.shape == (D, PW)
    TR = 256
    row = lambda a: a.reshape(1, -1)
    ge, be = row(small["ln_emb_g"]), row(small["ln_emb_b"])
    g1, b1, g2, b2 = small["ln1_g"], small["ln1_b"], small["ln2_g"], small["ln2_b"]
    lb = small["lb"]
    tc, ta, tb = _rope_tables(positions)

    (h0,) = _rowwise(lambda z, g, b: _ln_stats(z)[0] * g + b, [_Rows(x, D)], [ge, be],
                     [_Rows(D, D)], [], R=S, tr=TR, name="ln_emb")
    if comm is None:
        P = _mm(h0, w_in_k, S, PW, D, tn=896, name="proj_in")
    else:
        P, got = _mm(h0, w_in_k, S, PW, D, tn=896, exch=comm.gather_rest, name="proj_in")
        W = comm.weights(got)

    o_fw, st_fw, a_fw = _hgrn_scan(P, lb[0:1], S=S, rev=False, name="hgrn_fw")
    o_bw, st_bw, a_bw = _hgrn_scan(P, lb[1:2], S=S, rev=True, name="hgrn_bw")

    def hg_post(of, ob, gr, ng):
        o = of + ob
        sg = _sigmoid(gr)
        outs = []
        for h in range(HG_HEADS):
            sl = slice(h * HG_D, (h + 1) * HG_D)
            outs.append(_rms_fwd(o[:, sl], ng, RMS_EPS) * sg[:, sl])
        return jnp.concatenate(outs, axis=1)

    (y_hg,) = _rowwise(hg_post, [_Rows(o_fw, HW), _Rows(o_bw, HW), _Rows(P, HW, 4)], [small["hgrn_norm_g"]],
                       [_Rows(HW, HW)], [], R=S, tr=TR, name="hgrn_post")

    def mla_norm(cq, ckv, gq, gk):
        return _rms_fwd(cq, gq, RMS_EPS), _rms_fwd(ckv, gk, RMS_EPS)

    assert c_cq % QR == 0 and c_ckv % KR == 0
    cqn, ckvn = _rowwise(mla_norm, [_Rows(P, QR, c_cq // QR), _Rows(P, KR, c_ckv // KR)],
                         [small["mla_g_cq"], small["mla_g_ckv"]],
                         [_Rows(QR, QR), _Rows(KR, KR)], [], R=S, tr=TR, name="mla_norm")
    q_raw = _mm(cqn, W["mla_w_uq"], S, QW, QR, name="mla_uq")
    kv = _mm(ckvn, W["mla_w_ukv"], S, 2 * VW, KR, name="mla_ukv")

    def rope_fwd(qb, knb, vb_, krb, tcb, tab, tbb):
        kr = _rope_apply(krb, tcb, tab, tbb)
        qo, ko = [], []
        for h in range(MLA_HEADS):
            qo += [qb[:, 2 * h * LANES:(2 * h + 1) * LANES],
                   _rope_apply(qb[:, (2 * h + 1) * LANES:(2 * h + 2) * LANES], tcb, tab, tbb)]
            ko += [knb[:, h * LANES:(h + 1) * LANES], kr]
        return jnp.concatenate(qo, axis=1), jnp.concatenate(ko, axis=1), vb_

    qc, kc, vc = _rowwise(rope_fwd, [_Rows(q_raw, QW), _Rows(kv, VW), _Rows(kv, VW, 1), _Rows(P, LANES, c_kr // LANES),
                                     _Rows(tc, LANES), _Rows(ta, LANES), _Rows(tb, LANES)], [],
                          [_Rows(QW, QW, dtype=MXU), _Rows(QW, QW, dtype=MXU), _Rows(VW, VW, dtype=MXU)], [],
                          R=S, tr=TR, name="rope_fwd")
    mla_kw = dict(S=S, T=S, H=MLA_HEADS, dqk=2 * LANES, dv=MLA_V, q_col0=0, k_col0=0, v_col0=0,
                  scale=(MLA_NOPE + MLA_ROPE) ** -0.5, tq=1024, tk=512)
    y_mla, lse_mla = _attn_fwd(qc, kc, vc, name="mla_attn", **mla_kw)

    kvm = _mm(mem, W["mem_w_kv"], M, 2 * MW, D, name="mem_kv")
    mem_kw = dict(S=S, T=M, H=MEM_HEADS, dqk=MHD, dv=MHD, q_col0=c_qm // MHD, k_col0=0, v_col0=MEM_HEADS,
                  scale=MHD ** -0.5, tq=1024, tk=M)
    assert c_qm % MHD == 0
    y_mem, lse_mem = _attn_fwd(P, kvm, kvm, name="mem_attn", **mem_kw)

    ys = (y_hg, y_mla, y_mem)
    us = [_mm(ys[b], W["w_branch"][b], S, D, HW, name=f"branch{b}") for b in range(N_BRANCH)]
    TCW = _tile(D, 1024, LANES)
    ncw = D // TCW

    def merge_fwd(g0, g1_, g2_, u0, u1, u2):
        return _sigmoid(g0) * u0 + _sigmoid(g1_) * u1 + _sigmoid(g2_) * u2

    gate_rows = [_Rows(P, TCW, (c_gate + b * D) // TCW) for b in range(N_BRANCH)]
    assert c_gate % TCW == 0
    (merged,) = _rowwise(merge_fwd, gate_rows + [_Rows(u, TCW) for u in us], [],
                         [_Rows(D, TCW)], [], R=S, tr=TR, ncol=ncw, name="merge_fwd")
    mix = _mm(merged, W["w_o"], S, D, D, name="out_proj")

    def ln_res(hp, addv, g, b):
        z = ALPHA * hp + addv
        return z, _ln_stats(z)[0] * g + b

    z1, h1 = _rowwise(ln_res, [_Rows(h0, D), _Rows(mix, D)], [g1, b1],
                      [_Rows(D, D), _Rows(D, D)], [], R=S, tr=TR, name="ln1")

    ab = _mm(h1, W["w_ffn_gu"], S, 2 * DFF, D, name="ffn_gu")
    TF = _tile(DFF, 512, LANES)
    nf = DFF // TF

    def swiglu(abv):
        a, b = abv[:, :TF], abv[:, TF:]
        return a * _sigmoid(a) * b

    (cff,) = _rowwise(swiglu, [_Rows(ab, 2 * TF)], [], [_Rows(DFF, TF)], [],
                      R=S, tr=TR, ncol=nf, name="swiglu")
    ff = _mm(cff, W["w_ffn_down"], S, D, DFF, name="ffn_down")

    def loss_bwd(hp, addv, tgt, g, b):
        z = ALPHA * hp + addv
        xhat, rstd = _ln_stats(z)
        y = xhat * g + b
        err = y - tgt
        dy = err * (1.0 / D)
        dxh = dy * g
        m1 = jnp.mean(dxh, axis=-1, keepdims=True)
        m2 = jnp.mean(dxh * xhat, axis=-1, keepdims=True)
        dz = rstd * (dxh - m1 - xhat * m2)
        lrow = jnp.sum(_colsum(err * err), axis=-1, keepdims=True) * (0.5 / D)
        return dz, _colsum(dy * xhat), _colsum(dy), lrow

    dz2, dg2, db2, loss = _rowwise(loss_bwd, [_Rows(h1, D), _Rows(ff, D), _Rows(loss_target, D)], [g2, b2],
                                   [_Rows(D, D)], [D, D, 1], R=S, tr=TR, name="loss_ln2_bwd")
    dcff = _mm(dz2, W["w_ffn_down"], S, DFF, D, tb=True, name="ffn_down_dx")
    g_ffn_down = _mm(cff, dz2, DFF, D, S, ta=True, name="ffn_down_dw")

    def swiglu_bwd(abv, dc):
        a, b = abv[:, :TF], abv[:, TF:]
        sg = _sigmoid(a)
        return jnp.concatenate([dc * b * sg * (1.0 + a * (1.0 - sg)), dc * a * sg], axis=1)

    (dab,) = _rowwise(swiglu_bwd, [_Rows(ab, 2 * TF), _Rows(dcff, TF)], [], [_Rows(2 * DFF, 2 * TF)], [],
                      R=S, tr=TR, ncol=nf, name="swiglu_bwd")
    dh1 = _mm(dab, W["w_ffn_gu"], S, D, 2 * DFF, tb=True, name="ffn_gu_dx")
    g_ffn_gu = _mm(h1, dab, D, 2 * DFF, S, ta=True, name="ffn_gu_dw")

    def ln1_bwd(z, dmm, dz2v, g):
        return _ln_bwd_core(z, g, ALPHA * dz2v + dmm)

    dz1, dg1, db1 = _rowwise(ln1_bwd, [_Rows(z1, D), _Rows(dh1, D), _Rows(dz2, D)], [g1],
                             [_Rows(D, D)], [D, D], R=S, tr=TR, name="ln1_bwd")
    dmerged = _mm(dz1, W["w_o"], S, D, D, tb=True, name="out_proj_dx")
    g_w_o = _mm(merged, dz1, D, D, S, ta=True, name="out_proj_dw")

    def merge_bwd(g0, g1_, g2_, u0, u1, u2, dm):
        res_g, res_u = [], []
        for gv, uv in ((g0, u0), (g1_, u1), (g2_, u2)):
            sg = _sigmoid(gv)
            res_g.append(dm * uv * sg * (1.0 - sg))
            res_u.append(dm * sg)
        return (*res_g, *res_u)

    mres = _rowwise(merge_bwd, gate_rows + [_Rows(u, TCW) for u in us] + [_Rows(dmerged, TCW)], [],
                    [_Rows(D, TCW)] * (2 * N_BRANCH), [], R=S, tr=TR, ncol=ncw, name="merge_bwd")
    dgates, dus = mres[:N_BRANCH], mres[N_BRANCH:]
    dys = [_mm(dus[b], W["w_branch"][b], S, HW, D, tb=True, name=f"branch{b}_dx") for b in range(N_BRANCH)]
    g_w_branch = [_mm(ys[b], dus[b], HW, D, S, ta=True, name=f"branch{b}_dw") for b in range(N_BRANCH)]

    dq_mem, dk_mem, dv_mem = _attn_bwd(P, kvm, kvm, y_mem, lse_mem, dys[2], name="mem_attn_bwd", **mem_kw)
    dkvm = jnp.concatenate([dk_mem, dv_mem], axis=1)
    g_mem_w_kv = _mm(mem, dkvm, D, 2 * MW, M, ta=True, name="mem_kv_dw")

    g_w_branch = jnp.stack(g_w_branch)
    if comm is None:
        early = None
        dqc, dkc, dvv = _attn_bwd(qc, kc, vc, y_mla, lse_mla, dys[1], name="mla_attn_bwd", **mla_kw)
    else:
        exch = comm.scatter(dict(w_ffn_gu=g_ffn_gu, w_ffn_down=g_ffn_down, w_o=g_w_o, w_branch=g_w_branch,
                                 mem_w_kv=g_mem_w_kv))
        (dqc, dkc, dvv), early = _attn_bwd(qc, kc, vc, y_mla, lse_mla, dys[1], exch=exch, name="mla_attn_bwd",
                                           **mla_kw)

    def rope_bwd(dqb, dkb, tcb, tab, tbb):
        qo, kn = [], []
        dkr = jnp.zeros_like(tcb)
        for h in range(MLA_HEADS):
            qo += [dqb[:, 2 * h * LANES:(2 * h + 1) * LANES],
                   _rope_apply_t(dqb[:, (2 * h + 1) * LANES:(2 * h + 2) * LANES], tcb, tab, tbb)]
            kn.append(dkb[:, 2 * h * LANES:(2 * h + 1) * LANES])
            dkr = dkr + dkb[:, (2 * h + 1) * LANES:(2 * h + 2) * LANES]
        return jnp.concatenate(qo, axis=1), jnp.concatenate(kn, axis=1), _rope_apply_t(dkr, tcb, tab, tbb)

    dq_raw, dkn, dkr_raw = _rowwise(rope_bwd, [_Rows(dqc, QW), _Rows(dkc, QW), _Rows(tc, LANES),
                                               _Rows(ta, LANES), _Rows(tb, LANES)], [],
                                    [_Rows(QW, QW), _Rows(VW, VW), _Rows(LANES, LANES)], [],
                                    R=S, tr=TR, name="rope_bwd")
    dkv = jnp.concatenate([dkn, dvv], axis=1)
    dcqn = _mm(dq_raw, W["mla_w_uq"], S, QR, QW, tb=True, name="mla_uq_dx")
    g_mla_w_uq = _mm(cqn, dq_raw, QR, QW, S, ta=True, name="mla_uq_dw")
    dckvn = _mm(dkv, W["mla_w_ukv"], S, KR, 2 * VW, tb=True, name="mla_ukv_dx")
    g_mla_w_ukv = _mm(ckvn, dkv, KR, 2 * VW, S, ta=True, name="mla_ukv_dw")

    def mla_norm_bwd(cq, ckv, dq_, dk_, gq, gk):
        dcq, gq_rows = _rms_bwd(cq, gq, dq_, RMS_EPS)
        dck, gk_rows = _rms_bwd(ckv, gk, dk_, RMS_EPS)
        return dcq, dck, _colsum(gq_rows), _colsum(gk_rows)

    dcq, dckv, dg_cq, dg_ckv = _rowwise(
        mla_norm_bwd, [_Rows(P, QR, c_cq // QR), _Rows(P, KR, c_ckv // KR), _Rows(dcqn, QR), _Rows(dckvn, KR)],
        [small["mla_g_cq"], small["mla_g_ckv"]], [_Rows(QR, QR), _Rows(KR, KR)], [QR, KR],
        R=S, tr=TR, name="mla_norm_bwd")

    def hg_post_bwd(of, ob, gr, dy, ng):
        o = of + ob
        sg = _sigmoid(gr)
        do_, dgr = [], []
        dng = jnp.zeros((1, HG_D), F32)
        for h in range(HG_HEADS):
            sl = slice(h * HG_D, (h + 1) * HG_D)
            t = _rms_fwd(o[:, sl], ng, RMS_EPS)
            dgr.append(dy[:, sl] * t * sg[:, sl] * (1.0 - sg[:, sl]))
            dx, grow = _rms_bwd(o[:, sl], ng, dy[:, sl] * sg[:, sl], RMS_EPS)
            do_.append(dx)
            dng = dng + _colsum(grow)
        return jnp.concatenate(do_, axis=1), jnp.concatenate(dgr, axis=1), dng

    do_hg, dg_hg, dng = _rowwise(hg_post_bwd, [_Rows(o_fw, HW), _Rows(o_bw, HW), _Rows(P, HW, 4), _Rows(dys[0], HW)],
                                 [small["hgrn_norm_g"]], [_Rows(HW, HW), _Rows(HW, HW)], [HG_D],
                                 R=S, tr=TR, name="hgrn_post_bwd")
    dq_f, dv_f, dff_fw, dlb_f = _hgrn_scan_bwd(P, lb[0:1], st_fw, a_fw, do_hg, S=S, rev=False, name="hgrn_fw_bwd")
    dq_b, dv_b, dff_bw, dlb_b = _hgrn_scan_bwd(P, lb[1:2], st_bw, a_bw, do_hg, S=S, rev=True, name="hgrn_bw_bwd")
    THW = _tile(HW, 1024, LANES)
    dq_hg, dv_hg = _rowwise(lambda a, b, c, d: (a + b, c + d),
                            [_Rows(dq_f, THW), _Rows(dq_b, THW), _Rows(dv_f, THW), _Rows(dv_b, THW)], [],
                            [_Rows(HW, THW), _Rows(HW, THW)], [], R=S, tr=TR, ncol=HW // THW, name="hgrn_dir_sum")

    dP = jnp.concatenate([dq_hg, dv_hg, dff_fw, dff_bw, dg_hg, dcq, dckv, dq_mem, *dgates, dkr_raw], axis=1)
    dh0 = _mm(dP, w_in_k, S, D, PW, tb=True, tk=_tile(PW, 640, LANES), name="proj_in_dx")
    g_w_in = _mm(h0, dP, D, PW, S, ta=True, tn=896, name="proj_in_dw")

    def ln0_bwd(z, dmm, dz1v, g):
        return _ln_bwd_core(z, g, ALPHA * dz1v + dmm)

    grad_x, dge, dbe = _rowwise(ln0_bwd, [_Rows(x, D), _Rows(dh0, D), _Rows(dz1, D)], [ge],
                                [_Rows(D, D)], [D, D], R=S, tr=TR, name="ln_emb_bwd")

    big = dict(w_in=g_w_in, mla_w_uq=g_mla_w_uq, mla_w_ukv=g_mla_w_ukv, mem_w_kv=g_mem_w_kv,
               w_branch=g_w_branch, w_o=g_w_o, w_ffn_gu=g_ffn_gu, w_ffn_down=g_ffn_down)
    sm = dict(ln_emb_g=dge, ln_emb_b=dbe, dlb=jnp.concatenate([dlb_f, dlb_b], axis=0), hgrn_norm_g=dng,
              mla_g_cq=dg_cq, mla_g_ckv=dg_ckv, ln1_g=dg1, ln1_b=db1, ln2_g=dg2, ln2_b=db2)
    return loss, grad_x, big, sm, early


def _gather_exch(shards):
    copies = [(m, lambda x, y, c: 0, _chip) for m in _CHIP_MASKS]
    return _Exch([s[None] for s in shards], 4, copies, [(lambda x, y, c: 0, _chip)])


def _scatter_exch(pieces):
    copies = [(m, (lambda x, y, c, m=m: 2 * _chip(x ^ m[0], y ^ m[1], c) + (c ^ m[2])), _device) for m in _ALL_MASKS]
    local = [((lambda x, y, c: 2 * _chip(x, y, c) + c), _device)]
    return _Exch([p.reshape((8,) + p.shape[2:]) for p in pieces], 8, copies, local)


SHARE_BLOCK_BYTES = 4 << 20


def _sum_share(arr, *, name):
    n, rh, w = arr.shape
    tr = _tile(rh, max(16, SHARE_BLOCK_BYTES // (n * w * arr.dtype.itemsize) // 16 * 16), 16)
    nb = rh // tr

    def body(a_ref, o_ref, slots, send_sems, recv_sem, local_sems):
        i = pl.program_id(0)
        x, y, c = _coords()
        sibling = (x, y, 1 - c)

        def pushes(step, slot):
            rows = pl.ds(pl.multiple_of(c * rh + step * tr, SUBLANES), tr)
            return (pltpu.make_async_copy(slots.at[slot], o_ref.at[rows], local_sems.at[slot]),
                    pltpu.make_async_remote_copy(src_ref=slots.at[slot], dst_ref=o_ref.at[rows],
                                                 send_sem=send_sems.at[slot], recv_sem=recv_sem,
                                                 device_id=sibling, device_id_type=MESH))

        def drain(step, slot):
            loc, rem = pushes(step, slot)
            loc.wait()
            rem.wait_send()

        slot = i % 2

        @pl.when(i >= 2)
        def _():
            drain(i - 2, slot)

        acc = a_ref[0].astype(F32)
        for k in range(1, n):
            acc = acc + a_ref[k].astype(F32)
        slots[slot] = acc
        loc, rem = pushes(i, slot)
        loc.start()
        rem.start()

        @pl.when(i == nb - 1)
        def _():
            if nb >= 2:
                drain(i - 1, 1 - slot)
            drain(i, slot)
            other = o_ref.at[pl.ds(pl.multiple_of((1 - c) * rh, SUBLANES), rh)]
            pltpu.make_async_remote_copy(src_ref=other, dst_ref=other, send_sem=send_sems.at[0], recv_sem=recv_sem,
                                         device_id=sibling, device_id_type=MESH).wait_recv()

    return pl.pallas_call(
        body, name=name, grid=(nb,),
        in_specs=[pl.BlockSpec((n, tr, w), lambda i: (0, i, 0))],
        out_specs=pl.BlockSpec(memory_space=pl.ANY),
        out_shape=jax.ShapeDtypeStruct((2 * rh, w), F32),
        scratch_shapes=[pltpu.VMEM((2, tr, w), F32), pltpu.SemaphoreType.DMA((2,)), pltpu.SemaphoreType.DMA,
                        pltpu.SemaphoreType.DMA((2,))],
        compiler_params=pltpu.CompilerParams(dimension_semantics=("arbitrary",), has_side_effects=True,
                                             vmem_limit_bytes=VMEM_LIMIT),
    )(arr)


def _allreduce_small(v, *, name):
    r, w = v.shape

    def body(v_ref, o_ref, buf, send_sems, recv_sems):
        x, y, c = _coords()
        me = 4 * x + 2 * y + c
        buf[me] = v_ref[...]
        cps = []
        for k in range(7):
            m = ((k + 1) >> 2 & 1, (k + 1) >> 1 & 1, (k + 1) & 1)
            cp = pltpu.make_async_remote_copy(
                src_ref=v_ref, dst_ref=buf.at[me], send_sem=send_sems.at[k], recv_sem=recv_sems.at[k],
                device_id=(x ^ m[0], y ^ m[1], c ^ m[2]), device_id_type=MESH)
            cp.start()
            cps.append(cp)
        for cp in cps:
            cp.wait_recv()
        for cp in cps:
            cp.wait_send()
        acc = buf[0]
        for k in range(1, 8):
            acc = acc + buf[k]
        o_ref[...] = acc

    return pl.pallas_call(
        body, name=name,
        in_specs=[pl.BlockSpec(memory_space=pltpu.VMEM)],
        out_specs=pl.BlockSpec(memory_space=pltpu.VMEM),
        out_shape=jax.ShapeDtypeStruct((r, w), F32),
        scratch_shapes=[pltpu.VMEM((8, r, w), F32), pltpu.SemaphoreType.DMA((7,)), pltpu.SemaphoreType.DMA((7,))],
        compiler_params=pltpu.CompilerParams(has_side_effects=True),
    )(v)


_BIG = (("w_in", 1), ("mla_w_uq", 1), ("mla_w_ukv", 1), ("mem_w_kv", 0), ("w_branch", 1), ("w_o", 0),
        ("w_ffn_gate", 1), ("w_ffn_up", 1), ("w_ffn_down", 0))


def _assemble(gathered, ax):
    _, r, c = gathered.shape
    if ax == 0:
        return gathered.reshape(4 * r, c)
    return jnp.concatenate([gathered[j] for j in range(4)], axis=1)


def _split_pieces(g, ax):
    r, c = g.shape
    if ax == 0:
        return g.reshape(4, 2, r // 8, c).astype(BF16)
    rh, cs = r // 2, c // 4
    return jnp.stack([g[h * rh:(h + 1) * rh, j * cs:(j + 1) * cs].astype(BF16)
                      for j in range(4) for h in range(2)]).reshape(4, 2, rh, cs)


def _pad_cols(a, n):
    return jnp.pad(a, ((0, 0), (0, n - a.shape[1])))


def _to_kernel_layout(full, QR, KR):
    out = {}
    for n in ("mem_w_kv", "w_branch", "w_o", "w_ffn_down"):
        if n in full:
            out[n] = full[n]
    if "w_in" in full:
        w_in = full["w_in"]
        a = 5 * HG_HEADS * HG_D + QR + KR
        out["w_in"] = jnp.concatenate([w_in[:, :a], w_in[:, a + MLA_ROPE:], _pad_cols(w_in[:, a:a + MLA_ROPE], LANES)],
                                      axis=1)
    if "mla_w_uq" in full:
        uq = full["mla_w_uq"].reshape(QR, MLA_HEADS, MLA_NOPE + MLA_ROPE)
        out["mla_w_uq"] = jnp.pad(uq, ((0, 0), (0, 0), (0, 2 * LANES - MLA_NOPE - MLA_ROPE))).reshape(QR, -1)
    if "mla_w_ukv" in full:
        ukv = full["mla_w_ukv"].reshape(KR, MLA_HEADS, MLA_NOPE + MLA_V)
        out["mla_w_ukv"] = jnp.concatenate([ukv[:, :, :MLA_NOPE].reshape(KR, -1), ukv[:, :, MLA_NOPE:].reshape(KR, -1)],
                                           axis=1)
    if "w_ffn_gate" in full:
        gate, up = full["w_ffn_gate"], full["w_ffn_up"]
        DFF = gate.shape[1]
        TF = _tile(DFF, 512, LANES)
        blocks = []
        for j in range(DFF // TF):
            blocks += [gate[:, j * TF:(j + 1) * TF], up[:, j * TF:(j + 1) * TF]]
        out["w_ffn_gu"] = jnp.concatenate(blocks, axis=1)
    return out


def _from_kernel_layout(gk, QR, KR):
    out = {}
    for n in ("mem_w_kv", "w_o", "w_ffn_down"):
        if n in gk:
            out[n] = gk[n]
    if "w_branch" in gk:
        out["w_branch"] = gk["w_branch"].reshape(-1, gk["w_branch"].shape[-1])
    if "w_in" in gk:
        g = gk["w_in"]
        a = 5 * HG_HEADS * HG_D + QR + KR
        rest = g.shape[1] - LANES - a
        out["w_in"] = jnp.concatenate([g[:, :a], g[:, a + rest:a + rest + MLA_ROPE], g[:, a:a + rest]], axis=1)
    if "mla_w_uq" in gk:
        out["mla_w_uq"] = gk["mla_w_uq"].reshape(QR, MLA_HEADS, 2 * LANES)[:, :, :MLA_NOPE + MLA_ROPE].reshape(QR, -1)
    if "mla_w_ukv" in gk:
        VW = MLA_HEADS * MLA_V
        g = gk["mla_w_ukv"]
        out["mla_w_ukv"] = jnp.concatenate([g[:, :VW].reshape(KR, MLA_HEADS, MLA_NOPE),
                                            g[:, VW:].reshape(KR, MLA_HEADS, MLA_V)], axis=2).reshape(KR, -1)
    if "w_ffn_gu" in gk:
        g = gk["w_ffn_gu"]
        DFF = g.shape[1] // 2
        TF = _tile(DFF, 512, LANES)
        out["w_ffn_gate"] = jnp.concatenate([g[:, 2 * j * TF:(2 * j + 1) * TF] for j in range(DFF // TF)], axis=1)
        out["w_ffn_up"] = jnp.concatenate([g[:, (2 * j + 1) * TF:(2 * j + 2) * TF] for j in range(DFF // TF)], axis=1)
    return out


def _adamw(w, g, m, v, *, name):
    r, c = w.shape
    tr = max(SUBLANES, min(512, (1 << 20) // (4 * c)) // SUBLANES * SUBLANES)
    c1 = 1.0 / (1.0 - ADAM_B1 ** ADAM_STEP)
    c2 = 1.0 / (1.0 - ADAM_B2 ** ADAM_STEP)

    def fn(wv, gv, mv, vv):
        mn = ADAM_B1 * mv + (1.0 - ADAM_B1) * gv
        vn = ADAM_B2 * vv + (1.0 - ADAM_B2) * (gv * gv)
        delta = -ADAM_LR * ((mn * c1) / (jnp.sqrt(vn * c2) + ADAM_EPS) + ADAM_WD * wv)
        return delta, mn, vn

    return _rowwise(fn, [_Rows(a, c) for a in (w, g, m, v)], [], [_Rows(c, c)] * 3, [], R=r, tr=tr, name=name)


_SMALL = ("ln_emb_g", "ln_emb_b", "hgrn_lb_logits", "hgrn_norm_g", "mla_g_cq", "mla_g_ckv",
          "ln1_g", "ln1_b", "ln2_g", "ln2_b")


def _lb_from_logits(logits):
    return jnp.cumsum(jax.nn.softmax(logits, axis=1), axis=1)[:, 0]


def _small_rows(parts):
    flat = jnp.concatenate([p.reshape(-1) for p in parts])
    n = flat.shape[0]
    total = -(-n // (SUBLANES * LANES)) * SUBLANES * LANES
    return jnp.pad(flat, (0, total - n)).reshape(total // LANES, LANES)


def kernel(x, mem, positions, ln_emb_g, ln_emb_b, hgrn_lb_logits, w_in, hgrn_norm_g, mla_g_cq, mla_g_ckv, mla_w_uq, mla_w_ukv, mem_w_kv, w_branch, w_o, ln1_g, ln1_b, w_ffn_gate, w_ffn_up, w_ffn_down, ln2_g, ln2_b, loss_target, m_ln_emb_g, m_ln_emb_b, m_hgrn_lb_logits, m_w_in, m_hgrn_norm_g, m_mla_g_cq, m_mla_g_ckv, m_mla_w_uq, m_mla_w_ukv, m_mem_w_kv, m_w_branch, m_w_o, m_ln1_g, m_ln1_b, m_w_ffn_gate, m_w_ffn_up, m_w_ffn_down, m_ln2_g, m_ln2_b, v_ln_emb_g, v_ln_emb_b, v_hgrn_lb_logits, v_w_in, v_hgrn_norm_g, v_mla_g_cq, v_mla_g_ckv, v_mla_w_uq, v_mla_w_ukv, v_mem_w_kv, v_w_branch, v_w_o, v_ln1_g, v_ln1_b, v_w_ffn_gate, v_w_ffn_up, v_w_ffn_down, v_ln2_g, v_ln2_b):
    names = ["ln_emb_g", "ln_emb_b", "hgrn_lb_logits", "w_in", "hgrn_norm_g", "mla_g_cq", "mla_g_ckv", "mla_w_uq",
             "mla_w_ukv", "mem_w_kv", "w_branch", "w_o", "ln1_g", "ln1_b", "w_ffn_gate", "w_ffn_up", "w_ffn_down",
             "ln2_g", "ln2_b"]
    wts = dict(zip(names, [ln_emb_g, ln_emb_b, hgrn_lb_logits, w_in, hgrn_norm_g, mla_g_cq, mla_g_ckv, mla_w_uq,
                           mla_w_ukv, mem_w_kv, w_branch, w_o, ln1_g, ln1_b, w_ffn_gate, w_ffn_up, w_ffn_down,
                           ln2_g, ln2_b]))
    mom = dict(zip(names, [m_ln_emb_g, m_ln_emb_b, m_hgrn_lb_logits, m_w_in, m_hgrn_norm_g, m_mla_g_cq, m_mla_g_ckv,
                           m_mla_w_uq, m_mla_w_ukv, m_mem_w_kv, m_w_branch, m_w_o, m_ln1_g, m_ln1_b, m_w_ffn_gate,
                           m_w_ffn_up, m_w_ffn_down, m_ln2_g, m_ln2_b]))
    var = dict(zip(names, [v_ln_emb_g, v_ln_emb_b, v_hgrn_lb_logits, v_w_in, v_hgrn_norm_g, v_mla_g_cq, v_mla_g_ckv,
                           v_mla_w_uq, v_mla_w_ukv, v_mem_w_kv, v_w_branch, v_w_o, v_ln1_g, v_ln1_b, v_w_ffn_gate,
                           v_w_ffn_up, v_w_ffn_down, v_ln2_g, v_ln2_b]))
    xc, yc, cc = _coords()
    chip = _chip(xc, yc, cc)
    S, D = x.shape[1], x.shape[2]

    axis = dict(_BIG)
    shard = lambda n: wts[n].reshape(-1, wts[n].shape[-1]).astype(BF16)
    QR, KR = mla_w_uq.shape[1], mla_w_ukv.shape[1]
    MW, DFF = mem_w_kv.shape[2] // 2, 4 * w_ffn_gate.shape[2]
    (w_in_all,) = _exchange(_gather_exch([shard("w_in")]), name="gather_w_in")
    w_in_k = _to_kernel_layout(dict(w_in=_assemble(w_in_all, axis["w_in"])), QR, KR)["w_in"]
    rest_names = [n for n, _ in _BIG if n != "w_in"]

    class _Comm:
        gather_rest = _gather_exch([shard(n) for n in rest_names])

        @staticmethod
        def weights(received):
            full = {n: _assemble(g, axis[n]) for n, g in zip(rest_names, received)}
            full["w_branch"] = full["w_branch"].reshape(N_BRANCH, -1, D)
            return _to_kernel_layout(full, QR, KR)

        early_names = []

        @staticmethod
        def scatter(gk_part):
            gpart = _from_kernel_layout(gk_part, QR, KR)
            _Comm.early_names[:] = list(gpart)
            return _scatter_exch([_split_pieces(gpart[n], axis[n]) for n in gpart])

    lsh = hgrn_lb_logits.shape
    HW = 4 * lsh[2]
    placed = lax.dynamic_update_slice(jnp.zeros((lsh[0], lsh[1], HW), F32), hgrn_lb_logits, (0, 0, chip * lsh[2]))
    placed = jnp.where(cc == 0, placed, 0.0)
    logits = _allreduce_small(_small_rows([placed]), name="gather_logits").reshape(-1)[:placed.size].reshape(placed.shape)
    lb, lb_vjp = jax.vjp(_lb_from_logits, logits)

    small = dict(ln_emb_g=ln_emb_g, ln_emb_b=ln_emb_b, lb=lb, hgrn_norm_g=hgrn_norm_g, mla_g_cq=mla_g_cq,
                 mla_g_ckv=mla_g_ckv, ln1_g=ln1_g, ln1_b=ln1_b, ln2_g=ln2_g, ln2_b=ln2_b)
    loss_l, grad_x, gk, gs, early = _local_step(x[0], mem[0], positions[0], loss_target[0], w_in_k, small, None,
                                                MW, DFF, comm=_Comm)

    (dlogits,) = lb_vjp(gs["dlb"])
    sm_parts = [loss_l, gs["ln_emb_g"], gs["ln_emb_b"], dlogits, gs["hgrn_norm_g"], gs["mla_g_cq"], gs["mla_g_ckv"],
                gs["ln1_g"], gs["ln1_b"], gs["ln2_g"], gs["ln2_b"]]
    red = _allreduce_small(_small_rows(sm_parts), name="allreduce_small").reshape(-1)
    sm_out, off = [], 0
    for p in sm_parts:
        sm_out.append(red[off:off + p.size].reshape(p.shape))
        off += p.size
    loss = sm_out[0].reshape(())
    g_small = dict(zip(_SMALL, sm_out[1:]))
    g_small["hgrn_lb_logits"] = lax.dynamic_slice(g_small["hgrn_lb_logits"], (0, 0, chip * lsh[2]), lsh)
    for n in _SMALL:
        g_small[n] = g_small[n].reshape(wts[n].shape)

    late = _from_kernel_layout({n: gk[n] for n in ("w_in", "mla_w_uq", "mla_w_ukv")}, QR, KR)
    late_recv = _exchange(_scatter_exch([_split_pieces(late[n], axis[n]) for n in late]), name="scatter_late")
    g_big = {}
    for n, r in list(zip(_Comm.early_names, early)) + list(zip(late, late_recv)):
        g_big[n] = _sum_share(r, name="rs_sum_" + n).reshape(wts[n].shape)

    grads = {**g_small, **g_big}
    delta, new_m, new_v = {}, {}, {}
    for n, _ in _BIG:
        shp = wts[n].shape
        two_d = lambda a: a.reshape(-1, shp[-1])
        d_, m_, v_ = _adamw(two_d(wts[n]), two_d(grads[n]), two_d(mom[n]), two_d(var[n]), name="adamw_" + n)
        delta[n], new_m[n], new_v[n] = d_.reshape(shp), m_.reshape(shp), v_.reshape(shp)
    sw, sg_, sm_, sv_ = (_small_rows([d[n] for n in _SMALL]) for d in (wts, grads, mom, var))
    d_, m_, v_ = _adamw(sw, sg_, sm_, sv_, name="adamw_small")
    for res, packed_rows in ((delta, d_), (new_m, m_), (new_v, v_)):
        flat, off = packed_rows.reshape(-1), 0
        for n in _SMALL:
            res[n] = flat[off:off + wts[n].size].reshape(wts[n].shape)
            off += wts[n].size

    return (loss, grad_x[None], *[grads[n] for n in names], *[delta[n] for n in names],
            *[new_m[n] for n in names], *[new_v[n] for n in names])
```

```python
import jax
import jax.numpy as jnp
from jax import lax
from jax.experimental import pallas as pl
from jax.experimental.pallas import tpu as pltpu

F32 = jnp.float32
BF16 = jnp.bfloat16

HG_HEADS = 8
HG_D = 128
MLA_HEADS = 8
MLA_NOPE = 128
MLA_ROPE = 64
MLA_V = 128
MEM_HEADS = 4
N_BRANCH = 3
ROPE_THETA = 10000.0
DEPTH = 1
ALPHA = (2.0 * DEPTH) ** 0.25
LN_EPS = 1e-5
RMS_EPS = 1e-6
ADAM_LR = 0.001
ADAM_B1 = 0.9
ADAM_B2 = 0.999
ADAM_EPS = 1e-08
ADAM_WD = 0.01
ADAM_STEP = 10

LANES = 128
SUBLANES = 8
VMEM_LIMIT = 48 * 1024 * 1024

HG_CHUNK = 128
HG_SUB = 16
HG_PAIR = 2

MESH = pl.DeviceIdType.MESH
HI = lax.Precision.HIGHEST


def _cparams(sem=None):
    if sem is None:
        return pltpu.CompilerParams(vmem_limit_bytes=VMEM_LIMIT)
    return pltpu.CompilerParams(dimension_semantics=sem, vmem_limit_bytes=VMEM_LIMIT)


def _tile(dim, pref, quantum):
    t = min(pref, dim) // quantum * quantum
    while t >= quantum:
        if dim % t == 0:
            return t
        t -= quantum
    return dim


def _sigmoid(x):
    return 1.0 / (1.0 + jnp.exp(-x))


def _coords():
    return lax.axis_index("x"), lax.axis_index("y"), lax.axis_index("c")


def _chip(x, y, c):
    return 2 * x + y


def _device(x, y, c):
    return 4 * x + 2 * y + c


_CHIP_MASKS = ((1, 0, 0), (0, 1, 0), (1, 1, 0))
_ALL_MASKS = tuple((k >> 2 & 1, k >> 1 & 1, k & 1) for k in range(1, 8))
_HBM = pl.BlockSpec(memory_space=pl.ANY)


class _Exch:
    def __init__(self, srcs, n_dst, copies, local_copies):
        self.srcs, self.n_dst, self.copies, self.local_copies = list(srcs), n_dst, copies, local_copies
        self.n = len(self.srcs)

    def out_shape(self):
        return [jax.ShapeDtypeStruct((self.n_dst,) + s.shape[1:], s.dtype) for s in self.srcs]

    def scratch(self):
        n_rc, n_lc = self.n * len(self.copies), self.n * len(self.local_copies)
        return [pltpu.SemaphoreType.DMA((n_rc,)), pltpu.SemaphoreType.DMA((n_rc,)),
                pltpu.SemaphoreType.DMA((max(n_lc, 1),))]

    def _descriptors(self, src_refs, dst_refs, sems):
        send_sems, recv_sems, local_sems = sems
        x, y, c = _coords()
        n_rc, n_lc = len(self.copies), len(self.local_copies)
        remote, local = [], []
        for a in range(self.n):
            for k, (mask, sidx, didx) in enumerate(self.copies):
                remote.append(pltpu.make_async_remote_copy(
                    src_ref=src_refs[a].at[sidx(x, y, c)], dst_ref=dst_refs[a].at[didx(x, y, c)],
                    send_sem=send_sems.at[a * n_rc + k], recv_sem=recv_sems.at[a * n_rc + k],
                    device_id=(x ^ mask[0], y ^ mask[1], c ^ mask[2]), device_id_type=MESH))
            for k, (sidx, didx) in enumerate(self.local_copies):
                local.append(pltpu.make_async_copy(src_refs[a].at[sidx(x, y, c)], dst_refs[a].at[didx(x, y, c)],
                                                   local_sems.at[a * n_lc + k]))
        return remote, local

    def start(self, src_refs, dst_refs, sems):
        remote, local = self._descriptors(src_refs, dst_refs, sems)
        for cp in remote + local:
            cp.start()

    def wait(self, src_refs, dst_refs, sems):
        remote, local = self._descriptors(src_refs, dst_refs, sems)
        for cp in remote:
            cp.wait_recv()
        for cp in remote:
            cp.wait_send()
        for cp in local:
            cp.wait()


def _exchange(exch, *, name):
    n = exch.n

    def body(*refs):
        src_refs, dst_refs, sems = refs[:n], refs[n:2 * n], refs[2 * n:]
        exch.start(src_refs, dst_refs, sems)
        exch.wait(src_refs, dst_refs, sems)

    return pl.pallas_call(
        body, name=name, in_specs=[_HBM] * n, out_specs=[_HBM] * n, out_shape=exch.out_shape(),
        scratch_shapes=exch.scratch(), compiler_params=pltpu.CompilerParams(has_side_effects=True),
    )(*exch.srcs)


def _carried(call, exch, grid, in_specs, out_specs, out_shape, scratch_shapes, args, *, name):
    n_in, n_out, n_scr = len(in_specs), len(out_specs), len(scratch_shapes)
    n = 0 if exch is None else exch.n

    def body(*refs):
        o0 = n_in + n
        s0 = o0 + n_out + n
        ins, srcs = refs[:n_in], refs[n_in:o0]
        outs, dsts = refs[o0:o0 + n_out], refs[o0 + n_out:s0]
        scr, sems = refs[s0:s0 + n_scr], refs[s0 + n_scr:]
        if exch is not None:
            ids = [pl.program_id(d) for d in range(len(grid))]
            first = _all([i == 0 for i in ids])
            last = _all([i == g - 1 for i, g in zip(ids, grid)])

            @pl.when(first)
            def _():
                exch.start(srcs, dsts, sems)

        call(*ins, *outs, *scr)
        if exch is not None:
            @pl.when(last)
            def _():
                exch.wait(srcs, dsts, sems)

    if exch is None:
        params = pltpu.CompilerParams(dimension_semantics=("arbitrary",) * len(grid), vmem_limit_bytes=VMEM_LIMIT)
        extra_in, extra_out, extra_shape, extra_scr, extra_args = [], [], [], [], []
    else:
        params = pltpu.CompilerParams(dimension_semantics=("arbitrary",) * len(grid), vmem_limit_bytes=VMEM_LIMIT,
                                      has_side_effects=True)
        extra_in, extra_out, extra_shape = [_HBM] * n, [_HBM] * n, exch.out_shape()
        extra_scr, extra_args = exch.scratch(), exch.srcs
    res = pl.pallas_call(
        body, name=name, grid=grid, in_specs=list(in_specs) + extra_in, out_specs=list(out_specs) + extra_out,
        out_shape=list(out_shape) + extra_shape, scratch_shapes=list(scratch_shapes) + extra_scr,
        compiler_params=params,
    )(*args, *extra_args)
    return res[:n_out], res[n_out:]


def _all(conds):
    out = conds[0]
    for c in conds[1:]:
        out = jnp.logical_and(out, c)
    return out


def _mm(a, b, M, N, K, *, ta=False, tb=False, a_off=(0, 0), b_off=(0, 0), add=None, exch=None,
        tm=1024, tn=1024, tk=1024, name):
    tm = _tile(M, tm, LANES if ta else SUBLANES)
    tn = _tile(N, tn, LANES)
    tk = _tile(K, tk, LANES)
    nk = K // tk
    ar, ac = a_off
    br, bc = b_off

    if ta:
        assert ar % tk == 0 and ac % tm == 0
        a_spec = pl.BlockSpec((tk, tm), lambda i, j, k: (ar // tk + k, ac // tm + i))
    else:
        assert ar % tm == 0 and ac % tk == 0
        a_spec = pl.BlockSpec((tm, tk), lambda i, j, k: (ar // tm + i, ac // tk + k))
    if tb:
        assert br % tn == 0 and bc % tk == 0
        b_spec = pl.BlockSpec((tn, tk), lambda i, j, k: (br // tn + j, bc // tk + k))
    else:
        assert br % tk == 0 and bc % tn == 0
        b_spec = pl.BlockSpec((tk, tn), lambda i, j, k: (br // tk + k, bc // tn + j))
    o_spec = pl.BlockSpec((tm, tn), lambda i, j, k: (i, j))
    mixed = a.dtype != b.dtype

    def body(*refs):
        if add is None:
            a_ref, b_ref, o_ref, acc = refs
        else:
            a_ref, b_ref, add_ref, o_ref, acc = refs
        k = pl.program_id(2)
        av = a_ref[...]
        bv = b_ref[...]
        if ta:
            av = av.astype(F32).T
        if mixed:
            av = av.astype(BF16)
            bv = bv.astype(BF16)
        dims = (((1,), (1 if tb else 0,)), ((), ()))
        d = lax.dot_general(av, bv, dims, preferred_element_type=F32)

        def finish(total):
            o_ref[...] = total if add is None else total + add_ref[...]

        if nk == 1:
            finish(d)
        else:
            @pl.when(k == 0)
            def _():
                acc[...] = d

            @pl.when(jnp.logical_and(k > 0, k < nk - 1))
            def _():
                acc[...] += d

            @pl.when(k == nk - 1)
            def _():
                finish(acc[...] + d)

    in_specs = [a_spec, b_spec]
    args = [a, b]
    if add is not None:
        in_specs.append(o_spec)
        args.append(add)
    outs, received = _carried(body, exch, (M // tm, N // tn, nk), in_specs, [o_spec],
                              [jax.ShapeDtypeStruct((M, N), F32)], [pltpu.VMEM((tm, tn), F32)], args, name=name)
    return outs[0] if exch is None else (outs[0], received)


class _Rows:
    def __init__(self, arr, width, col0=0, lead=None, dtype=F32):
        self.arr, self.width, self.col0, self.lead, self.dtype = arr, width, col0, lead, dtype


def _rowwise(fn, rows, consts, outs, accs, *, R, tr, ncol=1, name):
    tr = _tile(R, tr, SUBLANES)
    nrow = R // tr

    def spec(r):
        if r.lead is None:
            return pl.BlockSpec((tr, r.width), lambda j, i, c0=r.col0: (i, c0 + j))
        return pl.BlockSpec((None, tr, r.width), lambda j, i, c0=r.col0, l=r.lead: (l, i, c0 + j))

    in_specs = [spec(r) for r in rows]
    for c in consts:
        in_specs.append(pl.BlockSpec(c.shape, lambda j, i, nd=c.ndim: (0,) * nd))
    out_specs = [spec(o) for o in outs]
    out_shape = [jax.ShapeDtypeStruct((R, o.arr), o.dtype) for o in outs]
    for w in accs:
        out_specs.append(pl.BlockSpec((1, w), lambda j, i: (0, j)))
        out_shape.append(jax.ShapeDtypeStruct((1, w * ncol), F32))
    n_in = len(rows) + len(consts)
    n_out = len(outs)

    def body(*refs):
        ins = [r[...] for r in refs[:n_in]]
        res = fn(*ins)
        if not isinstance(res, (tuple, list)):
            res = (res,)
        for k in range(n_out):
            refs[n_in + k][...] = res[k].astype(refs[n_in + k].dtype)
        i = pl.program_id(1)
        for k in range(len(accs)):
            a_ref = refs[n_in + n_out + k]

            @pl.when(i == 0)
            def _(a_ref=a_ref):
                a_ref[...] = jnp.zeros_like(a_ref)

            a_ref[...] += res[n_out + k]

    res = pl.pallas_call(
        body, name=name, grid=(ncol, nrow),
        in_specs=in_specs, out_specs=out_specs, out_shape=out_shape,
        compiler_params=_cparams(("parallel", "arbitrary")),
    )(*[r.arr for r in rows], *consts)
    return res


def _colsum(x):
    return jnp.sum(x, axis=0, keepdims=True)


def _ln_stats(z):
    mu = jnp.mean(z, axis=-1, keepdims=True)
    zc = z - mu
    var = jnp.mean(zc * zc, axis=-1, keepdims=True)
    rstd = lax.rsqrt(var + LN_EPS)
    return zc * rstd, rstd


def _ln_bwd_core(z, g, dy):
    xhat, rstd = _ln_stats(z)
    dxh = dy * g
    m1 = jnp.mean(dxh, axis=-1, keepdims=True)
    m2 = jnp.mean(dxh * xhat, axis=-1, keepdims=True)
    dz = rstd * (dxh - m1 - xhat * m2)
    return dz, _colsum(dy * xhat), _colsum(dy)


def _rms_fwd(x, g, eps):
    r = lax.rsqrt(jnp.mean(x * x, axis=-1, keepdims=True) + eps)
    return x * r * g


def _rms_bwd(x, g, dy, eps):
    r = lax.rsqrt(jnp.mean(x * x, axis=-1, keepdims=True) + eps)
    xr = x * r
    dyg = dy * g
    dx = r * (dyg - xr * jnp.mean(dyg * xr, axis=-1, keepdims=True))
    return dx, dy * xr


def _hg_gate(fr, lb):
    sig = _sigmoid(fr)
    f = lb + (1.0 - lb) * sig
    return sig, f


def _hg_masks(rev):
    C = HG_CHUNK
    t = lax.broadcasted_iota(jnp.int32, (C, C), 0)
    s = lax.broadcasted_iota(jnp.int32, (C, C), 1)
    tri = (s >= t) if rev else (s <= t)
    return tri


def _hg_offdiag(Q, K, b, i, rev):
    C, sb = HG_CHUNK, HG_SUB
    nb = C // sb
    if (not rev and i == 0) or (rev and i == nb - 1):
        return None
    ref = b[sb * i - 1:sb * i] if not rev else b[sb * (i + 1):sb * (i + 1) + 1]
    srow = lax.broadcasted_iota(jnp.int32, (C, 1), 0)
    smask = (srow < sb * i) if not rev else (srow >= sb * (i + 1))
    qscale = jnp.exp(jnp.minimum(b - ref, 0.0))
    kscale = jnp.where(smask, jnp.exp(jnp.minimum(ref - b, 0.0)), 0.0)
    return qscale, kscale


def _hg_att(Q, K, b, rev):
    C, sb = HG_CHUNK, HG_SUB
    lane = lax.broadcasted_iota(jnp.int32, (sb, C), 1)
    rloc = lax.broadcasted_iota(jnp.int32, (sb, 1), 0)
    rows = []
    for i in range(C // sb):
        sl = slice(sb * i, sb * i + sb)
        Qi, Ki, bi = Q[sl], K[sl], b[sl]
        od = _hg_offdiag(Q, K, b, i, rev)
        if od is None:
            acc = jnp.zeros((sb, C), F32)
        else:
            qs, ks = od
            acc = lax.dot_general(Qi * qs[sl], K * ks, (((1,), (1,)), ((), ())),
                                  precision=HI, preferred_element_type=F32)
        for j in range(sb):
            e = jnp.exp(jnp.minimum(bi - bi[j:j + 1], 0.0))
            col = jnp.sum(Qi * Ki[j:j + 1] * e, axis=-1, keepdims=True)
            vis = (rloc <= j) if rev else (rloc >= j)
            acc = jnp.where(lane == sb * i + j, jnp.where(vis, col, 0.0), acc)
        rows.append(acc)
    return jnp.concatenate(rows, axis=0)


def _hg_att_bwd(Q, K, b, dA, rev):
    C, sb = HG_CHUNK, HG_SUB
    rloc = lax.broadcasted_iota(jnp.int32, (sb, 1), 0)
    rrow = lax.broadcasted_iota(jnp.int32, (sb, HG_D), 0)
    trow = lax.broadcasted_iota(jnp.int32, (C, C), 1) // sb
    dAT = dA.T
    dQ_rows, dKd_rows = [], []
    dK = jnp.zeros((C, HG_D), F32)
    for i in range(C // sb):
        sl = slice(sb * i, sb * i + sb)
        Qi, Ki, bi, dAi = Q[sl], K[sl], b[sl], dA[sl]
        od = _hg_offdiag(Q, K, b, i, rev)
        if od is None:
            dQi = jnp.zeros((sb, HG_D), F32)
        else:
            qs, ks = od
            dQi = lax.dot_general(dAi, K * ks, (((1,), (0,)), ((), ())),
                                  precision=HI, preferred_element_type=F32) * qs[sl]
            zt = jnp.where(trow == i, dAT, 0.0)
            dK = dK + lax.dot_general(zt, Q * qs, (((1,), (0,)), ((), ())),
                                      precision=HI, preferred_element_type=F32) * ks
        dKd = jnp.zeros((sb, HG_D), F32)
        for j in range(sb):
            vis = (rloc <= j) if rev else (rloc >= j)
            e = jnp.where(vis, jnp.exp(jnp.minimum(bi - bi[j:j + 1], 0.0)), 0.0)
            dcol = dAi[:, sb * i + j:sb * i + j + 1]
            dQi = dQi + dcol * Ki[j:j + 1] * e
            krow = jnp.sum(dcol * Qi * e, axis=0, keepdims=True)
            dKd = jnp.where(rrow == j, krow, dKd)
        dQ_rows.append(dQi)
        dKd_rows.append(dKd)
    return jnp.concatenate(dQ_rows, axis=0), dK + jnp.concatenate(dKd_rows, axis=0)


def _hg_prep(qr, fr, lb, tri):
    sigq = _sigmoid(qr)
    Q = qr * sigq
    sig, f = _hg_gate(fr, lb)
    K = 1.0 - f
    logf = jnp.log(f)
    b = lax.dot_general(tri.astype(F32), logf, (((1,), (0,)), ((), ())),
                        precision=HI, preferred_element_type=F32)
    return sigq, Q, sig, f, K, b


def _hgrn_scan(P, lb, *, S, rev, name):
    C, H, D_ = HG_CHUNK, HG_HEADS, HG_D
    HP = HG_PAIR if H % HG_PAIR == 0 else 1
    NC = S // C
    fcol = (3 if rev else 2) * H

    def cidx(n):
        return NC - 1 - n if rev else n

    def body(q_ref, v_ref, f_ref, lb_ref, o_ref, st_ref, a_ref, state):
        n = pl.program_id(1)

        @pl.when(n == 0)
        def _():
            state[...] = jnp.zeros_like(state)

        tri = _hg_masks(rev)
        qa, va, fa, lba, sta = q_ref[...], v_ref[...], f_ref[...], lb_ref[...], state[...]
        st_ref[...] = sta
        outs, amats, states = [], [], []
        for hp in range(HP):
            sl = slice(hp * D_, (hp + 1) * D_)
            _, Q, _, _, K, b = _hg_prep(qa[:, sl], fa[:, sl], lba[:, sl], tri)
            V, ST0 = va[:, sl], sta[hp]
            e_b = jnp.exp(b)
            bE = b[0:1] if rev else b[C - 1:C]
            W = jnp.exp(bE - b)
            inter = lax.dot_general(Q * e_b, ST0, (((1,), (1,)), ((), ())), preferred_element_type=F32)
            A = _hg_att(Q, K, b, rev)
            amats.append(A)
            outs.append(inter + jnp.dot(A, V, preferred_element_type=F32))
            states.append(ST0 * jnp.exp(bE) + lax.dot_general(
                V, K * W, (((0,), (0,)), ((), ())), preferred_element_type=F32))
        a_ref[...] = jnp.stack(amats)
        o_ref[...] = jnp.concatenate(outs, axis=1)
        state[...] = jnp.stack(states)

    blk = lambda c0: pl.BlockSpec((C, HP * D_), lambda h, n, c0=c0: (cidx(n), c0 // HP + h))
    return pl.pallas_call(
        body, name=name, grid=(H // HP, NC),
        in_specs=[blk(0), blk(H), blk(fcol), pl.BlockSpec((1, HP * D_), lambda h, n: (0, h))],
        out_specs=[pl.BlockSpec((C, HP * D_), lambda h, n: (cidx(n), h)),
                   pl.BlockSpec((None, HP, D_, D_), lambda h, n: (cidx(n), h, 0, 0)),
                   pl.BlockSpec((None, HP, C, C), lambda h, n: (cidx(n), h, 0, 0))],
        out_shape=[jax.ShapeDtypeStruct((S, H * D_), F32),
                   jax.ShapeDtypeStruct((NC, H, D_, D_), F32),
                   jax.ShapeDtypeStruct((NC, H, C, C), F32)],
        scratch_shapes=[pltpu.VMEM((HP, D_, D_), F32)],
        compiler_params=_cparams(("parallel", "arbitrary")),
    )(P, P, P, lb)


def _hgrn_scan_bwd(P, lb, st, amat, do, *, S, rev, name):
    C, H, D_ = HG_CHUNK, HG_HEADS, HG_D
    HP = HG_PAIR if H % HG_PAIR == 0 else 1
    NC = S // C
    fcol = (3 if rev else 2) * H

    def cidx(n):
        return n if rev else NC - 1 - n

    def body(q_ref, v_ref, f_ref, lb_ref, st_ref, a_ref, do_ref, dq_ref, dv_ref, df_ref, dlb_ref, dstate):
        n = pl.program_id(1)

        @pl.when(n == 0)
        def _():
            dstate[...] = jnp.zeros_like(dstate)
            dlb_ref[...] = jnp.zeros_like(dlb_ref)

        tri = _hg_masks(rev)
        tri_t = _hg_masks(not rev).astype(F32)
        qa, va, fa, lba, doa = q_ref[...], v_ref[...], f_ref[...], lb_ref[...], do_ref[...]
        sta, ama, dsta = st_ref[...], a_ref[...], dstate[...]
        trow = lax.broadcasted_iota(jnp.int32, (C, 1), 0)
        dqs, dvs, dfs, dlbs, dstates = [], [], [], [], []
        for hp in range(HP):
            sl = slice(hp * D_, (hp + 1) * D_)
            lbv, qr = lba[:, sl], qa[:, sl]
            sigq, Q, sig, f, K, b = _hg_prep(qr, fa[:, sl], lbv, tri)
            V, ST0, A, dO, dST1 = va[:, sl], sta[hp], ama[hp], doa[:, sl], dsta[hp]
            e_b = jnp.exp(b)
            bE = b[0:1] if rev else b[C - 1:C]
            eE = jnp.exp(bE)
            W = jnp.exp(bE - b)
            Qe = Q * e_b
            KW = K * W
            dA = jnp.where(tri, lax.dot_general(dO, V, (((1,), (1,)), ((), ())), preferred_element_type=F32), 0.0)
            dV = (lax.dot_general(A, dO, (((0,), (0,)), ((), ())), preferred_element_type=F32)
                  + lax.dot_general(KW, dST1, (((1,), (1,)), ((), ())), preferred_element_type=F32))
            dQe = jnp.dot(dO, ST0, preferred_element_type=F32)
            dKW = jnp.dot(V, dST1, preferred_element_type=F32)
            dstates.append(dST1 * eE + lax.dot_general(dO, Qe, (((0,), (0,)), ((), ())), preferred_element_type=F32))
            dQa, dKa = _hg_att_bwd(Q, K, b, dA, rev)
            dQ = dQe * e_b + dQa
            dK = dKW * W + dKa
            extra = _colsum(KW * dKW) + eE * _colsum(ST0 * dST1)
            db = Q * dQ - K * dK + jnp.where(trow == (0 if rev else C - 1), extra, 0.0)
            dlogf = lax.dot_general(tri_t, db, (((1,), (0,)), ((), ())), precision=HI, preferred_element_type=F32)
            dfv = dlogf / f - dK
            dfs.append(dfv * (1.0 - lbv) * sig * (1.0 - sig))
            dlbs.append(_colsum(dfv * (1.0 - sig)))
            dqs.append(dQ * (sigq * (1.0 + qr * (1.0 - sigq))))
            dvs.append(dV)
        dstate[...] = jnp.stack(dstates)
        df_ref[...] = jnp.concatenate(dfs, axis=1)
        dlb_ref[...] += jnp.concatenate(dlbs, axis=1)
        dq_ref[...] = jnp.concatenate(dqs, axis=1)
        dv_ref[...] = jnp.concatenate(dvs, axis=1)

    blk = lambda c0: pl.BlockSpec((C, HP * D_), lambda h, n, c0=c0: (cidx(n), c0 // HP + h))
    oblk = pl.BlockSpec((C, HP * D_), lambda h, n: (cidx(n), h))
    return pl.pallas_call(
        body, name=name, grid=(H // HP, NC),
        in_specs=[blk(0), blk(H), blk(fcol), pl.BlockSpec((1, HP * D_), lambda h, n: (0, h)),
                  pl.BlockSpec((None, HP, D_, D_), lambda h, n: (cidx(n), h, 0, 0)),
                  pl.BlockSpec((None, HP, C, C), lambda h, n: (cidx(n), h, 0, 0)),
                  oblk],
        out_specs=[oblk, oblk, oblk, pl.BlockSpec((1, HP * D_), lambda h, n: (0, h))],
        out_shape=[jax.ShapeDtypeStruct((S, H * D_), F32)] * 3 + [jax.ShapeDtypeStruct((1, H * D_), F32)],
        scratch_shapes=[pltpu.VMEM((HP, D_, D_), F32)],
        compiler_params=_cparams(("parallel", "arbitrary")),
    )(P, P, P, lb, st, amat, do)


LOG2E = 1.4426950408889634
MXU = BF16
ATT_SUB = 512


def _mx(x):
    return x if x.dtype == MXU else x.astype(MXU)


def _attn_fwd(q, k, v, *, S, T, H, dqk, dv, q_col0, k_col0, v_col0, scale, tq, tk, name):
    tq = _tile(S, tq, SUBLANES)
    tk = _tile(T, tk, LANES)
    nk = T // tk
    ts = _tile(tq, ATT_SUB, SUBLANES)

    def body(q_ref, k_ref, v_ref, o_ref, lse_ref, m_s, l_s, acc):
        j = pl.program_id(2)

        @pl.when(j == 0)
        def _():
            m_s[...] = jnp.full_like(m_s, -jnp.inf)
            l_s[...] = jnp.zeros_like(l_s)
            acc[...] = jnp.zeros_like(acc)

        kv, vv = _mx(k_ref[...]), _mx(v_ref[...])
        m_all, l_all, a_all = m_s[...], l_s[...], acc[...]
        ms, ls, accs = [], [], []
        for r0 in range(0, tq, ts):
            rows = slice(r0, r0 + ts)
            s = lax.dot_general(_mx(q_ref[rows, :]), kv, (((1,), (1,)), ((), ())),
                                preferred_element_type=F32) * (scale * LOG2E)
            m_old = m_all[rows]
            m_new = jnp.maximum(m_old, jnp.max(s, axis=1)[:, None])
            corr = jnp.exp2(m_old - m_new)
            p = jnp.exp2(s - jnp.tile(m_new, (1, tk // LANES)))
            ms.append(m_new)
            ls.append(corr * l_all[rows] + jnp.sum(p, axis=1)[:, None])
            accs.append(jnp.tile(corr, (1, dv // LANES)) * a_all[rows] + jnp.dot(_mx(p), vv, preferred_element_type=F32))
        m_s[...] = jnp.concatenate(ms, axis=0)
        l_s[...] = jnp.concatenate(ls, axis=0)
        acc[...] = jnp.concatenate(accs, axis=0)

        @pl.when(j == nk - 1)
        def _():
            o_ref[...] = acc[...] / jnp.tile(l_s[...], (1, dv // LANES))
            lse_ref[...] = ((m_s[...] + jnp.log2(l_s[...])) * (1.0 / LOG2E))[:, :1]

    return pl.pallas_call(
        body, name=name, grid=(H, S // tq, nk),
        in_specs=[pl.BlockSpec((tq, dqk), lambda h, i, j: (i, q_col0 + h)),
                  pl.BlockSpec((tk, dqk), lambda h, i, j: (j, k_col0 + h)),
                  pl.BlockSpec((tk, dv), lambda h, i, j: (j, v_col0 + h))],
        out_specs=[pl.BlockSpec((tq, dv), lambda h, i, j: (i, h)),
                   pl.BlockSpec((None, tq, 1), lambda h, i, j: (h, i, 0))],
        out_shape=[jax.ShapeDtypeStruct((S, H * dv), F32), jax.ShapeDtypeStruct((H, S, 1), F32)],
        scratch_shapes=[pltpu.VMEM((tq, LANES), F32), pltpu.VMEM((tq, LANES), F32), pltpu.VMEM((tq, dv), F32)],
        compiler_params=_cparams(("parallel", "parallel", "arbitrary")),
    )(q, k, v)


def _attn_bwd(q, k, v, o, lse, do, *, S, T, H, dqk, dv, q_col0, k_col0, v_col0, scale, tq, tk, exch=None, name):
    tq = _tile(S, tq, SUBLANES)
    tk = _tile(T, tk, LANES)
    nq = S // tq
    ts = _tile(tq, ATT_SUB, SUBLANES)

    def body(q_ref, k_ref, v_ref, o_ref, lse_ref, do_ref, dq_ref, dk_ref, dv_ref, dk_acc, dv_acc):
        j = pl.program_id(1)
        i = pl.program_id(2)

        @pl.when(jnp.logical_and(i == 0, j == 0))
        def _():
            dq_ref[...] = jnp.zeros_like(dq_ref)

        @pl.when(i == 0)
        def _():
            dk_acc[...] = jnp.zeros_like(dk_acc)
            dv_acc[...] = jnp.zeros_like(dv_acc)

        kv, vv = _mx(k_ref[...]), _mx(v_ref[...])
        dk_new, dv_new = dk_acc[...], dv_acc[...]
        lse2 = lse_ref[...] * LOG2E
        dqs = []
        for r0 in range(0, tq, ts):
            rows = slice(r0, r0 + ts)
            qv, dov = _mx(q_ref[rows, :]), do_ref[rows, :]
            s = lax.dot_general(qv, kv, (((1,), (1,)), ((), ())), preferred_element_type=F32) * (scale * LOG2E)
            p = jnp.exp2(s - lse2[rows])
            delta = jnp.sum(dov * o_ref[rows, :], axis=-1, keepdims=True)
            dob = _mx(dov)
            dp = lax.dot_general(dob, vv, (((1,), (1,)), ((), ())), preferred_element_type=F32)
            ds = _mx(p * (dp - delta) * scale)
            dv_new = dv_new + lax.dot_general(_mx(p), dob, (((0,), (0,)), ((), ())), preferred_element_type=F32)
            dk_new = dk_new + lax.dot_general(ds, qv, (((0,), (0,)), ((), ())), preferred_element_type=F32)
            dqs.append(jnp.dot(ds, kv, preferred_element_type=F32))
        dq_ref[pl.ds(pl.multiple_of(i * tq, tq), tq), :] += jnp.concatenate(dqs, axis=0)
        dk_acc[...] = dk_new
        dv_acc[...] = dv_new

        @pl.when(i == nq - 1)
        def _():
            dk_ref[...] = dk_new
            dv_ref[...] = dv_new

    outs, received = _carried(
        body, exch, (H, T // tk, nq),
        [pl.BlockSpec((tq, dqk), lambda h, j, i: (i, q_col0 + h)),
         pl.BlockSpec((tk, dqk), lambda h, j, i: (j, k_col0 + h)),
         pl.BlockSpec((tk, dv), lambda h, j, i: (j, v_col0 + h)),
         pl.BlockSpec((tq, dv), lambda h, j, i: (i, h)),
         pl.BlockSpec((None, tq, 1), lambda h, j, i: (h, i, 0)),
         pl.BlockSpec((tq, dv), lambda h, j, i: (i, h))],
        [pl.BlockSpec((S, dqk), lambda h, j, i: (0, h)),
         pl.BlockSpec((tk, dqk), lambda h, j, i: (j, h)),
         pl.BlockSpec((tk, dv), lambda h, j, i: (j, h))],
        [jax.ShapeDtypeStruct((S, H * dqk), F32), jax.ShapeDtypeStruct((T, H * dqk), F32),
         jax.ShapeDtypeStruct((T, H * dv), F32)],
        [pltpu.VMEM((tk, dqk), F32), pltpu.VMEM((tk, dv), F32)], [q, k, v, o, lse, do], name=name)
    return outs if exch is None else (outs, received)


def _rope_tables(positions):
    half = MLA_ROPE // 2
    inv_freq = jnp.power(ROPE_THETA, -jnp.arange(half, dtype=F32) / half)
    ang = positions.astype(F32)[:, None] * inv_freq
    cos, sin = jnp.cos(ang), jnp.sin(ang)
    z = jnp.zeros_like(cos)
    tc = jnp.concatenate([cos, cos, z, z], axis=1)
    ta = jnp.concatenate([-sin, z, z, z], axis=1)
    tb = jnp.concatenate([z, sin, z, z], axis=1)
    return tc, ta, tb


def _rope_apply(v, tc, ta, tb):
    half = MLA_ROPE // 2
    return v * tc + pltpu.roll(v, LANES - half, 1) * ta + pltpu.roll(v, half, 1) * tb


def _rope_apply_t(d, tc, ta, tb):
    half = MLA_ROPE // 2
    return d * tc + pltpu.roll(d * ta, half, 1) + pltpu.roll(d * tb, LANES - half, 1)


def _local_step(x, mem, positions, loss_target, w_in_k, small, W, MW, DFF, comm=None):
    S, D = x.shape
    M = mem.shape[0]
    HW = HG_HEADS * HG_D
    QR = small["mla_g_cq"].shape[1]
    KR = small["mla_g_ckv"].shape[1]
    MHD = MW // MEM_HEADS
    QW = MLA_HEADS * 2 * LANES
    VW = MLA_HEADS * MLA_V
    c_hg, c_cq, c_ckv, c_qm, c_gate = 0, 5 * HW, 5 * HW + QR, 5 * HW + QR + KR, 5 * HW + QR + KR + MW
    c_kr = c_gate + N_BRANCH * D
    PW = c_kr + LANES
    assert w_in_k.shape == (D, PW)
    TR = 256
    row = lambda a: a.reshape(1, -1)
    ge, be = row(small["ln_emb_g"]), row(small["ln_emb_b"])
    g1, b1, g2, b2 = small["ln1_g"], small["ln1_b"], small["ln2_g"], small["ln2_b"]
    lb = small["lb"]
    tc, ta, tb = _rope_tables(positions)

    (h0,) = _rowwise(lambda z, g, b: _ln_stats(z)[0] * g + b, [_Rows(x, D)], [ge, be],
                     [_Rows(D, D)], [], R=S, tr=TR, name="ln_emb")
    if comm is None:
        P = _mm(h0, w_in_k, S, PW, D, tn=896, name="proj_in")
    else:
        P, got = _mm(h0, w_in_k, S, PW, D, tn=896, exch=comm.gather_rest, name="proj_in")
        W = comm.weights(got)

    o_fw, st_fw, a_fw = _hgrn_scan(P, lb[0:1], S=S, rev=False, name="hgrn_fw")
    o_bw, st_bw, a_bw = _hgrn_scan(P, lb[1:2], S=S, rev=True, name="hgrn_bw")

    def hg_post(of, ob, gr, ng):
        o = of + ob
        sg = _sigmoid(gr)
        outs = []
        for h in range(HG_HEADS):
            sl = slice(h * HG_D, (h + 1) * HG_D)
            outs.append(_rms_fwd(o[:, sl], ng, RMS_EPS) * sg[:, sl])
        return jnp.concatenate(outs, axis=1)

    (y_hg,) = _rowwise(hg_post, [_Rows(o_fw, HW), _Rows(o_bw, HW), _Rows(P, HW, 4)], [small["hgrn_norm_g"]],
                       [_Rows(HW, HW)], [], R=S, tr=TR, name="hgrn_post")

    def mla_norm(cq, ckv, gq, gk):
        return _rms_fwd(cq, gq, RMS_EPS), _rms_fwd(ckv, gk, RMS_EPS)

    assert c_cq % QR == 0 and c_ckv % KR == 0
    cqn, ckvn = _rowwise(mla_norm, [_Rows(P, QR, c_cq // QR), _Rows(P, KR, c_ckv // KR)],
                         [small["mla_g_cq"], small["mla_g_ckv"]],
                         [_Rows(QR, QR), _Rows(KR, KR)], [], R=S, tr=TR, name="mla_norm")
    q_raw = _mm(cqn, W["mla_w_uq"], S, QW, QR, name="mla_uq")
    kv = _mm(ckvn, W["mla_w_ukv"], S, 2 * VW, KR, name="mla_ukv")

    def rope_fwd(qb, knb, vb_, krb, tcb, tab, tbb):
        kr = _rope_apply(krb, tcb, tab, tbb)
        qo, ko = [], []
        for h in range(MLA_HEADS):
            qo += [qb[:, 2 * h * LANES:(2 * h + 1) * LANES],
                   _rope_apply(qb[:, (2 * h + 1) * LANES:(2 * h + 2) * LANES], tcb, tab, tbb)]
            ko += [knb[:, h * LANES:(h + 1) * LANES], kr]
        return jnp.concatenate(qo, axis=1), jnp.concatenate(ko, axis=1), vb_

    qc, kc, vc = _rowwise(rope_fwd, [_Rows(q_raw, QW), _Rows(kv, VW), _Rows(kv, VW, 1), _Rows(P, LANES, c_kr // LANES),
                                     _Rows(tc, LANES), _Rows(ta, LANES), _Rows(tb, LANES)], [],
                          [_Rows(QW, QW, dtype=MXU), _Rows(QW, QW, dtype=MXU), _Rows(VW, VW, dtype=MXU)], [],
                          R=S, tr=TR, name="rope_fwd")
    mla_kw = dict(S=S, T=S, H=MLA_HEADS, dqk=2 * LANES, dv=MLA_V, q_col0=0, k_col0=0, v_col0=0,
                  scale=(MLA_NOPE + MLA_ROPE) ** -0.5, tq=1024, tk=512)
    y_mla, lse_mla = _attn_fwd(qc, kc, vc, name="mla_attn", **mla_kw)

    kvm = _mm(mem, W["mem_w_kv"], M, 2 * MW, D, name="mem_kv")
    mem_kw = dict(S=S, T=M, H=MEM_HEADS, dqk=MHD, dv=MHD, q_col0=c_qm // MHD, k_col0=0, v_col0=MEM_HEADS,
                  scale=MHD ** -0.5, tq=1024, tk=M)
    assert c_qm % MHD == 0
    y_mem, lse_mem = _attn_fwd(P, kvm, kvm, name="mem_attn", **mem_kw)

    ys = (y_hg, y_mla, y_mem)
    us = [_mm(ys[b], W["w_branch"][b], S, D, HW, name=f"branch{b}") for b in range(N_BRANCH)]
    TCW = _tile(D, 1024, LANES)
    ncw = D // TCW

    def merge_fwd(g0, g1_, g2_, u0, u1, u2):
        return _sigmoid(g0) * u0 + _sigmoid(g1_) * u1 + _sigmoid(g2_) * u2

    gate_rows = [_Rows(P, TCW, (c_gate + b * D) // TCW) for b in range(N_BRANCH)]
    assert c_gate % TCW == 0
    (merged,) = _rowwise(merge_fwd, gate_rows + [_Rows(u, TCW) for u in us], [],
                         [_Rows(D, TCW)], [], R=S, tr=TR, ncol=ncw, name="merge_fwd")
    mix = _mm(merged, W["w_o"], S, D, D, name="out_proj")

    def ln_res(hp, addv, g, b):
        z = ALPHA * hp + addv
        return z, _ln_stats(z)[0] * g + b

    z1, h1 = _rowwise(ln_res, [_Rows(h0, D), _Rows(mix, D)], [g1, b1],
                      [_Rows(D, D), _Rows(D, D)], [], R=S, tr=TR, name="ln1")

    ab = _mm(h1, W["w_ffn_gu"], S, 2 * DFF, D, name="ffn_gu")
    TF = _tile(DFF, 512, LANES)
    nf = DFF // TF

    def swiglu(abv):
        a, b = abv[:, :TF], abv[:, TF:]
        return a * _sigmoid(a) * b

    (cff,) = _rowwise(swiglu, [_Rows(ab, 2 * TF)], [], [_Rows(DFF, TF)], [],
                      R=S, tr=TR, ncol=nf, name="swiglu")
    ff = _mm(cff, W["w_ffn_down"], S, D, DFF, name="ffn_down")

    def loss_bwd(hp, addv, tgt, g, b):
        z = ALPHA * hp + addv
        xhat, rstd = _ln_stats(z)
        y = xhat * g + b
        err = y - tgt
        dy = err * (1.0 / D)
        dxh = dy * g
        m1 = jnp.mean(dxh, axis=-1, keepdims=True)
        m2 = jnp.mean(dxh * xhat, axis=-1, keepdims=True)
        dz = rstd * (dxh - m1 - xhat * m2)
        lrow = jnp.sum(_colsum(err * err), axis=-1, keepdims=True) * (0.5 / D)
        return dz, _colsum(dy * xhat), _colsum(dy), lrow

    dz2, dg2, db2, loss = _rowwise(loss_bwd, [_Rows(h1, D), _Rows(ff, D), _Rows(loss_target, D)], [g2, b2],
                                   [_Rows(D, D)], [D, D, 1], R=S, tr=TR, name="loss_ln2_bwd")
    dcff = _mm(dz2, W["w_ffn_down"], S, DFF, D, tb=True, name="ffn_down_dx")
    g_ffn_down = _mm(cff, dz2, DFF, D, S, ta=True, name="ffn_down_dw")

    def swiglu_bwd(abv, dc):
        a, b = abv[:, :TF], abv[:, TF:]
        sg = _sigmoid(a)
        return jnp.concatenate([dc * b * sg * (1.0 + a * (1.0 - sg)), dc * a * sg], axis=1)

    (dab,) = _rowwise(swiglu_bwd, [_Rows(ab, 2 * TF), _Rows(dcff, TF)], [], [_Rows(2 * DFF, 2 * TF)], [],
                      R=S, tr=TR, ncol=nf, name="swiglu_bwd")
    dh1 = _mm(dab, W["w_ffn_gu"], S, D, 2 * DFF, tb=True, name="ffn_gu_dx")
    g_ffn_gu = _mm(h1, dab, D, 2 * DFF, S, ta=True, name="ffn_gu_dw")

    def ln1_bwd(z, dmm, dz2v, g):
        return _ln_bwd_core(z, g, ALPHA * dz2v + dmm)

    dz1, dg1, db1 = _rowwise(ln1_bwd, [_Rows(z1, D), _Rows(dh1, D), _Rows(dz2, D)], [g1],
                             [_Rows(D, D)], [D, D], R=S, tr=TR, name="ln1_bwd")
    dmerged = _mm(dz1, W["w_o"], S, D, D, tb=True, name="out_proj_dx")
    g_w_o = _mm(merged, dz1, D, D, S, ta=True, name="out_proj_dw")

    def merge_bwd(g0, g1_, g2_, u0, u1, u2, dm):
        res_g, res_u = [], []
        for gv, uv in ((g0, u0), (g1_, u1), (g2_, u2)):
            sg = _sigmoid(gv)
            res_g.append(dm * uv * sg * (1.0 - sg))
            res_u.append(dm * sg)
        return (*res_g, *res_u)

    mres = _rowwise(merge_bwd, gate_rows + [_Rows(u, TCW) for u in us] + [_Rows(dmerged, TCW)], [],
                    [_Rows(D, TCW)] * (2 * N_BRANCH), [], R=S, tr=TR, ncol=ncw, name="merge_bwd")
    dgates, dus = mres[:N_BRANCH], mres[N_BRANCH:]
    dys = [_mm(dus[b], W["w_branch"][b], S, HW, D, tb=True, name=f"branch{b}_dx") for b in range(N_BRANCH)]
    g_w_branch = [_mm(ys[b], dus[b], HW, D, S, ta=True, name=f"branch{b}_dw") for b in range(N_BRANCH)]

    dq_mem, dk_mem, dv_mem = _attn_bwd(P, kvm, kvm, y_mem, lse_mem, dys[2], name="mem_attn_bwd", **mem_kw)
    dkvm = jnp.concatenate([dk_mem, dv_mem], axis=1)
    g_mem_w_kv = _mm(mem, dkvm, D, 2 * MW, M, ta=True, name="mem_kv_dw")

    g_w_branch = jnp.stack(g_w_branch)
    delivered = {}
    if comm is None:
        dqc, dkc, dvv = _attn_bwd(qc, kc, vc, y_mla, lse_mla, dys[1], name="mla_attn_bwd", **mla_kw)
    else:
        exch = comm.scatter(dict(w_ffn_gu=g_ffn_gu, w_ffn_down=g_ffn_down, w_o=g_w_o, w_branch=g_w_branch,
                                 mem_w_kv=g_mem_w_kv))
        (dqc, dkc, dvv), got = _attn_bwd(qc, kc, vc, y_mla, lse_mla, dys[1], exch=exch, name="mla_attn_bwd",
                                         **mla_kw)
        delivered.update(zip(exch.names, got))

    def rope_bwd(dqb, dkb, tcb, tab, tbb):
        qo, kn = [], []
        dkr = jnp.zeros_like(tcb)
        for h in range(MLA_HEADS):
            qo += [dqb[:, 2 * h * LANES:(2 * h + 1) * LANES],
                   _rope_apply_t(dqb[:, (2 * h + 1) * LANES:(2 * h + 2) * LANES], tcb, tab, tbb)]
            kn.append(dkb[:, 2 * h * LANES:(2 * h + 1) * LANES])
            dkr = dkr + dkb[:, (2 * h + 1) * LANES:(2 * h + 2) * LANES]
        return jnp.concatenate(qo, axis=1), jnp.concatenate(kn, axis=1), _rope_apply_t(dkr, tcb, tab, tbb)

    dq_raw, dkn, dkr_raw = _rowwise(rope_bwd, [_Rows(dqc, QW), _Rows(dkc, QW), _Rows(tc, LANES),
                                               _Rows(ta, LANES), _Rows(tb, LANES)], [],
                                    [_Rows(QW, QW), _Rows(VW, VW), _Rows(LANES, LANES)], [],
                                    R=S, tr=TR, name="rope_bwd")
    dkv = jnp.concatenate([dkn, dvv], axis=1)
    dcqn = _mm(dq_raw, W["mla_w_uq"], S, QR, QW, tb=True, name="mla_uq_dx")
    g_mla_w_uq = _mm(cqn, dq_raw, QR, QW, S, ta=True, name="mla_uq_dw")
    dckvn = _mm(dkv, W["mla_w_ukv"], S, KR, 2 * VW, tb=True, name="mla_ukv_dx")
    g_mla_w_ukv = _mm(ckvn, dkv, KR, 2 * VW, S, ta=True, name="mla_ukv_dw")

    def mla_norm_bwd(cq, ckv, dq_, dk_, gq, gk):
        dcq, gq_rows = _rms_bwd(cq, gq, dq_, RMS_EPS)
        dck, gk_rows = _rms_bwd(ckv, gk, dk_, RMS_EPS)
        return dcq, dck, _colsum(gq_rows), _colsum(gk_rows)

    dcq, dckv, dg_cq, dg_ckv = _rowwise(
        mla_norm_bwd, [_Rows(P, QR, c_cq // QR), _Rows(P, KR, c_ckv // KR), _Rows(dcqn, QR), _Rows(dckvn, KR)],
        [small["mla_g_cq"], small["mla_g_ckv"]], [_Rows(QR, QR), _Rows(KR, KR)], [QR, KR],
        R=S, tr=TR, name="mla_norm_bwd")

    def hg_post_bwd(of, ob, gr, dy, ng):
        o = of + ob
        sg = _sigmoid(gr)
        do_, dgr = [], []
        dng = jnp.zeros((1, HG_D), F32)
        for h in range(HG_HEADS):
            sl = slice(h * HG_D, (h + 1) * HG_D)
            t = _rms_fwd(o[:, sl], ng, RMS_EPS)
            dgr.append(dy[:, sl] * t * sg[:, sl] * (1.0 - sg[:, sl]))
            dx, grow = _rms_bwd(o[:, sl], ng, dy[:, sl] * sg[:, sl], RMS_EPS)
            do_.append(dx)
            dng = dng + _colsum(grow)
        return jnp.concatenate(do_, axis=1), jnp.concatenate(dgr, axis=1), dng

    do_hg, dg_hg, dng = _rowwise(hg_post_bwd, [_Rows(o_fw, HW), _Rows(o_bw, HW), _Rows(P, HW, 4), _Rows(dys[0], HW)],
                                 [small["hgrn_norm_g"]], [_Rows(HW, HW), _Rows(HW, HW)], [HG_D],
                                 R=S, tr=TR, name="hgrn_post_bwd")
    dq_f, dv_f, dff_fw, dlb_f = _hgrn_scan_bwd(P, lb[0:1], st_fw, a_fw, do_hg, S=S, rev=False, name="hgrn_fw_bwd")
    dq_b, dv_b, dff_bw, dlb_b = _hgrn_scan_bwd(P, lb[1:2], st_bw, a_bw, do_hg, S=S, rev=True, name="hgrn_bw_bwd")
    THW = _tile(HW, 1024, LANES)
    dq_hg, dv_hg = _rowwise(lambda a, b, c, d: (a + b, c + d),
                            [_Rows(dq_f, THW), _Rows(dq_b, THW), _Rows(dv_f, THW), _Rows(dv_b, THW)], [],
                            [_Rows(HW, THW), _Rows(HW, THW)], [], R=S, tr=TR, ncol=HW // THW, name="hgrn_dir_sum")

    dP = jnp.concatenate([dq_hg, dv_hg, dff_fw, dff_bw, dg_hg, dcq, dckv, dq_mem, *dgates, dkr_raw], axis=1)
    g_w_in = _mm(h0, dP, D, PW, S, ta=True, tn=896, name="proj_in_dw")
    dx_kw = dict(tb=True, tk=_tile(PW, 640, LANES), name="proj_in_dx")
    if comm is None:
        dh0 = _mm(dP, w_in_k, S, D, PW, **dx_kw)
    else:
        exch = comm.scatter(dict(w_in=g_w_in, mla_w_uq=g_mla_w_uq, mla_w_ukv=g_mla_w_ukv))
        dh0, got = _mm(dP, w_in_k, S, D, PW, exch=exch, **dx_kw)
        delivered.update(zip(exch.names, got))

    def ln0_bwd(z, dmm, dz1v, g):
        return _ln_bwd_core(z, g, ALPHA * dz1v + dmm)

    grad_x, dge, dbe = _rowwise(ln0_bwd, [_Rows(x, D), _Rows(dh0, D), _Rows(dz1, D)], [ge],
                                [_Rows(D, D)], [D, D], R=S, tr=TR, name="ln_emb_bwd")

    big = dict(w_in=g_w_in, mla_w_uq=g_mla_w_uq, mla_w_ukv=g_mla_w_ukv, mem_w_kv=g_mem_w_kv,
               w_branch=g_w_branch, w_o=g_w_o, w_ffn_gu=g_ffn_gu, w_ffn_down=g_ffn_down)
    sm = dict(ln_emb_g=dge, ln_emb_b=dbe, dlb=jnp.concatenate([dlb_f, dlb_b], axis=0), hgrn_norm_g=dng,
              mla_g_cq=dg_cq, mla_g_ckv=dg_ckv, ln1_g=dg1, ln1_b=db1, ln2_g=dg2, ln2_b=db2)
    return loss, grad_x, big, sm, delivered


def _gather_exch(shards):
    copies = [(m, lambda x, y, c: 0, _chip) for m in _CHIP_MASKS]
    return _Exch([s[None] for s in shards], 4, copies, [(lambda x, y, c: 0, _chip)])


def _scatter_exch(pieces):
    copies = [(m, (lambda x, y, c, m=m: 2 * _chip(x ^ m[0], y ^ m[1], c) + (c ^ m[2])), _device) for m in _ALL_MASKS]
    local = [((lambda x, y, c: 2 * _chip(x, y, c) + c), _device)]
    return _Exch([p.reshape((8,) + p.shape[2:]) for p in pieces], 8, copies, local)


SHARE_BLOCK_BYTES = 4 << 20


def _sum_share(arr, *, name):
    n, rh, w = arr.shape
    tr = _tile(rh, max(16, SHARE_BLOCK_BYTES // (n * w * arr.dtype.itemsize) // 16 * 16), 16)
    nb = rh // tr

    def body(a_ref, o_ref, slots, send_sems, recv_sem, local_sems):
        i = pl.program_id(0)
        x, y, c = _coords()
        sibling = (x, y, 1 - c)

        def pushes(step, slot):
            rows = pl.ds(pl.multiple_of(c * rh + step * tr, SUBLANES), tr)
            return (pltpu.make_async_copy(slots.at[slot], o_ref.at[rows], local_sems.at[slot]),
                    pltpu.make_async_remote_copy(src_ref=slots.at[slot], dst_ref=o_ref.at[rows],
                                                 send_sem=send_sems.at[slot], recv_sem=recv_sem,
                                                 device_id=sibling, device_id_type=MESH))

        def drain(step, slot):
            loc, rem = pushes(step, slot)
            loc.wait()
            rem.wait_send()

        slot = i % 2

        @pl.when(i >= 2)
        def _():
            drain(i - 2, slot)

        acc = a_ref[0].astype(F32)
        for k in range(1, n):
            acc = acc + a_ref[k].astype(F32)
        slots[slot] = acc
        loc, rem = pushes(i, slot)
        loc.start()
        rem.start()

        @pl.when(i == nb - 1)
        def _():
            if nb >= 2:
                drain(i - 1, 1 - slot)
            drain(i, slot)
            other = o_ref.at[pl.ds(pl.multiple_of((1 - c) * rh, SUBLANES), rh)]
            pltpu.make_async_remote_copy(src_ref=other, dst_ref=other, send_sem=send_sems.at[0], recv_sem=recv_sem,
                                         device_id=sibling, device_id_type=MESH).wait_recv()

    return pl.pallas_call(
        body, name=name, grid=(nb,),
        in_specs=[pl.BlockSpec((n, tr, w), lambda i: (0, i, 0))],
        out_specs=pl.BlockSpec(memory_space=pl.ANY),
        out_shape=jax.ShapeDtypeStruct((2 * rh, w), F32),
        scratch_shapes=[pltpu.VMEM((2, tr, w), F32), pltpu.SemaphoreType.DMA((2,)), pltpu.SemaphoreType.DMA,
                        pltpu.SemaphoreType.DMA((2,))],
        compiler_params=pltpu.CompilerParams(dimension_semantics=("arbitrary",), has_side_effects=True,
                                             vmem_limit_bytes=VMEM_LIMIT),
    )(arr)


def _allreduce_small(v, *, name):
    r, w = v.shape

    def body(v_ref, o_ref, buf, send_sems, recv_sems):
        x, y, c = _coords()
        me = 4 * x + 2 * y + c
        buf[me] = v_ref[...]
        cps = []
        for k in range(7):
            m = ((k + 1) >> 2 & 1, (k + 1) >> 1 & 1, (k + 1) & 1)
            cp = pltpu.make_async_remote_copy(
                src_ref=v_ref, dst_ref=buf.at[me], send_sem=send_sems.at[k], recv_sem=recv_sems.at[k],
                device_id=(x ^ m[0], y ^ m[1], c ^ m[2]), device_id_type=MESH)
            cp.start()
            cps.append(cp)
        for cp in cps:
            cp.wait_recv()
        for cp in cps:
            cp.wait_send()
        acc = buf[0]
        for k in range(1, 8):
            acc = acc + buf[k]
        o_ref[...] = acc

    return pl.pallas_call(
        body, name=name,
        in_specs=[pl.BlockSpec(memory_space=pltpu.VMEM)],
        out_specs=pl.BlockSpec(memory_space=pltpu.VMEM),
        out_shape=jax.ShapeDtypeStruct((r, w), F32),
        scratch_shapes=[pltpu.VMEM((8, r, w), F32), pltpu.SemaphoreType.DMA((7,)), pltpu.SemaphoreType.DMA((7,))],
        compiler_params=pltpu.CompilerParams(has_side_effects=True),
    )(v)


_BIG = (("w_in", 1), ("mla_w_uq", 1), ("mla_w_ukv", 1), ("mem_w_kv", 0), ("w_branch", 1), ("w_o", 0),
        ("w_ffn_gate", 1), ("w_ffn_up", 1), ("w_ffn_down", 0))


def _assemble(gathered, ax):
    _, r, c = gathered.shape
    if ax == 0:
        return gathered.reshape(4 * r, c)
    return jnp.concatenate([gathered[j] for j in range(4)], axis=1)


def _split_pieces(g, ax):
    r, c = g.shape
    if ax == 0:
        return g.reshape(4, 2, r // 8, c).astype(BF16)
    rh, cs = r // 2, c // 4
    return jnp.stack([g[h * rh:(h + 1) * rh, j * cs:(j + 1) * cs].astype(BF16)
                      for j in range(4) for h in range(2)]).reshape(4, 2, rh, cs)


def _pad_cols(a, n):
    return jnp.pad(a, ((0, 0), (0, n - a.shape[1])))


def _to_kernel_layout(full, QR, KR):
    out = {}
    for n in ("mem_w_kv", "w_branch", "w_o", "w_ffn_down"):
        if n in full:
            out[n] = full[n]
    if "w_in" in full:
        w_in = full["w_in"]
        a = 5 * HG_HEADS * HG_D + QR + KR
        out["w_in"] = jnp.concatenate([w_in[:, :a], w_in[:, a + MLA_ROPE:], _pad_cols(w_in[:, a:a + MLA_ROPE], LANES)],
                                      axis=1)
    if "mla_w_uq" in full:
        uq = full["mla_w_uq"].reshape(QR, MLA_HEADS, MLA_NOPE + MLA_ROPE)
        out["mla_w_uq"] = jnp.pad(uq, ((0, 0), (0, 0), (0, 2 * LANES - MLA_NOPE - MLA_ROPE))).reshape(QR, -1)
    if "mla_w_ukv" in full:
        ukv = full["mla_w_ukv"].reshape(KR, MLA_HEADS, MLA_NOPE + MLA_V)
        out["mla_w_ukv"] = jnp.concatenate([ukv[:, :, :MLA_NOPE].reshape(KR, -1), ukv[:, :, MLA_NOPE:].reshape(KR, -1)],
                                           axis=1)
    if "w_ffn_gate" in full:
        gate, up = full["w_ffn_gate"], full["w_ffn_up"]
        DFF = gate.shape[1]
        TF = _tile(DFF, 512, LANES)
        blocks = []
        for j in range(DFF // TF):
            blocks += [gate[:, j * TF:(j + 1) * TF], up[:, j * TF:(j + 1) * TF]]
        out["w_ffn_gu"] = jnp.concatenate(blocks, axis=1)
    return out


def _from_kernel_layout(gk, QR, KR):
    out = {}
    for n in ("mem_w_kv", "w_o", "w_ffn_down"):
        if n in gk:
            out[n] = gk[n]
    if "w_branch" in gk:
        out["w_branch"] = gk["w_branch"].reshape(-1, gk["w_branch"].shape[-1])
    if "w_in" in gk:
        g = gk["w_in"]
        a = 5 * HG_HEADS * HG_D + QR + KR
        rest = g.shape[1] - LANES - a
        out["w_in"] = jnp.concatenate([g[:, :a], g[:, a + rest:a + rest + MLA_ROPE], g[:, a:a + rest]], axis=1)
    if "mla_w_uq" in gk:
        out["mla_w_uq"] = gk["mla_w_uq"].reshape(QR, MLA_HEADS, 2 * LANES)[:, :, :MLA_NOPE + MLA_ROPE].reshape(QR, -1)
    if "mla_w_ukv" in gk:
        VW = MLA_HEADS * MLA_V
        g = gk["mla_w_ukv"]
        out["mla_w_ukv"] = jnp.concatenate([g[:, :VW].reshape(KR, MLA_HEADS, MLA_NOPE),
                                            g[:, VW:].reshape(KR, MLA_HEADS, MLA_V)], axis=2).reshape(KR, -1)
    if "w_ffn_gu" in gk:
        g = gk["w_ffn_gu"]
        DFF = g.shape[1] // 2
        TF = _tile(DFF, 512, LANES)
        out["w_ffn_gate"] = jnp.concatenate([g[:, 2 * j * TF:(2 * j + 1) * TF] for j in range(DFF // TF)], axis=1)
        out["w_ffn_up"] = jnp.concatenate([g[:, (2 * j + 1) * TF:(2 * j + 2) * TF] for j in range(DFF // TF)], axis=1)
    return out


def _adamw(w, g, m, v, *, name):
    r, c = w.shape
    tr = max(SUBLANES, min(512, (1 << 20) // (4 * c)) // SUBLANES * SUBLANES)
    c1 = 1.0 / (1.0 - ADAM_B1 ** ADAM_STEP)
    c2 = 1.0 / (1.0 - ADAM_B2 ** ADAM_STEP)

    def fn(wv, gv, mv, vv):
        mn = ADAM_B1 * mv + (1.0 - ADAM_B1) * gv
        vn = ADAM_B2 * vv + (1.0 - ADAM_B2) * (gv * gv)
        delta = -ADAM_LR * ((mn * c1) / (jnp.sqrt(vn * c2) + ADAM_EPS) + ADAM_WD * wv)
        return delta, mn, vn

    return _rowwise(fn, [_Rows(a, c) for a in (w, g, m, v)], [], [_Rows(c, c)] * 3, [], R=r, tr=tr, name=name)


_SMALL = ("ln_emb_g", "ln_emb_b", "hgrn_lb_logits", "hgrn_norm_g", "mla_g_cq", "mla_g_ckv",
          "ln1_g", "ln1_b", "ln2_g", "ln2_b")


def _lb_from_logits(logits):
    return jnp.cumsum(jax.nn.softmax(logits, axis=1), axis=1)[:, 0]


def _small_rows(parts):
    flat = jnp.concatenate([p.reshape(-1) for p in parts])
    n = flat.shape[0]
    total = -(-n // (SUBLANES * LANES)) * SUBLANES * LANES
    return jnp.pad(flat, (0, total - n)).reshape(total // LANES, LANES)


def kernel(x, mem, positions, ln_emb_g, ln_emb_b, hgrn_lb_logits, w_in, hgrn_norm_g, mla_g_cq, mla_g_ckv, mla_w_uq, mla_w_ukv, mem_w_kv, w_branch, w_o, ln1_g, ln1_b, w_ffn_gate, w_ffn_up, w_ffn_down, ln2_g, ln2_b, loss_target, m_ln_emb_g, m_ln_emb_b, m_hgrn_lb_logits, m_w_in, m_hgrn_norm_g, m_mla_g_cq, m_mla_g_ckv, m_mla_w_uq, m_mla_w_ukv, m_mem_w_kv, m_w_branch, m_w_o, m_ln1_g, m_ln1_b, m_w_ffn_gate, m_w_ffn_up, m_w_ffn_down, m_ln2_g, m_ln2_b, v_ln_emb_g, v_ln_emb_b, v_hgrn_lb_logits, v_w_in, v_hgrn_norm_g, v_mla_g_cq, v_mla_g_ckv, v_mla_w_uq, v_mla_w_ukv, v_mem_w_kv, v_w_branch, v_w_o, v_ln1_g, v_ln1_b, v_w_ffn_gate, v_w_ffn_up, v_w_ffn_down, v_ln2_g, v_ln2_b):
    names = ["ln_emb_g", "ln_emb_b", "hgrn_lb_logits", "w_in", "hgrn_norm_g", "mla_g_cq", "mla_g_ckv", "mla_w_uq",
             "mla_w_ukv", "mem_w_kv", "w_branch", "w_o", "ln1_g", "ln1_b", "w_ffn_gate", "w_ffn_up", "w_ffn_down",
             "ln2_g", "ln2_b"]
    wts = dict(zip(names, [ln_emb_g, ln_emb_b, hgrn_lb_logits, w_in, hgrn_norm_g, mla_g_cq, mla_g_ckv, mla_w_uq,
                           mla_w_ukv, mem_w_kv, w_branch, w_o, ln1_g, ln1_b, w_ffn_gate, w_ffn_up, w_ffn_down,
                           ln2_g, ln2_b]))
    mom = dict(zip(names, [m_ln_emb_g, m_ln_emb_b, m_hgrn_lb_logits, m_w_in, m_hgrn_norm_g, m_mla_g_cq, m_mla_g_ckv,
                           m_mla_w_uq, m_mla_w_ukv, m_mem_w_kv, m_w_branch, m_w_o, m_ln1_g, m_ln1_b, m_w_ffn_gate,
                           m_w_ffn_up, m_w_ffn_down, m_ln2_g, m_ln2_b]))
    var = dict(zip(names, [v_ln_emb_g, v_ln_emb_b, v_hgrn_lb_logits, v_w_in, v_hgrn_norm_g, v_mla_g_cq, v_mla_g_ckv,
                           v_mla_w_uq, v_mla_w_ukv, v_mem_w_kv, v_w_branch, v_w_o, v_ln1_g, v_ln1_b, v_w_ffn_gate,
                           v_w_ffn_up, v_w_ffn_down, v_ln2_g, v_ln2_b]))
    xc, yc, cc = _coords()
    chip = _chip(xc, yc, cc)
    S, D = x.shape[1], x.shape[2]

    axis = dict(_BIG)
    shard = lambda n: wts[n].reshape(-1, wts[n].shape[-1]).astype(BF16)
    QR, KR = mla_w_uq.shape[1], mla_w_ukv.shape[1]
    MW, DFF = mem_w_kv.shape[2] // 2, 4 * w_ffn_gate.shape[2]
    (w_in_all,) = _exchange(_gather_exch([shard("w_in")]), name="gather_w_in")
    w_in_k = _to_kernel_layout(dict(w_in=_assemble(w_in_all, axis["w_in"])), QR, KR)["w_in"]
    rest_names = [n for n, _ in _BIG if n != "w_in"]

    class _Comm:
        gather_rest = _gather_exch([shard(n) for n in rest_names])

        @staticmethod
        def weights(received):
            full = {n: _assemble(g, axis[n]) for n, g in zip(rest_names, received)}
            full["w_branch"] = full["w_branch"].reshape(N_BRANCH, -1, D)
            return _to_kernel_layout(full, QR, KR)

        @staticmethod
        def scatter(gk_part):
            gpart = _from_kernel_layout(gk_part, QR, KR)
            exch = _scatter_exch([_split_pieces(gpart[n], axis[n]) for n in gpart])
            exch.names = list(gpart)
            return exch

    lsh = hgrn_lb_logits.shape
    HW = 4 * lsh[2]
    placed = lax.dynamic_update_slice(jnp.zeros((lsh[0], lsh[1], HW), F32), hgrn_lb_logits, (0, 0, chip * lsh[2]))
    placed = jnp.where(cc == 0, placed, 0.0)
    logits = _allreduce_small(_small_rows([placed]), name="gather_logits").reshape(-1)[:placed.size].reshape(placed.shape)
    lb, lb_vjp = jax.vjp(_lb_from_logits, logits)

    small = dict(ln_emb_g=ln_emb_g, ln_emb_b=ln_emb_b, lb=lb, hgrn_norm_g=hgrn_norm_g, mla_g_cq=mla_g_cq,
                 mla_g_ckv=mla_g_ckv, ln1_g=ln1_g, ln1_b=ln1_b, ln2_g=ln2_g, ln2_b=ln2_b)
    loss_l, grad_x, _, gs, delivered = _local_step(x[0], mem[0], positions[0], loss_target[0], w_in_k, small, None,
                                                   MW, DFF, comm=_Comm)

    (dlogits,) = lb_vjp(gs["dlb"])
    sm_parts = [loss_l, gs["ln_emb_g"], gs["ln_emb_b"], dlogits, gs["hgrn_norm_g"], gs["mla_g_cq"], gs["mla_g_ckv"],
                gs["ln1_g"], gs["ln1_b"], gs["ln2_g"], gs["ln2_b"]]
    red = _allreduce_small(_small_rows(sm_parts), name="allreduce_small").reshape(-1)
    sm_out, off = [], 0
    for p in sm_parts:
        sm_out.append(red[off:off + p.size].reshape(p.shape))
        off += p.size
    loss = sm_out[0].reshape(())
    g_small = dict(zip(_SMALL, sm_out[1:]))
    g_small["hgrn_lb_logits"] = lax.dynamic_slice(g_small["hgrn_lb_logits"], (0, 0, chip * lsh[2]), lsh)
    for n in _SMALL:
        g_small[n] = g_small[n].reshape(wts[n].shape)

    g_big = {n: _sum_share(delivered[n], name="rs_sum_" + n).reshape(wts[n].shape) for n, _ in _BIG}

    grads = {**g_small, **g_big}
    delta, new_m, new_v = {}, {}, {}
    for n, _ in _BIG:
        shp = wts[n].shape
        two_d = lambda a: a.reshape(-1, shp[-1])
        d_, m_, v_ = _adamw(two_d(wts[n]), two_d(grads[n]), two_d(mom[n]), two_d(var[n]), name="adamw_" + n)
        delta[n], new_m[n], new_v[n] = d_.reshape(shp), m_.reshape(shp), v_.reshape(shp)
    sw, sg_, sm_, sv_ = (_small_rows([d[n] for n in _SMALL]) for d in (wts, grads, mom, var))
    d_, m_, v_ = _adamw(sw, sg_, sm_, sv_, name="adamw_small")
    for res, packed_rows in ((delta, d_), (new_m, m_), (new_v, v_)):
        flat, off = packed_rows.reshape(-1), 0
        for n in _SMALL:
            res[n] = flat[off:off + wts[n].size].reshape(wts[n].shape)
            off += wts[n].size

    return (loss, grad_x[None], *[grads[n] for n in names], *[delta[n] for n in names],
            *[new_m[n] for n in names], *[new_v[n] for n in names])
```

```python
import jax
import jax.numpy as jnp
from jax import lax
from jax.experimental import pallas as pl
from jax.experimental.pallas import tpu as pltpu

F32 = jnp.float32
BF16 = jnp.bfloat16

HG_HEADS = 8
HG_D = 128
MLA_HEADS = 8
MLA_NOPE = 128
MLA_ROPE = 64
MLA_V = 128
MEM_HEADS = 4
N_BRANCH = 3
ROPE_THETA = 10000.0
DEPTH = 1
ALPHA = (2.0 * DEPTH) ** 0.25
LN_EPS = 1e-5
RMS_EPS = 1e-6
ADAM_LR = 0.001
ADAM_B1 = 0.9
ADAM_B2 = 0.999
ADAM_EPS = 1e-08
ADAM_WD = 0.01
ADAM_STEP = 10

LANES = 128
SUBLANES = 8
VMEM_LIMIT = 48 * 1024 * 1024

HG_CHUNK = 128
HG_SUB = 16
HG_PAIR = 2

MESH = pl.DeviceIdType.MESH
HI = lax.Precision.HIGHEST
HG_OFF_PREC = lax.Precision.DEFAULT


def _cparams(sem=None):
    if sem is None:
        return pltpu.CompilerParams(vmem_limit_bytes=VMEM_LIMIT)
    return pltpu.CompilerParams(dimension_semantics=sem, vmem_limit_bytes=VMEM_LIMIT)


def _tile(dim, pref, quantum):
    t = min(pref, dim) // quantum * quantum
    while t >= quantum:
        if dim % t == 0:
            return t
        t -= quantum
    return dim


def _sigmoid(x):
    return 1.0 / (1.0 + jnp.exp(-x))


def _coords():
    return lax.axis_index("x"), lax.axis_index("y"), lax.axis_index("c")


def _chip(x, y, c):
    return 2 * x + y


def _device(x, y, c):
    return 4 * x + 2 * y + c


_CHIP_MASKS = ((1, 0, 0), (0, 1, 0), (1, 1, 0))
_ALL_MASKS = tuple((k >> 2 & 1, k >> 1 & 1, k & 1) for k in range(1, 8))
_HBM = pl.BlockSpec(memory_space=pl.ANY)


class _Exch:
    def __init__(self, srcs, n_dst, copies, local_copies):
        self.srcs, self.n_dst, self.copies, self.local_copies = list(srcs), n_dst, copies, local_copies
        self.n = len(self.srcs)

    def out_shape(self):
        return [jax.ShapeDtypeStruct((self.n_dst,) + s.shape[1:], s.dtype) for s in self.srcs]

    def scratch(self):
        n_rc, n_lc = self.n * len(self.copies), self.n * len(self.local_copies)
        return [pltpu.SemaphoreType.DMA((n_rc,)), pltpu.SemaphoreType.DMA((n_rc,)),
                pltpu.SemaphoreType.DMA((max(n_lc, 1),))]

    def _descriptors(self, src_refs, dst_refs, sems):
        send_sems, recv_sems, local_sems = sems
        x, y, c = _coords()
        n_rc, n_lc = len(self.copies), len(self.local_copies)
        remote, local = [], []
        for a in range(self.n):
            for k, (mask, sidx, didx) in enumerate(self.copies):
                remote.append(pltpu.make_async_remote_copy(
                    src_ref=src_refs[a].at[sidx(x, y, c)], dst_ref=dst_refs[a].at[didx(x, y, c)],
                    send_sem=send_sems.at[a * n_rc + k], recv_sem=recv_sems.at[a * n_rc + k],
                    device_id=(x ^ mask[0], y ^ mask[1], c ^ mask[2]), device_id_type=MESH))
            for k, (sidx, didx) in enumerate(self.local_copies):
                local.append(pltpu.make_async_copy(src_refs[a].at[sidx(x, y, c)], dst_refs[a].at[didx(x, y, c)],
                                                   local_sems.at[a * n_lc + k]))
        return remote, local

    def start(self, src_refs, dst_refs, sems):
        remote, local = self._descriptors(src_refs, dst_refs, sems)
        for cp in remote + local:
            cp.start()

    def wait(self, src_refs, dst_refs, sems):
        remote, local = self._descriptors(src_refs, dst_refs, sems)
        for cp in remote:
            cp.wait_recv()
        for cp in remote:
            cp.wait_send()
        for cp in local:
            cp.wait()


def _exchange(exch, *, name):
    n = exch.n

    def body(*refs):
        src_refs, dst_refs, sems = refs[:n], refs[n:2 * n], refs[2 * n:]
        exch.start(src_refs, dst_refs, sems)
        exch.wait(src_refs, dst_refs, sems)

    return pl.pallas_call(
        body, name=name, in_specs=[_HBM] * n, out_specs=[_HBM] * n, out_shape=exch.out_shape(),
        scratch_shapes=exch.scratch(), compiler_params=pltpu.CompilerParams(has_side_effects=True),
    )(*exch.srcs)


def _carried(call, exch, grid, in_specs, out_specs, out_shape, scratch_shapes, args, *, name):
    n_in, n_out, n_scr = len(in_specs), len(out_specs), len(scratch_shapes)
    n = 0 if exch is None else exch.n

    def body(*refs):
        o0 = n_in + n
        s0 = o0 + n_out + n
        ins, srcs = refs[:n_in], refs[n_in:o0]
        outs, dsts = refs[o0:o0 + n_out], refs[o0 + n_out:s0]
        scr, sems = refs[s0:s0 + n_scr], refs[s0 + n_scr:]
        if exch is not None:
            ids = [pl.program_id(d) for d in range(len(grid))]
            first = _all([i == 0 for i in ids])
            last = _all([i == g - 1 for i, g in zip(ids, grid)])

            @pl.when(first)
            def _():
                exch.start(srcs, dsts, sems)

        call(*ins, *outs, *scr)
        if exch is not None:
            @pl.when(last)
            def _():
                exch.wait(srcs, dsts, sems)

    if exch is None:
        params = pltpu.CompilerParams(dimension_semantics=("arbitrary",) * len(grid), vmem_limit_bytes=VMEM_LIMIT)
        extra_in, extra_out, extra_shape, extra_scr, extra_args = [], [], [], [], []
    else:
        params = pltpu.CompilerParams(dimension_semantics=("arbitrary",) * len(grid), vmem_limit_bytes=VMEM_LIMIT,
                                      has_side_effects=True)
        extra_in, extra_out, extra_shape = [_HBM] * n, [_HBM] * n, exch.out_shape()
        extra_scr, extra_args = exch.scratch(), exch.srcs
    res = pl.pallas_call(
        body, name=name, grid=grid, in_specs=list(in_specs) + extra_in, out_specs=list(out_specs) + extra_out,
        out_shape=list(out_shape) + extra_shape, scratch_shapes=list(scratch_shapes) + extra_scr,
        compiler_params=params,
    )(*args, *extra_args)
    return res[:n_out], res[n_out:]


def _all(conds):
    out = conds[0]
    for c in conds[1:]:
        out = jnp.logical_and(out, c)
    return out


def _mm(a, b, M, N, K, *, ta=False, tb=False, a_off=(0, 0), b_off=(0, 0), add=None, exch=None, epilogue=None,
        tm=1024, tn=1024, tk=1024, name):
    tm = _tile(M, tm, LANES if ta else SUBLANES)
    tn = _tile(N, tn, LANES)
    tk = _tile(K, tk, LANES)
    nk = K // tk
    ar, ac = a_off
    br, bc = b_off

    if ta:
        assert ar % tk == 0 and ac % tm == 0
        a_spec = pl.BlockSpec((tk, tm), lambda i, j, k: (ar // tk + k, ac // tm + i))
    else:
        assert ar % tm == 0 and ac % tk == 0
        a_spec = pl.BlockSpec((tm, tk), lambda i, j, k: (ar // tm + i, ac // tk + k))
    if tb:
        assert br % tn == 0 and bc % tk == 0
        b_spec = pl.BlockSpec((tn, tk), lambda i, j, k: (br // tn + j, bc // tk + k))
    else:
        assert br % tk == 0 and bc % tn == 0
        b_spec = pl.BlockSpec((tk, tn), lambda i, j, k: (br // tk + k, bc // tn + j))
    o_spec = pl.BlockSpec((tm, tn), lambda i, j, k: (i, j))
    mixed = a.dtype != b.dtype

    epi_fn, epi_ins, epi_outs = (None, [], [1]) if epilogue is None else epilogue
    n_in = 2 + (add is not None) + len(epi_ins)

    def body(*refs):
        a_ref, b_ref = refs[:2]
        add_ref = refs[2] if add is not None else None
        epi_refs = refs[n_in - len(epi_ins):n_in]
        o_refs, acc = refs[n_in:n_in + len(epi_outs)], refs[-1]
        k = pl.program_id(2)
        av = a_ref[...]
        bv = b_ref[...]
        if ta:
            av = av.astype(F32).T
        if mixed:
            av = av.astype(BF16)
            bv = bv.astype(BF16)
        dims = (((1,), (1 if tb else 0,)), ((), ()))
        d = lax.dot_general(av, bv, dims, preferred_element_type=F32)

        def finish(total):
            if add is not None:
                total = total + add_ref[...]
            tiles = [total] if epi_fn is None else epi_fn(total, *[r[...] for r in epi_refs])
            for o_ref, t in zip(o_refs, tiles):
                o_ref[...] = t

        if nk == 1:
            finish(d)
        else:
            @pl.when(k == 0)
            def _():
                acc[...] = d

            @pl.when(jnp.logical_and(k > 0, k < nk - 1))
            def _():
                acc[...] += d

            @pl.when(k == nk - 1)
            def _():
                finish(acc[...] + d)

    in_specs = [a_spec, b_spec]
    args = [a, b]
    if add is not None:
        in_specs.append(o_spec)
        args.append(add)
    wide = lambda w: pl.BlockSpec((tm, tn * w[0] // w[1]), lambda i, j, k: (i, j))
    for arr, w in epi_ins:
        in_specs.append(wide(w))
        args.append(arr)
    out_specs = [o_spec if w == 1 else wide(w) for w in epi_outs]
    out_shape = [jax.ShapeDtypeStruct((M, N if w == 1 else N * w[0] // w[1]), F32) for w in epi_outs]
    outs, received = _carried(body, exch, (M // tm, N // tn, nk), in_specs, out_specs, out_shape,
                              [pltpu.VMEM((tm, tn), F32)], args, name=name)
    outs = outs[0] if epilogue is None else outs
    return outs if exch is None else (outs, received)


class _Rows:
    def __init__(self, arr, width, col0=0, lead=None, dtype=F32):
        self.arr, self.width, self.col0, self.lead, self.dtype = arr, width, col0, lead, dtype


def _rowwise(fn, rows, consts, outs, accs, *, R, tr, ncol=1, name):
    tr = _tile(R, tr, SUBLANES)
    nrow = R // tr

    def spec(r):
        if r.lead is None:
            return pl.BlockSpec((tr, r.width), lambda j, i, c0=r.col0: (i, c0 + j))
        return pl.BlockSpec((None, tr, r.width), lambda j, i, c0=r.col0, l=r.lead: (l, i, c0 + j))

    in_specs = [spec(r) for r in rows]
    for c in consts:
        in_specs.append(pl.BlockSpec(c.shape, lambda j, i, nd=c.ndim: (0,) * nd))
    out_specs = [spec(o) for o in outs]
    out_shape = [jax.ShapeDtypeStruct((R, o.arr), o.dtype) for o in outs]
    for w in accs:
        out_specs.append(pl.BlockSpec((1, w), lambda j, i: (0, j)))
        out_shape.append(jax.ShapeDtypeStruct((1, w * ncol), F32))
    n_in = len(rows) + len(consts)
    n_out = len(outs)

    def body(*refs):
        ins = [r[...] for r in refs[:n_in]]
        res = fn(*ins)
        if not isinstance(res, (tuple, list)):
            res = (res,)
        for k in range(n_out):
            refs[n_in + k][...] = res[k].astype(refs[n_in + k].dtype)
        i = pl.program_id(1)
        for k in range(len(accs)):
            a_ref = refs[n_in + n_out + k]

            @pl.when(i == 0)
            def _(a_ref=a_ref):
                a_ref[...] = jnp.zeros_like(a_ref)

            a_ref[...] += res[n_out + k]

    res = pl.pallas_call(
        body, name=name, grid=(ncol, nrow),
        in_specs=in_specs, out_specs=out_specs, out_shape=out_shape,
        compiler_params=_cparams(("parallel", "arbitrary")),
    )(*[r.arr for r in rows], *consts)
    return res


def _colsum(x):
    return jnp.sum(x, axis=0, keepdims=True)


def _ln_stats(z):
    mu = jnp.mean(z, axis=-1, keepdims=True)
    zc = z - mu
    var = jnp.mean(zc * zc, axis=-1, keepdims=True)
    rstd = lax.rsqrt(var + LN_EPS)
    return zc * rstd, rstd


def _ln_bwd_core(z, g, dy):
    xhat, rstd = _ln_stats(z)
    dxh = dy * g
    m1 = jnp.mean(dxh, axis=-1, keepdims=True)
    m2 = jnp.mean(dxh * xhat, axis=-1, keepdims=True)
    dz = rstd * (dxh - m1 - xhat * m2)
    return dz, _colsum(dy * xhat), _colsum(dy)


def _rms_fwd(x, g, eps):
    r = lax.rsqrt(jnp.mean(x * x, axis=-1, keepdims=True) + eps)
    return x * r * g


def _rms_bwd(x, g, dy, eps):
    r = lax.rsqrt(jnp.mean(x * x, axis=-1, keepdims=True) + eps)
    xr = x * r
    dyg = dy * g
    dx = r * (dyg - xr * jnp.mean(dyg * xr, axis=-1, keepdims=True))
    return dx, dy * xr


def _hg_gate(fr, lb):
    sig = _sigmoid(fr)
    f = lb + (1.0 - lb) * sig
    return sig, f


def _hg_masks(rev):
    C = HG_CHUNK
    t = lax.broadcasted_iota(jnp.int32, (C, C), 0)
    s = lax.broadcasted_iota(jnp.int32, (C, C), 1)
    tri = (s >= t) if rev else (s <= t)
    return tri


def _hg_offdiag(Q, K, b, i, rev):
    C, sb = HG_CHUNK, HG_SUB
    nb = C // sb
    if (not rev and i == 0) or (rev and i == nb - 1):
        return None
    ref = b[sb * i - 1:sb * i] if not rev else b[sb * (i + 1):sb * (i + 1) + 1]
    srow = lax.broadcasted_iota(jnp.int32, (C, 1), 0)
    smask = (srow < sb * i) if not rev else (srow >= sb * (i + 1))
    qscale = jnp.exp(jnp.minimum(b - ref, 0.0))
    kscale = jnp.where(smask, jnp.exp(jnp.minimum(ref - b, 0.0)), 0.0)
    return qscale, kscale


def _hg_att(Q, K, b, rev):
    C, sb = HG_CHUNK, HG_SUB
    lane = lax.broadcasted_iota(jnp.int32, (sb, C), 1)
    rloc = lax.broadcasted_iota(jnp.int32, (sb, 1), 0)
    rows = []
    for i in range(C // sb):
        sl = slice(sb * i, sb * i + sb)
        Qi, Ki, bi = Q[sl], K[sl], b[sl]
        od = _hg_offdiag(Q, K, b, i, rev)
        if od is None:
            acc = jnp.zeros((sb, C), F32)
        else:
            qs, ks = od
            acc = lax.dot_general(Qi * qs[sl], K * ks, (((1,), (1,)), ((), ())),
                                  precision=HG_OFF_PREC, preferred_element_type=F32)
        for j in range(sb):
            e = jnp.exp(jnp.minimum(bi - bi[j:j + 1], 0.0))
            col = jnp.sum(Qi * Ki[j:j + 1] * e, axis=-1, keepdims=True)
            vis = (rloc <= j) if rev else (rloc >= j)
            acc = jnp.where(lane == sb * i + j, jnp.where(vis, col, 0.0), acc)
        rows.append(acc)
    return jnp.concatenate(rows, axis=0)


def _hg_att_bwd(Q, K, b, dA, rev):
    C, sb = HG_CHUNK, HG_SUB
    rloc = lax.broadcasted_iota(jnp.int32, (sb, 1), 0)
    rrow = lax.broadcasted_iota(jnp.int32, (sb, HG_D), 0)
    trow = lax.broadcasted_iota(jnp.int32, (C, C), 1) // sb
    dAT = dA.T
    dQ_rows, dKd_rows = [], []
    dK = jnp.zeros((C, HG_D), F32)
    for i in range(C // sb):
        sl = slice(sb * i, sb * i + sb)
        Qi, Ki, bi, dAi = Q[sl], K[sl], b[sl], dA[sl]
        od = _hg_offdiag(Q, K, b, i, rev)
        if od is None:
            dQi = jnp.zeros((sb, HG_D), F32)
        else:
            qs, ks = od
            dQi = lax.dot_general(dAi, K * ks, (((1,), (0,)), ((), ())),
                                  precision=HI, preferred_element_type=F32) * qs[sl]
            zt = jnp.where(trow == i, dAT, 0.0)
            dK = dK + lax.dot_general(zt, Q * qs, (((1,), (0,)), ((), ())),
                                      precision=HI, preferred_element_type=F32) * ks
        dKd = jnp.zeros((sb, HG_D), F32)
        for j in range(sb):
            vis = (rloc <= j) if rev else (rloc >= j)
            e = jnp.where(vis, jnp.exp(jnp.minimum(bi - bi[j:j + 1], 0.0)), 0.0)
            dcol = dAi[:, sb * i + j:sb * i + j + 1]
            dQi = dQi + dcol * Ki[j:j + 1] * e
            krow = jnp.sum(dcol * Qi * e, axis=0, keepdims=True)
            dKd = jnp.where(rrow == j, krow, dKd)
        dQ_rows.append(dQi)
        dKd_rows.append(dKd)
    return jnp.concatenate(dQ_rows, axis=0), dK + jnp.concatenate(dKd_rows, axis=0)


def _hg_prep(qr, fr, lb, tri):
    sigq = _sigmoid(qr)
    Q = qr * sigq
    sig, f = _hg_gate(fr, lb)
    K = 1.0 - f
    logf = jnp.log(f)
    b = lax.dot_general(tri.astype(F32), logf, (((1,), (0,)), ((), ())),
                        precision=HI, preferred_element_type=F32)
    return sigq, Q, sig, f, K, b


def _hgrn_scan(P, lb, *, S, rev, name):
    C, H, D_ = HG_CHUNK, HG_HEADS, HG_D
    HP = HG_PAIR if H % HG_PAIR == 0 else 1
    NC = S // C
    fcol = (3 if rev else 2) * H

    def cidx(n):
        return NC - 1 - n if rev else n

    def body(q_ref, v_ref, f_ref, lb_ref, o_ref, st_ref, a_ref, state):
        n = pl.program_id(1)

        @pl.when(n == 0)
        def _():
            state[...] = jnp.zeros_like(state)

        tri = _hg_masks(rev)
        qa, va, fa, lba, sta = q_ref[...], v_ref[...], f_ref[...], lb_ref[...], state[...]
        st_ref[...] = sta
        outs, amats, states = [], [], []
        for hp in range(HP):
            sl = slice(hp * D_, (hp + 1) * D_)
            _, Q, _, _, K, b = _hg_prep(qa[:, sl], fa[:, sl], lba[:, sl], tri)
            V, ST0 = va[:, sl], sta[hp]
            e_b = jnp.exp(b)
            bE = b[0:1] if rev else b[C - 1:C]
            W = jnp.exp(bE - b)
            inter = lax.dot_general(Q * e_b, ST0, (((1,), (1,)), ((), ())), preferred_element_type=F32)
            A = _hg_att(Q, K, b, rev)
            amats.append(A)
            outs.append(inter + jnp.dot(A, V, preferred_element_type=F32))
            states.append(ST0 * jnp.exp(bE) + lax.dot_general(
                V, K * W, (((0,), (0,)), ((), ())), preferred_element_type=F32))
        a_ref[...] = jnp.stack(amats)
        o_ref[...] = jnp.concatenate(outs, axis=1)
        state[...] = jnp.stack(states)

    blk = lambda c0: pl.BlockSpec((C, HP * D_), lambda h, n, c0=c0: (cidx(n), c0 // HP + h))
    return pl.pallas_call(
        body, name=name, grid=(H // HP, NC),
        in_specs=[blk(0), blk(H), blk(fcol), pl.BlockSpec((1, HP * D_), lambda h, n: (0, h))],
        out_specs=[pl.BlockSpec((C, HP * D_), lambda h, n: (cidx(n), h)),
                   pl.BlockSpec((None, HP, D_, D_), lambda h, n: (cidx(n), h, 0, 0)),
                   pl.BlockSpec((None, HP, C, C), lambda h, n: (cidx(n), h, 0, 0))],
        out_shape=[jax.ShapeDtypeStruct((S, H * D_), F32),
                   jax.ShapeDtypeStruct((NC, H, D_, D_), F32),
                   jax.ShapeDtypeStruct((NC, H, C, C), F32)],
        scratch_shapes=[pltpu.VMEM((HP, D_, D_), F32)],
        compiler_params=_cparams(("parallel", "arbitrary")),
    )(P, P, P, lb)


def _hgrn_scan_bwd(P, lb, st, amat, do, *, S, rev, name):
    C, H, D_ = HG_CHUNK, HG_HEADS, HG_D
    HP = HG_PAIR if H % HG_PAIR == 0 else 1
    NC = S // C
    fcol = (3 if rev else 2) * H

    def cidx(n):
        return n if rev else NC - 1 - n

    def body(q_ref, v_ref, f_ref, lb_ref, st_ref, a_ref, do_ref, dq_ref, dv_ref, df_ref, dlb_ref, dstate):
        n = pl.program_id(1)

        @pl.when(n == 0)
        def _():
            dstate[...] = jnp.zeros_like(dstate)
            dlb_ref[...] = jnp.zeros_like(dlb_ref)

        tri = _hg_masks(rev)
        tri_t = _hg_masks(not rev).astype(F32)
        qa, va, fa, lba, doa = q_ref[...], v_ref[...], f_ref[...], lb_ref[...], do_ref[...]
        sta, ama, dsta = st_ref[...], a_ref[...], dstate[...]
        trow = lax.broadcasted_iota(jnp.int32, (C, 1), 0)
        dqs, dvs, dfs, dlbs, dstates = [], [], [], [], []
        for hp in range(HP):
            sl = slice(hp * D_, (hp + 1) * D_)
            lbv, qr = lba[:, sl], qa[:, sl]
            sigq, Q, sig, f, K, b = _hg_prep(qr, fa[:, sl], lbv, tri)
            V, ST0, A, dO, dST1 = va[:, sl], sta[hp], ama[hp], doa[:, sl], dsta[hp]
            e_b = jnp.exp(b)
            bE = b[0:1] if rev else b[C - 1:C]
            eE = jnp.exp(bE)
            W = jnp.exp(bE - b)
            Qe = Q * e_b
            KW = K * W
            dA = jnp.where(tri, lax.dot_general(dO, V, (((1,), (1,)), ((), ())), preferred_element_type=F32), 0.0)
            dV = (lax.dot_general(A, dO, (((0,), (0,)), ((), ())), preferred_element_type=F32)
                  + lax.dot_general(KW, dST1, (((1,), (1,)), ((), ())), preferred_element_type=F32))
            dQe = jnp.dot(dO, ST0, preferred_element_type=F32)
            dKW = jnp.dot(V, dST1, preferred_element_type=F32)
            dstates.append(dST1 * eE + lax.dot_general(dO, Qe, (((0,), (0,)), ((), ())), preferred_element_type=F32))
            dQa, dKa = _hg_att_bwd(Q, K, b, dA, rev)
            dQ = dQe * e_b + dQa
            dK = dKW * W + dKa
            extra = _colsum(KW * dKW) + eE * _colsum(ST0 * dST1)
            db = Q * dQ - K * dK + jnp.where(trow == (0 if rev else C - 1), extra, 0.0)
            dlogf = lax.dot_general(tri_t, db, (((1,), (0,)), ((), ())), precision=HI, preferred_element_type=F32)
            dfv = dlogf / f - dK
            dfs.append(dfv * (1.0 - lbv) * sig * (1.0 - sig))
            dlbs.append(_colsum(dfv * (1.0 - sig)))
            dqs.append(dQ * (sigq * (1.0 + qr * (1.0 - sigq))))
            dvs.append(dV)
        dstate[...] = jnp.stack(dstates)
        df_ref[...] = jnp.concatenate(dfs, axis=1)
        dlb_ref[...] += jnp.concatenate(dlbs, axis=1)
        dq_ref[...] = jnp.concatenate(dqs, axis=1)
        dv_ref[...] = jnp.concatenate(dvs, axis=1)

    blk = lambda c0: pl.BlockSpec((C, HP * D_), lambda h, n, c0=c0: (cidx(n), c0 // HP + h))
    oblk = pl.BlockSpec((C, HP * D_), lambda h, n: (cidx(n), h))
    return pl.pallas_call(
        body, name=name, grid=(H // HP, NC),
        in_specs=[blk(0), blk(H), blk(fcol), pl.BlockSpec((1, HP * D_), lambda h, n: (0, h)),
                  pl.BlockSpec((None, HP, D_, D_), lambda h, n: (cidx(n), h, 0, 0)),
                  pl.BlockSpec((None, HP, C, C), lambda h, n: (cidx(n), h, 0, 0)),
                  oblk],
        out_specs=[oblk, oblk, oblk, pl.BlockSpec((1, HP * D_), lambda h, n: (0, h))],
        out_shape=[jax.ShapeDtypeStruct((S, H * D_), F32)] * 3 + [jax.ShapeDtypeStruct((1, H * D_), F32)],
        scratch_shapes=[pltpu.VMEM((HP, D_, D_), F32)],
        compiler_params=_cparams(("parallel", "arbitrary")),
    )(P, P, P, lb, st, amat, do)


LOG2E = 1.4426950408889634
MXU = BF16
ATT_SUB = 512


def _mx(x):
    return x if x.dtype == MXU else x.astype(MXU)


def _attn_fwd(q, k, v, *, S, T, H, dqk, dv, q_col0, k_col0, v_col0, scale, tq, tk, name):
    tq = _tile(S, tq, SUBLANES)
    tk = _tile(T, tk, LANES)
    nk = T // tk
    ts = _tile(tq, ATT_SUB, SUBLANES)

    def body(q_ref, k_ref, v_ref, o_ref, lse_ref, m_s, l_s, acc):
        j = pl.program_id(2)

        @pl.when(j == 0)
        def _():
            m_s[...] = jnp.full_like(m_s, -jnp.inf)
            l_s[...] = jnp.zeros_like(l_s)
            acc[...] = jnp.zeros_like(acc)

        kv, vv = _mx(k_ref[...]), _mx(v_ref[...])
        m_all, l_all, a_all = m_s[...], l_s[...], acc[...]
        ms, ls, accs = [], [], []
        for r0 in range(0, tq, ts):
            rows = slice(r0, r0 + ts)
            s = lax.dot_general(_mx(q_ref[rows, :]), kv, (((1,), (1,)), ((), ())),
                                preferred_element_type=F32) * (scale * LOG2E)
            m_old = m_all[rows]
            m_new = jnp.maximum(m_old, jnp.max(s, axis=1)[:, None])
            corr = jnp.exp2(m_old - m_new)
            p = jnp.exp2(s - jnp.tile(m_new, (1, tk // LANES)))
            ms.append(m_new)
            ls.append(corr * l_all[rows] + jnp.sum(p, axis=1)[:, None])
            accs.append(jnp.tile(corr, (1, dv // LANES)) * a_all[rows] + jnp.dot(_mx(p), vv, preferred_element_type=F32))
        m_s[...] = jnp.concatenate(ms, axis=0)
        l_s[...] = jnp.concatenate(ls, axis=0)
        acc[...] = jnp.concatenate(accs, axis=0)

        @pl.when(j == nk - 1)
        def _():
            o_ref[...] = acc[...] / jnp.tile(l_s[...], (1, dv // LANES))
            lse_ref[...] = ((m_s[...] + jnp.log2(l_s[...])) * (1.0 / LOG2E))[:, :1]

    return pl.pallas_call(
        body, name=name, grid=(H, S // tq, nk),
        in_specs=[pl.BlockSpec((tq, dqk), lambda h, i, j: (i, q_col0 + h)),
                  pl.BlockSpec((tk, dqk), lambda h, i, j: (j, k_col0 + h)),
                  pl.BlockSpec((tk, dv), lambda h, i, j: (j, v_col0 + h))],
        out_specs=[pl.BlockSpec((tq, dv), lambda h, i, j: (i, h)),
                   pl.BlockSpec((None, tq, 1), lambda h, i, j: (h, i, 0))],
        out_shape=[jax.ShapeDtypeStruct((S, H * dv), F32), jax.ShapeDtypeStruct((H, S, 1), F32)],
        scratch_shapes=[pltpu.VMEM((tq, LANES), F32), pltpu.VMEM((tq, LANES), F32), pltpu.VMEM((tq, dv), F32)],
        compiler_params=_cparams(("parallel", "parallel", "arbitrary")),
    )(q, k, v)


def _attn_bwd(q, k, v, o, lse, do, *, S, T, H, dqk, dv, q_col0, k_col0, v_col0, scale, tq, tk, exch=None, name):
    tq = _tile(S, tq, SUBLANES)
    tk = _tile(T, tk, LANES)
    nq = S // tq
    ts = _tile(tq, ATT_SUB, SUBLANES)

    def body(q_ref, k_ref, v_ref, o_ref, lse_ref, do_ref, dq_ref, dk_ref, dv_ref, dk_acc, dv_acc):
        j = pl.program_id(1)
        i = pl.program_id(2)

        @pl.when(jnp.logical_and(i == 0, j == 0))
        def _():
            dq_ref[...] = jnp.zeros_like(dq_ref)

        @pl.when(i == 0)
        def _():
            dk_acc[...] = jnp.zeros_like(dk_acc)
            dv_acc[...] = jnp.zeros_like(dv_acc)

        kv, vv = _mx(k_ref[...]), _mx(v_ref[...])
        dk_new, dv_new = dk_acc[...], dv_acc[...]
        lse2 = lse_ref[...] * LOG2E
        dqs = []
        for r0 in range(0, tq, ts):
            rows = slice(r0, r0 + ts)
            qv, dov = _mx(q_ref[rows, :]), do_ref[rows, :]
            s = lax.dot_general(qv, kv, (((1,), (1,)), ((), ())), preferred_element_type=F32) * (scale * LOG2E)
            p = jnp.exp2(s - lse2[rows])
            delta = jnp.sum(dov * o_ref[rows, :], axis=-1, keepdims=True)
            dob = _mx(dov)
            dp = lax.dot_general(dob, vv, (((1,), (1,)), ((), ())), preferred_element_type=F32)
            ds = _mx(p * (dp - delta) * scale)
            dv_new = dv_new + lax.dot_general(_mx(p), dob, (((0,), (0,)), ((), ())), preferred_element_type=F32)
            dk_new = dk_new + lax.dot_general(ds, qv, (((0,), (0,)), ((), ())), preferred_element_type=F32)
            dqs.append(jnp.dot(ds, kv, preferred_element_type=F32))
        dq_ref[pl.ds(pl.multiple_of(i * tq, tq), tq), :] += jnp.concatenate(dqs, axis=0)
        dk_acc[...] = dk_new
        dv_acc[...] = dv_new

        @pl.when(i == nq - 1)
        def _():
            dk_ref[...] = dk_new
            dv_ref[...] = dv_new

    outs, received = _carried(
        body, exch, (H, T // tk, nq),
        [pl.BlockSpec((tq, dqk), lambda h, j, i: (i, q_col0 + h)),
         pl.BlockSpec((tk, dqk), lambda h, j, i: (j, k_col0 + h)),
         pl.BlockSpec((tk, dv), lambda h, j, i: (j, v_col0 + h)),
         pl.BlockSpec((tq, dv), lambda h, j, i: (i, h)),
         pl.BlockSpec((None, tq, 1), lambda h, j, i: (h, i, 0)),
         pl.BlockSpec((tq, dv), lambda h, j, i: (i, h))],
        [pl.BlockSpec((S, dqk), lambda h, j, i: (0, h)),
         pl.BlockSpec((tk, dqk), lambda h, j, i: (j, h)),
         pl.BlockSpec((tk, dv), lambda h, j, i: (j, h))],
        [jax.ShapeDtypeStruct((S, H * dqk), F32), jax.ShapeDtypeStruct((T, H * dqk), F32),
         jax.ShapeDtypeStruct((T, H * dv), F32)],
        [pltpu.VMEM((tk, dqk), F32), pltpu.VMEM((tk, dv), F32)], [q, k, v, o, lse, do], name=name)
    return outs if exch is None else (outs, received)


def _rope_tables(positions):
    half = MLA_ROPE // 2
    inv_freq = jnp.power(ROPE_THETA, -jnp.arange(half, dtype=F32) / half)
    ang = positions.astype(F32)[:, None] * inv_freq
    cos, sin = jnp.cos(ang), jnp.sin(ang)
    z = jnp.zeros_like(cos)
    tc = jnp.concatenate([cos, cos, z, z], axis=1)
    ta = jnp.concatenate([-sin, z, z, z], axis=1)
    tb = jnp.concatenate([z, sin, z, z], axis=1)
    return tc, ta, tb


def _rope_apply(v, tc, ta, tb):
    half = MLA_ROPE // 2
    return v * tc + pltpu.roll(v, LANES - half, 1) * ta + pltpu.roll(v, half, 1) * tb


def _rope_apply_t(d, tc, ta, tb):
    half = MLA_ROPE // 2
    return d * tc + pltpu.roll(d * ta, half, 1) + pltpu.roll(d * tb, LANES - half, 1)


def _local_step(x, mem, positions, loss_target, w_in_k, small, W, MW, DFF, comm=None):
    S, D = x.shape
    M = mem.shape[0]
    HW = HG_HEADS * HG_D
    QR = small["mla_g_cq"].shape[1]
    KR = small["mla_g_ckv"].shape[1]
    MHD = MW // MEM_HEADS
    QW = MLA_HEADS * 2 * LANES
    VW = MLA_HEADS * MLA_V
    c_hg, c_cq, c_ckv, c_qm, c_gate = 0, 5 * HW, 5 * HW + QR, 5 * HW + QR + KR, 5 * HW + QR + KR + MW
    c_kr = c_gate + N_BRANCH * D
    PW = c_kr + LANES
    assert w_in_k.shape == (D, PW)
    TR = 256
    row = lambda a: a.reshape(1, -1)
    ge, be = row(small["ln_emb_g"]), row(small["ln_emb_b"])
    g1, b1, g2, b2 = small["ln1_g"], small["ln1_b"], small["ln2_g"], small["ln2_b"]
    lb = small["lb"]
    tc, ta, tb = _rope_tables(positions)

    (h0,) = _rowwise(lambda z, g, b: _ln_stats(z)[0] * g + b, [_Rows(x, D)], [ge, be],
                     [_Rows(D, D)], [], R=S, tr=TR, name="ln_emb")
    if comm is None:
        P = _mm(h0, w_in_k, S, PW, D, tn=896, name="proj_in")
    else:
        P, got = _mm(h0, w_in_k, S, PW, D, tn=896, exch=comm.gather_rest, name="proj_in")
        W = comm.weights(got)

    o_fw, st_fw, a_fw = _hgrn_scan(P, lb[0:1], S=S, rev=False, name="hgrn_fw")
    o_bw, st_bw, a_bw = _hgrn_scan(P, lb[1:2], S=S, rev=True, name="hgrn_bw")

    def hg_post(of, ob, gr, ng):
        o = of + ob
        sg = _sigmoid(gr)
        outs = []
        for h in range(HG_HEADS):
            sl = slice(h * HG_D, (h + 1) * HG_D)
            outs.append(_rms_fwd(o[:, sl], ng, RMS_EPS) * sg[:, sl])
        return jnp.concatenate(outs, axis=1)

    (y_hg,) = _rowwise(hg_post, [_Rows(o_fw, HW), _Rows(o_bw, HW), _Rows(P, HW, 4)], [small["hgrn_norm_g"]],
                       [_Rows(HW, HW)], [], R=S, tr=TR, name="hgrn_post")

    def mla_norm(cq, ckv, gq, gk):
        return _rms_fwd(cq, gq, RMS_EPS), _rms_fwd(ckv, gk, RMS_EPS)

    assert c_cq % QR == 0 and c_ckv % KR == 0
    cqn, ckvn = _rowwise(mla_norm, [_Rows(P, QR, c_cq // QR), _Rows(P, KR, c_ckv // KR)],
                         [small["mla_g_cq"], small["mla_g_ckv"]],
                         [_Rows(QR, QR), _Rows(KR, KR)], [], R=S, tr=TR, name="mla_norm")
    q_raw = _mm(cqn, W["mla_w_uq"], S, QW, QR, name="mla_uq")
    kv = _mm(ckvn, W["mla_w_ukv"], S, 2 * VW, KR, name="mla_ukv")

    def rope_fwd(qb, knb, vb_, krb, tcb, tab, tbb):
        kr = _rope_apply(krb, tcb, tab, tbb)
        qo, ko = [], []
        for h in range(MLA_HEADS):
            qo += [qb[:, 2 * h * LANES:(2 * h + 1) * LANES],
                   _rope_apply(qb[:, (2 * h + 1) * LANES:(2 * h + 2) * LANES], tcb, tab, tbb)]
            ko += [knb[:, h * LANES:(h + 1) * LANES], kr]
        return jnp.concatenate(qo, axis=1), jnp.concatenate(ko, axis=1), vb_

    qc, kc, vc = _rowwise(rope_fwd, [_Rows(q_raw, QW), _Rows(kv, VW), _Rows(kv, VW, 1), _Rows(P, LANES, c_kr // LANES),
                                     _Rows(tc, LANES), _Rows(ta, LANES), _Rows(tb, LANES)], [],
                          [_Rows(QW, QW, dtype=MXU), _Rows(QW, QW, dtype=MXU), _Rows(VW, VW, dtype=MXU)], [],
                          R=S, tr=TR, name="rope_fwd")
    mla_kw = dict(S=S, T=S, H=MLA_HEADS, dqk=2 * LANES, dv=MLA_V, q_col0=0, k_col0=0, v_col0=0,
                  scale=(MLA_NOPE + MLA_ROPE) ** -0.5, tq=1024, tk=512)
    y_mla, lse_mla = _attn_fwd(qc, kc, vc, name="mla_attn", **{**mla_kw, "tk": 1024})

    kvm = _mm(mem, W["mem_w_kv"], M, 2 * MW, D, name="mem_kv")
    mem_kw = dict(S=S, T=M, H=MEM_HEADS, dqk=MHD, dv=MHD, q_col0=c_qm // MHD, k_col0=0, v_col0=MEM_HEADS,
                  scale=MHD ** -0.5, tq=1024, tk=M)
    assert c_qm % MHD == 0
    y_mem, lse_mem = _attn_fwd(P, kvm, kvm, name="mem_attn", **mem_kw)

    ys = (y_hg, y_mla, y_mem)
    us = [_mm(ys[b], W["w_branch"][b], S, D, HW, name=f"branch{b}") for b in range(N_BRANCH)]
    TCW = _tile(D, 1024, LANES)
    ncw = D // TCW

    def merge_fwd(g0, g1_, g2_, u0, u1, u2):
        return _sigmoid(g0) * u0 + _sigmoid(g1_) * u1 + _sigmoid(g2_) * u2

    gate_rows = [_Rows(P, TCW, (c_gate + b * D) // TCW) for b in range(N_BRANCH)]
    assert c_gate % TCW == 0
    (merged,) = _rowwise(merge_fwd, gate_rows + [_Rows(u, TCW) for u in us], [],
                         [_Rows(D, TCW)], [], R=S, tr=TR, ncol=ncw, name="merge_fwd")
    mix = _mm(merged, W["w_o"], S, D, D, name="out_proj")

    def ln_res(hp, addv, g, b):
        z = ALPHA * hp + addv
        return z, _ln_stats(z)[0] * g + b

    z1, h1 = _rowwise(ln_res, [_Rows(h0, D), _Rows(mix, D)], [g1, b1],
                      [_Rows(D, D), _Rows(D, D)], [], R=S, tr=TR, name="ln1")

    TF = _tile(DFF, 512, LANES)

    def swiglu(abv):
        a, b = abv[:, :TF], abv[:, TF:]
        return [abv, a * _sigmoid(a) * b]

    ab, cff = _mm(h1, W["w_ffn_gu"], S, 2 * DFF, D, tn=2 * TF, epilogue=(swiglu, [], [1, (1, 2)]), name="ffn_gu")
    ff = _mm(cff, W["w_ffn_down"], S, D, DFF, name="ffn_down")

    def loss_bwd(hp, addv, tgt, g, b):
        z = ALPHA * hp + addv
        xhat, rstd = _ln_stats(z)
        y = xhat * g + b
        err = y - tgt
        dy = err * (1.0 / D)
        dxh = dy * g
        m1 = jnp.mean(dxh, axis=-1, keepdims=True)
        m2 = jnp.mean(dxh * xhat, axis=-1, keepdims=True)
        dz = rstd * (dxh - m1 - xhat * m2)
        lrow = jnp.sum(_colsum(err * err), axis=-1, keepdims=True) * (0.5 / D)
        return dz, _colsum(dy * xhat), _colsum(dy), lrow

    dz2, dg2, db2, loss = _rowwise(loss_bwd, [_Rows(h1, D), _Rows(ff, D), _Rows(loss_target, D)], [g2, b2],
                                   [_Rows(D, D)], [D, D, 1], R=S, tr=TR, name="loss_ln2_bwd")
    def swiglu_bwd(dc, abv):
        a, b = abv[:, :TF], abv[:, TF:]
        sg = _sigmoid(a)
        return [jnp.concatenate([dc * b * sg * (1.0 + a * (1.0 - sg)), dc * a * sg], axis=1)]

    (dab,) = _mm(dz2, W["w_ffn_down"], S, DFF, D, tb=True, tn=TF, epilogue=(swiglu_bwd, [(ab, (2, 1))], [(2, 1)]),
                 name="ffn_down_dx")
    g_ffn_down = _mm(cff, dz2, DFF, D, S, ta=True, name="ffn_down_dw")
    dh1 = _mm(dab, W["w_ffn_gu"], S, D, 2 * DFF, tb=True, name="ffn_gu_dx")
    g_ffn_gu = _mm(h1, dab, D, 2 * DFF, S, ta=True, name="ffn_gu_dw")

    def ln1_bwd(z, dmm, dz2v, g):
        return _ln_bwd_core(z, g, ALPHA * dz2v + dmm)

    dz1, dg1, db1 = _rowwise(ln1_bwd, [_Rows(z1, D), _Rows(dh1, D), _Rows(dz2, D)], [g1],
                             [_Rows(D, D)], [D, D], R=S, tr=TR, name="ln1_bwd")
    dmerged = _mm(dz1, W["w_o"], S, D, D, tb=True, name="out_proj_dx")
    g_w_o = _mm(merged, dz1, D, D, S, ta=True, name="out_proj_dw")

    def merge_bwd(g0, g1_, g2_, u0, u1, u2, dm):
        res_g, res_u = [], []
        for gv, uv in ((g0, u0), (g1_, u1), (g2_, u2)):
            sg = _sigmoid(gv)
            res_g.append(dm * uv * sg * (1.0 - sg))
            res_u.append(dm * sg)
        return (*res_g, *res_u)

    mres = _rowwise(merge_bwd, gate_rows + [_Rows(u, TCW) for u in us] + [_Rows(dmerged, TCW)], [],
                    [_Rows(D, TCW)] * (2 * N_BRANCH), [], R=S, tr=TR, ncol=ncw, name="merge_bwd")
    dgates, dus = mres[:N_BRANCH], mres[N_BRANCH:]
    dys = [_mm(dus[b], W["w_branch"][b], S, HW, D, tb=True, name=f"branch{b}_dx") for b in range(N_BRANCH)]
    g_w_branch = [_mm(ys[b], dus[b], HW, D, S, ta=True, name=f"branch{b}_dw") for b in range(N_BRANCH)]

    dq_mem, dk_mem, dv_mem = _attn_bwd(P, kvm, kvm, y_mem, lse_mem, dys[2], name="mem_attn_bwd", **mem_kw)
    dkvm = jnp.concatenate([dk_mem, dv_mem], axis=1)
    g_mem_w_kv = _mm(mem, dkvm, D, 2 * MW, M, ta=True, name="mem_kv_dw")

    g_w_branch = jnp.stack(g_w_branch)
    delivered = {}
    if comm is None:
        dqc, dkc, dvv = _attn_bwd(qc, kc, vc, y_mla, lse_mla, dys[1], name="mla_attn_bwd", **mla_kw)
    else:
        exch = comm.scatter(dict(w_ffn_gu=g_ffn_gu, w_ffn_down=g_ffn_down, w_o=g_w_o, w_branch=g_w_branch,
                                 mem_w_kv=g_mem_w_kv))
        (dqc, dkc, dvv), got = _attn_bwd(qc, kc, vc, y_mla, lse_mla, dys[1], exch=exch, name="mla_attn_bwd",
                                         **mla_kw)
        delivered.update(zip(exch.names, got))

    def rope_bwd(dqb, dkb, tcb, tab, tbb):
        qo, kn = [], []
        dkr = jnp.zeros_like(tcb)
        for h in range(MLA_HEADS):
            qo += [dqb[:, 2 * h * LANES:(2 * h + 1) * LANES],
                   _rope_apply_t(dqb[:, (2 * h + 1) * LANES:(2 * h + 2) * LANES], tcb, tab, tbb)]
            kn.append(dkb[:, 2 * h * LANES:(2 * h + 1) * LANES])
            dkr = dkr + dkb[:, (2 * h + 1) * LANES:(2 * h + 2) * LANES]
        return jnp.concatenate(qo, axis=1), jnp.concatenate(kn, axis=1), _rope_apply_t(dkr, tcb, tab, tbb)

    dq_raw, dkn, dkr_raw = _rowwise(rope_bwd, [_Rows(dqc, QW), _Rows(dkc, QW), _Rows(tc, LANES),
                                               _Rows(ta, LANES), _Rows(tb, LANES)], [],
                                    [_Rows(QW, QW), _Rows(VW, VW), _Rows(LANES, LANES)], [],
                                    R=S, tr=TR, name="rope_bwd")
    dkv = jnp.concatenate([dkn, dvv], axis=1)
    dcqn = _mm(dq_raw, W["mla_w_uq"], S, QR, QW, tb=True, name="mla_uq_dx")
    g_mla_w_uq = _mm(cqn, dq_raw, QR, QW, S, ta=True, name="mla_uq_dw")
    dckvn = _mm(dkv, W["mla_w_ukv"], S, KR, 2 * VW, tb=True, name="mla_ukv_dx")
    g_mla_w_ukv = _mm(ckvn, dkv, KR, 2 * VW, S, ta=True, name="mla_ukv_dw")

    def mla_norm_bwd(cq, ckv, dq_, dk_, gq, gk):
        dcq, gq_rows = _rms_bwd(cq, gq, dq_, RMS_EPS)
        dck, gk_rows = _rms_bwd(ckv, gk, dk_, RMS_EPS)
        return dcq, dck, _colsum(gq_rows), _colsum(gk_rows)

    dcq, dckv, dg_cq, dg_ckv = _rowwise(
        mla_norm_bwd, [_Rows(P, QR, c_cq // QR), _Rows(P, KR, c_ckv // KR), _Rows(dcqn, QR), _Rows(dckvn, KR)],
        [small["mla_g_cq"], small["mla_g_ckv"]], [_Rows(QR, QR), _Rows(KR, KR)], [QR, KR],
        R=S, tr=TR, name="mla_norm_bwd")

    def hg_post_bwd(of, ob, gr, dy, ng):
        o = of + ob
        sg = _sigmoid(gr)
        do_, dgr = [], []
        dng = jnp.zeros((1, HG_D), F32)
        for h in range(HG_HEADS):
            sl = slice(h * HG_D, (h + 1) * HG_D)
            t = _rms_fwd(o[:, sl], ng, RMS_EPS)
            dgr.append(dy[:, sl] * t * sg[:, sl] * (1.0 - sg[:, sl]))
            dx, grow = _rms_bwd(o[:, sl], ng, dy[:, sl] * sg[:, sl], RMS_EPS)
            do_.append(dx)
            dng = dng + _colsum(grow)
        return jnp.concatenate(do_, axis=1), jnp.concatenate(dgr, axis=1), dng

    do_hg, dg_hg, dng = _rowwise(hg_post_bwd, [_Rows(o_fw, HW), _Rows(o_bw, HW), _Rows(P, HW, 4), _Rows(dys[0], HW)],
                                 [small["hgrn_norm_g"]], [_Rows(HW, HW), _Rows(HW, HW)], [HG_D],
                                 R=S, tr=TR, name="hgrn_post_bwd")
    dq_f, dv_f, dff_fw, dlb_f = _hgrn_scan_bwd(P, lb[0:1], st_fw, a_fw, do_hg, S=S, rev=False, name="hgrn_fw_bwd")
    dq_b, dv_b, dff_bw, dlb_b = _hgrn_scan_bwd(P, lb[1:2], st_bw, a_bw, do_hg, S=S, rev=True, name="hgrn_bw_bwd")
    THW = _tile(HW, 1024, LANES)
    dq_hg, dv_hg = _rowwise(lambda a, b, c, d: (a + b, c + d),
                            [_Rows(dq_f, THW), _Rows(dq_b, THW), _Rows(dv_f, THW), _Rows(dv_b, THW)], [],
                            [_Rows(HW, THW), _Rows(HW, THW)], [], R=S, tr=TR, ncol=HW // THW, name="hgrn_dir_sum")

    dP = jnp.concatenate([dq_hg, dv_hg, dff_fw, dff_bw, dg_hg, dcq, dckv, dq_mem, *dgates, dkr_raw], axis=1)
    g_w_in = _mm(h0, dP, D, PW, S, ta=True, tn=896, name="proj_in_dw")
    dx_kw = dict(tb=True, tk=_tile(PW, 640, LANES), name="proj_in_dx")
    if comm is None:
        dh0 = _mm(dP, w_in_k, S, D, PW, **dx_kw)
    else:
        exch = comm.scatter(dict(w_in=g_w_in, mla_w_uq=g_mla_w_uq, mla_w_ukv=g_mla_w_ukv))
        dh0, got = _mm(dP, w_in_k, S, D, PW, exch=exch, **dx_kw)
        delivered.update(zip(exch.names, got))

    def ln0_bwd(z, dmm, dz1v, g):
        return _ln_bwd_core(z, g, ALPHA * dz1v + dmm)

    grad_x, dge, dbe = _rowwise(ln0_bwd, [_Rows(x, D), _Rows(dh0, D), _Rows(dz1, D)], [ge],
                                [_Rows(D, D)], [D, D], R=S, tr=TR, name="ln_emb_bwd")

    big = dict(w_in=g_w_in, mla_w_uq=g_mla_w_uq, mla_w_ukv=g_mla_w_ukv, mem_w_kv=g_mem_w_kv,
               w_branch=g_w_branch, w_o=g_w_o, w_ffn_gu=g_ffn_gu, w_ffn_down=g_ffn_down)
    sm = dict(ln_emb_g=dge, ln_emb_b=dbe, dlb=jnp.concatenate([dlb_f, dlb_b], axis=0), hgrn_norm_g=dng,
              mla_g_cq=dg_cq, mla_g_ckv=dg_ckv, ln1_g=dg1, ln1_b=db1, ln2_g=dg2, ln2_b=db2)
    return loss, grad_x, big, sm, delivered


def _gather_exch(shards):
    copies = [(m, lambda x, y, c: 0, _chip) for m in _CHIP_MASKS]
    return _Exch([s[None] for s in shards], 4, copies, [(lambda x, y, c: 0, _chip)])


def _scatter_exch(pieces):
    copies = [(m, (lambda x, y, c, m=m: 2 * _chip(x ^ m[0], y ^ m[1], c) + (c ^ m[2])), _device) for m in _ALL_MASKS]
    local = [((lambda x, y, c: 2 * _chip(x, y, c) + c), _device)]
    return _Exch([p.reshape((8,) + p.shape[2:]) for p in pieces], 8, copies, local)


SHARE_BLOCK_BYTES = 4 << 20


def _sum_share(arr, *, name):
    n, rh, w = arr.shape
    tr = _tile(rh, max(16, SHARE_BLOCK_BYTES // (n * w * arr.dtype.itemsize) // 16 * 16), 16)
    nb = rh // tr

    def body(a_ref, o_ref, slots, send_sems, recv_sem, local_sems):
        i = pl.program_id(0)
        x, y, c = _coords()
        sibling = (x, y, 1 - c)

        def pushes(step, slot):
            rows = pl.ds(pl.multiple_of(c * rh + step * tr, SUBLANES), tr)
            return (pltpu.make_async_copy(slots.at[slot], o_ref.at[rows], local_sems.at[slot]),
                    pltpu.make_async_remote_copy(src_ref=slots.at[slot], dst_ref=o_ref.at[rows],
                                                 send_sem=send_sems.at[slot], recv_sem=recv_sem,
                                                 device_id=sibling, device_id_type=MESH))

        def drain(step, slot):
            loc, rem = pushes(step, slot)
            loc.wait()
            rem.wait_send()

        slot = i % 2

        @pl.when(i >= 2)
        def _():
            drain(i - 2, slot)

        acc = a_ref[0].astype(F32)
        for k in range(1, n):
            acc = acc + a_ref[k].astype(F32)
        slots[slot] = acc
        loc, rem = pushes(i, slot)
        loc.start()
        rem.start()

        @pl.when(i == nb - 1)
        def _():
            if nb >= 2:
                drain(i - 1, 1 - slot)
            drain(i, slot)
            other = o_ref.at[pl.ds(pl.multiple_of((1 - c) * rh, SUBLANES), rh)]
            pltpu.make_async_remote_copy(src_ref=other, dst_ref=other, send_sem=send_sems.at[0], recv_sem=recv_sem,
                                         device_id=sibling, device_id_type=MESH).wait_recv()

    return pl.pallas_call(
        body, name=name, grid=(nb,),
        in_specs=[pl.BlockSpec((n, tr, w), lambda i: (0, i, 0))],
        out_specs=pl.BlockSpec(memory_space=pl.ANY),
        out_shape=jax.ShapeDtypeStruct((2 * rh, w), F32),
        scratch_shapes=[pltpu.VMEM((2, tr, w), F32), pltpu.SemaphoreType.DMA((2,)), pltpu.SemaphoreType.DMA,
                        pltpu.SemaphoreType.DMA((2,))],
        compiler_params=pltpu.CompilerParams(dimension_semantics=("arbitrary",), has_side_effects=True,
                                             vmem_limit_bytes=VMEM_LIMIT),
    )(arr)


def _allreduce_small(v, *, name):
    r, w = v.shape

    def body(v_ref, o_ref, buf, send_sems, recv_sems):
        x, y, c = _coords()
        me = 4 * x + 2 * y + c
        buf[me] = v_ref[...]
        cps = []
        for k in range(7):
            m = ((k + 1) >> 2 & 1, (k + 1) >> 1 & 1, (k + 1) & 1)
            cp = pltpu.make_async_remote_copy(
                src_ref=v_ref, dst_ref=buf.at[me], send_sem=send_sems.at[k], recv_sem=recv_sems.at[k],
                device_id=(x ^ m[0], y ^ m[1], c ^ m[2]), device_id_type=MESH)
            cp.start()
            cps.append(cp)
        for cp in cps:
            cp.wait_recv()
        for cp in cps:
            cp.wait_send()
        acc = buf[0]
        for k in range(1, 8):
            acc = acc + buf[k]
        o_ref[...] = acc

    return pl.pallas_call(
        body, name=name,
        in_specs=[pl.BlockSpec(memory_space=pltpu.VMEM)],
        out_specs=pl.BlockSpec(memory_space=pltpu.VMEM),
        out_shape=jax.ShapeDtypeStruct((r, w), F32),
        scratch_shapes=[pltpu.VMEM((8, r, w), F32), pltpu.SemaphoreType.DMA((7,)), pltpu.SemaphoreType.DMA((7,))],
        compiler_params=pltpu.CompilerParams(has_side_effects=True),
    )(v)


_BIG = (("w_in", 1), ("mla_w_uq", 1), ("mla_w_ukv", 1), ("mem_w_kv", 0), ("w_branch", 1), ("w_o", 0),
        ("w_ffn_gate", 1), ("w_ffn_up", 1), ("w_ffn_down", 0))


def _assemble(gathered, ax):
    _, r, c = gathered.shape
    if ax == 0:
        return gathered.reshape(4 * r, c)
    return jnp.concatenate([gathered[j] for j in range(4)], axis=1)


def _split_pieces(g, ax):
    r, c = g.shape
    if ax == 0:
        return g.reshape(4, 2, r // 8, c).astype(BF16)
    rh, cs = r // 2, c // 4
    return jnp.stack([g[h * rh:(h + 1) * rh, j * cs:(j + 1) * cs].astype(BF16)
                      for j in range(4) for h in range(2)]).reshape(4, 2, rh, cs)


def _pad_cols(a, n):
    return jnp.pad(a, ((0, 0), (0, n - a.shape[1])))


def _to_kernel_layout(full, QR, KR):
    out = {}
    for n in ("mem_w_kv", "w_branch", "w_o", "w_ffn_down"):
        if n in full:
            out[n] = full[n]
    if "w_in" in full:
        w_in = full["w_in"]
        a = 5 * HG_HEADS * HG_D + QR + KR
        out["w_in"] = jnp.concatenate([w_in[:, :a], w_in[:, a + MLA_ROPE:], _pad_cols(w_in[:, a:a + MLA_ROPE], LANES)],
                                      axis=1)
    if "mla_w_uq" in full:
        uq = full["mla_w_uq"].reshape(QR, MLA_HEADS, MLA_NOPE + MLA_ROPE)
        out["mla_w_uq"] = jnp.pad(uq, ((0, 0), (0, 0), (0, 2 * LANES - MLA_NOPE - MLA_ROPE))).reshape(QR, -1)
    if "mla_w_ukv" in full:
        ukv = full["mla_w_ukv"].reshape(KR, MLA_HEADS, MLA_NOPE + MLA_V)
        out["mla_w_ukv"] = jnp.concatenate([ukv[:, :, :MLA_NOPE].reshape(KR, -1), ukv[:, :, MLA_NOPE:].reshape(KR, -1)],
                                           axis=1)
    if "w_ffn_gate" in full:
        gate, up = full["w_ffn_gate"], full["w_ffn_up"]
        DFF = gate.shape[1]
        TF = _tile(DFF, 512, LANES)
        blocks = []
        for j in range(DFF // TF):
            blocks += [gate[:, j * TF:(j + 1) * TF], up[:, j * TF:(j + 1) * TF]]
        out["w_ffn_gu"] = jnp.concatenate(blocks, axis=1)
    return out


def _from_kernel_layout(gk, QR, KR):
    out = {}
    for n in ("mem_w_kv", "w_o", "w_ffn_down"):
        if n in gk:
            out[n] = gk[n]
    if "w_branch" in gk:
        out["w_branch"] = gk["w_branch"].reshape(-1, gk["w_branch"].shape[-1])
    if "w_in" in gk:
        g = gk["w_in"]
        a = 5 * HG_HEADS * HG_D + QR + KR
        rest = g.shape[1] - LANES - a
        out["w_in"] = jnp.concatenate([g[:, :a], g[:, a + rest:a + rest + MLA_ROPE], g[:, a:a + rest]], axis=1)
    if "mla_w_uq" in gk:
        out["mla_w_uq"] = gk["mla_w_uq"].reshape(QR, MLA_HEADS, 2 * LANES)[:, :, :MLA_NOPE + MLA_ROPE].reshape(QR, -1)
    if "mla_w_ukv" in gk:
        VW = MLA_HEADS * MLA_V
        g = gk["mla_w_ukv"]
        out["mla_w_ukv"] = jnp.concatenate([g[:, :VW].reshape(KR, MLA_HEADS, MLA_NOPE),
                                            g[:, VW:].reshape(KR, MLA_HEADS, MLA_V)], axis=2).reshape(KR, -1)
    if "w_ffn_gu" in gk:
        g = gk["w_ffn_gu"]
        DFF = g.shape[1] // 2
        TF = _tile(DFF, 512, LANES)
        out["w_ffn_gate"] = jnp.concatenate([g[:, 2 * j * TF:(2 * j + 1) * TF] for j in range(DFF // TF)], axis=1)
        out["w_ffn_up"] = jnp.concatenate([g[:, (2 * j + 1) * TF:(2 * j + 2) * TF] for j in range(DFF // TF)], axis=1)
    return out


def _adamw(w, g, m, v, *, name):
    r, c = w.shape
    tr = max(SUBLANES, min(512, (1 << 20) // (4 * c)) // SUBLANES * SUBLANES)
    c1 = 1.0 / (1.0 - ADAM_B1 ** ADAM_STEP)
    c2 = 1.0 / (1.0 - ADAM_B2 ** ADAM_STEP)

    def fn(wv, gv, mv, vv):
        mn = ADAM_B1 * mv + (1.0 - ADAM_B1) * gv
        vn = ADAM_B2 * vv + (1.0 - ADAM_B2) * (gv * gv)
        delta = -ADAM_LR * ((mn * c1) / (jnp.sqrt(vn * c2) + ADAM_EPS) + ADAM_WD * wv)
        return delta, mn, vn

    return _rowwise(fn, [_Rows(a, c) for a in (w, g, m, v)], [], [_Rows(c, c)] * 3, [], R=r, tr=tr, name=name)


_SMALL = ("ln_emb_g", "ln_emb_b", "hgrn_lb_logits", "hgrn_norm_g", "mla_g_cq", "mla_g_ckv",
          "ln1_g", "ln1_b", "ln2_g", "ln2_b")


def _lb_from_logits(logits):
    return jnp.cumsum(jax.nn.softmax(logits, axis=1), axis=1)[:, 0]


def _small_rows(parts):
    flat = jnp.concatenate([p.reshape(-1) for p in parts])
    n = flat.shape[0]
    total = -(-n // (SUBLANES * LANES)) * SUBLANES * LANES
    return jnp.pad(flat, (0, total - n)).reshape(total // LANES, LANES)


def kernel(x, mem, positions, ln_emb_g, ln_emb_b, hgrn_lb_logits, w_in, hgrn_norm_g, mla_g_cq, mla_g_ckv, mla_w_uq, mla_w_ukv, mem_w_kv, w_branch, w_o, ln1_g, ln1_b, w_ffn_gate, w_ffn_up, w_ffn_down, ln2_g, ln2_b, loss_target, m_ln_emb_g, m_ln_emb_b, m_hgrn_lb_logits, m_w_in, m_hgrn_norm_g, m_mla_g_cq, m_mla_g_ckv, m_mla_w_uq, m_mla_w_ukv, m_mem_w_kv, m_w_branch, m_w_o, m_ln1_g, m_ln1_b, m_w_ffn_gate, m_w_ffn_up, m_w_ffn_down, m_ln2_g, m_ln2_b, v_ln_emb_g, v_ln_emb_b, v_hgrn_lb_logits, v_w_in, v_hgrn_norm_g, v_mla_g_cq, v_mla_g_ckv, v_mla_w_uq, v_mla_w_ukv, v_mem_w_kv, v_w_branch, v_w_o, v_ln1_g, v_ln1_b, v_w_ffn_gate, v_w_ffn_up, v_w_ffn_down, v_ln2_g, v_ln2_b):
    names = ["ln_emb_g", "ln_emb_b", "hgrn_lb_logits", "w_in", "hgrn_norm_g", "mla_g_cq", "mla_g_ckv", "mla_w_uq",
             "mla_w_ukv", "mem_w_kv", "w_branch", "w_o", "ln1_g", "ln1_b", "w_ffn_gate", "w_ffn_up", "w_ffn_down",
             "ln2_g", "ln2_b"]
    wts = dict(zip(names, [ln_emb_g, ln_emb_b, hgrn_lb_logits, w_in, hgrn_norm_g, mla_g_cq, mla_g_ckv, mla_w_uq,
                           mla_w_ukv, mem_w_kv, w_branch, w_o, ln1_g, ln1_b, w_ffn_gate, w_ffn_up, w_ffn_down,
                           ln2_g, ln2_b]))
    mom = dict(zip(names, [m_ln_emb_g, m_ln_emb_b, m_hgrn_lb_logits, m_w_in, m_hgrn_norm_g, m_mla_g_cq, m_mla_g_ckv,
                           m_mla_w_uq, m_mla_w_ukv, m_mem_w_kv, m_w_branch, m_w_o, m_ln1_g, m_ln1_b, m_w_ffn_gate,
                           m_w_ffn_up, m_w_ffn_down, m_ln2_g, m_ln2_b]))
    var = dict(zip(names, [v_ln_emb_g, v_ln_emb_b, v_hgrn_lb_logits, v_w_in, v_hgrn_norm_g, v_mla_g_cq, v_mla_g_ckv,
                           v_mla_w_uq, v_mla_w_ukv, v_mem_w_kv, v_w_branch, v_w_o, v_ln1_g, v_ln1_b, v_w_ffn_gate,
                           v_w_ffn_up, v_w_ffn_down, v_ln2_g, v_ln2_b]))
    xc, yc, cc = _coords()
    chip = _chip(xc, yc, cc)
    S, D = x.shape[1], x.shape[2]

    axis = dict(_BIG)
    shard = lambda n: wts[n].reshape(-1, wts[n].shape[-1]).astype(BF16)
    QR, KR = mla_w_uq.shape[1], mla_w_ukv.shape[1]
    MW, DFF = mem_w_kv.shape[2] // 2, 4 * w_ffn_gate.shape[2]
    (w_in_all,) = _exchange(_gather_exch([shard("w_in")]), name="gather_w_in")
    w_in_k = _to_kernel_layout(dict(w_in=_assemble(w_in_all, axis["w_in"])), QR, KR)["w_in"]
    rest_names = [n for n, _ in _BIG if n != "w_in"]

    class _Comm:
        gather_rest = _gather_exch([shard(n) for n in rest_names])

        @staticmethod
        def weights(received):
            full = {n: _assemble(g, axis[n]) for n, g in zip(rest_names, received)}
            full["w_branch"] = full["w_branch"].reshape(N_BRANCH, -1, D)
            return _to_kernel_layout(full, QR, KR)

        @staticmethod
        def scatter(gk_part):
            gpart = _from_kernel_layout(gk_part, QR, KR)
            exch = _scatter_exch([_split_pieces(gpart[n], axis[n]) for n in gpart])
            exch.names = list(gpart)
            return exch

    lsh = hgrn_lb_logits.shape
    HW = 4 * lsh[2]
    placed = lax.dynamic_update_slice(jnp.zeros((lsh[0], lsh[1], HW), F32), hgrn_lb_logits, (0, 0, chip * lsh[2]))
    placed = jnp.where(cc == 0, placed, 0.0)
    logits = _allreduce_small(_small_rows([placed]), name="gather_logits").reshape(-1)[:placed.size].reshape(placed.shape)
    lb, lb_vjp = jax.vjp(_lb_from_logits, logits)

    small = dict(ln_emb_g=ln_emb_g, ln_emb_b=ln_emb_b, lb=lb, hgrn_norm_g=hgrn_norm_g, mla_g_cq=mla_g_cq,
                 mla_g_ckv=mla_g_ckv, ln1_g=ln1_g, ln1_b=ln1_b, ln2_g=ln2_g, ln2_b=ln2_b)
    loss_l, grad_x, _, gs, delivered = _local_step(x[0], mem[0], positions[0], loss_target[0], w_in_k, small, None,
                                                   MW, DFF, comm=_Comm)

    (dlogits,) = lb_vjp(gs["dlb"])
    sm_parts = [loss_l, gs["ln_emb_g"], gs["ln_emb_b"], dlogits, gs["hgrn_norm_g"], gs["mla_g_cq"], gs["mla_g_ckv"],
                gs["ln1_g"], gs["ln1_b"], gs["ln2_g"], gs["ln2_b"]]
    red = _allreduce_small(_small_rows(sm_parts), name="allreduce_small").reshape(-1)
    sm_out, off = [], 0
    for p in sm_parts:
        sm_out.append(red[off:off + p.size].reshape(p.shape))
        off += p.size
    loss = sm_out[0].reshape(())
    g_small = dict(zip(_SMALL, sm_out[1:]))
    g_small["hgrn_lb_logits"] = lax.dynamic_slice(g_small["hgrn_lb_logits"], (0, 0, chip * lsh[2]), lsh)
    for n in _SMALL:
        g_small[n] = g_small[n].reshape(wts[n].shape)

    g_big = {n: _sum_share(delivered[n], name="rs_sum_" + n).reshape(wts[n].shape) for n, _ in _BIG}

    grads = {**g_small, **g_big}
    delta, new_m, new_v = {}, {}, {}
    for n, _ in _BIG:
        shp = wts[n].shape
        two_d = lambda a: a.reshape(-1, shp[-1])
        d_, m_, v_ = _adamw(two_d(wts[n]), two_d(grads[n]), two_d(mom[n]), two_d(var[n]), name="adamw_" + n)
        delta[n], new_m[n], new_v[n] = d_.reshape(shp), m_.reshape(shp), v_.reshape(shp)
    sw, sg_, sm_, sv_ = (_small_rows([d[n] for n in _SMALL]) for d in (wts, grads, mom, var))
    d_, m_, v_ = _adamw(sw, sg_, sm_, sv_, name="adamw_small")
    for res, packed_rows in ((delta, d_), (new_m, m_), (new_v, v_)):
        flat, off = packed_rows.reshape(-1), 0
        for n in _SMALL:
            res[n] = flat[off:off + wts[n].size].reshape(wts[n].shape)
            off += wts[n].size

    return (loss, grad_x[None], *[grads[n] for n in names], *[delta[n] for n in names],
            *[new_m[n] for n in names], *[new_v[n] for n in names])
```

```python
import jax
import jax.numpy as jnp
from jax import lax
from jax.experimental import pallas as pl
from jax.experimental.pallas import tpu as pltpu

F32 = jnp.float32
BF16 = jnp.bfloat16

HG_HEADS = 8
HG_D = 128
MLA_HEADS = 8
MLA_NOPE = 128
MLA_ROPE = 64
MLA_V = 128
MEM_HEADS = 4
N_BRANCH = 3
ROPE_THETA = 10000.0
DEPTH = 1
ALPHA = (2.0 * DEPTH) ** 0.25
LN_EPS = 1e-5
RMS_EPS = 1e-6
ADAM_LR = 0.001
ADAM_B1 = 0.9
ADAM_B2 = 0.999
ADAM_EPS = 1e-08
ADAM_WD = 0.01
ADAM_STEP = 10

LANES = 128
SUBLANES = 8
VMEM_LIMIT = 48 * 1024 * 1024

HG_CHUNK = 128
HG_SUB = 16
HG_PAIR = 2

MESH = pl.DeviceIdType.MESH
HI = lax.Precision.HIGHEST
HG_OFF_PREC = lax.Precision.DEFAULT


def _cparams(sem=None):
    if sem is None:
        return pltpu.CompilerParams(vmem_limit_bytes=VMEM_LIMIT)
    return pltpu.CompilerParams(dimension_semantics=sem, vmem_limit_bytes=VMEM_LIMIT)


def _tile(dim, pref, quantum):
    t = min(pref, dim) // quantum * quantum
    while t >= quantum:
        if dim % t == 0:
            return t
        t -= quantum
    return dim


def _sigmoid(x):
    return 1.0 / (1.0 + jnp.exp(-x))


def _coords():
    return lax.axis_index("x"), lax.axis_index("y"), lax.axis_index("c")


def _chip(x, y, c):
    return 2 * x + y


def _device(x, y, c):
    return 4 * x + 2 * y + c


_CHIP_MASKS = ((1, 0, 0), (0, 1, 0), (1, 1, 0))
_ALL_MASKS = tuple((k >> 2 & 1, k >> 1 & 1, k & 1) for k in range(1, 8))
_HBM = pl.BlockSpec(memory_space=pl.ANY)


class _Exch:
    def __init__(self, srcs, n_dst, copies, local_copies):
        self.srcs, self.n_dst, self.copies, self.local_copies = list(srcs), n_dst, copies, local_copies
        self.n = len(self.srcs)

    def out_shape(self):
        return [jax.ShapeDtypeStruct((self.n_dst,) + s.shape[1:], s.dtype) for s in self.srcs]

    def scratch(self):
        n_rc, n_lc = self.n * len(self.copies), self.n * len(self.local_copies)
        return [pltpu.SemaphoreType.DMA((n_rc,)), pltpu.SemaphoreType.DMA((n_rc,)),
                pltpu.SemaphoreType.DMA((max(n_lc, 1),))]

    def _descriptors(self, src_refs, dst_refs, sems):
        send_sems, recv_sems, local_sems = sems
        x, y, c = _coords()
        n_rc, n_lc = len(self.copies), len(self.local_copies)
        remote, local = [], []
        for a in range(self.n):
            for k, (mask, sidx, didx) in enumerate(self.copies):
                remote.append(pltpu.make_async_remote_copy(
                    src_ref=src_refs[a].at[sidx(x, y, c)], dst_ref=dst_refs[a].at[didx(x, y, c)],
                    send_sem=send_sems.at[a * n_rc + k], recv_sem=recv_sems.at[a * n_rc + k],
                    device_id=(x ^ mask[0], y ^ mask[1], c ^ mask[2]), device_id_type=MESH))
            for k, (sidx, didx) in enumerate(self.local_copies):
                local.append(pltpu.make_async_copy(src_refs[a].at[sidx(x, y, c)], dst_refs[a].at[didx(x, y, c)],
                                                   local_sems.at[a * n_lc + k]))
        return remote, local

    def start(self, src_refs, dst_refs, sems):
        remote, local = self._descriptors(src_refs, dst_refs, sems)
        for cp in remote + local:
            cp.start()

    def wait(self, src_refs, dst_refs, sems):
        remote, local = self._descriptors(src_refs, dst_refs, sems)
        for cp in remote:
            cp.wait_recv()
        for cp in remote:
            cp.wait_send()
        for cp in local:
            cp.wait()


def _exchange(exch, *, name):
    n = exch.n

    def body(*refs):
        src_refs, dst_refs, sems = refs[:n], refs[n:2 * n], refs[2 * n:]
        exch.start(src_refs, dst_refs, sems)
        exch.wait(src_refs, dst_refs, sems)

    return pl.pallas_call(
        body, name=name, in_specs=[_HBM] * n, out_specs=[_HBM] * n, out_shape=exch.out_shape(),
        scratch_shapes=exch.scratch(), compiler_params=pltpu.CompilerParams(has_side_effects=True),
    )(*exch.srcs)


def _carried(call, exch, grid, in_specs, out_specs, out_shape, scratch_shapes, args, *, name):
    n_in, n_out, n_scr = len(in_specs), len(out_specs), len(scratch_shapes)
    n = 0 if exch is None else exch.n

    def body(*refs):
        o0 = n_in + n
        s0 = o0 + n_out + n
        ins, srcs = refs[:n_in], refs[n_in:o0]
        outs, dsts = refs[o0:o0 + n_out], refs[o0 + n_out:s0]
        scr, sems = refs[s0:s0 + n_scr], refs[s0 + n_scr:]
        if exch is not None:
            ids = [pl.program_id(d) for d in range(len(grid))]
            first = _all([i == 0 for i in ids])
            last = _all([i == g - 1 for i, g in zip(ids, grid)])

            @pl.when(first)
            def _():
                exch.start(srcs, dsts, sems)

        call(*ins, *outs, *scr)
        if exch is not None:
            @pl.when(last)
            def _():
                exch.wait(srcs, dsts, sems)

    if exch is None:
        params = pltpu.CompilerParams(dimension_semantics=("arbitrary",) * len(grid), vmem_limit_bytes=VMEM_LIMIT)
        extra_in, extra_out, extra_shape, extra_scr, extra_args = [], [], [], [], []
    else:
        params = pltpu.CompilerParams(dimension_semantics=("arbitrary",) * len(grid), vmem_limit_bytes=VMEM_LIMIT,
                                      has_side_effects=True)
        extra_in, extra_out, extra_shape = [_HBM] * n, [_HBM] * n, exch.out_shape()
        extra_scr, extra_args = exch.scratch(), exch.srcs
    res = pl.pallas_call(
        body, name=name, grid=grid, in_specs=list(in_specs) + extra_in, out_specs=list(out_specs) + extra_out,
        out_shape=list(out_shape) + extra_shape, scratch_shapes=list(scratch_shapes) + extra_scr,
        compiler_params=params,
    )(*args, *extra_args)
    return res[:n_out], res[n_out:]


def _all(conds):
    out = conds[0]
    for c in conds[1:]:
        out = jnp.logical_and(out, c)
    return out


def _mm(a, b, M, N, K, *, ta=False, tb=False, a_off=(0, 0), b_off=(0, 0), add=None, exch=None, epilogue=None,
        tm=1024, tn=1024, tk=1024, name):
    tm = _tile(M, tm, LANES if ta else SUBLANES)
    tn = _tile(N, tn, LANES)
    tk = _tile(K, tk, LANES)
    nk = K // tk
    ar, ac = a_off
    br, bc = b_off

    if ta:
        assert ar % tk == 0 and ac % tm == 0
        a_spec = pl.BlockSpec((tk, tm), lambda i, j, k: (ar // tk + k, ac // tm + i))
    else:
        assert ar % tm == 0 and ac % tk == 0
        a_spec = pl.BlockSpec((tm, tk), lambda i, j, k: (ar // tm + i, ac // tk + k))
    if tb:
        assert br % tn == 0 and bc % tk == 0
        b_spec = pl.BlockSpec((tn, tk), lambda i, j, k: (br // tn + j, bc // tk + k))
    else:
        assert br % tk == 0 and bc % tn == 0
        b_spec = pl.BlockSpec((tk, tn), lambda i, j, k: (br // tk + k, bc // tn + j))
    o_spec = pl.BlockSpec((tm, tn), lambda i, j, k: (i, j))
    mixed = a.dtype != b.dtype

    epi_fn, epi_ins, epi_outs = (None, [], [1]) if epilogue is None else epilogue
    n_in = 2 + (add is not None) + len(epi_ins)

    def body(*refs):
        a_ref, b_ref = refs[:2]
        add_ref = refs[2] if add is not None else None
        epi_refs = refs[n_in - len(epi_ins):n_in]
        o_refs, acc = refs[n_in:n_in + len(epi_outs)], refs[-1]
        k = pl.program_id(2)
        av = a_ref[...]
        bv = b_ref[...]
        if ta:
            av = av.astype(F32).T
        if mixed:
            av = av.astype(BF16)
            bv = bv.astype(BF16)
        dims = (((1,), (1 if tb else 0,)), ((), ()))
        d = lax.dot_general(av, bv, dims, preferred_element_type=F32)

        def finish(total):
            if add is not None:
                total = total + add_ref[...]
            tiles = [total] if epi_fn is None else epi_fn(total, *[r[...] for r in epi_refs])
            for o_ref, t in zip(o_refs, tiles):
                o_ref[...] = t

        if nk == 1:
            finish(d)
        else:
            @pl.when(k == 0)
            def _():
                acc[...] = d

            @pl.when(jnp.logical_and(k > 0, k < nk - 1))
            def _():
                acc[...] += d

            @pl.when(k == nk - 1)
            def _():
                finish(acc[...] + d)

    in_specs = [a_spec, b_spec]
    args = [a, b]
    if add is not None:
        in_specs.append(o_spec)
        args.append(add)
    wide = lambda w: pl.BlockSpec((tm, tn * w[0] // w[1]), lambda i, j, k: (i, j))
    for arr, w in epi_ins:
        in_specs.append(wide(w))
        args.append(arr)
    out_specs = [o_spec if w == 1 else wide(w) for w in epi_outs]
    out_shape = [jax.ShapeDtypeStruct((M, N if w == 1 else N * w[0] // w[1]), F32) for w in epi_outs]
    outs, received = _carried(body, exch, (M // tm, N // tn, nk), in_specs, out_specs, out_shape,
                              [pltpu.VMEM((tm, tn), F32)], args, name=name)
    outs = outs[0] if epilogue is None else outs
    return outs if exch is None else (outs, received)


class _Rows:
    def __init__(self, arr, width, col0=0, lead=None, dtype=F32):
        self.arr, self.width, self.col0, self.lead, self.dtype = arr, width, col0, lead, dtype


def _rowwise(fn, rows, consts, outs, accs, *, R, tr, ncol=1, name):
    tr = _tile(R, tr, SUBLANES)
    nrow = R // tr

    def spec(r):
        if r.lead is None:
            return pl.BlockSpec((tr, r.width), lambda j, i, c0=r.col0: (i, c0 + j))
        return pl.BlockSpec((None, tr, r.width), lambda j, i, c0=r.col0, l=r.lead: (l, i, c0 + j))

    in_specs = [spec(r) for r in rows]
    for c in consts:
        in_specs.append(pl.BlockSpec(c.shape, lambda j, i, nd=c.ndim: (0,) * nd))
    out_specs = [spec(o) for o in outs]
    out_shape = [jax.ShapeDtypeStruct((R, o.arr), o.dtype) for o in outs]
    for w in accs:
        out_specs.append(pl.BlockSpec((1, w), lambda j, i: (0, j)))
        out_shape.append(jax.ShapeDtypeStruct((1, w * ncol), F32))
    n_in = len(rows) + len(consts)
    n_out = len(outs)

    def body(*refs):
        ins = [r[...] for r in refs[:n_in]]
        res = fn(*ins)
        if not isinstance(res, (tuple, list)):
            res = (res,)
        for k in range(n_out):
            refs[n_in + k][...] = res[k].astype(refs[n_in + k].dtype)
        i = pl.program_id(1)
        for k in range(len(accs)):
            a_ref = refs[n_in + n_out + k]

            @pl.when(i == 0)
            def _(a_ref=a_ref):
                a_ref[...] = jnp.zeros_like(a_ref)

            a_ref[...] += res[n_out + k]

    res = pl.pallas_call(
        body, name=name, grid=(ncol, nrow),
        in_specs=in_specs, out_specs=out_specs, out_shape=out_shape,
        compiler_params=_cparams(("parallel", "arbitrary")),
    )(*[r.arr for r in rows], *consts)
    return res


def _colsum(x):
    return jnp.sum(x, axis=0, keepdims=True)


def _ln_stats(z):
    mu = jnp.mean(z, axis=-1, keepdims=True)
    zc = z - mu
    var = jnp.mean(zc * zc, axis=-1, keepdims=True)
    rstd = lax.rsqrt(var + LN_EPS)
    return zc * rstd, rstd


def _ln_bwd_core(z, g, dy):
    xhat, rstd = _ln_stats(z)
    dxh = dy * g
    m1 = jnp.mean(dxh, axis=-1, keepdims=True)
    m2 = jnp.mean(dxh * xhat, axis=-1, keepdims=True)
    dz = rstd * (dxh - m1 - xhat * m2)
    return dz, _colsum(dy * xhat), _colsum(dy)


def _rms_fwd(x, g, eps):
    r = lax.rsqrt(jnp.mean(x * x, axis=-1, keepdims=True) + eps)
    return x * r * g


def _rms_bwd(x, g, dy, eps):
    r = lax.rsqrt(jnp.mean(x * x, axis=-1, keepdims=True) + eps)
    xr = x * r
    dyg = dy * g
    dx = r * (dyg - xr * jnp.mean(dyg * xr, axis=-1, keepdims=True))
    return dx, dy * xr


def _hg_gate(fr, lb):
    sig = _sigmoid(fr)
    f = lb + (1.0 - lb) * sig
    return sig, f


def _hg_masks(rev):
    C = HG_CHUNK
    t = lax.broadcasted_iota(jnp.int32, (C, C), 0)
    s = lax.broadcasted_iota(jnp.int32, (C, C), 1)
    tri = (s >= t) if rev else (s <= t)
    return tri


def _hg_offdiag(Q, K, b, i, rev):
    C, sb = HG_CHUNK, HG_SUB
    nb = C // sb
    if (not rev and i == 0) or (rev and i == nb - 1):
        return None
    ref = b[sb * i - 1:sb * i] if not rev else b[sb * (i + 1):sb * (i + 1) + 1]
    srow = lax.broadcasted_iota(jnp.int32, (C, 1), 0)
    smask = (srow < sb * i) if not rev else (srow >= sb * (i + 1))
    qscale = jnp.exp(jnp.minimum(b - ref, 0.0))
    kscale = jnp.where(smask, jnp.exp(jnp.minimum(ref - b, 0.0)), 0.0)
    return qscale, kscale


def _hg_att(Q, K, b, rev):
    C, sb = HG_CHUNK, HG_SUB
    lane = lax.broadcasted_iota(jnp.int32, (sb, C), 1)
    rloc = lax.broadcasted_iota(jnp.int32, (sb, 1), 0)
    rows = []
    for i in range(C // sb):
        sl = slice(sb * i, sb * i + sb)
        Qi, Ki, bi = Q[sl], K[sl], b[sl]
        od = _hg_offdiag(Q, K, b, i, rev)
        if od is None:
            acc = jnp.zeros((sb, C), F32)
        else:
            qs, ks = od
            acc = lax.dot_general(Qi * qs[sl], K * ks, (((1,), (1,)), ((), ())),
                                  precision=HG_OFF_PREC, preferred_element_type=F32)
        for j in range(sb):
            e = jnp.exp(jnp.minimum(bi - bi[j:j + 1], 0.0))
            col = jnp.sum(Qi * Ki[j:j + 1] * e, axis=-1, keepdims=True)
            vis = (rloc <= j) if rev else (rloc >= j)
            acc = jnp.where(lane == sb * i + j, jnp.where(vis, col, 0.0), acc)
        rows.append(acc)
    return jnp.concatenate(rows, axis=0)


def _hg_att_bwd(Q, K, b, dA, rev):
    C, sb = HG_CHUNK, HG_SUB
    rloc = lax.broadcasted_iota(jnp.int32, (sb, 1), 0)
    rrow = lax.broadcasted_iota(jnp.int32, (sb, HG_D), 0)
    trow = lax.broadcasted_iota(jnp.int32, (C, C), 1) // sb
    dAT = dA.T
    dQ_rows, dKd_rows = [], []
    dK = jnp.zeros((C, HG_D), F32)
    for i in range(C // sb):
        sl = slice(sb * i, sb * i + sb)
        Qi, Ki, bi, dAi = Q[sl], K[sl], b[sl], dA[sl]
        od = _hg_offdiag(Q, K, b, i, rev)
        if od is None:
            dQi = jnp.zeros((sb, HG_D), F32)
        else:
            qs, ks = od
            dQi = lax.dot_general(dAi, K * ks, (((1,), (0,)), ((), ())),
                                  precision=HI, preferred_element_type=F32) * qs[sl]
            zt = jnp.where(trow == i, dAT, 0.0)
            dK = dK + lax.dot_general(zt, Q * qs, (((1,), (0,)), ((), ())),
                                      precision=HI, preferred_element_type=F32) * ks
        dKd = jnp.zeros((sb, HG_D), F32)
        for j in range(sb):
            vis = (rloc <= j) if rev else (rloc >= j)
            e = jnp.where(vis, jnp.exp(jnp.minimum(bi - bi[j:j + 1], 0.0)), 0.0)
            dcol = dAi[:, sb * i + j:sb * i + j + 1]
            dQi = dQi + dcol * Ki[j:j + 1] * e
            krow = jnp.sum(dcol * Qi * e, axis=0, keepdims=True)
            dKd = jnp.where(rrow == j, krow, dKd)
        dQ_rows.append(dQi)
        dKd_rows.append(dKd)
    return jnp.concatenate(dQ_rows, axis=0), dK + jnp.concatenate(dKd_rows, axis=0)


def _hg_prep(qr, fr, lb, tri):
    sigq = _sigmoid(qr)
    Q = qr * sigq
    sig, f = _hg_gate(fr, lb)
    K = 1.0 - f
    logf = jnp.log(f)
    b = lax.dot_general(tri.astype(F32), logf, (((1,), (0,)), ((), ())),
                        precision=HI, preferred_element_type=F32)
    return sigq, Q, sig, f, K, b


def _hgrn_scan(P, lb, *, S, rev, name):
    C, H, D_ = HG_CHUNK, HG_HEADS, HG_D
    HP = HG_PAIR if H % HG_PAIR == 0 else 1
    NC = S // C
    fcol = (3 if rev else 2) * H

    def cidx(n):
        return NC - 1 - n if rev else n

    def body(q_ref, v_ref, f_ref, lb_ref, o_ref, st_ref, a_ref, state):
        n = pl.program_id(1)

        @pl.when(n == 0)
        def _():
            state[...] = jnp.zeros_like(state)

        tri = _hg_masks(rev)
        qa, va, fa, lba, sta = q_ref[...], v_ref[...], f_ref[...], lb_ref[...], state[...]
        st_ref[...] = sta
        outs, amats, states = [], [], []
        for hp in range(HP):
            sl = slice(hp * D_, (hp + 1) * D_)
            _, Q, _, _, K, b = _hg_prep(qa[:, sl], fa[:, sl], lba[:, sl], tri)
            V, ST0 = va[:, sl], sta[hp]
            e_b = jnp.exp(b)
            bE = b[0:1] if rev else b[C - 1:C]
            W = jnp.exp(bE - b)
            inter = lax.dot_general(Q * e_b, ST0, (((1,), (1,)), ((), ())), preferred_element_type=F32)
            A = _hg_att(Q, K, b, rev)
            amats.append(A)
            outs.append(inter + jnp.dot(A, V, preferred_element_type=F32))
            states.append(ST0 * jnp.exp(bE) + lax.dot_general(
                V, K * W, (((0,), (0,)), ((), ())), preferred_element_type=F32))
        a_ref[...] = jnp.stack(amats)
        o_ref[...] = jnp.concatenate(outs, axis=1)
        state[...] = jnp.stack(states)

    blk = lambda c0: pl.BlockSpec((C, HP * D_), lambda h, n, c0=c0: (cidx(n), c0 // HP + h))
    return pl.pallas_call(
        body, name=name, grid=(H // HP, NC),
        in_specs=[blk(0), blk(H), blk(fcol), pl.BlockSpec((1, HP * D_), lambda h, n: (0, h))],
        out_specs=[pl.BlockSpec((C, HP * D_), lambda h, n: (cidx(n), h)),
                   pl.BlockSpec((None, HP, D_, D_), lambda h, n: (cidx(n), h, 0, 0)),
                   pl.BlockSpec((None, HP, C, C), lambda h, n: (cidx(n), h, 0, 0))],
        out_shape=[jax.ShapeDtypeStruct((S, H * D_), F32),
                   jax.ShapeDtypeStruct((NC, H, D_, D_), F32),
                   jax.ShapeDtypeStruct((NC, H, C, C), F32)],
        scratch_shapes=[pltpu.VMEM((HP, D_, D_), F32)],
        compiler_params=_cparams(("parallel", "arbitrary")),
    )(P, P, P, lb)


def _hgrn_scan_bwd(P, lb, st, amat, do, *, S, rev, name):
    C, H, D_ = HG_CHUNK, HG_HEADS, HG_D
    HP = HG_PAIR if H % HG_PAIR == 0 else 1
    NC = S // C
    fcol = (3 if rev else 2) * H

    def cidx(n):
        return n if rev else NC - 1 - n

    def body(q_ref, v_ref, f_ref, lb_ref, st_ref, a_ref, do_ref, dq_ref, dv_ref, df_ref, dlb_ref, dstate):
        n = pl.program_id(1)

        @pl.when(n == 0)
        def _():
            dstate[...] = jnp.zeros_like(dstate)
            dlb_ref[...] = jnp.zeros_like(dlb_ref)

        tri = _hg_masks(rev)
        tri_t = _hg_masks(not rev).astype(F32)
        qa, va, fa, lba, doa = q_ref[...], v_ref[...], f_ref[...], lb_ref[...], do_ref[...]
        sta, ama, dsta = st_ref[...], a_ref[...], dstate[...]
        trow = lax.broadcasted_iota(jnp.int32, (C, 1), 0)
        dqs, dvs, dfs, dlbs, dstates = [], [], [], [], []
        for hp in range(HP):
            sl = slice(hp * D_, (hp + 1) * D_)
            lbv, qr = lba[:, sl], qa[:, sl]
            sigq, Q, sig, f, K, b = _hg_prep(qr, fa[:, sl], lbv, tri)
            V, ST0, A, dO, dST1 = va[:, sl], sta[hp], ama[hp], doa[:, sl], dsta[hp]
            e_b = jnp.exp(b)
            bE = b[0:1] if rev else b[C - 1:C]
            eE = jnp.exp(bE)
            W = jnp.exp(bE - b)
            Qe = Q * e_b
            KW = K * W
            dA = jnp.where(tri, lax.dot_general(dO, V, (((1,), (1,)), ((), ())), preferred_element_type=F32), 0.0)
            dV = (lax.dot_general(A, dO, (((0,), (0,)), ((), ())), preferred_element_type=F32)
                  + lax.dot_general(KW, dST1, (((1,), (1,)), ((), ())), preferred_element_type=F32))
            dQe = jnp.dot(dO, ST0, preferred_element_type=F32)
            dKW = jnp.dot(V, dST1, preferred_element_type=F32)
            dstates.append(dST1 * eE + lax.dot_general(dO, Qe, (((0,), (0,)), ((), ())), preferred_element_type=F32))
            dQa, dKa = _hg_att_bwd(Q, K, b, dA, rev)
            dQ = dQe * e_b + dQa
            dK = dKW * W + dKa
            extra = _colsum(KW * dKW) + eE * _colsum(ST0 * dST1)
            db = Q * dQ - K * dK + jnp.where(trow == (0 if rev else C - 1), extra, 0.0)
            dlogf = lax.dot_general(tri_t, db, (((1,), (0,)), ((), ())), precision=HI, preferred_element_type=F32)
            dfv = dlogf / f - dK
            dfs.append(dfv * (1.0 - lbv) * sig * (1.0 - sig))
            dlbs.append(_colsum(dfv * (1.0 - sig)))
            dqs.append(dQ * (sigq * (1.0 + qr * (1.0 - sigq))))
            dvs.append(dV)
        dstate[...] = jnp.stack(dstates)
        df_ref[...] = jnp.concatenate(dfs, axis=1)
        dlb_ref[...] += jnp.concatenate(dlbs, axis=1)
        dq_ref[...] = jnp.concatenate(dqs, axis=1)
        dv_ref[...] = jnp.concatenate(dvs, axis=1)

    blk = lambda c0: pl.BlockSpec((C, HP * D_), lambda h, n, c0=c0: (cidx(n), c0 // HP + h))
    oblk = pl.BlockSpec((C, HP * D_), lambda h, n: (cidx(n), h))
    return pl.pallas_call(
        body, name=name, grid=(H // HP, NC),
        in_specs=[blk(0), blk(H), blk(fcol), pl.BlockSpec((1, HP * D_), lambda h, n: (0, h)),
                  pl.BlockSpec((None, HP, D_, D_), lambda h, n: (cidx(n), h, 0, 0)),
                  pl.BlockSpec((None, HP, C, C), lambda h, n: (cidx(n), h, 0, 0)),
                  oblk],
        out_specs=[oblk, oblk, oblk, pl.BlockSpec((1, HP * D_), lambda h, n: (0, h))],
        out_shape=[jax.ShapeDtypeStruct((S, H * D_), F32)] * 3 + [jax.ShapeDtypeStruct((1, H * D_), F32)],
        scratch_shapes=[pltpu.VMEM((HP, D_, D_), F32)],
        compiler_params=_cparams(("parallel", "arbitrary")),
    )(P, P, P, lb, st, amat, do)


LOG2E = 1.4426950408889634
MXU = BF16
ATT_SUB = 512


def _mx(x):
    return x if x.dtype == MXU else x.astype(MXU)


def _attn_fwd(q, k, v, *, S, T, H, dqk, dv, q_col0, k_col0, v_col0, scale, tq, tk, name):
    tq = _tile(S, tq, SUBLANES)
    tk = _tile(T, tk, LANES)
    nk = T // tk
    ts = _tile(tq, ATT_SUB, SUBLANES)

    def body(q_ref, k_ref, v_ref, o_ref, lse_ref, m_s, l_s, acc):
        j = pl.program_id(2)

        @pl.when(j == 0)
        def _():
            m_s[...] = jnp.full_like(m_s, -jnp.inf)
            l_s[...] = jnp.zeros_like(l_s)
            acc[...] = jnp.zeros_like(acc)

        kv, vv = _mx(k_ref[...]), _mx(v_ref[...])
        m_all, l_all, a_all = m_s[...], l_s[...], acc[...]
        ms, ls, accs = [], [], []
        for r0 in range(0, tq, ts):
            rows = slice(r0, r0 + ts)
            s = lax.dot_general(_mx(q_ref[rows, :]), kv, (((1,), (1,)), ((), ())),
                                preferred_element_type=F32) * (scale * LOG2E)
            m_old = m_all[rows]
            m_new = jnp.maximum(m_old, jnp.max(s, axis=1)[:, None])
            corr = jnp.exp2(m_old - m_new)
            p = jnp.exp2(s - jnp.tile(m_new, (1, tk // LANES)))
            ms.append(m_new)
            ls.append(corr * l_all[rows] + jnp.sum(p, axis=1)[:, None])
            accs.append(jnp.tile(corr, (1, dv // LANES)) * a_all[rows] + jnp.dot(_mx(p), vv, preferred_element_type=F32))
        m_s[...] = jnp.concatenate(ms, axis=0)
        l_s[...] = jnp.concatenate(ls, axis=0)
        acc[...] = jnp.concatenate(accs, axis=0)

        @pl.when(j == nk - 1)
        def _():
            o_ref[...] = acc[...] / jnp.tile(l_s[...], (1, dv // LANES))
            lse_ref[...] = ((m_s[...] + jnp.log2(l_s[...])) * (1.0 / LOG2E))[:, :1]

    return pl.pallas_call(
        body, name=name, grid=(H, S // tq, nk),
        in_specs=[pl.BlockSpec((tq, dqk), lambda h, i, j: (i, q_col0 + h)),
                  pl.BlockSpec((tk, dqk), lambda h, i, j: (j, k_col0 + h)),
                  pl.BlockSpec((tk, dv), lambda h, i, j: (j, v_col0 + h))],
        out_specs=[pl.BlockSpec((tq, dv), lambda h, i, j: (i, h)),
                   pl.BlockSpec((None, tq, 1), lambda h, i, j: (h, i, 0))],
        out_shape=[jax.ShapeDtypeStruct((S, H * dv), F32), jax.ShapeDtypeStruct((H, S, 1), F32)],
        scratch_shapes=[pltpu.VMEM((tq, LANES), F32), pltpu.VMEM((tq, LANES), F32), pltpu.VMEM((tq, dv), F32)],
        compiler_params=_cparams(("parallel", "parallel", "arbitrary")),
    )(q, k, v)


def _attn_bwd(q, k, v, o, lse, do, *, S, T, H, dqk, dv, q_col0, k_col0, v_col0, scale, tq, tk, exch=None, name):
    tq = _tile(S, tq, SUBLANES)
    tk = _tile(T, tk, LANES)
    nq = S // tq
    ts = _tile(tq, ATT_SUB, SUBLANES)

    def body(q_ref, k_ref, v_ref, o_ref, lse_ref, do_ref, dq_ref, dk_ref, dv_ref, dk_acc, dv_acc):
        j = pl.program_id(1)
        i = pl.program_id(2)

        @pl.when(jnp.logical_and(i == 0, j == 0))
        def _():
            dq_ref[...] = jnp.zeros_like(dq_ref)

        @pl.when(i == 0)
        def _():
            dk_acc[...] = jnp.zeros_like(dk_acc)
            dv_acc[...] = jnp.zeros_like(dv_acc)

        kv, vv = _mx(k_ref[...]), _mx(v_ref[...])
        dk_new, dv_new = dk_acc[...], dv_acc[...]
        lse2 = lse_ref[...] * LOG2E
        dqs = []
        for r0 in range(0, tq, ts):
            rows = slice(r0, r0 + ts)
            qv, dov = _mx(q_ref[rows, :]), do_ref[rows, :]
            s = lax.dot_general(qv, kv, (((1,), (1,)), ((), ())), preferred_element_type=F32) * (scale * LOG2E)
            p = jnp.exp2(s - lse2[rows])
            delta = jnp.sum(dov * o_ref[rows, :], axis=-1, keepdims=True)
            dob = _mx(dov)
            dp = lax.dot_general(dob, vv, (((1,), (1,)), ((), ())), preferred_element_type=F32)
            ds = _mx(p * (dp - delta) * scale)
            dv_new = dv_new + lax.dot_general(_mx(p), dob, (((0,), (0,)), ((), ())), preferred_element_type=F32)
            dk_new = dk_new + lax.dot_general(ds, qv, (((0,), (0,)), ((), ())), preferred_element_type=F32)
            dqs.append(jnp.dot(ds, kv, preferred_element_type=F32))
        dq_ref[pl.ds(pl.multiple_of(i * tq, tq), tq), :] += jnp.concatenate(dqs, axis=0)
        dk_acc[...] = dk_new
        dv_acc[...] = dv_new

        @pl.when(i == nq - 1)
        def _():
            dk_ref[...] = dk_new
            dv_ref[...] = dv_new

    outs, received = _carried(
        body, exch, (H, T // tk, nq),
        [pl.BlockSpec((tq, dqk), lambda h, j, i: (i, q_col0 + h)),
         pl.BlockSpec((tk, dqk), lambda h, j, i: (j, k_col0 + h)),
         pl.BlockSpec((tk, dv), lambda h, j, i: (j, v_col0 + h)),
         pl.BlockSpec((tq, dv), lambda h, j, i: (i, h)),
         pl.BlockSpec((None, tq, 1), lambda h, j, i: (h, i, 0)),
         pl.BlockSpec((tq, dv), lambda h, j, i: (i, h))],
        [pl.BlockSpec((S, dqk), lambda h, j, i: (0, h)),
         pl.BlockSpec((tk, dqk), lambda h, j, i: (j, h)),
         pl.BlockSpec((tk, dv), lambda h, j, i: (j, h))],
        [jax.ShapeDtypeStruct((S, H * dqk), F32), jax.ShapeDtypeStruct((T, H * dqk), F32),
         jax.ShapeDtypeStruct((T, H * dv), F32)],
        [pltpu.VMEM((tk, dqk), F32), pltpu.VMEM((tk, dv), F32)], [q, k, v, o, lse, do], name=name)
    return outs if exch is None else (outs, received)


def _rope_tables(positions):
    half = MLA_ROPE // 2
    inv_freq = jnp.power(ROPE_THETA, -jnp.arange(half, dtype=F32) / half)
    ang = positions.astype(F32)[:, None] * inv_freq
    cos, sin = jnp.cos(ang), jnp.sin(ang)
    z = jnp.zeros_like(cos)
    tc = jnp.concatenate([cos, cos, z, z], axis=1)
    ta = jnp.concatenate([-sin, z, z, z], axis=1)
    tb = jnp.concatenate([z, sin, z, z], axis=1)
    return tc, ta, tb


def _rope_apply(v, tc, ta, tb):
    half = MLA_ROPE // 2
    return v * tc + pltpu.roll(v, LANES - half, 1) * ta + pltpu.roll(v, half, 1) * tb


def _rope_apply_t(d, tc, ta, tb):
    half = MLA_ROPE // 2
    return d * tc + pltpu.roll(d * ta, half, 1) + pltpu.roll(d * tb, LANES - half, 1)


def _local_step(x, mem, positions, loss_target, w_in_k, small, W, MW, DFF, comm=None):
    S, D = x.shape
    M = mem.shape[0]
    HW = HG_HEADS * HG_D
    QR = small["mla_g_cq"].shape[1]
    KR = small["mla_g_ckv"].shape[1]
    MHD = MW // MEM_HEADS
    QW = MLA_HEADS * 2 * LANES
    VW = MLA_HEADS * MLA_V
    c_hg, c_cq, c_ckv, c_qm, c_gate = 0, 5 * HW, 5 * HW + QR, 5 * HW + QR + KR, 5 * HW + QR + KR + MW
    c_kr = c_gate + N_BRANCH * D
    PW = c_kr + LANES
    assert w_in_k.shape == (D, PW)
    TR = 256
    row = lambda a: a.reshape(1, -1)
    ge, be = row(small["ln_emb_g"]), row(small["ln_emb_b"])
    g1, b1, g2, b2 = small["ln1_g"], small["ln1_b"], small["ln2_g"], small["ln2_b"]
    lb = small["lb"]
    tc, ta, tb = _rope_tables(positions)

    (h0,) = _rowwise(lambda z, g, b: _ln_stats(z)[0] * g + b, [_Rows(x, D)], [ge, be],
                     [_Rows(D, D)], [], R=S, tr=TR, name="ln_emb")
    if comm is None:
        P = _mm(h0, w_in_k, S, PW, D, tn=896, name="proj_in")
    else:
        P, got = _mm(h0, w_in_k, S, PW, D, tn=896, exch=comm.gather_rest, name="proj_in")
        W = comm.weights(got)

    o_fw, st_fw, a_fw = _hgrn_scan(P, lb[0:1], S=S, rev=False, name="hgrn_fw")
    o_bw, st_bw, a_bw = _hgrn_scan(P, lb[1:2], S=S, rev=True, name="hgrn_bw")

    def hg_post(of, ob, gr, ng):
        o = of + ob
        sg = _sigmoid(gr)
        outs = []
        for h in range(HG_HEADS):
            sl = slice(h * HG_D, (h + 1) * HG_D)
            outs.append(_rms_fwd(o[:, sl], ng, RMS_EPS) * sg[:, sl])
        return jnp.concatenate(outs, axis=1)

    (y_hg,) = _rowwise(hg_post, [_Rows(o_fw, HW), _Rows(o_bw, HW), _Rows(P, HW, 4)], [small["hgrn_norm_g"]],
                       [_Rows(HW, HW)], [], R=S, tr=TR, name="hgrn_post")

    def mla_norm(cq, ckv, gq, gk):
        return _rms_fwd(cq, gq, RMS_EPS), _rms_fwd(ckv, gk, RMS_EPS)

    assert c_cq % QR == 0 and c_ckv % KR == 0
    cqn, ckvn = _rowwise(mla_norm, [_Rows(P, QR, c_cq // QR), _Rows(P, KR, c_ckv // KR)],
                         [small["mla_g_cq"], small["mla_g_ckv"]],
                         [_Rows(QR, QR), _Rows(KR, KR)], [], R=S, tr=TR, name="mla_norm")
    q_raw = _mm(cqn, W["mla_w_uq"], S, QW, QR, name="mla_uq")
    kv = _mm(ckvn, W["mla_w_ukv"], S, 2 * VW, KR, name="mla_ukv")

    def rope_fwd(qb, knb, vb_, krb, tcb, tab, tbb):
        kr = _rope_apply(krb, tcb, tab, tbb)
        qo, ko = [], []
        for h in range(MLA_HEADS):
            qo += [qb[:, 2 * h * LANES:(2 * h + 1) * LANES],
                   _rope_apply(qb[:, (2 * h + 1) * LANES:(2 * h + 2) * LANES], tcb, tab, tbb)]
            ko += [knb[:, h * LANES:(h + 1) * LANES], kr]
        return jnp.concatenate(qo, axis=1), jnp.concatenate(ko, axis=1), vb_

    qc, kc, vc = _rowwise(rope_fwd, [_Rows(q_raw, QW), _Rows(kv, VW), _Rows(kv, VW, 1), _Rows(P, LANES, c_kr // LANES),
                                     _Rows(tc, LANES), _Rows(ta, LANES), _Rows(tb, LANES)], [],
                          [_Rows(QW, QW, dtype=MXU), _Rows(QW, QW, dtype=MXU), _Rows(VW, VW, dtype=MXU)], [],
                          R=S, tr=TR, name="rope_fwd")
    mla_kw = dict(S=S, T=S, H=MLA_HEADS, dqk=2 * LANES, dv=MLA_V, q_col0=0, k_col0=0, v_col0=0,
                  scale=(MLA_NOPE + MLA_ROPE) ** -0.5, tq=1024, tk=2048)
    y_mla, lse_mla = _attn_fwd(qc, kc, vc, name="mla_attn", **mla_kw)

    kvm = _mm(mem, W["mem_w_kv"], M, 2 * MW, D, name="mem_kv")
    mem_kw = dict(S=S, T=M, H=MEM_HEADS, dqk=MHD, dv=MHD, q_col0=c_qm // MHD, k_col0=0, v_col0=MEM_HEADS,
                  scale=MHD ** -0.5, tq=1024, tk=M)
    assert c_qm % MHD == 0
    y_mem, lse_mem = _attn_fwd(P, kvm, kvm, name="mem_attn", **mem_kw)

    ys = (y_hg, y_mla, y_mem)
    us = [_mm(ys[b], W["w_branch"][b], S, D, HW, name=f"branch{b}") for b in range(N_BRANCH)]
    TCW = _tile(D, 1024, LANES)
    ncw = D // TCW

    def merge_fwd(g0, g1_, g2_, u0, u1, u2):
        return _sigmoid(g0) * u0 + _sigmoid(g1_) * u1 + _sigmoid(g2_) * u2

    gate_rows = [_Rows(P, TCW, (c_gate + b * D) // TCW) for b in range(N_BRANCH)]
    assert c_gate % TCW == 0
    (merged,) = _rowwise(merge_fwd, gate_rows + [_Rows(u, TCW) for u in us], [],
                         [_Rows(D, TCW)], [], R=S, tr=TR, ncol=ncw, name="merge_fwd")
    mix = _mm(merged, W["w_o"], S, D, D, name="out_proj")

    def ln_res(hp, addv, g, b):
        z = ALPHA * hp + addv
        return z, _ln_stats(z)[0] * g + b

    z1, h1 = _rowwise(ln_res, [_Rows(h0, D), _Rows(mix, D)], [g1, b1],
                      [_Rows(D, D), _Rows(D, D)], [], R=S, tr=TR, name="ln1")

    TF = _tile(DFF, 512, LANES)

    def swiglu(abv):
        a, b = abv[:, :TF], abv[:, TF:]
        return [abv, a * _sigmoid(a) * b]

    ab, cff = _mm(h1, W["w_ffn_gu"], S, 2 * DFF, D, tn=2 * TF, epilogue=(swiglu, [], [1, (1, 2)]), name="ffn_gu")
    ff = _mm(cff, W["w_ffn_down"], S, D, DFF, name="ffn_down")

    def loss_bwd(hp, addv, tgt, g, b):
        z = ALPHA * hp + addv
        xhat, rstd = _ln_stats(z)
        y = xhat * g + b
        err = y - tgt
        dy = err * (1.0 / D)
        dxh = dy * g
        m1 = jnp.mean(dxh, axis=-1, keepdims=True)
        m2 = jnp.mean(dxh * xhat, axis=-1, keepdims=True)
        dz = rstd * (dxh - m1 - xhat * m2)
        lrow = jnp.sum(_colsum(err * err), axis=-1, keepdims=True) * (0.5 / D)
        return dz, _colsum(dy * xhat), _colsum(dy), lrow

    dz2, dg2, db2, loss = _rowwise(loss_bwd, [_Rows(h1, D), _Rows(ff, D), _Rows(loss_target, D)], [g2, b2],
                                   [_Rows(D, D)], [D, D, 1], R=S, tr=TR, name="loss_ln2_bwd")
    def swiglu_bwd(dc, abv):
        a, b = abv[:, :TF], abv[:, TF:]
        sg = _sigmoid(a)
        return [jnp.concatenate([dc * b * sg * (1.0 + a * (1.0 - sg)), dc * a * sg], axis=1)]

    (dab,) = _mm(dz2, W["w_ffn_down"], S, DFF, D, tb=True, tn=TF, epilogue=(swiglu_bwd, [(ab, (2, 1))], [(2, 1)]),
                 name="ffn_down_dx")
    g_ffn_down = _mm(cff, dz2, DFF, D, S, ta=True, name="ffn_down_dw")
    dh1 = _mm(dab, W["w_ffn_gu"], S, D, 2 * DFF, tb=True, name="ffn_gu_dx")
    g_ffn_gu = _mm(h1, dab, D, 2 * DFF, S, ta=True, name="ffn_gu_dw")

    def ln1_bwd(z, dmm, dz2v, g):
        return _ln_bwd_core(z, g, ALPHA * dz2v + dmm)

    dz1, dg1, db1 = _rowwise(ln1_bwd, [_Rows(z1, D), _Rows(dh1, D), _Rows(dz2, D)], [g1],
                             [_Rows(D, D)], [D, D], R=S, tr=TR, name="ln1_bwd")
    dmerged = _mm(dz1, W["w_o"], S, D, D, tb=True, name="out_proj_dx")
    g_w_o = _mm(merged, dz1, D, D, S, ta=True, name="out_proj_dw")

    def merge_bwd(g0, g1_, g2_, u0, u1, u2, dm):
        res_g, res_u = [], []
        for gv, uv in ((g0, u0), (g1_, u1), (g2_, u2)):
            sg = _sigmoid(gv)
            res_g.append(dm * uv * sg * (1.0 - sg))
            res_u.append(dm * sg)
        return (*res_g, *res_u)

    mres = _rowwise(merge_bwd, gate_rows + [_Rows(u, TCW) for u in us] + [_Rows(dmerged, TCW)], [],
                    [_Rows(D, TCW)] * (2 * N_BRANCH), [], R=S, tr=TR, ncol=ncw, name="merge_bwd")
    dgates, dus = mres[:N_BRANCH], mres[N_BRANCH:]
    dys = [_mm(dus[b], W["w_branch"][b], S, HW, D, tb=True, name=f"branch{b}_dx") for b in range(N_BRANCH)]
    g_w_branch = [_mm(ys[b], dus[b], HW, D, S, ta=True, name=f"branch{b}_dw") for b in range(N_BRANCH)]

    dq_mem, dk_mem, dv_mem = _attn_bwd(P, kvm, kvm, y_mem, lse_mem, dys[2], name="mem_attn_bwd", **mem_kw)
    dkvm = jnp.concatenate([dk_mem, dv_mem], axis=1)
    g_mem_w_kv = _mm(mem, dkvm, D, 2 * MW, M, ta=True, name="mem_kv_dw")

    g_w_branch = jnp.stack(g_w_branch)
    delivered = {}
    if comm is None:
        dqc, dkc, dvv = _attn_bwd(qc, kc, vc, y_mla, lse_mla, dys[1], name="mla_attn_bwd", **mla_kw)
    else:
        exch = comm.scatter(dict(w_ffn_gu=g_ffn_gu, w_ffn_down=g_ffn_down, w_o=g_w_o, w_branch=g_w_branch,
                                 mem_w_kv=g_mem_w_kv))
        (dqc, dkc, dvv), got = _attn_bwd(qc, kc, vc, y_mla, lse_mla, dys[1], exch=exch, name="mla_attn_bwd",
                                         **mla_kw)
        delivered.update(zip(exch.names, got))

    def rope_bwd(dqb, dkb, tcb, tab, tbb):
        qo, kn = [], []
        dkr = jnp.zeros_like(tcb)
        for h in range(MLA_HEADS):
            qo += [dqb[:, 2 * h * LANES:(2 * h + 1) * LANES],
                   _rope_apply_t(dqb[:, (2 * h + 1) * LANES:(2 * h + 2) * LANES], tcb, tab, tbb)]
            kn.append(dkb[:, 2 * h * LANES:(2 * h + 1) * LANES])
            dkr = dkr + dkb[:, (2 * h + 1) * LANES:(2 * h + 2) * LANES]
        return jnp.concatenate(qo, axis=1), jnp.concatenate(kn, axis=1), _rope_apply_t(dkr, tcb, tab, tbb)

    dq_raw, dkn, dkr_raw = _rowwise(rope_bwd, [_Rows(dqc, QW), _Rows(dkc, QW), _Rows(tc, LANES),
                                               _Rows(ta, LANES), _Rows(tb, LANES)], [],
                                    [_Rows(QW, QW), _Rows(VW, VW), _Rows(LANES, LANES)], [],
                                    R=S, tr=TR, name="rope_bwd")
    dkv = jnp.concatenate([dkn, dvv], axis=1)
    dcqn = _mm(dq_raw, W["mla_w_uq"], S, QR, QW, tb=True, name="mla_uq_dx")
    g_mla_w_uq = _mm(cqn, dq_raw, QR, QW, S, ta=True, name="mla_uq_dw")
    dckvn = _mm(dkv, W["mla_w_ukv"], S, KR, 2 * VW, tb=True, name="mla_ukv_dx")
    g_mla_w_ukv = _mm(ckvn, dkv, KR, 2 * VW, S, ta=True, name="mla_ukv_dw")

    def mla_norm_bwd(cq, ckv, dq_, dk_, gq, gk):
        dcq, gq_rows = _rms_bwd(cq, gq, dq_, RMS_EPS)
        dck, gk_rows = _rms_bwd(ckv, gk, dk_, RMS_EPS)
        return dcq, dck, _colsum(gq_rows), _colsum(gk_rows)

    dcq, dckv, dg_cq, dg_ckv = _rowwise(
        mla_norm_bwd, [_Rows(P, QR, c_cq // QR), _Rows(P, KR, c_ckv // KR), _Rows(dcqn, QR), _Rows(dckvn, KR)],
        [small["mla_g_cq"], small["mla_g_ckv"]], [_Rows(QR, QR), _Rows(KR, KR)], [QR, KR],
        R=S, tr=TR, name="mla_norm_bwd")

    def hg_post_bwd(of, ob, gr, dy, ng):
        o = of + ob
        sg = _sigmoid(gr)
        do_, dgr = [], []
        dng = jnp.zeros((1, HG_D), F32)
        for h in range(HG_HEADS):
            sl = slice(h * HG_D, (h + 1) * HG_D)
            t = _rms_fwd(o[:, sl], ng, RMS_EPS)
            dgr.append(dy[:, sl] * t * sg[:, sl] * (1.0 - sg[:, sl]))
            dx, grow = _rms_bwd(o[:, sl], ng, dy[:, sl] * sg[:, sl], RMS_EPS)
            do_.append(dx)
            dng = dng + _colsum(grow)
        return jnp.concatenate(do_, axis=1), jnp.concatenate(dgr, axis=1), dng

    do_hg, dg_hg, dng = _rowwise(hg_post_bwd, [_Rows(o_fw, HW), _Rows(o_bw, HW), _Rows(P, HW, 4), _Rows(dys[0], HW)],
                                 [small["hgrn_norm_g"]], [_Rows(HW, HW), _Rows(HW, HW)], [HG_D],
                                 R=S, tr=TR, name="hgrn_post_bwd")
    dq_f, dv_f, dff_fw, dlb_f = _hgrn_scan_bwd(P, lb[0:1], st_fw, a_fw, do_hg, S=S, rev=False, name="hgrn_fw_bwd")
    dq_b, dv_b, dff_bw, dlb_b = _hgrn_scan_bwd(P, lb[1:2], st_bw, a_bw, do_hg, S=S, rev=True, name="hgrn_bw_bwd")
    THW = _tile(HW, 1024, LANES)
    dq_hg, dv_hg = _rowwise(lambda a, b, c, d: (a + b, c + d),
                            [_Rows(dq_f, THW), _Rows(dq_b, THW), _Rows(dv_f, THW), _Rows(dv_b, THW)], [],
                            [_Rows(HW, THW), _Rows(HW, THW)], [], R=S, tr=TR, ncol=HW // THW, name="hgrn_dir_sum")

    dP = jnp.concatenate([dq_hg, dv_hg, dff_fw, dff_bw, dg_hg, dcq, dckv, dq_mem, *dgates, dkr_raw], axis=1)
    g_w_in = _mm(h0, dP, D, PW, S, ta=True, tn=896, name="proj_in_dw")
    dx_kw = dict(tb=True, tk=_tile(PW, 640, LANES), name="proj_in_dx")
    if comm is None:
        dh0 = _mm(dP, w_in_k, S, D, PW, **dx_kw)
    else:
        exch = comm.scatter(dict(w_in=g_w_in, mla_w_uq=g_mla_w_uq, mla_w_ukv=g_mla_w_ukv))
        dh0, got = _mm(dP, w_in_k, S, D, PW, exch=exch, **dx_kw)
        delivered.update(zip(exch.names, got))

    def ln0_bwd(z, dmm, dz1v, g):
        return _ln_bwd_core(z, g, ALPHA * dz1v + dmm)

    grad_x, dge, dbe = _rowwise(ln0_bwd, [_Rows(x, D), _Rows(dh0, D), _Rows(dz1, D)], [ge],
                                [_Rows(D, D)], [D, D], R=S, tr=TR, name="ln_emb_bwd")

    big = dict(w_in=g_w_in, mla_w_uq=g_mla_w_uq, mla_w_ukv=g_mla_w_ukv, mem_w_kv=g_mem_w_kv,
               w_branch=g_w_branch, w_o=g_w_o, w_ffn_gu=g_ffn_gu, w_ffn_down=g_ffn_down)
    sm = dict(ln_emb_g=dge, ln_emb_b=dbe, dlb=jnp.concatenate([dlb_f, dlb_b], axis=0), hgrn_norm_g=dng,
              mla_g_cq=dg_cq, mla_g_ckv=dg_ckv, ln1_g=dg1, ln1_b=db1, ln2_g=dg2, ln2_b=db2)
    return loss, grad_x, big, sm, delivered


def _gather_exch(shards):
    copies = [(m, lambda x, y, c: 0, _chip) for m in _CHIP_MASKS]
    return _Exch([s[None] for s in shards], 4, copies, [(lambda x, y, c: 0, _chip)])


def _scatter_exch(pieces):
    copies = [(m, (lambda x, y, c, m=m: 2 * _chip(x ^ m[0], y ^ m[1], c) + (c ^ m[2])), _device) for m in _ALL_MASKS]
    local = [((lambda x, y, c: 2 * _chip(x, y, c) + c), _device)]
    return _Exch([p.reshape((8,) + p.shape[2:]) for p in pieces], 8, copies, local)


SHARE_BLOCK_BYTES = 4 << 20


def _sum_share(arr, *, name):
    n, rh, w = arr.shape
    tr = _tile(rh, max(16, SHARE_BLOCK_BYTES // (n * w * arr.dtype.itemsize) // 16 * 16), 16)
    nb = rh // tr

    def body(a_ref, o_ref, slots, send_sems, recv_sem, local_sems):
        i = pl.program_id(0)
        x, y, c = _coords()
        sibling = (x, y, 1 - c)

        def pushes(step, slot):
            rows = pl.ds(pl.multiple_of(c * rh + step * tr, SUBLANES), tr)
            return (pltpu.make_async_copy(slots.at[slot], o_ref.at[rows], local_sems.at[slot]),
                    pltpu.make_async_remote_copy(src_ref=slots.at[slot], dst_ref=o_ref.at[rows],
                                                 send_sem=send_sems.at[slot], recv_sem=recv_sem,
                                                 device_id=sibling, device_id_type=MESH))

        def drain(step, slot):
            loc, rem = pushes(step, slot)
            loc.wait()
            rem.wait_send()

        slot = i % 2

        @pl.when(i >= 2)
        def _():
            drain(i - 2, slot)

        acc = a_ref[0].astype(F32)
        for k in range(1, n):
            acc = acc + a_ref[k].astype(F32)
        slots[slot] = acc
        loc, rem = pushes(i, slot)
        loc.start()
        rem.start()

        @pl.when(i == nb - 1)
        def _():
            if nb >= 2:
                drain(i - 1, 1 - slot)
            drain(i, slot)
            other = o_ref.at[pl.ds(pl.multiple_of((1 - c) * rh, SUBLANES), rh)]
            pltpu.make_async_remote_copy(src_ref=other, dst_ref=other, send_sem=send_sems.at[0], recv_sem=recv_sem,
                                         device_id=sibling, device_id_type=MESH).wait_recv()

    return pl.pallas_call(
        body, name=name, grid=(nb,),
        in_specs=[pl.BlockSpec((n, tr, w), lambda i: (0, i, 0))],
        out_specs=pl.BlockSpec(memory_space=pl.ANY),
        out_shape=jax.ShapeDtypeStruct((2 * rh, w), F32),
        scratch_shapes=[pltpu.VMEM((2, tr, w), F32), pltpu.SemaphoreType.DMA((2,)), pltpu.SemaphoreType.DMA,
                        pltpu.SemaphoreType.DMA((2,))],
        compiler_params=pltpu.CompilerParams(dimension_semantics=("arbitrary",), has_side_effects=True,
                                             vmem_limit_bytes=VMEM_LIMIT),
    )(arr)


def _allreduce_small(v, *, name):
    r, w = v.shape

    def body(v_ref, o_ref, buf, send_sems, recv_sems):
        x, y, c = _coords()
        me = 4 * x + 2 * y + c
        buf[me] = v_ref[...]
        cps = []
        for k in range(7):
            m = ((k + 1) >> 2 & 1, (k + 1) >> 1 & 1, (k + 1) & 1)
            cp = pltpu.make_async_remote_copy(
                src_ref=v_ref, dst_ref=buf.at[me], send_sem=send_sems.at[k], recv_sem=recv_sems.at[k],
                device_id=(x ^ m[0], y ^ m[1], c ^ m[2]), device_id_type=MESH)
            cp.start()
            cps.append(cp)
        for cp in cps:
            cp.wait_recv()
        for cp in cps:
            cp.wait_send()
        acc = buf[0]
        for k in range(1, 8):
            acc = acc + buf[k]
        o_ref[...] = acc

    return pl.pallas_call(
        body, name=name,
        in_specs=[pl.BlockSpec(memory_space=pltpu.VMEM)],
        out_specs=pl.BlockSpec(memory_space=pltpu.VMEM),
        out_shape=jax.ShapeDtypeStruct((r, w), F32),
        scratch_shapes=[pltpu.VMEM((8, r, w), F32), pltpu.SemaphoreType.DMA((7,)), pltpu.SemaphoreType.DMA((7,))],
        compiler_params=pltpu.CompilerParams(has_side_effects=True),
    )(v)


_BIG = (("w_in", 1), ("mla_w_uq", 1), ("mla_w_ukv", 1), ("mem_w_kv", 0), ("w_branch", 1), ("w_o", 0),
        ("w_ffn_gate", 1), ("w_ffn_up", 1), ("w_ffn_down", 0))


def _assemble(gathered, ax):
    _, r, c = gathered.shape
    if ax == 0:
        return gathered.reshape(4 * r, c)
    return jnp.concatenate([gathered[j] for j in range(4)], axis=1)


def _split_pieces(g, ax):
    r, c = g.shape
    if ax == 0:
        return g.reshape(4, 2, r // 8, c).astype(BF16)
    rh, cs = r // 2, c // 4
    return jnp.stack([g[h * rh:(h + 1) * rh, j * cs:(j + 1) * cs].astype(BF16)
                      for j in range(4) for h in range(2)]).reshape(4, 2, rh, cs)


def _pad_cols(a, n):
    return jnp.pad(a, ((0, 0), (0, n - a.shape[1])))


def _to_kernel_layout(full, QR, KR):
    out = {}
    for n in ("mem_w_kv", "w_branch", "w_o", "w_ffn_down"):
        if n in full:
            out[n] = full[n]
    if "w_in" in full:
        w_in = full["w_in"]
        a = 5 * HG_HEADS * HG_D + QR + KR
        out["w_in"] = jnp.concatenate([w_in[:, :a], w_in[:, a + MLA_ROPE:], _pad_cols(w_in[:, a:a + MLA_ROPE], LANES)],
                                      axis=1)
    if "mla_w_uq" in full:
        uq = full["mla_w_uq"].reshape(QR, MLA_HEADS, MLA_NOPE + MLA_ROPE)
        out["mla_w_uq"] = jnp.pad(uq, ((0, 0), (0, 0), (0, 2 * LANES - MLA_NOPE - MLA_ROPE))).reshape(QR, -1)
    if "mla_w_ukv" in full:
        ukv = full["mla_w_ukv"].reshape(KR, MLA_HEADS, MLA_NOPE + MLA_V)
        out["mla_w_ukv"] = jnp.concatenate([ukv[:, :, :MLA_NOPE].reshape(KR, -1), ukv[:, :, MLA_NOPE:].reshape(KR, -1)],
                                           axis=1)
    if "w_ffn_gate" in full:
        gate, up = full["w_ffn_gate"], full["w_ffn_up"]
        DFF = gate.shape[1]
        TF = _tile(DFF, 512, LANES)
        blocks = []
        for j in range(DFF // TF):
            blocks += [gate[:, j * TF:(j + 1) * TF], up[:, j * TF:(j + 1) * TF]]
        out["w_ffn_gu"] = jnp.concatenate(blocks, axis=1)
    return out


def _from_kernel_layout(gk, QR, KR):
    out = {}
    for n in ("mem_w_kv", "w_o", "w_ffn_down"):
        if n in gk:
            out[n] = gk[n]
    if "w_branch" in gk:
        out["w_branch"] = gk["w_branch"].reshape(-1, gk["w_branch"].shape[-1])
    if "w_in" in gk:
        g = gk["w_in"]
        a = 5 * HG_HEADS * HG_D + QR + KR
        rest = g.shape[1] - LANES - a
        out["w_in"] = jnp.concatenate([g[:, :a], g[:, a + rest:a + rest + MLA_ROPE], g[:, a:a + rest]], axis=1)
    if "mla_w_uq" in gk:
        out["mla_w_uq"] = gk["mla_w_uq"].reshape(QR, MLA_HEADS, 2 * LANES)[:, :, :MLA_NOPE + MLA_ROPE].reshape(QR, -1)
    if "mla_w_ukv" in gk:
        VW = MLA_HEADS * MLA_V
        g = gk["mla_w_ukv"]
        out["mla_w_ukv"] = jnp.concatenate([g[:, :VW].reshape(KR, MLA_HEADS, MLA_NOPE),
                                            g[:, VW:].reshape(KR, MLA_HEADS, MLA_V)], axis=2).reshape(KR, -1)
    if "w_ffn_gu" in gk:
        g = gk["w_ffn_gu"]
        DFF = g.shape[1] // 2
        TF = _tile(DFF, 512, LANES)
        out["w_ffn_gate"] = jnp.concatenate([g[:, 2 * j * TF:(2 * j + 1) * TF] for j in range(DFF // TF)], axis=1)
        out["w_ffn_up"] = jnp.concatenate([g[:, (2 * j + 1) * TF:(2 * j + 2) * TF] for j in range(DFF // TF)], axis=1)
    return out


def _adamw(w, g, m, v, *, name):
    r, c = w.shape
    tr = max(SUBLANES, min(512, (1 << 20) // (4 * c)) // SUBLANES * SUBLANES)
    c1 = 1.0 / (1.0 - ADAM_B1 ** ADAM_STEP)
    c2 = 1.0 / (1.0 - ADAM_B2 ** ADAM_STEP)

    def fn(wv, gv, mv, vv):
        mn = ADAM_B1 * mv + (1.0 - ADAM_B1) * gv
        vn = ADAM_B2 * vv + (1.0 - ADAM_B2) * (gv * gv)
        delta = -ADAM_LR * ((mn * c1) / (jnp.sqrt(vn * c2) + ADAM_EPS) + ADAM_WD * wv)
        return delta, mn, vn

    return _rowwise(fn, [_Rows(a, c) for a in (w, g, m, v)], [], [_Rows(c, c)] * 3, [], R=r, tr=tr, name=name)


_SMALL = ("ln_emb_g", "ln_emb_b", "hgrn_lb_logits", "hgrn_norm_g", "mla_g_cq", "mla_g_ckv",
          "ln1_g", "ln1_b", "ln2_g", "ln2_b")


def _lb_from_logits(logits):
    return jnp.cumsum(jax.nn.softmax(logits, axis=1), axis=1)[:, 0]


def _small_rows(parts):
    flat = jnp.concatenate([p.reshape(-1) for p in parts])
    n = flat.shape[0]
    total = -(-n // (SUBLANES * LANES)) * SUBLANES * LANES
    return jnp.pad(flat, (0, total - n)).reshape(total // LANES, LANES)


def kernel(x, mem, positions, ln_emb_g, ln_emb_b, hgrn_lb_logits, w_in, hgrn_norm_g, mla_g_cq, mla_g_ckv, mla_w_uq, mla_w_ukv, mem_w_kv, w_branch, w_o, ln1_g, ln1_b, w_ffn_gate, w_ffn_up, w_ffn_down, ln2_g, ln2_b, loss_target, m_ln_emb_g, m_ln_emb_b, m_hgrn_lb_logits, m_w_in, m_hgrn_norm_g, m_mla_g_cq, m_mla_g_ckv, m_mla_w_uq, m_mla_w_ukv, m_mem_w_kv, m_w_branch, m_w_o, m_ln1_g, m_ln1_b, m_w_ffn_gate, m_w_ffn_up, m_w_ffn_down, m_ln2_g, m_ln2_b, v_ln_emb_g, v_ln_emb_b, v_hgrn_lb_logits, v_w_in, v_hgrn_norm_g, v_mla_g_cq, v_mla_g_ckv, v_mla_w_uq, v_mla_w_ukv, v_mem_w_kv, v_w_branch, v_w_o, v_ln1_g, v_ln1_b, v_w_ffn_gate, v_w_ffn_up, v_w_ffn_down, v_ln2_g, v_ln2_b):
    names = ["ln_emb_g", "ln_emb_b", "hgrn_lb_logits", "w_in", "hgrn_norm_g", "mla_g_cq", "mla_g_ckv", "mla_w_uq",
             "mla_w_ukv", "mem_w_kv", "w_branch", "w_o", "ln1_g", "ln1_b", "w_ffn_gate", "w_ffn_up", "w_ffn_down",
             "ln2_g", "ln2_b"]
    wts = dict(zip(names, [ln_emb_g, ln_emb_b, hgrn_lb_logits, w_in, hgrn_norm_g, mla_g_cq, mla_g_ckv, mla_w_uq,
                           mla_w_ukv, mem_w_kv, w_branch, w_o, ln1_g, ln1_b, w_ffn_gate, w_ffn_up, w_ffn_down,
                           ln2_g, ln2_b]))
    mom = dict(zip(names, [m_ln_emb_g, m_ln_emb_b, m_hgrn_lb_logits, m_w_in, m_hgrn_norm_g, m_mla_g_cq, m_mla_g_ckv,
                           m_mla_w_uq, m_mla_w_ukv, m_mem_w_kv, m_w_branch, m_w_o, m_ln1_g, m_ln1_b, m_w_ffn_gate,
                           m_w_ffn_up, m_w_ffn_down, m_ln2_g, m_ln2_b]))
    var = dict(zip(names, [v_ln_emb_g, v_ln_emb_b, v_hgrn_lb_logits, v_w_in, v_hgrn_norm_g, v_mla_g_cq, v_mla_g_ckv,
                           v_mla_w_uq, v_mla_w_ukv, v_mem_w_kv, v_w_branch, v_w_o, v_ln1_g, v_ln1_b, v_w_ffn_gate,
                           v_w_ffn_up, v_w_ffn_down, v_ln2_g, v_ln2_b]))
    xc, yc, cc = _coords()
    chip = _chip(xc, yc, cc)
    S, D = x.shape[1], x.shape[2]

    axis = dict(_BIG)
    shard = lambda n: wts[n].reshape(-1, wts[n].shape[-1]).astype(BF16)
    QR, KR = mla_w_uq.shape[1], mla_w_ukv.shape[1]
    MW, DFF = mem_w_kv.shape[2] // 2, 4 * w_ffn_gate.shape[2]
    (w_in_all,) = _exchange(_gather_exch([shard("w_in")]), name="gather_w_in")
    w_in_k = _to_kernel_layout(dict(w_in=_assemble(w_in_all, axis["w_in"])), QR, KR)["w_in"]
    rest_names = [n for n, _ in _BIG if n != "w_in"]

    class _Comm:
        gather_rest = _gather_exch([shard(n) for n in rest_names])

        @staticmethod
        def weights(received):
            full = {n: _assemble(g, axis[n]) for n, g in zip(rest_names, received)}
            full["w_branch"] = full["w_branch"].reshape(N_BRANCH, -1, D)
            return _to_kernel_layout(full, QR, KR)

        @staticmethod
        def scatter(gk_part):
            gpart = _from_kernel_layout(gk_part, QR, KR)
            exch = _scatter_exch([_split_pieces(gpart[n], axis[n]) for n in gpart])
            exch.names = list(gpart)
            return exch

    lsh = hgrn_lb_logits.shape
    HW = 4 * lsh[2]
    placed = lax.dynamic_update_slice(jnp.zeros((lsh[0], lsh[1], HW), F32), hgrn_lb_logits, (0, 0, chip * lsh[2]))
    placed = jnp.where(cc == 0, placed, 0.0)
    logits = _allreduce_small(_small_rows([placed]), name="gather_logits").reshape(-1)[:placed.size].reshape(placed.shape)
    lb, lb_vjp = jax.vjp(_lb_from_logits, logits)

    small = dict(ln_emb_g=ln_emb_g, ln_emb_b=ln_emb_b, lb=lb, hgrn_norm_g=hgrn_norm_g, mla_g_cq=mla_g_cq,
                 mla_g_ckv=mla_g_ckv, ln1_g=ln1_g, ln1_b=ln1_b, ln2_g=ln2_g, ln2_b=ln2_b)
    loss_l, grad_x, _, gs, delivered = _local_step(x[0], mem[0], positions[0], loss_target[0], w_in_k, small, None,
                                                   MW, DFF, comm=_Comm)

    (dlogits,) = lb_vjp(gs["dlb"])
    sm_parts = [loss_l, gs["ln_emb_g"], gs["ln_emb_b"], dlogits, gs["hgrn_norm_g"], gs["mla_g_cq"], gs["mla_g_ckv"],
                gs["ln1_g"], gs["ln1_b"], gs["ln2_g"], gs["ln2_b"]]
    red = _allreduce_small(_small_rows(sm_parts), name="allreduce_small").reshape(-1)
    sm_out, off = [], 0
    for p in sm_parts:
        sm_out.append(red[off:off + p.size].reshape(p.shape))
        off += p.size
    loss = sm_out[0].reshape(())
    g_small = dict(zip(_SMALL, sm_out[1:]))
    g_small["hgrn_lb_logits"] = lax.dynamic_slice(g_small["hgrn_lb_logits"], (0, 0, chip * lsh[2]), lsh)
    for n in _SMALL:
        g_small[n] = g_small[n].reshape(wts[n].shape)

    g_big = {n: _sum_share(delivered[n], name="rs_sum_" + n).reshape(wts[n].shape) for n, _ in _BIG}

    grads = {**g_small, **g_big}
    delta, new_m, new_v = {}, {}, {}
    for n, _ in _BIG:
        shp = wts[n].shape
        two_d = lambda a: a.reshape(-1, shp[-1])
        d_, m_, v_ = _adamw(two_d(wts[n]), two_d(grads[n]), two_d(mom[n]), two_d(var[n]), name="adamw_" + n)
        delta[n], new_m[n], new_v[n] = d_.reshape(shp), m_.reshape(shp), v_.reshape(shp)
    sw, sg_, sm_, sv_ = (_small_rows([d[n] for n in _SMALL]) for d in (wts, grads, mom, var))
    d_, m_, v_ = _adamw(sw, sg_, sm_, sv_, name="adamw_small")
    for res, packed_rows in ((delta, d_), (new_m, m_), (new_v, v_)):
        flat, off = packed_rows.reshape(-1), 0
        for n in _SMALL:
            res[n] = flat[off:off + wts[n].size].reshape(wts[n].shape)
            off += wts[n].size

    return (loss, grad_x[None], *[grads[n] for n in names], *[delta[n] for n in names],
            *[new_m[n] for n in names], *[new_v[n] for n in names])
```

```python
import jax
import jax.numpy as jnp
from jax import lax
from jax.experimental import pallas as pl
from jax.experimental.pallas import tpu as pltpu

F32 = jnp.float32
BF16 = jnp.bfloat16

HG_HEADS = 8
HG_D = 128
MLA_HEADS = 8
MLA_NOPE = 128
MLA_ROPE = 64
MLA_V = 128
MEM_HEADS = 4
N_BRANCH = 3
ROPE_THETA = 10000.0
DEPTH = 1
ALPHA = (2.0 * DEPTH) ** 0.25
LN_EPS = 1e-5
RMS_EPS = 1e-6
ADAM_LR = 0.001
ADAM_B1 = 0.9
ADAM_B2 = 0.999
ADAM_EPS = 1e-08
ADAM_WD = 0.01
ADAM_STEP = 10

LANES = 128
SUBLANES = 8
VMEM_LIMIT = 48 * 1024 * 1024

HG_CHUNK = 128
HG_SUB = 16
HG_PAIR = 2

MESH = pl.DeviceIdType.MESH
HI = lax.Precision.HIGHEST
HG_OFF_PREC = lax.Precision.DEFAULT


def _cparams(sem=None):
    if sem is None:
        return pltpu.CompilerParams(vmem_limit_bytes=VMEM_LIMIT)
    return pltpu.CompilerParams(dimension_semantics=sem, vmem_limit_bytes=VMEM_LIMIT)


def _tile(dim, pref, quantum):
    t = min(pref, dim) // quantum * quantum
    while t >= quantum:
        if dim % t == 0:
            return t
        t -= quantum
    return dim


def _sigmoid(x):
    return 1.0 / (1.0 + jnp.exp(-x))


def _coords():
    return lax.axis_index("x"), lax.axis_index("y"), lax.axis_index("c")


def _chip(x, y, c):
    return 2 * x + y


def _device(x, y, c):
    return 4 * x + 2 * y + c


_CHIP_MASKS = ((1, 0, 0), (0, 1, 0), (1, 1, 0))
_ALL_MASKS = tuple((k >> 2 & 1, k >> 1 & 1, k & 1) for k in range(1, 8))
_HBM = pl.BlockSpec(memory_space=pl.ANY)


class _Exch:
    def __init__(self, srcs, n_dst, copies, local_copies):
        self.srcs, self.n_dst, self.copies, self.local_copies = list(srcs), n_dst, copies, local_copies
        self.n = len(self.srcs)

    def out_shape(self):
        return [jax.ShapeDtypeStruct((self.n_dst,) + s.shape[1:], s.dtype) for s in self.srcs]

    def scratch(self):
        n_rc, n_lc = self.n * len(self.copies), self.n * len(self.local_copies)
        return [pltpu.SemaphoreType.DMA((n_rc,)), pltpu.SemaphoreType.DMA((n_rc,)),
                pltpu.SemaphoreType.DMA((max(n_lc, 1),))]

    def _descriptors(self, src_refs, dst_refs, sems):
        send_sems, recv_sems, local_sems = sems
        x, y, c = _coords()
        n_rc, n_lc = len(self.copies), len(self.local_copies)
        remote, local = [], []
        for a in range(self.n):
            for k, (mask, sidx, didx) in enumerate(self.copies):
                remote.append(pltpu.make_async_remote_copy(
                    src_ref=src_refs[a].at[sidx(x, y, c)], dst_ref=dst_refs[a].at[didx(x, y, c)],
                    send_sem=send_sems.at[a * n_rc + k], recv_sem=recv_sems.at[a * n_rc + k],
                    device_id=(x ^ mask[0], y ^ mask[1], c ^ mask[2]), device_id_type=MESH))
            for k, (sidx, didx) in enumerate(self.local_copies):
                local.append(pltpu.make_async_copy(src_refs[a].at[sidx(x, y, c)], dst_refs[a].at[didx(x, y, c)],
                                                   local_sems.at[a * n_lc + k]))
        return remote, local

    def start(self, src_refs, dst_refs, sems):
        remote, local = self._descriptors(src_refs, dst_refs, sems)
        for cp in remote + local:
            cp.start()

    def wait(self, src_refs, dst_refs, sems):
        remote, local = self._descriptors(src_refs, dst_refs, sems)
        for cp in remote:
            cp.wait_recv()
        for cp in remote:
            cp.wait_send()
        for cp in local:
            cp.wait()


def _exchange(exch, *, name):
    n = exch.n

    def body(*refs):
        src_refs, dst_refs, sems = refs[:n], refs[n:2 * n], refs[2 * n:]
        exch.start(src_refs, dst_refs, sems)
        exch.wait(src_refs, dst_refs, sems)

    return pl.pallas_call(
        body, name=name, in_specs=[_HBM] * n, out_specs=[_HBM] * n, out_shape=exch.out_shape(),
        scratch_shapes=exch.scratch(), compiler_params=pltpu.CompilerParams(has_side_effects=True),
    )(*exch.srcs)


def _carried(call, exch, grid, in_specs, out_specs, out_shape, scratch_shapes, args, *, name):
    n_in, n_out, n_scr = len(in_specs), len(out_specs), len(scratch_shapes)
    n = 0 if exch is None else exch.n

    def body(*refs):
        o0 = n_in + n
        s0 = o0 + n_out + n
        ins, srcs = refs[:n_in], refs[n_in:o0]
        outs, dsts = refs[o0:o0 + n_out], refs[o0 + n_out:s0]
        scr, sems = refs[s0:s0 + n_scr], refs[s0 + n_scr:]
        if exch is not None:
            ids = [pl.program_id(d) for d in range(len(grid))]
            first = _all([i == 0 for i in ids])
            last = _all([i == g - 1 for i, g in zip(ids, grid)])

            @pl.when(first)
            def _():
                exch.start(srcs, dsts, sems)

        call(*ins, *outs, *scr)
        if exch is not None:
            @pl.when(last)
            def _():
                exch.wait(srcs, dsts, sems)

    if exch is None:
        params = pltpu.CompilerParams(dimension_semantics=("arbitrary",) * len(grid), vmem_limit_bytes=VMEM_LIMIT)
        extra_in, extra_out, extra_shape, extra_scr, extra_args = [], [], [], [], []
    else:
        params = pltpu.CompilerParams(dimension_semantics=("arbitrary",) * len(grid), vmem_limit_bytes=VMEM_LIMIT,
                                      has_side_effects=True)
        extra_in, extra_out, extra_shape = [_HBM] * n, [_HBM] * n, exch.out_shape()
        extra_scr, extra_args = exch.scratch(), exch.srcs
    res = pl.pallas_call(
        body, name=name, grid=grid, in_specs=list(in_specs) + extra_in, out_specs=list(out_specs) + extra_out,
        out_shape=list(out_shape) + extra_shape, scratch_shapes=list(scratch_shapes) + extra_scr,
        compiler_params=params,
    )(*args, *extra_args)
    return res[:n_out], res[n_out:]


def _all(conds):
    out = conds[0]
    for c in conds[1:]:
        out = jnp.logical_and(out, c)
    return out


def _mm(a, b, M, N, K, *, ta=False, tb=False, a_off=(0, 0), b_off=(0, 0), add=None, exch=None, epilogue=None,
        tm=1024, tn=1024, tk=1024, name):
    tm = _tile(M, tm, LANES if ta else SUBLANES)
    tn = _tile(N, tn, LANES)
    tk = _tile(K, tk, LANES)
    nk = K // tk
    ar, ac = a_off
    br, bc = b_off

    if ta:
        assert ar % tk == 0 and ac % tm == 0
        a_spec = pl.BlockSpec((tk, tm), lambda i, j, k: (ar // tk + k, ac // tm + i))
    else:
        assert ar % tm == 0 and ac % tk == 0
        a_spec = pl.BlockSpec((tm, tk), lambda i, j, k: (ar // tm + i, ac // tk + k))
    if tb:
        assert br % tn == 0 and bc % tk == 0
        b_spec = pl.BlockSpec((tn, tk), lambda i, j, k: (br // tn + j, bc // tk + k))
    else:
        assert br % tk == 0 and bc % tn == 0
        b_spec = pl.BlockSpec((tk, tn), lambda i, j, k: (br // tk + k, bc // tn + j))
    o_spec = pl.BlockSpec((tm, tn), lambda i, j, k: (i, j))
    mixed = a.dtype != b.dtype

    epi_fn, epi_ins, epi_outs = (None, [], [1]) if epilogue is None else epilogue
    n_in = 2 + (add is not None) + len(epi_ins)

    def body(*refs):
        a_ref, b_ref = refs[:2]
        add_ref = refs[2] if add is not None else None
        epi_refs = refs[n_in - len(epi_ins):n_in]
        o_refs, acc = refs[n_in:n_in + len(epi_outs)], refs[-1]
        k = pl.program_id(2)
        av = a_ref[...]
        bv = b_ref[...]
        if ta:
            av = av.astype(F32).T
        if mixed:
            av = av.astype(BF16)
            bv = bv.astype(BF16)
        dims = (((1,), (1 if tb else 0,)), ((), ()))
        d = lax.dot_general(av, bv, dims, preferred_element_type=F32)

        def finish(total):
            if add is not None:
                total = total + add_ref[...]
            tiles = [total] if epi_fn is None else epi_fn(total, *[r[...] for r in epi_refs])
            for o_ref, t in zip(o_refs, tiles):
                o_ref[...] = t

        if nk == 1:
            finish(d)
        else:
            @pl.when(k == 0)
            def _():
                acc[...] = d

            @pl.when(jnp.logical_and(k > 0, k < nk - 1))
            def _():
                acc[...] += d

            @pl.when(k == nk - 1)
            def _():
                finish(acc[...] + d)

    in_specs = [a_spec, b_spec]
    args = [a, b]
    if add is not None:
        in_specs.append(o_spec)
        args.append(add)
    wide = lambda w: pl.BlockSpec((tm, tn * w[0] // w[1]), lambda i, j, k: (i, j))
    for arr, w in epi_ins:
        in_specs.append(wide(w))
        args.append(arr)
    out_specs = [o_spec if w == 1 else wide(w) for w in epi_outs]
    out_shape = [jax.ShapeDtypeStruct((M, N if w == 1 else N * w[0] // w[1]), F32) for w in epi_outs]
    outs, received = _carried(body, exch, (M // tm, N // tn, nk), in_specs, out_specs, out_shape,
                              [pltpu.VMEM((tm, tn), F32)], args, name=name)
    outs = outs[0] if epilogue is None else outs
    return outs if exch is None else (outs, received)


class _Rows:
    def __init__(self, arr, width, col0=0, lead=None, dtype=F32):
        self.arr, self.width, self.col0, self.lead, self.dtype = arr, width, col0, lead, dtype


def _rowwise(fn, rows, consts, outs, accs, *, R, tr, ncol=1, name):
    tr = _tile(R, tr, SUBLANES)
    nrow = R // tr

    def spec(r):
        if r.lead is None:
            return pl.BlockSpec((tr, r.width), lambda j, i, c0=r.col0: (i, c0 + j))
        return pl.BlockSpec((None, tr, r.width), lambda j, i, c0=r.col0, l=r.lead: (l, i, c0 + j))

    in_specs = [spec(r) for r in rows]
    for c in consts:
        in_specs.append(pl.BlockSpec(c.shape, lambda j, i, nd=c.ndim: (0,) * nd))
    out_specs = [spec(o) for o in outs]
    out_shape = [jax.ShapeDtypeStruct((R, o.arr), o.dtype) for o in outs]
    for w in accs:
        out_specs.append(pl.BlockSpec((1, w), lambda j, i: (0, j)))
        out_shape.append(jax.ShapeDtypeStruct((1, w * ncol), F32))
    n_in = len(rows) + len(consts)
    n_out = len(outs)

    def body(*refs):
        ins = [r[...] for r in refs[:n_in]]
        res = fn(*ins)
        if not isinstance(res, (tuple, list)):
            res = (res,)
        for k in range(n_out):
            refs[n_in + k][...] = res[k].astype(refs[n_in + k].dtype)
        i = pl.program_id(1)
        for k in range(len(accs)):
            a_ref = refs[n_in + n_out + k]

            @pl.when(i == 0)
            def _(a_ref=a_ref):
                a_ref[...] = jnp.zeros_like(a_ref)

            a_ref[...] += res[n_out + k]

    res = pl.pallas_call(
        body, name=name, grid=(ncol, nrow),
        in_specs=in_specs, out_specs=out_specs, out_shape=out_shape,
        compiler_params=_cparams(("parallel", "arbitrary")),
    )(*[r.arr for r in rows], *consts)
    return res


def _colsum(x):
    return jnp.sum(x, axis=0, keepdims=True)


def _ln_stats(z):
    mu = jnp.mean(z, axis=-1, keepdims=True)
    zc = z - mu
    var = jnp.mean(zc * zc, axis=-1, keepdims=True)
    rstd = lax.rsqrt(var + LN_EPS)
    return zc * rstd, rstd


def _ln_bwd_core(z, g, dy):
    xhat, rstd = _ln_stats(z)
    dxh = dy * g
    m1 = jnp.mean(dxh, axis=-1, keepdims=True)
    m2 = jnp.mean(dxh * xhat, axis=-1, keepdims=True)
    dz = rstd * (dxh - m1 - xhat * m2)
    return dz, _colsum(dy * xhat), _colsum(dy)


def _rms_fwd(x, g, eps):
    r = lax.rsqrt(jnp.mean(x * x, axis=-1, keepdims=True) + eps)
    return x * r * g


def _rms_bwd(x, g, dy, eps):
    r = lax.rsqrt(jnp.mean(x * x, axis=-1, keepdims=True) + eps)
    xr = x * r
    dyg = dy * g
    dx = r * (dyg - xr * jnp.mean(dyg * xr, axis=-1, keepdims=True))
    return dx, dy * xr


def _hg_gate(fr, lb):
    sig = _sigmoid(fr)
    f = lb + (1.0 - lb) * sig
    return sig, f


def _hg_masks(rev):
    C = HG_CHUNK
    t = lax.broadcasted_iota(jnp.int32, (C, C), 0)
    s = lax.broadcasted_iota(jnp.int32, (C, C), 1)
    tri = (s >= t) if rev else (s <= t)
    return tri


def _hg_offdiag(Q, K, b, i, rev):
    C, sb = HG_CHUNK, HG_SUB
    nb = C // sb
    if (not rev and i == 0) or (rev and i == nb - 1):
        return None
    ref = b[sb * i - 1:sb * i] if not rev else b[sb * (i + 1):sb * (i + 1) + 1]
    srow = lax.broadcasted_iota(jnp.int32, (C, 1), 0)
    smask = (srow < sb * i) if not rev else (srow >= sb * (i + 1))
    qscale = jnp.exp(jnp.minimum(b - ref, 0.0))
    kscale = jnp.where(smask, jnp.exp(jnp.minimum(ref - b, 0.0)), 0.0)
    return qscale, kscale


def _hg_att(Q, K, b, rev):
    C, sb = HG_CHUNK, HG_SUB
    lane = lax.broadcasted_iota(jnp.int32, (sb, C), 1)
    rloc = lax.broadcasted_iota(jnp.int32, (sb, 1), 0)
    rows = []
    for i in range(C // sb):
        sl = slice(sb * i, sb * i + sb)
        Qi, Ki, bi = Q[sl], K[sl], b[sl]
        od = _hg_offdiag(Q, K, b, i, rev)
        if od is None:
            acc = jnp.zeros((sb, C), F32)
        else:
            qs, ks = od
            acc = lax.dot_general(Qi * qs[sl], K * ks, (((1,), (1,)), ((), ())),
                                  precision=HG_OFF_PREC, preferred_element_type=F32)
        for j in range(sb):
            e = jnp.exp(jnp.minimum(bi - bi[j:j + 1], 0.0))
            col = jnp.sum(Qi * Ki[j:j + 1] * e, axis=-1, keepdims=True)
            vis = (rloc <= j) if rev else (rloc >= j)
            acc = jnp.where(lane == sb * i + j, jnp.where(vis, col, 0.0), acc)
        rows.append(acc)
    return jnp.concatenate(rows, axis=0)


def _hg_att_bwd(Q, K, b, dA, rev):
    C, sb = HG_CHUNK, HG_SUB
    rloc = lax.broadcasted_iota(jnp.int32, (sb, 1), 0)
    rrow = lax.broadcasted_iota(jnp.int32, (sb, HG_D), 0)
    trow = lax.broadcasted_iota(jnp.int32, (C, C), 1) // sb
    dAT = dA.T
    dQ_rows, dKd_rows = [], []
    dK = jnp.zeros((C, HG_D), F32)
    for i in range(C // sb):
        sl = slice(sb * i, sb * i + sb)
        Qi, Ki, bi, dAi = Q[sl], K[sl], b[sl], dA[sl]
        od = _hg_offdiag(Q, K, b, i, rev)
        if od is None:
            dQi = jnp.zeros((sb, HG_D), F32)
        else:
            qs, ks = od
            dQi = lax.dot_general(dAi, K * ks, (((1,), (0,)), ((), ())),
                                  precision=HI, preferred_element_type=F32) * qs[sl]
            zt = jnp.where(trow == i, dAT, 0.0)
            dK = dK + lax.dot_general(zt, Q * qs, (((1,), (0,)), ((), ())),
                                      precision=HI, preferred_element_type=F32) * ks
        dKd = jnp.zeros((sb, HG_D), F32)
        for j in range(sb):
            vis = (rloc <= j) if rev else (rloc >= j)
            e = jnp.where(vis, jnp.exp(jnp.minimum(bi - bi[j:j + 1], 0.0)), 0.0)
            dcol = dAi[:, sb * i + j:sb * i + j + 1]
            dQi = dQi + dcol * Ki[j:j + 1] * e
            krow = jnp.sum(dcol * Qi * e, axis=0, keepdims=True)
            dKd = jnp.where(rrow == j, krow, dKd)
        dQ_rows.append(dQi)
        dKd_rows.append(dKd)
    return jnp.concatenate(dQ_rows, axis=0), dK + jnp.concatenate(dKd_rows, axis=0)


def _hg_prep(qr, fr, lb, tri):
    sigq = _sigmoid(qr)
    Q = qr * sigq
    sig, f = _hg_gate(fr, lb)
    K = 1.0 - f
    logf = jnp.log(f)
    b = lax.dot_general(tri.astype(F32), logf, (((1,), (0,)), ((), ())),
                        precision=HI, preferred_element_type=F32)
    return sigq, Q, sig, f, K, b


def _hgrn_scan(P, lb, *, S, rev, name):
    C, H, D_ = HG_CHUNK, HG_HEADS, HG_D
    HP = HG_PAIR if H % HG_PAIR == 0 else 1
    NC = S // C
    fcol = (3 if rev else 2) * H

    def cidx(n):
        return NC - 1 - n if rev else n

    def body(q_ref, v_ref, f_ref, lb_ref, o_ref, st_ref, a_ref, state):
        n = pl.program_id(1)

        @pl.when(n == 0)
        def _():
            state[...] = jnp.zeros_like(state)

        tri = _hg_masks(rev)
        qa, va, fa, lba, sta = q_ref[...], v_ref[...], f_ref[...], lb_ref[...], state[...]
        st_ref[...] = sta
        outs, amats, states = [], [], []
        for hp in range(HP):
            sl = slice(hp * D_, (hp + 1) * D_)
            _, Q, _, _, K, b = _hg_prep(qa[:, sl], fa[:, sl], lba[:, sl], tri)
            V, ST0 = va[:, sl], sta[hp]
            e_b = jnp.exp(b)
            bE = b[0:1] if rev else b[C - 1:C]
            W = jnp.exp(bE - b)
            inter = lax.dot_general(Q * e_b, ST0, (((1,), (1,)), ((), ())), preferred_element_type=F32)
            A = _hg_att(Q, K, b, rev)
            amats.append(A)
            outs.append(inter + jnp.dot(A, V, preferred_element_type=F32))
            states.append(ST0 * jnp.exp(bE) + lax.dot_general(
                V, K * W, (((0,), (0,)), ((), ())), preferred_element_type=F32))
        a_ref[...] = jnp.stack(amats)
        o_ref[...] = jnp.concatenate(outs, axis=1)
        state[...] = jnp.stack(states)

    blk = lambda c0: pl.BlockSpec((C, HP * D_), lambda h, n, c0=c0: (cidx(n), c0 // HP + h))
    return pl.pallas_call(
        body, name=name, grid=(H // HP, NC),
        in_specs=[blk(0), blk(H), blk(fcol), pl.BlockSpec((1, HP * D_), lambda h, n: (0, h))],
        out_specs=[pl.BlockSpec((C, HP * D_), lambda h, n: (cidx(n), h)),
                   pl.BlockSpec((None, HP, D_, D_), lambda h, n: (cidx(n), h, 0, 0)),
                   pl.BlockSpec((None, HP, C, C), lambda h, n: (cidx(n), h, 0, 0))],
        out_shape=[jax.ShapeDtypeStruct((S, H * D_), F32),
                   jax.ShapeDtypeStruct((NC, H, D_, D_), F32),
                   jax.ShapeDtypeStruct((NC, H, C, C), F32)],
        scratch_shapes=[pltpu.VMEM((HP, D_, D_), F32)],
        compiler_params=_cparams(("parallel", "arbitrary")),
    )(P, P, P, lb)


def _hgrn_scan_bwd(P, lb, st, amat, do, *, S, rev, name):
    C, H, D_ = HG_CHUNK, HG_HEADS, HG_D
    HP = HG_PAIR if H % HG_PAIR == 0 else 1
    NC = S // C
    fcol = (3 if rev else 2) * H

    def cidx(n):
        return n if rev else NC - 1 - n

    def body(q_ref, v_ref, f_ref, lb_ref, st_ref, a_ref, do_ref, dq_ref, dv_ref, df_ref, dlb_ref, dstate):
        n = pl.program_id(1)

        @pl.when(n == 0)
        def _():
            dstate[...] = jnp.zeros_like(dstate)
            dlb_ref[...] = jnp.zeros_like(dlb_ref)

        tri = _hg_masks(rev)
        tri_t = _hg_masks(not rev).astype(F32)
        qa, va, fa, lba, doa = q_ref[...], v_ref[...], f_ref[...], lb_ref[...], do_ref[...]
        sta, ama, dsta = st_ref[...], a_ref[...], dstate[...]
        trow = lax.broadcasted_iota(jnp.int32, (C, 1), 0)
        dqs, dvs, dfs, dlbs, dstates = [], [], [], [], []
        for hp in range(HP):
            sl = slice(hp * D_, (hp + 1) * D_)
            lbv, qr = lba[:, sl], qa[:, sl]
            sigq, Q, sig, f, K, b = _hg_prep(qr, fa[:, sl], lbv, tri)
            V, ST0, A, dO, dST1 = va[:, sl], sta[hp], ama[hp], doa[:, sl], dsta[hp]
            e_b = jnp.exp(b)
            bE = b[0:1] if rev else b[C - 1:C]
            eE = jnp.exp(bE)
            W = jnp.exp(bE - b)
            Qe = Q * e_b
            KW = K * W
            dA = jnp.where(tri, lax.dot_general(dO, V, (((1,), (1,)), ((), ())), preferred_element_type=F32), 0.0)
            dV = (lax.dot_general(A, dO, (((0,), (0,)), ((), ())), preferred_element_type=F32)
                  + lax.dot_general(KW, dST1, (((1,), (1,)), ((), ())), preferred_element_type=F32))
            dQe = jnp.dot(dO, ST0, preferred_element_type=F32)
            dKW = jnp.dot(V, dST1, preferred_element_type=F32)
            dstates.append(dST1 * eE + lax.dot_general(dO, Qe, (((0,), (0,)), ((), ())), preferred_element_type=F32))
            dQa, dKa = _hg_att_bwd(Q, K, b, dA, rev)
            dQ = dQe * e_b + dQa
            dK = dKW * W + dKa
            extra = _colsum(KW * dKW) + eE * _colsum(ST0 * dST1)
            db = Q * dQ - K * dK + jnp.where(trow == (0 if rev else C - 1), extra, 0.0)
            dlogf = lax.dot_general(tri_t, db, (((1,), (0,)), ((), ())), precision=HI, preferred_element_type=F32)
            dfv = dlogf / f - dK
            dfs.append(dfv * (1.0 - lbv) * sig * (1.0 - sig))
            dlbs.append(_colsum(dfv * (1.0 - sig)))
            dqs.append(dQ * (sigq * (1.0 + qr * (1.0 - sigq))))
            dvs.append(dV)
        dstate[...] = jnp.stack(dstates)
        df_ref[...] = jnp.concatenate(dfs, axis=1)
        dlb_ref[...] += jnp.concatenate(dlbs, axis=1)
        dq_ref[...] = jnp.concatenate(dqs, axis=1)
        dv_ref[...] = jnp.concatenate(dvs, axis=1)

    blk = lambda c0: pl.BlockSpec((C, HP * D_), lambda h, n, c0=c0: (cidx(n), c0 // HP + h))
    oblk = pl.BlockSpec((C, HP * D_), lambda h, n: (cidx(n), h))
    return pl.pallas_call(
        body, name=name, grid=(H // HP, NC),
        in_specs=[blk(0), blk(H), blk(fcol), pl.BlockSpec((1, HP * D_), lambda h, n: (0, h)),
                  pl.BlockSpec((None, HP, D_, D_), lambda h, n: (cidx(n), h, 0, 0)),
                  pl.BlockSpec((None, HP, C, C), lambda h, n: (cidx(n), h, 0, 0)),
                  oblk],
        out_specs=[oblk, oblk, oblk, pl.BlockSpec((1, HP * D_), lambda h, n: (0, h))],
        out_shape=[jax.ShapeDtypeStruct((S, H * D_), F32)] * 3 + [jax.ShapeDtypeStruct((1, H * D_), F32)],
        scratch_shapes=[pltpu.VMEM((HP, D_, D_), F32)],
        compiler_params=_cparams(("parallel", "arbitrary")),
    )(P, P, P, lb, st, amat, do)


LOG2E = 1.4426950408889634
MXU = BF16
ATT_SUB = 512


def _mx(x):
    return x if x.dtype == MXU else x.astype(MXU)


def _attn_fwd(q, k, v, *, S, T, H, dqk, dv, q_col0, k_col0, v_col0, scale, tq, tk, exch=None, name):
    tq = _tile(S, tq, SUBLANES)
    tk = _tile(T, tk, LANES)
    nk = T // tk
    ts = _tile(tq, ATT_SUB, SUBLANES)

    def body(q_ref, k_ref, v_ref, o_ref, lse_ref, m_s, l_s, acc):
        j = pl.program_id(2)

        @pl.when(j == 0)
        def _():
            m_s[...] = jnp.full_like(m_s, -jnp.inf)
            l_s[...] = jnp.zeros_like(l_s)
            acc[...] = jnp.zeros_like(acc)

        kv, vv = _mx(k_ref[...]), _mx(v_ref[...])
        m_all, l_all, a_all = m_s[...], l_s[...], acc[...]
        ms, ls, accs = [], [], []
        for r0 in range(0, tq, ts):
            rows = slice(r0, r0 + ts)
            s = lax.dot_general(_mx(q_ref[rows, :]), kv, (((1,), (1,)), ((), ())),
                                preferred_element_type=F32) * (scale * LOG2E)
            m_old = m_all[rows]
            m_new = jnp.maximum(m_old, jnp.max(s, axis=1)[:, None])
            corr = jnp.exp2(m_old - m_new)
            p = jnp.exp2(s - jnp.tile(m_new, (1, tk // LANES)))
            ms.append(m_new)
            ls.append(corr * l_all[rows] + jnp.sum(p, axis=1)[:, None])
            accs.append(jnp.tile(corr, (1, dv // LANES)) * a_all[rows] + jnp.dot(_mx(p), vv, preferred_element_type=F32))
        m_s[...] = jnp.concatenate(ms, axis=0)
        l_s[...] = jnp.concatenate(ls, axis=0)
        acc[...] = jnp.concatenate(accs, axis=0)

        @pl.when(j == nk - 1)
        def _():
            o_ref[...] = acc[...] / jnp.tile(l_s[...], (1, dv // LANES))
            lse_ref[...] = ((m_s[...] + jnp.log2(l_s[...])) * (1.0 / LOG2E))[:, :1]

    outs, received = _carried(
        body, exch, (H, S // tq, nk),
        [pl.BlockSpec((tq, dqk), lambda h, i, j: (i, q_col0 + h)),
         pl.BlockSpec((tk, dqk), lambda h, i, j: (j, k_col0 + h)),
         pl.BlockSpec((tk, dv), lambda h, i, j: (j, v_col0 + h))],
        [pl.BlockSpec((tq, dv), lambda h, i, j: (i, h)),
         pl.BlockSpec((None, tq, 1), lambda h, i, j: (h, i, 0))],
        [jax.ShapeDtypeStruct((S, H * dv), F32), jax.ShapeDtypeStruct((H, S, 1), F32)],
        [pltpu.VMEM((tq, LANES), F32), pltpu.VMEM((tq, LANES), F32), pltpu.VMEM((tq, dv), F32)],
        [q, k, v], name=name)
    return outs if exch is None else (outs, received)


def _attn_bwd(q, k, v, o, lse, do, *, S, T, H, dqk, dv, q_col0, k_col0, v_col0, scale, tq, tk, exch=None, name):
    tq = _tile(S, tq, SUBLANES)
    tk = _tile(T, tk, LANES)
    nq = S // tq
    ts = _tile(tq, ATT_SUB, SUBLANES)

    def body(q_ref, k_ref, v_ref, o_ref, lse_ref, do_ref, dq_ref, dk_ref, dv_ref, dk_acc, dv_acc):
        j = pl.program_id(1)
        i = pl.program_id(2)

        @pl.when(jnp.logical_and(i == 0, j == 0))
        def _():
            dq_ref[...] = jnp.zeros_like(dq_ref)

        @pl.when(i == 0)
        def _():
            dk_acc[...] = jnp.zeros_like(dk_acc)
            dv_acc[...] = jnp.zeros_like(dv_acc)

        kv, vv = _mx(k_ref[...]), _mx(v_ref[...])
        dk_new, dv_new = dk_acc[...], dv_acc[...]
        lse2 = lse_ref[...] * LOG2E
        dqs = []
        for r0 in range(0, tq, ts):
            rows = slice(r0, r0 + ts)
            qv, dov = _mx(q_ref[rows, :]), do_ref[rows, :]
            s = lax.dot_general(qv, kv, (((1,), (1,)), ((), ())), preferred_element_type=F32) * (scale * LOG2E)
            p = jnp.exp2(s - lse2[rows])
            delta = jnp.sum(dov * o_ref[rows, :], axis=-1, keepdims=True)
            dob = _mx(dov)
            dp = lax.dot_general(dob, vv, (((1,), (1,)), ((), ())), preferred_element_type=F32)
            ds = _mx(p * (dp - delta) * scale)
            dv_new = dv_new + lax.dot_general(_mx(p), dob, (((0,), (0,)), ((), ())), preferred_element_type=F32)
            dk_new = dk_new + lax.dot_general(ds, qv, (((0,), (0,)), ((), ())), preferred_element_type=F32)
            dqs.append(jnp.dot(ds, kv, preferred_element_type=F32))
        dq_ref[pl.ds(pl.multiple_of(i * tq, tq), tq), :] += jnp.concatenate(dqs, axis=0)
        dk_acc[...] = dk_new
        dv_acc[...] = dv_new

        @pl.when(i == nq - 1)
        def _():
            dk_ref[...] = dk_new
            dv_ref[...] = dv_new

    outs, received = _carried(
        body, exch, (H, T // tk, nq),
        [pl.BlockSpec((tq, dqk), lambda h, j, i: (i, q_col0 + h)),
         pl.BlockSpec((tk, dqk), lambda h, j, i: (j, k_col0 + h)),
         pl.BlockSpec((tk, dv), lambda h, j, i: (j, v_col0 + h)),
         pl.BlockSpec((tq, dv), lambda h, j, i: (i, h)),
         pl.BlockSpec((None, tq, 1), lambda h, j, i: (h, i, 0)),
         pl.BlockSpec((tq, dv), lambda h, j, i: (i, h))],
        [pl.BlockSpec((S, dqk), lambda h, j, i: (0, h)),
         pl.BlockSpec((tk, dqk), lambda h, j, i: (j, h)),
         pl.BlockSpec((tk, dv), lambda h, j, i: (j, h))],
        [jax.ShapeDtypeStruct((S, H * dqk), F32), jax.ShapeDtypeStruct((T, H * dqk), F32),
         jax.ShapeDtypeStruct((T, H * dv), F32)],
        [pltpu.VMEM((tk, dqk), F32), pltpu.VMEM((tk, dv), F32)], [q, k, v, o, lse, do], name=name)
    return outs if exch is None else (outs, received)


def _rope_tables(positions):
    half = MLA_ROPE // 2
    inv_freq = jnp.power(ROPE_THETA, -jnp.arange(half, dtype=F32) / half)
    ang = positions.astype(F32)[:, None] * inv_freq
    cos, sin = jnp.cos(ang), jnp.sin(ang)
    z = jnp.zeros_like(cos)
    tc = jnp.concatenate([cos, cos, z, z], axis=1)
    ta = jnp.concatenate([-sin, z, z, z], axis=1)
    tb = jnp.concatenate([z, sin, z, z], axis=1)
    return tc, ta, tb


def _rope_apply(v, tc, ta, tb):
    half = MLA_ROPE // 2
    return v * tc + pltpu.roll(v, LANES - half, 1) * ta + pltpu.roll(v, half, 1) * tb


def _rope_apply_t(d, tc, ta, tb):
    half = MLA_ROPE // 2
    return d * tc + pltpu.roll(d * ta, half, 1) + pltpu.roll(d * tb, LANES - half, 1)


def _local_step(x, mem, positions, loss_target, w_in_k, small, W, MW, DFF, comm=None):
    S, D = x.shape
    M = mem.shape[0]
    HW = HG_HEADS * HG_D
    QR = small["mla_g_cq"].shape[1]
    KR = small["mla_g_ckv"].shape[1]
    MHD = MW // MEM_HEADS
    QW = MLA_HEADS * 2 * LANES
    VW = MLA_HEADS * MLA_V
    c_hg, c_cq, c_ckv, c_qm, c_gate = 0, 5 * HW, 5 * HW + QR, 5 * HW + QR + KR, 5 * HW + QR + KR + MW
    c_kr = c_gate + N_BRANCH * D
    PW = c_kr + LANES
    assert w_in_k.shape == (D, PW)
    TR = 256
    row = lambda a: a.reshape(1, -1)
    ge, be = row(small["ln_emb_g"]), row(small["ln_emb_b"])
    g1, b1, g2, b2 = small["ln1_g"], small["ln1_b"], small["ln2_g"], small["ln2_b"]
    lb = small["lb"]
    tc, ta, tb = _rope_tables(positions)

    (h0,) = _rowwise(lambda z, g, b: _ln_stats(z)[0] * g + b, [_Rows(x, D)], [ge, be],
                     [_Rows(D, D)], [], R=S, tr=TR, name="ln_emb")
    if comm is None:
        P = _mm(h0, w_in_k, S, PW, D, tn=896, name="proj_in")
    else:
        P, got = _mm(h0, w_in_k, S, PW, D, tn=896, exch=comm.gather_mix, name="proj_in")
        W = comm.weights(comm.mix_names, got)

    o_fw, st_fw, a_fw = _hgrn_scan(P, lb[0:1], S=S, rev=False, name="hgrn_fw")
    o_bw, st_bw, a_bw = _hgrn_scan(P, lb[1:2], S=S, rev=True, name="hgrn_bw")

    def hg_post(of, ob, gr, ng):
        o = of + ob
        sg = _sigmoid(gr)
        outs = []
        for h in range(HG_HEADS):
            sl = slice(h * HG_D, (h + 1) * HG_D)
            outs.append(_rms_fwd(o[:, sl], ng, RMS_EPS) * sg[:, sl])
        return jnp.concatenate(outs, axis=1)

    (y_hg,) = _rowwise(hg_post, [_Rows(o_fw, HW), _Rows(o_bw, HW), _Rows(P, HW, 4)], [small["hgrn_norm_g"]],
                       [_Rows(HW, HW)], [], R=S, tr=TR, name="hgrn_post")

    def mla_norm(cq, ckv, gq, gk):
        return _rms_fwd(cq, gq, RMS_EPS), _rms_fwd(ckv, gk, RMS_EPS)

    assert c_cq % QR == 0 and c_ckv % KR == 0
    cqn, ckvn = _rowwise(mla_norm, [_Rows(P, QR, c_cq // QR), _Rows(P, KR, c_ckv // KR)],
                         [small["mla_g_cq"], small["mla_g_ckv"]],
                         [_Rows(QR, QR), _Rows(KR, KR)], [], R=S, tr=TR, name="mla_norm")
    q_raw = _mm(cqn, W["mla_w_uq"], S, QW, QR, name="mla_uq")
    kv = _mm(ckvn, W["mla_w_ukv"], S, 2 * VW, KR, name="mla_ukv")

    def rope_fwd(qb, knb, vb_, krb, tcb, tab, tbb):
        kr = _rope_apply(krb, tcb, tab, tbb)
        qo, ko = [], []
        for h in range(MLA_HEADS):
            qo += [qb[:, 2 * h * LANES:(2 * h + 1) * LANES],
                   _rope_apply(qb[:, (2 * h + 1) * LANES:(2 * h + 2) * LANES], tcb, tab, tbb)]
            ko += [knb[:, h * LANES:(h + 1) * LANES], kr]
        return jnp.concatenate(qo, axis=1), jnp.concatenate(ko, axis=1), vb_

    qc, kc, vc = _rowwise(rope_fwd, [_Rows(q_raw, QW), _Rows(kv, VW), _Rows(kv, VW, 1), _Rows(P, LANES, c_kr // LANES),
                                     _Rows(tc, LANES), _Rows(ta, LANES), _Rows(tb, LANES)], [],
                          [_Rows(QW, QW, dtype=MXU), _Rows(QW, QW, dtype=MXU), _Rows(VW, VW, dtype=MXU)], [],
                          R=S, tr=TR, name="rope_fwd")
    mla_kw = dict(S=S, T=S, H=MLA_HEADS, dqk=2 * LANES, dv=MLA_V, q_col0=0, k_col0=0, v_col0=0,
                  scale=(MLA_NOPE + MLA_ROPE) ** -0.5, tq=2048, tk=2048)
    if comm is None:
        y_mla, lse_mla = _attn_fwd(qc, kc, vc, name="mla_attn", **mla_kw)
    else:
        (y_mla, lse_mla), got = _attn_fwd(qc, kc, vc, exch=comm.gather_ffn, name="mla_attn", **mla_kw)
        W = {**W, **comm.weights(comm.ffn_names, got)}

    kvm = _mm(mem, W["mem_w_kv"], M, 2 * MW, D, name="mem_kv")
    mem_kw = dict(S=S, T=M, H=MEM_HEADS, dqk=MHD, dv=MHD, q_col0=c_qm // MHD, k_col0=0, v_col0=MEM_HEADS,
                  scale=MHD ** -0.5, tq=1024, tk=M)
    assert c_qm % MHD == 0
    y_mem, lse_mem = _attn_fwd(P, kvm, kvm, name="mem_attn", **mem_kw)

    ys = (y_hg, y_mla, y_mem)
    us = [_mm(ys[b], W["w_branch"][b], S, D, HW, name=f"branch{b}") for b in range(N_BRANCH)]
    TCW = _tile(D, 1024, LANES)
    ncw = D // TCW

    def merge_fwd(g0, g1_, g2_, u0, u1, u2):
        return _sigmoid(g0) * u0 + _sigmoid(g1_) * u1 + _sigmoid(g2_) * u2

    gate_rows = [_Rows(P, TCW, (c_gate + b * D) // TCW) for b in range(N_BRANCH)]
    assert c_gate % TCW == 0
    (merged,) = _rowwise(merge_fwd, gate_rows + [_Rows(u, TCW) for u in us], [],
                         [_Rows(D, TCW)], [], R=S, tr=TR, ncol=ncw, name="merge_fwd")
    mix = _mm(merged, W["w_o"], S, D, D, name="out_proj")

    def ln_res(hp, addv, g, b):
        z = ALPHA * hp + addv
        return z, _ln_stats(z)[0] * g + b

    z1, h1 = _rowwise(ln_res, [_Rows(h0, D), _Rows(mix, D)], [g1, b1],
                      [_Rows(D, D), _Rows(D, D)], [], R=S, tr=TR, name="ln1")

    TF = _tile(DFF, 512, LANES)

    def swiglu(abv):
        a, b = abv[:, :TF], abv[:, TF:]
        return [abv, a * _sigmoid(a) * b]

    ab, cff = _mm(h1, W["w_ffn_gu"], S, 2 * DFF, D, tn=2 * TF, epilogue=(swiglu, [], [1, (1, 2)]), name="ffn_gu")
    ff = _mm(cff, W["w_ffn_down"], S, D, DFF, name="ffn_down")

    def loss_bwd(hp, addv, tgt, g, b):
        z = ALPHA * hp + addv
        xhat, rstd = _ln_stats(z)
        y = xhat * g + b
        err = y - tgt
        dy = err * (1.0 / D)
        dxh = dy * g
        m1 = jnp.mean(dxh, axis=-1, keepdims=True)
        m2 = jnp.mean(dxh * xhat, axis=-1, keepdims=True)
        dz = rstd * (dxh - m1 - xhat * m2)
        lrow = jnp.sum(_colsum(err * err), axis=-1, keepdims=True) * (0.5 / D)
        return dz, _colsum(dy * xhat), _colsum(dy), lrow

    dz2, dg2, db2, loss = _rowwise(loss_bwd, [_Rows(h1, D), _Rows(ff, D), _Rows(loss_target, D)], [g2, b2],
                                   [_Rows(D, D)], [D, D, 1], R=S, tr=TR, name="loss_ln2_bwd")
    def swiglu_bwd(dc, abv):
        a, b = abv[:, :TF], abv[:, TF:]
        sg = _sigmoid(a)
        return [jnp.concatenate([dc * b * sg * (1.0 + a * (1.0 - sg)), dc * a * sg], axis=1)]

    (dab,) = _mm(dz2, W["w_ffn_down"], S, DFF, D, tb=True, tn=TF, epilogue=(swiglu_bwd, [(ab, (2, 1))], [(2, 1)]),
                 name="ffn_down_dx")
    g_ffn_down = _mm(cff, dz2, DFF, D, S, ta=True, name="ffn_down_dw")
    dh1 = _mm(dab, W["w_ffn_gu"], S, D, 2 * DFF, tb=True, name="ffn_gu_dx")
    g_ffn_gu = _mm(h1, dab, D, 2 * DFF, S, ta=True, name="ffn_gu_dw")

    def ln1_bwd(z, dmm, dz2v, g):
        return _ln_bwd_core(z, g, ALPHA * dz2v + dmm)

    dz1, dg1, db1 = _rowwise(ln1_bwd, [_Rows(z1, D), _Rows(dh1, D), _Rows(dz2, D)], [g1],
                             [_Rows(D, D)], [D, D], R=S, tr=TR, name="ln1_bwd")
    dmerged = _mm(dz1, W["w_o"], S, D, D, tb=True, name="out_proj_dx")
    g_w_o = _mm(merged, dz1, D, D, S, ta=True, name="out_proj_dw")

    def merge_bwd(g0, g1_, g2_, u0, u1, u2, dm):
        res_g, res_u = [], []
        for gv, uv in ((g0, u0), (g1_, u1), (g2_, u2)):
            sg = _sigmoid(gv)
            res_g.append(dm * uv * sg * (1.0 - sg))
            res_u.append(dm * sg)
        return (*res_g, *res_u)

    mres = _rowwise(merge_bwd, gate_rows + [_Rows(u, TCW) for u in us] + [_Rows(dmerged, TCW)], [],
                    [_Rows(D, TCW)] * (2 * N_BRANCH), [], R=S, tr=TR, ncol=ncw, name="merge_bwd")
    dgates, dus = mres[:N_BRANCH], mres[N_BRANCH:]
    dys = [_mm(dus[b], W["w_branch"][b], S, HW, D, tb=True, name=f"branch{b}_dx") for b in range(N_BRANCH)]
    g_w_branch = [_mm(ys[b], dus[b], HW, D, S, ta=True, name=f"branch{b}_dw") for b in range(N_BRANCH)]

    dq_mem, dk_mem, dv_mem = _attn_bwd(P, kvm, kvm, y_mem, lse_mem, dys[2], name="mem_attn_bwd", **mem_kw)
    dkvm = jnp.concatenate([dk_mem, dv_mem], axis=1)
    g_mem_w_kv = _mm(mem, dkvm, D, 2 * MW, M, ta=True, name="mem_kv_dw")

    g_w_branch = jnp.stack(g_w_branch)
    delivered = {}
    if comm is None:
        dqc, dkc, dvv = _attn_bwd(qc, kc, vc, y_mla, lse_mla, dys[1], name="mla_attn_bwd", **mla_kw)
    else:
        exch = comm.scatter(dict(w_ffn_gu=g_ffn_gu, w_ffn_down=g_ffn_down, w_o=g_w_o, w_branch=g_w_branch,
                                 mem_w_kv=g_mem_w_kv))
        (dqc, dkc, dvv), got = _attn_bwd(qc, kc, vc, y_mla, lse_mla, dys[1], exch=exch, name="mla_attn_bwd",
                                         **mla_kw)
        delivered.update(zip(exch.names, got))

    def rope_bwd(dqb, dkb, tcb, tab, tbb):
        qo, kn = [], []
        dkr = jnp.zeros_like(tcb)
        for h in range(MLA_HEADS):
            qo += [dqb[:, 2 * h * LANES:(2 * h + 1) * LANES],
                   _rope_apply_t(dqb[:, (2 * h + 1) * LANES:(2 * h + 2) * LANES], tcb, tab, tbb)]
            kn.append(dkb[:, 2 * h * LANES:(2 * h + 1) * LANES])
            dkr = dkr + dkb[:, (2 * h + 1) * LANES:(2 * h + 2) * LANES]
        return jnp.concatenate(qo, axis=1), jnp.concatenate(kn, axis=1), _rope_apply_t(dkr, tcb, tab, tbb)

    dq_raw, dkn, dkr_raw = _rowwise(rope_bwd, [_Rows(dqc, QW), _Rows(dkc, QW), _Rows(tc, LANES),
                                               _Rows(ta, LANES), _Rows(tb, LANES)], [],
                                    [_Rows(QW, QW), _Rows(VW, VW), _Rows(LANES, LANES)], [],
                                    R=S, tr=TR, name="rope_bwd")
    dkv = jnp.concatenate([dkn, dvv], axis=1)
    dcqn = _mm(dq_raw, W["mla_w_uq"], S, QR, QW, tb=True, name="mla_uq_dx")
    g_mla_w_uq = _mm(cqn, dq_raw, QR, QW, S, ta=True, name="mla_uq_dw")
    dckvn = _mm(dkv, W["mla_w_ukv"], S, KR, 2 * VW, tb=True, name="mla_ukv_dx")
    g_mla_w_ukv = _mm(ckvn, dkv, KR, 2 * VW, S, ta=True, name="mla_ukv_dw")

    def mla_norm_bwd(cq, ckv, dq_, dk_, gq, gk):
        dcq, gq_rows = _rms_bwd(cq, gq, dq_, RMS_EPS)
        dck, gk_rows = _rms_bwd(ckv, gk, dk_, RMS_EPS)
        return dcq, dck, _colsum(gq_rows), _colsum(gk_rows)

    dcq, dckv, dg_cq, dg_ckv = _rowwise(
        mla_norm_bwd, [_Rows(P, QR, c_cq // QR), _Rows(P, KR, c_ckv // KR), _Rows(dcqn, QR), _Rows(dckvn, KR)],
        [small["mla_g_cq"], small["mla_g_ckv"]], [_Rows(QR, QR), _Rows(KR, KR)], [QR, KR],
        R=S, tr=TR, name="mla_norm_bwd")

    def hg_post_bwd(of, ob, gr, dy, ng):
        o = of + ob
        sg = _sigmoid(gr)
        do_, dgr = [], []
        dng = jnp.zeros((1, HG_D), F32)
        for h in range(HG_HEADS):
            sl = slice(h * HG_D, (h + 1) * HG_D)
            t = _rms_fwd(o[:, sl], ng, RMS_EPS)
            dgr.append(dy[:, sl] * t * sg[:, sl] * (1.0 - sg[:, sl]))
            dx, grow = _rms_bwd(o[:, sl], ng, dy[:, sl] * sg[:, sl], RMS_EPS)
            do_.append(dx)
            dng = dng + _colsum(grow)
        return jnp.concatenate(do_, axis=1), jnp.concatenate(dgr, axis=1), dng

    do_hg, dg_hg, dng = _rowwise(hg_post_bwd, [_Rows(o_fw, HW), _Rows(o_bw, HW), _Rows(P, HW, 4), _Rows(dys[0], HW)],
                                 [small["hgrn_norm_g"]], [_Rows(HW, HW), _Rows(HW, HW)], [HG_D],
                                 R=S, tr=TR, name="hgrn_post_bwd")
    dq_f, dv_f, dff_fw, dlb_f = _hgrn_scan_bwd(P, lb[0:1], st_fw, a_fw, do_hg, S=S, rev=False, name="hgrn_fw_bwd")
    dq_b, dv_b, dff_bw, dlb_b = _hgrn_scan_bwd(P, lb[1:2], st_bw, a_bw, do_hg, S=S, rev=True, name="hgrn_bw_bwd")
    THW = _tile(HW, 1024, LANES)
    dq_hg, dv_hg = _rowwise(lambda a, b, c, d: (a + b, c + d),
                            [_Rows(dq_f, THW), _Rows(dq_b, THW), _Rows(dv_f, THW), _Rows(dv_b, THW)], [],
                            [_Rows(HW, THW), _Rows(HW, THW)], [], R=S, tr=TR, ncol=HW // THW, name="hgrn_dir_sum")

    dP = jnp.concatenate([dq_hg, dv_hg, dff_fw, dff_bw, dg_hg, dcq, dckv, dq_mem, *dgates, dkr_raw], axis=1)
    g_w_in = _mm(h0, dP, D, PW, S, ta=True, tn=896, name="proj_in_dw")
    dx_kw = dict(tb=True, tk=_tile(PW, 640, LANES), name="proj_in_dx")
    if comm is None:
        dh0 = _mm(dP, w_in_k, S, D, PW, **dx_kw)
    else:
        exch = comm.scatter(dict(w_in=g_w_in, mla_w_uq=g_mla_w_uq, mla_w_ukv=g_mla_w_ukv))
        dh0, got = _mm(dP, w_in_k, S, D, PW, exch=exch, **dx_kw)
        delivered.update(zip(exch.names, got))

    def ln0_bwd(z, dmm, dz1v, g):
        return _ln_bwd_core(z, g, ALPHA * dz1v + dmm)

    grad_x, dge, dbe = _rowwise(ln0_bwd, [_Rows(x, D), _Rows(dh0, D), _Rows(dz1, D)], [ge],
                                [_Rows(D, D)], [D, D], R=S, tr=TR, name="ln_emb_bwd")

    big = dict(w_in=g_w_in, mla_w_uq=g_mla_w_uq, mla_w_ukv=g_mla_w_ukv, mem_w_kv=g_mem_w_kv,
               w_branch=g_w_branch, w_o=g_w_o, w_ffn_gu=g_ffn_gu, w_ffn_down=g_ffn_down)
    sm = dict(ln_emb_g=dge, ln_emb_b=dbe, dlb=jnp.concatenate([dlb_f, dlb_b], axis=0), hgrn_norm_g=dng,
              mla_g_cq=dg_cq, mla_g_ckv=dg_ckv, ln1_g=dg1, ln1_b=db1, ln2_g=dg2, ln2_b=db2)
    return loss, grad_x, big, sm, delivered


def _gather_exch(shards):
    copies = [(m, lambda x, y, c: 0, _chip) for m in _CHIP_MASKS]
    return _Exch([s[None] for s in shards], 4, copies, [(lambda x, y, c: 0, _chip)])


def _scatter_exch(pieces):
    copies = [(m, (lambda x, y, c, m=m: 2 * _chip(x ^ m[0], y ^ m[1], c) + (c ^ m[2])), _device) for m in _ALL_MASKS]
    local = [((lambda x, y, c: 2 * _chip(x, y, c) + c), _device)]
    return _Exch([p.reshape((8,) + p.shape[2:]) for p in pieces], 8, copies, local)


SHARE_BLOCK_BYTES = 4 << 20


def _sum_share(arr, *, name):
    n, rh, w = arr.shape
    tr = _tile(rh, max(16, SHARE_BLOCK_BYTES // (n * w * arr.dtype.itemsize) // 16 * 16), 16)
    nb = rh // tr

    def body(a_ref, o_ref, slots, send_sems, recv_sem, local_sems):
        i = pl.program_id(0)
        x, y, c = _coords()
        sibling = (x, y, 1 - c)

        def pushes(step, slot):
            rows = pl.ds(pl.multiple_of(c * rh + step * tr, SUBLANES), tr)
            return (pltpu.make_async_copy(slots.at[slot], o_ref.at[rows], local_sems.at[slot]),
                    pltpu.make_async_remote_copy(src_ref=slots.at[slot], dst_ref=o_ref.at[rows],
                                                 send_sem=send_sems.at[slot], recv_sem=recv_sem,
                                                 device_id=sibling, device_id_type=MESH))

        def drain(step, slot):
            loc, rem = pushes(step, slot)
            loc.wait()
            rem.wait_send()

        slot = i % 2

        @pl.when(i >= 2)
        def _():
            drain(i - 2, slot)

        acc = a_ref[0].astype(F32)
        for k in range(1, n):
            acc = acc + a_ref[k].astype(F32)
        slots[slot] = acc
        loc, rem = pushes(i, slot)
        loc.start()
        rem.start()

        @pl.when(i == nb - 1)
        def _():
            if nb >= 2:
                drain(i - 1, 1 - slot)
            drain(i, slot)
            other = o_ref.at[pl.ds(pl.multiple_of((1 - c) * rh, SUBLANES), rh)]
            pltpu.make_async_remote_copy(src_ref=other, dst_ref=other, send_sem=send_sems.at[0], recv_sem=recv_sem,
                                         device_id=sibling, device_id_type=MESH).wait_recv()

    return pl.pallas_call(
        body, name=name, grid=(nb,),
        in_specs=[pl.BlockSpec((n, tr, w), lambda i: (0, i, 0))],
        out_specs=pl.BlockSpec(memory_space=pl.ANY),
        out_shape=jax.ShapeDtypeStruct((2 * rh, w), F32),
        scratch_shapes=[pltpu.VMEM((2, tr, w), F32), pltpu.SemaphoreType.DMA((2,)), pltpu.SemaphoreType.DMA,
                        pltpu.SemaphoreType.DMA((2,))],
        compiler_params=pltpu.CompilerParams(dimension_semantics=("arbitrary",), has_side_effects=True,
                                             vmem_limit_bytes=VMEM_LIMIT),
    )(arr)


def _allreduce_small(v, *, name):
    r, w = v.shape

    def body(v_ref, o_ref, buf, send_sems, recv_sems):
        x, y, c = _coords()
        me = 4 * x + 2 * y + c
        buf[me] = v_ref[...]
        cps = []
        for k in range(7):
            m = ((k + 1) >> 2 & 1, (k + 1) >> 1 & 1, (k + 1) & 1)
            cp = pltpu.make_async_remote_copy(
                src_ref=v_ref, dst_ref=buf.at[me], send_sem=send_sems.at[k], recv_sem=recv_sems.at[k],
                device_id=(x ^ m[0], y ^ m[1], c ^ m[2]), device_id_type=MESH)
            cp.start()
            cps.append(cp)
        for cp in cps:
            cp.wait_recv()
        for cp in cps:
            cp.wait_send()
        acc = buf[0]
        for k in range(1, 8):
            acc = acc + buf[k]
        o_ref[...] = acc

    return pl.pallas_call(
        body, name=name,
        in_specs=[pl.BlockSpec(memory_space=pltpu.VMEM)],
        out_specs=pl.BlockSpec(memory_space=pltpu.VMEM),
        out_shape=jax.ShapeDtypeStruct((r, w), F32),
        scratch_shapes=[pltpu.VMEM((8, r, w), F32), pltpu.SemaphoreType.DMA((7,)), pltpu.SemaphoreType.DMA((7,))],
        compiler_params=pltpu.CompilerParams(has_side_effects=True),
    )(v)


_BIG = (("w_in", 1), ("mla_w_uq", 1), ("mla_w_ukv", 1), ("mem_w_kv", 0), ("w_branch", 1), ("w_o", 0),
        ("w_ffn_gate", 1), ("w_ffn_up", 1), ("w_ffn_down", 0))


def _assemble(gathered, ax):
    _, r, c = gathered.shape
    if ax == 0:
        return gathered.reshape(4 * r, c)
    return jnp.concatenate([gathered[j] for j in range(4)], axis=1)


def _split_pieces(g, ax):
    r, c = g.shape
    if ax == 0:
        return g.reshape(4, 2, r // 8, c).astype(BF16)
    rh, cs = r // 2, c // 4
    return jnp.stack([g[h * rh:(h + 1) * rh, j * cs:(j + 1) * cs].astype(BF16)
                      for j in range(4) for h in range(2)]).reshape(4, 2, rh, cs)


def _pad_cols(a, n):
    return jnp.pad(a, ((0, 0), (0, n - a.shape[1])))


def _to_kernel_layout(full, QR, KR):
    out = {}
    for n in ("mem_w_kv", "w_branch", "w_o", "w_ffn_down"):
        if n in full:
            out[n] = full[n]
    if "w_in" in full:
        w_in = full["w_in"]
        a = 5 * HG_HEADS * HG_D + QR + KR
        out["w_in"] = jnp.concatenate([w_in[:, :a], w_in[:, a + MLA_ROPE:], _pad_cols(w_in[:, a:a + MLA_ROPE], LANES)],
                                      axis=1)
    if "mla_w_uq" in full:
        uq = full["mla_w_uq"].reshape(QR, MLA_HEADS, MLA_NOPE + MLA_ROPE)
        out["mla_w_uq"] = jnp.pad(uq, ((0, 0), (0, 0), (0, 2 * LANES - MLA_NOPE - MLA_ROPE))).reshape(QR, -1)
    if "mla_w_ukv" in full:
        ukv = full["mla_w_ukv"].reshape(KR, MLA_HEADS, MLA_NOPE + MLA_V)
        out["mla_w_ukv"] = jnp.concatenate([ukv[:, :, :MLA_NOPE].reshape(KR, -1), ukv[:, :, MLA_NOPE:].reshape(KR, -1)],
                                           axis=1)
    if "w_ffn_gate" in full:
        gate, up = full["w_ffn_gate"], full["w_ffn_up"]
        DFF = gate.shape[1]
        TF = _tile(DFF, 512, LANES)
        blocks = []
        for j in range(DFF // TF):
            blocks += [gate[:, j * TF:(j + 1) * TF], up[:, j * TF:(j + 1) * TF]]
        out["w_ffn_gu"] = jnp.concatenate(blocks, axis=1)
    return out


def _from_kernel_layout(gk, QR, KR):
    out = {}
    for n in ("mem_w_kv", "w_o", "w_ffn_down"):
        if n in gk:
            out[n] = gk[n]
    if "w_branch" in gk:
        out["w_branch"] = gk["w_branch"].reshape(-1, gk["w_branch"].shape[-1])
    if "w_in" in gk:
        g = gk["w_in"]
        a = 5 * HG_HEADS * HG_D + QR + KR
        rest = g.shape[1] - LANES - a
        out["w_in"] = jnp.concatenate([g[:, :a], g[:, a + rest:a + rest + MLA_ROPE], g[:, a:a + rest]], axis=1)
    if "mla_w_uq" in gk:
        out["mla_w_uq"] = gk["mla_w_uq"].reshape(QR, MLA_HEADS, 2 * LANES)[:, :, :MLA_NOPE + MLA_ROPE].reshape(QR, -1)
    if "mla_w_ukv" in gk:
        VW = MLA_HEADS * MLA_V
        g = gk["mla_w_ukv"]
        out["mla_w_ukv"] = jnp.concatenate([g[:, :VW].reshape(KR, MLA_HEADS, MLA_NOPE),
                                            g[:, VW:].reshape(KR, MLA_HEADS, MLA_V)], axis=2).reshape(KR, -1)
    if "w_ffn_gu" in gk:
        g = gk["w_ffn_gu"]
        DFF = g.shape[1] // 2
        TF = _tile(DFF, 512, LANES)
        out["w_ffn_gate"] = jnp.concatenate([g[:, 2 * j * TF:(2 * j + 1) * TF] for j in range(DFF // TF)], axis=1)
        out["w_ffn_up"] = jnp.concatenate([g[:, (2 * j + 1) * TF:(2 * j + 2) * TF] for j in range(DFF // TF)], axis=1)
    return out


def _adamw(w, g, m, v, *, name):
    r, c = w.shape
    tr = max(SUBLANES, min(512, (1 << 20) // (4 * c)) // SUBLANES * SUBLANES)
    c1 = 1.0 / (1.0 - ADAM_B1 ** ADAM_STEP)
    c2 = 1.0 / (1.0 - ADAM_B2 ** ADAM_STEP)

    def fn(wv, gv, mv, vv):
        mn = ADAM_B1 * mv + (1.0 - ADAM_B1) * gv
        vn = ADAM_B2 * vv + (1.0 - ADAM_B2) * (gv * gv)
        delta = -ADAM_LR * ((mn * c1) / (jnp.sqrt(vn * c2) + ADAM_EPS) + ADAM_WD * wv)
        return delta, mn, vn

    return _rowwise(fn, [_Rows(a, c) for a in (w, g, m, v)], [], [_Rows(c, c)] * 3, [], R=r, tr=tr, name=name)


_SMALL = ("ln_emb_g", "ln_emb_b", "hgrn_lb_logits", "hgrn_norm_g", "mla_g_cq", "mla_g_ckv",
          "ln1_g", "ln1_b", "ln2_g", "ln2_b")


def _lb_from_logits(logits):
    return jnp.cumsum(jax.nn.softmax(logits, axis=1), axis=1)[:, 0]


def _small_rows(parts):
    flat = jnp.concatenate([p.reshape(-1) for p in parts])
    n = flat.shape[0]
    total = -(-n // (SUBLANES * LANES)) * SUBLANES * LANES
    return jnp.pad(flat, (0, total - n)).reshape(total // LANES, LANES)


def kernel(x, mem, positions, ln_emb_g, ln_emb_b, hgrn_lb_logits, w_in, hgrn_norm_g, mla_g_cq, mla_g_ckv, mla_w_uq, mla_w_ukv, mem_w_kv, w_branch, w_o, ln1_g, ln1_b, w_ffn_gate, w_ffn_up, w_ffn_down, ln2_g, ln2_b, loss_target, m_ln_emb_g, m_ln_emb_b, m_hgrn_lb_logits, m_w_in, m_hgrn_norm_g, m_mla_g_cq, m_mla_g_ckv, m_mla_w_uq, m_mla_w_ukv, m_mem_w_kv, m_w_branch, m_w_o, m_ln1_g, m_ln1_b, m_w_ffn_gate, m_w_ffn_up, m_w_ffn_down, m_ln2_g, m_ln2_b, v_ln_emb_g, v_ln_emb_b, v_hgrn_lb_logits, v_w_in, v_hgrn_norm_g, v_mla_g_cq, v_mla_g_ckv, v_mla_w_uq, v_mla_w_ukv, v_mem_w_kv, v_w_branch, v_w_o, v_ln1_g, v_ln1_b, v_w_ffn_gate, v_w_ffn_up, v_w_ffn_down, v_ln2_g, v_ln2_b):
    names = ["ln_emb_g", "ln_emb_b", "hgrn_lb_logits", "w_in", "hgrn_norm_g", "mla_g_cq", "mla_g_ckv", "mla_w_uq",
             "mla_w_ukv", "mem_w_kv", "w_branch", "w_o", "ln1_g", "ln1_b", "w_ffn_gate", "w_ffn_up", "w_ffn_down",
             "ln2_g", "ln2_b"]
    wts = dict(zip(names, [ln_emb_g, ln_emb_b, hgrn_lb_logits, w_in, hgrn_norm_g, mla_g_cq, mla_g_ckv, mla_w_uq,
                           mla_w_ukv, mem_w_kv, w_branch, w_o, ln1_g, ln1_b, w_ffn_gate, w_ffn_up, w_ffn_down,
                           ln2_g, ln2_b]))
    mom = dict(zip(names, [m_ln_emb_g, m_ln_emb_b, m_hgrn_lb_logits, m_w_in, m_hgrn_norm_g, m_mla_g_cq, m_mla_g_ckv,
                           m_mla_w_uq, m_mla_w_ukv, m_mem_w_kv, m_w_branch, m_w_o, m_ln1_g, m_ln1_b, m_w_ffn_gate,
                           m_w_ffn_up, m_w_ffn_down, m_ln2_g, m_ln2_b]))
    var = dict(zip(names, [v_ln_emb_g, v_ln_emb_b, v_hgrn_lb_logits, v_w_in, v_hgrn_norm_g, v_mla_g_cq, v_mla_g_ckv,
                           v_mla_w_uq, v_mla_w_ukv, v_mem_w_kv, v_w_branch, v_w_o, v_ln1_g, v_ln1_b, v_w_ffn_gate,
                           v_w_ffn_up, v_w_ffn_down, v_ln2_g, v_ln2_b]))
    xc, yc, cc = _coords()
    chip = _chip(xc, yc, cc)
    S, D = x.shape[1], x.shape[2]

    axis = dict(_BIG)
    shard = lambda n: wts[n].reshape(-1, wts[n].shape[-1]).astype(BF16)
    QR, KR = mla_w_uq.shape[1], mla_w_ukv.shape[1]
    MW, DFF = mem_w_kv.shape[2] // 2, 4 * w_ffn_gate.shape[2]
    (w_in_all,) = _exchange(_gather_exch([shard("w_in")]), name="gather_w_in")
    w_in_k = _to_kernel_layout(dict(w_in=_assemble(w_in_all, axis["w_in"])), QR, KR)["w_in"]
    class _Comm:
        ffn_names = ["w_ffn_gate", "w_ffn_up", "w_ffn_down"]
        mix_names = [n for n, _ in _BIG if n != "w_in" and not n.startswith("w_ffn")]
        gather_mix = _gather_exch([shard(n) for n in mix_names])
        gather_ffn = _gather_exch([shard(n) for n in ffn_names])

        @staticmethod
        def weights(names, received):
            full = {n: _assemble(g, axis[n]) for n, g in zip(names, received)}
            if "w_branch" in full:
                full["w_branch"] = full["w_branch"].reshape(N_BRANCH, -1, D)
            return _to_kernel_layout(full, QR, KR)

        @staticmethod
        def scatter(gk_part):
            gpart = _from_kernel_layout(gk_part, QR, KR)
            exch = _scatter_exch([_split_pieces(gpart[n], axis[n]) for n in gpart])
            exch.names = list(gpart)
            return exch

    lsh = hgrn_lb_logits.shape
    HW = 4 * lsh[2]
    placed = lax.dynamic_update_slice(jnp.zeros((lsh[0], lsh[1], HW), F32), hgrn_lb_logits, (0, 0, chip * lsh[2]))
    placed = jnp.where(cc == 0, placed, 0.0)
    logits = _allreduce_small(_small_rows([placed]), name="gather_logits").reshape(-1)[:placed.size].reshape(placed.shape)
    lb, lb_vjp = jax.vjp(_lb_from_logits, logits)

    small = dict(ln_emb_g=ln_emb_g, ln_emb_b=ln_emb_b, lb=lb, hgrn_norm_g=hgrn_norm_g, mla_g_cq=mla_g_cq,
                 mla_g_ckv=mla_g_ckv, ln1_g=ln1_g, ln1_b=ln1_b, ln2_g=ln2_g, ln2_b=ln2_b)
    loss_l, grad_x, _, gs, delivered = _local_step(x[0], mem[0], positions[0], loss_target[0], w_in_k, small, None,
                                                   MW, DFF, comm=_Comm)

    (dlogits,) = lb_vjp(gs["dlb"])
    sm_parts = [loss_l, gs["ln_emb_g"], gs["ln_emb_b"], dlogits, gs["hgrn_norm_g"], gs["mla_g_cq"], gs["mla_g_ckv"],
                gs["ln1_g"], gs["ln1_b"], gs["ln2_g"], gs["ln2_b"]]
    red = _allreduce_small(_small_rows(sm_parts), name="allreduce_small").reshape(-1)
    sm_out, off = [], 0
    for p in sm_parts:
        sm_out.append(red[off:off + p.size].reshape(p.shape))
        off += p.size
    loss = sm_out[0].reshape(())
    g_small = dict(zip(_SMALL, sm_out[1:]))
    g_small["hgrn_lb_logits"] = lax.dynamic_slice(g_small["hgrn_lb_logits"], (0, 0, chip * lsh[2]), lsh)
    for n in _SMALL:
        g_small[n] = g_small[n].reshape(wts[n].shape)

    g_big = {n: _sum_share(delivered[n], name="rs_sum_" + n).reshape(wts[n].shape) for n, _ in _BIG}

    grads = {**g_small, **g_big}
    delta, new_m, new_v = {}, {}, {}
    for n, _ in _BIG:
        shp = wts[n].shape
        two_d = lambda a: a.reshape(-1, shp[-1])
        d_, m_, v_ = _adamw(two_d(wts[n]), two_d(grads[n]), two_d(mom[n]), two_d(var[n]), name="adamw_" + n)
        delta[n], new_m[n], new_v[n] = d_.reshape(shp), m_.reshape(shp), v_.reshape(shp)
    sw, sg_, sm_, sv_ = (_small_rows([d[n] for n in _SMALL]) for d in (wts, grads, mom, var))
    d_, m_, v_ = _adamw(sw, sg_, sm_, sv_, name="adamw_small")
    for res, packed_rows in ((delta, d_), (new_m, m_), (new_v, v_)):
        flat, off = packed_rows.reshape(-1), 0
        for n in _SMALL:
            res[n] = flat[off:off + wts[n].size].reshape(wts[n].shape)
            off += wts[n].size

    return (loss, grad_x[None], *[grads[n] for n in names], *[delta[n] for n in names],
            *[new_m[n] for n in names], *[new_v[n] for n in names])
```

```python
import jax
import jax.numpy as jnp
from jax import lax
from jax.experimental import pallas as pl
from jax.experimental.pallas import tpu as pltpu

F32 = jnp.float32
BF16 = jnp.bfloat16

HG_HEADS = 8
HG_D = 128
MLA_HEADS = 8
MLA_NOPE = 128
MLA_ROPE = 64
MLA_V = 128
MEM_HEADS = 4
N_BRANCH = 3
ROPE_THETA = 10000.0
DEPTH = 1
ALPHA = (2.0 * DEPTH) ** 0.25
LN_EPS = 1e-5
RMS_EPS = 1e-6
ADAM_LR = 0.001
ADAM_B1 = 0.9
ADAM_B2 = 0.999
ADAM_EPS = 1e-08
ADAM_WD = 0.01
ADAM_STEP = 10

LANES = 128
SUBLANES = 8
VMEM_LIMIT = 48 * 1024 * 1024

HG_CHUNK = 128
HG_SUB = 16
HG_PAIR = 2

MESH = pl.DeviceIdType.MESH
HI = lax.Precision.HIGHEST
HG_OFF_PREC = lax.Precision.DEFAULT


def _cparams(sem=None):
    if sem is None:
        return pltpu.CompilerParams(vmem_limit_bytes=VMEM_LIMIT)
    return pltpu.CompilerParams(dimension_semantics=sem, vmem_limit_bytes=VMEM_LIMIT)


def _tile(dim, pref, quantum):
    t = min(pref, dim) // quantum * quantum
    while t >= quantum:
        if dim % t == 0:
            return t
        t -= quantum
    return dim


def _sigmoid(x):
    return 1.0 / (1.0 + jnp.exp(-x))


def _coords():
    return lax.axis_index("x"), lax.axis_index("y"), lax.axis_index("c")


def _chip(x, y, c):
    return 2 * x + y


def _device(x, y, c):
    return 4 * x + 2 * y + c


_CHIP_MASKS = ((1, 0, 0), (0, 1, 0), (1, 1, 0))
_ALL_MASKS = tuple((k >> 2 & 1, k >> 1 & 1, k & 1) for k in range(1, 8))
_HBM = pl.BlockSpec(memory_space=pl.ANY)


class _Exch:
    def __init__(self, srcs, n_dst, copies, local_copies):
        self.srcs, self.n_dst, self.copies, self.local_copies = list(srcs), n_dst, copies, local_copies
        self.n = len(self.srcs)

    def out_shape(self):
        return [jax.ShapeDtypeStruct((self.n_dst,) + s.shape[1:], s.dtype) for s in self.srcs]

    def scratch(self):
        n_rc, n_lc = self.n * len(self.copies), self.n * len(self.local_copies)
        return [pltpu.SemaphoreType.DMA((n_rc,)), pltpu.SemaphoreType.DMA((n_rc,)),
                pltpu.SemaphoreType.DMA((max(n_lc, 1),))]

    def _descriptors(self, src_refs, dst_refs, sems):
        send_sems, recv_sems, local_sems = sems
        x, y, c = _coords()
        n_rc, n_lc = len(self.copies), len(self.local_copies)
        remote, local = [], []
        for a in range(self.n):
            for k, (mask, sidx, didx) in enumerate(self.copies):
                remote.append(pltpu.make_async_remote_copy(
                    src_ref=src_refs[a].at[sidx(x, y, c)], dst_ref=dst_refs[a].at[didx(x, y, c)],
                    send_sem=send_sems.at[a * n_rc + k], recv_sem=recv_sems.at[a * n_rc + k],
                    device_id=(x ^ mask[0], y ^ mask[1], c ^ mask[2]), device_id_type=MESH))
            for k, (sidx, didx) in enumerate(self.local_copies):
                local.append(pltpu.make_async_copy(src_refs[a].at[sidx(x, y, c)], dst_refs[a].at[didx(x, y, c)],
                                                   local_sems.at[a * n_lc + k]))
        return remote, local

    def start(self, src_refs, dst_refs, sems):
        remote, local = self._descriptors(src_refs, dst_refs, sems)
        for cp in remote + local:
            cp.start()

    def wait(self, src_refs, dst_refs, sems):
        remote, local = self._descriptors(src_refs, dst_refs, sems)
        for cp in remote:
            cp.wait_recv()
        for cp in remote:
            cp.wait_send()
        for cp in local:
            cp.wait()


def _exchange(exch, *, name):
    n = exch.n

    def body(*refs):
        src_refs, dst_refs, sems = refs[:n], refs[n:2 * n], refs[2 * n:]
        exch.start(src_refs, dst_refs, sems)
        exch.wait(src_refs, dst_refs, sems)

    return pl.pallas_call(
        body, name=name, in_specs=[_HBM] * n, out_specs=[_HBM] * n, out_shape=exch.out_shape(),
        scratch_shapes=exch.scratch(), compiler_params=pltpu.CompilerParams(has_side_effects=True),
    )(*exch.srcs)


def _carried(call, exch, grid, in_specs, out_specs, out_shape, scratch_shapes, args, *, name):
    n_in, n_out, n_scr = len(in_specs), len(out_specs), len(scratch_shapes)
    n = 0 if exch is None else exch.n

    def body(*refs):
        o0 = n_in + n
        s0 = o0 + n_out + n
        ins, srcs = refs[:n_in], refs[n_in:o0]
        outs, dsts = refs[o0:o0 + n_out], refs[o0 + n_out:s0]
        scr, sems = refs[s0:s0 + n_scr], refs[s0 + n_scr:]
        if exch is not None:
            ids = [pl.program_id(d) for d in range(len(grid))]
            first = _all([i == 0 for i in ids])
            last = _all([i == g - 1 for i, g in zip(ids, grid)])

            @pl.when(first)
            def _():
                exch.start(srcs, dsts, sems)

        call(*ins, *outs, *scr)
        if exch is not None:
            @pl.when(last)
            def _():
                exch.wait(srcs, dsts, sems)

    if exch is None:
        params = pltpu.CompilerParams(dimension_semantics=("arbitrary",) * len(grid), vmem_limit_bytes=VMEM_LIMIT)
        extra_in, extra_out, extra_shape, extra_scr, extra_args = [], [], [], [], []
    else:
        params = pltpu.CompilerParams(dimension_semantics=("arbitrary",) * len(grid), vmem_limit_bytes=VMEM_LIMIT,
                                      has_side_effects=True)
        extra_in, extra_out, extra_shape = [_HBM] * n, [_HBM] * n, exch.out_shape()
        extra_scr, extra_args = exch.scratch(), exch.srcs
    res = pl.pallas_call(
        body, name=name, grid=grid, in_specs=list(in_specs) + extra_in, out_specs=list(out_specs) + extra_out,
        out_shape=list(out_shape) + extra_shape, scratch_shapes=list(scratch_shapes) + extra_scr,
        compiler_params=params,
    )(*args, *extra_args)
    return res[:n_out], res[n_out:]


def _all(conds):
    out = conds[0]
    for c in conds[1:]:
        out = jnp.logical_and(out, c)
    return out


def _mm(a, b, M, N, K, *, ta=False, tb=False, a_off=(0, 0), b_off=(0, 0), add=None, exch=None, epilogue=None,
        tm=1024, tn=1024, tk=1024, name):
    tm = _tile(M, tm, LANES if ta else SUBLANES)
    tn = _tile(N, tn, LANES)
    tk = _tile(K, tk, LANES)
    nk = K // tk
    ar, ac = a_off
    br, bc = b_off

    if ta:
        assert ar % tk == 0 and ac % tm == 0
        a_spec = pl.BlockSpec((tk, tm), lambda i, j, k: (ar // tk + k, ac // tm + i))
    else:
        assert ar % tm == 0 and ac % tk == 0
        a_spec = pl.BlockSpec((tm, tk), lambda i, j, k: (ar // tm + i, ac // tk + k))
    if tb:
        assert br % tn == 0 and bc % tk == 0
        b_spec = pl.BlockSpec((tn, tk), lambda i, j, k: (br // tn + j, bc // tk + k))
    else:
        assert br % tk == 0 and bc % tn == 0
        b_spec = pl.BlockSpec((tk, tn), lambda i, j, k: (br // tk + k, bc // tn + j))
    o_spec = pl.BlockSpec((tm, tn), lambda i, j, k: (i, j))
    mixed = a.dtype != b.dtype

    epi_fn, epi_ins, epi_outs = (None, [], [1]) if epilogue is None else epilogue
    n_in = 2 + (add is not None) + len(epi_ins)

    def body(*refs):
        a_ref, b_ref = refs[:2]
        add_ref = refs[2] if add is not None else None
        epi_refs = refs[n_in - len(epi_ins):n_in]
        o_refs, acc = refs[n_in:n_in + len(epi_outs)], refs[-1]
        k = pl.program_id(2)
        av = a_ref[...]
        bv = b_ref[...]
        if ta:
            av = av.astype(F32).T
        if mixed:
            av = av.astype(BF16)
            bv = bv.astype(BF16)
        dims = (((1,), (1 if tb else 0,)), ((), ()))
        d = lax.dot_general(av, bv, dims, preferred_element_type=F32)

        def finish(total):
            if add is not None:
                total = total + add_ref[...]
            tiles = [total] if epi_fn is None else epi_fn(total, *[r[...] for r in epi_refs])
            for o_ref, t in zip(o_refs, tiles):
                o_ref[...] = t

        if nk == 1:
            finish(d)
        else:
            @pl.when(k == 0)
            def _():
                acc[...] = d

            @pl.when(jnp.logical_and(k > 0, k < nk - 1))
            def _():
                acc[...] += d

            @pl.when(k == nk - 1)
            def _():
                finish(acc[...] + d)

    in_specs = [a_spec, b_spec]
    args = [a, b]
    if add is not None:
        in_specs.append(o_spec)
        args.append(add)
    wide = lambda w: pl.BlockSpec((tm, tn * w[0] // w[1]), lambda i, j, k: (i, j))
    for arr, w in epi_ins:
        in_specs.append(wide(w))
        args.append(arr)
    out_specs = [o_spec if w == 1 else wide(w) for w in epi_outs]
    out_shape = [jax.ShapeDtypeStruct((M, N if w == 1 else N * w[0] // w[1]), F32) for w in epi_outs]
    outs, received = _carried(body, exch, (M // tm, N // tn, nk), in_specs, out_specs, out_shape,
                              [pltpu.VMEM((tm, tn), F32)], args, name=name)
    outs = outs[0] if epilogue is None else outs
    return outs if exch is None else (outs, received)


class _Rows:
    def __init__(self, arr, width, col0=0, lead=None, dtype=F32):
        self.arr, self.width, self.col0, self.lead, self.dtype = arr, width, col0, lead, dtype


def _rowwise(fn, rows, consts, outs, accs, *, R, tr, ncol=1, name):
    tr = _tile(R, tr, SUBLANES)
    nrow = R // tr

    def spec(r):
        if r.lead is None:
            return pl.BlockSpec((tr, r.width), lambda j, i, c0=r.col0: (i, c0 + j))
        return pl.BlockSpec((None, tr, r.width), lambda j, i, c0=r.col0, l=r.lead: (l, i, c0 + j))

    in_specs = [spec(r) for r in rows]
    for c in consts:
        in_specs.append(pl.BlockSpec(c.shape, lambda j, i, nd=c.ndim: (0,) * nd))
    out_specs = [spec(o) for o in outs]
    out_shape = [jax.ShapeDtypeStruct((R, o.arr), o.dtype) for o in outs]
    for w in accs:
        out_specs.append(pl.BlockSpec((1, w), lambda j, i: (0, j)))
        out_shape.append(jax.ShapeDtypeStruct((1, w * ncol), F32))
    n_in = len(rows) + len(consts)
    n_out = len(outs)

    def body(*refs):
        ins = [r[...] for r in refs[:n_in]]
        res = fn(*ins)
        if not isinstance(res, (tuple, list)):
            res = (res,)
        for k in range(n_out):
            refs[n_in + k][...] = res[k].astype(refs[n_in + k].dtype)
        i = pl.program_id(1)
        for k in range(len(accs)):
            a_ref = refs[n_in + n_out + k]

            @pl.when(i == 0)
            def _(a_ref=a_ref):
                a_ref[...] = jnp.zeros_like(a_ref)

            a_ref[...] += res[n_out + k]

    res = pl.pallas_call(
        body, name=name, grid=(ncol, nrow),
        in_specs=in_specs, out_specs=out_specs, out_shape=out_shape,
        compiler_params=_cparams(("parallel", "arbitrary")),
    )(*[r.arr for r in rows], *consts)
    return res


def _colsum(x):
    return jnp.sum(x, axis=0, keepdims=True)


def _ln_stats(z):
    mu = jnp.mean(z, axis=-1, keepdims=True)
    zc = z - mu
    var = jnp.mean(zc * zc, axis=-1, keepdims=True)
    rstd = lax.rsqrt(var + LN_EPS)
    return zc * rstd, rstd


def _ln_bwd_core(z, g, dy):
    xhat, rstd = _ln_stats(z)
    dxh = dy * g
    m1 = jnp.mean(dxh, axis=-1, keepdims=True)
    m2 = jnp.mean(dxh * xhat, axis=-1, keepdims=True)
    dz = rstd * (dxh - m1 - xhat * m2)
    return dz, _colsum(dy * xhat), _colsum(dy)


def _rms_fwd(x, g, eps):
    r = lax.rsqrt(jnp.mean(x * x, axis=-1, keepdims=True) + eps)
    return x * r * g


def _rms_bwd(x, g, dy, eps):
    r = lax.rsqrt(jnp.mean(x * x, axis=-1, keepdims=True) + eps)
    xr = x * r
    dyg = dy * g
    dx = r * (dyg - xr * jnp.mean(dyg * xr, axis=-1, keepdims=True))
    return dx, dy * xr


def _hg_gate(fr, lb):
    sig = _sigmoid(fr)
    f = lb + (1.0 - lb) * sig
    return sig, f


def _hg_masks(rev):
    C = HG_CHUNK
    t = lax.broadcasted_iota(jnp.int32, (C, C), 0)
    s = lax.broadcasted_iota(jnp.int32, (C, C), 1)
    tri = (s >= t) if rev else (s <= t)
    return tri


def _hg_offdiag(Q, K, b, i, rev):
    C, sb = HG_CHUNK, HG_SUB
    nb = C // sb
    if (not rev and i == 0) or (rev and i == nb - 1):
        return None
    ref = b[sb * i - 1:sb * i] if not rev else b[sb * (i + 1):sb * (i + 1) + 1]
    srow = lax.broadcasted_iota(jnp.int32, (C, 1), 0)
    smask = (srow < sb * i) if not rev else (srow >= sb * (i + 1))
    qscale = jnp.exp(jnp.minimum(b - ref, 0.0))
    kscale = jnp.where(smask, jnp.exp(jnp.minimum(ref - b, 0.0)), 0.0)
    return qscale, kscale


def _hg_att(Q, K, b, rev):
    C, sb = HG_CHUNK, HG_SUB
    lane = lax.broadcasted_iota(jnp.int32, (sb, C), 1)
    rloc = lax.broadcasted_iota(jnp.int32, (sb, 1), 0)
    rows = []
    for i in range(C // sb):
        sl = slice(sb * i, sb * i + sb)
        Qi, Ki, bi = Q[sl], K[sl], b[sl]
        od = _hg_offdiag(Q, K, b, i, rev)
        if od is None:
            acc = jnp.zeros((sb, C), F32)
        else:
            qs, ks = od
            acc = lax.dot_general(Qi * qs[sl], K * ks, (((1,), (1,)), ((), ())),
                                  precision=HG_OFF_PREC, preferred_element_type=F32)
        for j in range(sb):
            e = jnp.exp(jnp.minimum(bi - bi[j:j + 1], 0.0))
            col = jnp.sum(Qi * Ki[j:j + 1] * e, axis=-1, keepdims=True)
            vis = (rloc <= j) if rev else (rloc >= j)
            acc = jnp.where(lane == sb * i + j, jnp.where(vis, col, 0.0), acc)
        rows.append(acc)
    return jnp.concatenate(rows, axis=0)


def _hg_att_bwd(Q, K, b, dA, rev):
    C, sb = HG_CHUNK, HG_SUB
    rloc = lax.broadcasted_iota(jnp.int32, (sb, 1), 0)
    rrow = lax.broadcasted_iota(jnp.int32, (sb, HG_D), 0)
    trow = lax.broadcasted_iota(jnp.int32, (C, C), 1) // sb
    dAT = dA.T
    dQ_rows, dKd_rows = [], []
    dK = jnp.zeros((C, HG_D), F32)
    for i in range(C // sb):
        sl = slice(sb * i, sb * i + sb)
        Qi, Ki, bi, dAi = Q[sl], K[sl], b[sl], dA[sl]
        od = _hg_offdiag(Q, K, b, i, rev)
        if od is None:
            dQi = jnp.zeros((sb, HG_D), F32)
        else:
            qs, ks = od
            dQi = lax.dot_general(dAi, K * ks, (((1,), (0,)), ((), ())),
                                  precision=HI, preferred_element_type=F32) * qs[sl]
            zt = jnp.where(trow == i, dAT, 0.0)
            dK = dK + lax.dot_general(zt, Q * qs, (((1,), (0,)), ((), ())),
                                      precision=HI, preferred_element_type=F32) * ks
        dKd = jnp.zeros((sb, HG_D), F32)
        for j in range(sb):
            vis = (rloc <= j) if rev else (rloc >= j)
            e = jnp.where(vis, jnp.exp(jnp.minimum(bi - bi[j:j + 1], 0.0)), 0.0)
            dcol = dAi[:, sb * i + j:sb * i + j + 1]
            dQi = dQi + dcol * Ki[j:j + 1] * e
            krow = jnp.sum(dcol * Qi * e, axis=0, keepdims=True)
            dKd = jnp.where(rrow == j, krow, dKd)
        dQ_rows.append(dQi)
        dKd_rows.append(dKd)
    return jnp.concatenate(dQ_rows, axis=0), dK + jnp.concatenate(dKd_rows, axis=0)


def _hg_prep(qr, fr, lb, tri):
    sigq = _sigmoid(qr)
    Q = qr * sigq
    sig, f = _hg_gate(fr, lb)
    K = 1.0 - f
    logf = jnp.log(f)
    b = lax.dot_general(tri.astype(F32), logf, (((1,), (0,)), ((), ())),
                        precision=HI, preferred_element_type=F32)
    return sigq, Q, sig, f, K, b


def _hgrn_scan(P, lb, *, S, rev, name):
    C, H, D_ = HG_CHUNK, HG_HEADS, HG_D
    HP = HG_PAIR if H % HG_PAIR == 0 else 1
    NC = S // C
    fcol = (3 if rev else 2) * H

    def cidx(n):
        return NC - 1 - n if rev else n

    def body(q_ref, v_ref, f_ref, lb_ref, o_ref, st_ref, a_ref, state):
        n = pl.program_id(1)

        @pl.when(n == 0)
        def _():
            state[...] = jnp.zeros_like(state)

        tri = _hg_masks(rev)
        qa, va, fa, lba, sta = q_ref[...], v_ref[...], f_ref[...], lb_ref[...], state[...]
        st_ref[...] = sta
        outs, amats, states = [], [], []
        for hp in range(HP):
            sl = slice(hp * D_, (hp + 1) * D_)
            _, Q, _, _, K, b = _hg_prep(qa[:, sl], fa[:, sl], lba[:, sl], tri)
            V, ST0 = va[:, sl], sta[hp]
            e_b = jnp.exp(b)
            bE = b[0:1] if rev else b[C - 1:C]
            W = jnp.exp(bE - b)
            inter = lax.dot_general(Q * e_b, ST0, (((1,), (1,)), ((), ())), preferred_element_type=F32)
            A = _hg_att(Q, K, b, rev)
            amats.append(A)
            outs.append(inter + jnp.dot(A, V, preferred_element_type=F32))
            states.append(ST0 * jnp.exp(bE) + lax.dot_general(
                V, K * W, (((0,), (0,)), ((), ())), preferred_element_type=F32))
        a_ref[...] = jnp.stack(amats)
        o_ref[...] = jnp.concatenate(outs, axis=1)
        state[...] = jnp.stack(states)

    blk = lambda c0: pl.BlockSpec((C, HP * D_), lambda h, n, c0=c0: (cidx(n), c0 // HP + h))
    return pl.pallas_call(
        body, name=name, grid=(H // HP, NC),
        in_specs=[blk(0), blk(H), blk(fcol), pl.BlockSpec((1, HP * D_), lambda h, n: (0, h))],
        out_specs=[pl.BlockSpec((C, HP * D_), lambda h, n: (cidx(n), h)),
                   pl.BlockSpec((None, HP, D_, D_), lambda h, n: (cidx(n), h, 0, 0)),
                   pl.BlockSpec((None, HP, C, C), lambda h, n: (cidx(n), h, 0, 0))],
        out_shape=[jax.ShapeDtypeStruct((S, H * D_), F32),
                   jax.ShapeDtypeStruct((NC, H, D_, D_), F32),
                   jax.ShapeDtypeStruct((NC, H, C, C), F32)],
        scratch_shapes=[pltpu.VMEM((HP, D_, D_), F32)],
        compiler_params=_cparams(("parallel", "arbitrary")),
    )(P, P, P, lb)


def _hgrn_scan_bwd(P, lb, st, amat, do, *, S, rev, name):
    C, H, D_ = HG_CHUNK, HG_HEADS, HG_D
    HP = HG_PAIR if H % HG_PAIR == 0 else 1
    NC = S // C
    fcol = (3 if rev else 2) * H

    def cidx(n):
        return n if rev else NC - 1 - n

    def body(q_ref, v_ref, f_ref, lb_ref, st_ref, a_ref, do_ref, dq_ref, dv_ref, df_ref, dlb_ref, dstate):
        n = pl.program_id(1)

        @pl.when(n == 0)
        def _():
            dstate[...] = jnp.zeros_like(dstate)
            dlb_ref[...] = jnp.zeros_like(dlb_ref)

        tri = _hg_masks(rev)
        tri_t = _hg_masks(not rev).astype(F32)
        qa, va, fa, lba, doa = q_ref[...], v_ref[...], f_ref[...], lb_ref[...], do_ref[...]
        sta, ama, dsta = st_ref[...], a_ref[...], dstate[...]
        trow = lax.broadcasted_iota(jnp.int32, (C, 1), 0)
        dqs, dvs, dfs, dlbs, dstates = [], [], [], [], []
        for hp in range(HP):
            sl = slice(hp * D_, (hp + 1) * D_)
            lbv, qr = lba[:, sl], qa[:, sl]
            sigq, Q, sig, f, K, b = _hg_prep(qr, fa[:, sl], lbv, tri)
            V, ST0, A, dO, dST1 = va[:, sl], sta[hp], ama[hp], doa[:, sl], dsta[hp]
            e_b = jnp.exp(b)
            bE = b[0:1] if rev else b[C - 1:C]
            eE = jnp.exp(bE)
            W = jnp.exp(bE - b)
            Qe = Q * e_b
            KW = K * W
            dA = jnp.where(tri, lax.dot_general(dO, V, (((1,), (1,)), ((), ())), preferred_element_type=F32), 0.0)
            dV = (lax.dot_general(A, dO, (((0,), (0,)), ((), ())), preferred_element_type=F32)
                  + lax.dot_general(KW, dST1, (((1,), (1,)), ((), ())), preferred_element_type=F32))
            dQe = jnp.dot(dO, ST0, preferred_element_type=F32)
            dKW = jnp.dot(V, dST1, preferred_element_type=F32)
            dstates.append(dST1 * eE + lax.dot_general(dO, Qe, (((0,), (0,)), ((), ())), preferred_element_type=F32))
            dQa, dKa = _hg_att_bwd(Q, K, b, dA, rev)
            dQ = dQe * e_b + dQa
            dK = dKW * W + dKa
            extra = _colsum(KW * dKW) + eE * _colsum(ST0 * dST1)
            db = Q * dQ - K * dK + jnp.where(trow == (0 if rev else C - 1), extra, 0.0)
            dlogf = lax.dot_general(tri_t, db, (((1,), (0,)), ((), ())), precision=HI, preferred_element_type=F32)
            dfv = dlogf / f - dK
            dfs.append(dfv * (1.0 - lbv) * sig * (1.0 - sig))
            dlbs.append(_colsum(dfv * (1.0 - sig)))
            dqs.append(dQ * (sigq * (1.0 + qr * (1.0 - sigq))))
            dvs.append(dV)
        dstate[...] = jnp.stack(dstates)
        df_ref[...] = jnp.concatenate(dfs, axis=1)
        dlb_ref[...] += jnp.concatenate(dlbs, axis=1)
        dq_ref[...] = jnp.concatenate(dqs, axis=1)
        dv_ref[...] = jnp.concatenate(dvs, axis=1)

    blk = lambda c0: pl.BlockSpec((C, HP * D_), lambda h, n, c0=c0: (cidx(n), c0 // HP + h))
    oblk = pl.BlockSpec((C, HP * D_), lambda h, n: (cidx(n), h))
    return pl.pallas_call(
        body, name=name, grid=(H // HP, NC),
        in_specs=[blk(0), blk(H), blk(fcol), pl.BlockSpec((1, HP * D_), lambda h, n: (0, h)),
                  pl.BlockSpec((None, HP, D_, D_), lambda h, n: (cidx(n), h, 0, 0)),
                  pl.BlockSpec((None, HP, C, C), lambda h, n: (cidx(n), h, 0, 0)),
                  oblk],
        out_specs=[oblk, oblk, oblk, pl.BlockSpec((1, HP * D_), lambda h, n: (0, h))],
        out_shape=[jax.ShapeDtypeStruct((S, H * D_), F32)] * 3 + [jax.ShapeDtypeStruct((1, H * D_), F32)],
        scratch_shapes=[pltpu.VMEM((HP, D_, D_), F32)],
        compiler_params=_cparams(("parallel", "arbitrary")),
    )(P, P, P, lb, st, amat, do)


LOG2E = 1.4426950408889634
MXU = BF16
ATT_SUB = 512


def _mx(x):
    return x if x.dtype == MXU else x.astype(MXU)


def _attn_fwd(q, k, v, *, S, T, H, dqk, dv, q_col0, k_col0, v_col0, scale, tq, tk, exch=None, name):
    tq = _tile(S, tq, SUBLANES)
    tk = _tile(T, tk, LANES)
    nk = T // tk
    ts = _tile(tq, ATT_SUB, SUBLANES)

    def body(q_ref, k_ref, v_ref, o_ref, lse_ref, m_s, l_s, acc):
        j = pl.program_id(2)

        @pl.when(j == 0)
        def _():
            m_s[...] = jnp.full_like(m_s, -jnp.inf)
            l_s[...] = jnp.zeros_like(l_s)
            acc[...] = jnp.zeros_like(acc)

        kv, vv = _mx(k_ref[...]), _mx(v_ref[...])
        m_all, l_all, a_all = m_s[...], l_s[...], acc[...]
        ms, ls, accs = [], [], []
        for r0 in range(0, tq, ts):
            rows = slice(r0, r0 + ts)
            s = lax.dot_general(_mx(q_ref[rows, :]), kv, (((1,), (1,)), ((), ())),
                                preferred_element_type=F32) * (scale * LOG2E)
            m_old = m_all[rows]
            m_new = jnp.maximum(m_old, jnp.max(s, axis=1)[:, None])
            corr = jnp.exp2(m_old - m_new)
            p = jnp.exp2(s - jnp.tile(m_new, (1, tk // LANES)))
            ms.append(m_new)
            ls.append(corr * l_all[rows] + jnp.sum(p, axis=1)[:, None])
            accs.append(jnp.tile(corr, (1, dv // LANES)) * a_all[rows] + jnp.dot(_mx(p), vv, preferred_element_type=F32))
        m_s[...] = jnp.concatenate(ms, axis=0)
        l_s[...] = jnp.concatenate(ls, axis=0)
        acc[...] = jnp.concatenate(accs, axis=0)

        @pl.when(j == nk - 1)
        def _():
            o_ref[...] = acc[...] / jnp.tile(l_s[...], (1, dv // LANES))
            lse_ref[...] = ((m_s[...] + jnp.log2(l_s[...])) * (1.0 / LOG2E))[:, :1]

    outs, received = _carried(
        body, exch, (H, S // tq, nk),
        [pl.BlockSpec((tq, dqk), lambda h, i, j: (i, q_col0 + h)),
         pl.BlockSpec((tk, dqk), lambda h, i, j: (j, k_col0 + h)),
         pl.BlockSpec((tk, dv), lambda h, i, j: (j, v_col0 + h))],
        [pl.BlockSpec((tq, dv), lambda h, i, j: (i, h)),
         pl.BlockSpec((None, tq, 1), lambda h, i, j: (h, i, 0))],
        [jax.ShapeDtypeStruct((S, H * dv), F32), jax.ShapeDtypeStruct((H, S, 1), F32)],
        [pltpu.VMEM((tq, LANES), F32), pltpu.VMEM((tq, LANES), F32), pltpu.VMEM((tq, dv), F32)],
        [q, k, v], name=name)
    return outs if exch is None else (outs, received)


def _attn_bwd(q, k, v, o, lse, do, *, S, T, H, dqk, dv, q_col0, k_col0, v_col0, scale, tq, tk, exch=None, name):
    tq = _tile(S, tq, SUBLANES)
    tk = _tile(T, tk, LANES)
    nq = S // tq
    ts = _tile(tq, ATT_SUB, SUBLANES)

    def body(q_ref, k_ref, v_ref, o_ref, lse_ref, do_ref, dq_ref, dk_ref, dv_ref, dk_acc, dv_acc):
        j = pl.program_id(1)
        i = pl.program_id(2)

        @pl.when(jnp.logical_and(i == 0, j == 0))
        def _():
            dq_ref[...] = jnp.zeros_like(dq_ref)

        @pl.when(i == 0)
        def _():
            dk_acc[...] = jnp.zeros_like(dk_acc)
            dv_acc[...] = jnp.zeros_like(dv_acc)

        kv, vv = _mx(k_ref[...]), _mx(v_ref[...])
        dk_new, dv_new = dk_acc[...], dv_acc[...]
        lse2 = lse_ref[...] * LOG2E
        dqs = []
        for r0 in range(0, tq, ts):
            rows = slice(r0, r0 + ts)
            qv, dov = _mx(q_ref[rows, :]), do_ref[rows, :]
            s = lax.dot_general(qv, kv, (((1,), (1,)), ((), ())), preferred_element_type=F32) * (scale * LOG2E)
            p = jnp.exp2(s - lse2[rows])
            delta = jnp.sum(dov * o_ref[rows, :], axis=-1, keepdims=True)
            dob = _mx(dov)
            dp = lax.dot_general(dob, vv, (((1,), (1,)), ((), ())), preferred_element_type=F32)
            ds = _mx(p * (dp - delta) * scale)
            dv_new = dv_new + lax.dot_general(_mx(p), dob, (((0,), (0,)), ((), ())), preferred_element_type=F32)
            dk_new = dk_new + lax.dot_general(ds, qv, (((0,), (0,)), ((), ())), preferred_element_type=F32)
            dqs.append(jnp.dot(ds, kv, preferred_element_type=F32))
        dq_ref[pl.ds(pl.multiple_of(i * tq, tq), tq), :] += jnp.concatenate(dqs, axis=0)
        dk_acc[...] = dk_new
        dv_acc[...] = dv_new

        @pl.when(i == nq - 1)
        def _():
            dk_ref[...] = dk_new
            dv_ref[...] = dv_new

    outs, received = _carried(
        body, exch, (H, T // tk, nq),
        [pl.BlockSpec((tq, dqk), lambda h, j, i: (i, q_col0 + h)),
         pl.BlockSpec((tk, dqk), lambda h, j, i: (j, k_col0 + h)),
         pl.BlockSpec((tk, dv), lambda h, j, i: (j, v_col0 + h)),
         pl.BlockSpec((tq, dv), lambda h, j, i: (i, h)),
         pl.BlockSpec((None, tq, 1), lambda h, j, i: (h, i, 0)),
         pl.BlockSpec((tq, dv), lambda h, j, i: (i, h))],
        [pl.BlockSpec((S, dqk), lambda h, j, i: (0, h)),
         pl.BlockSpec((tk, dqk), lambda h, j, i: (j, h)),
         pl.BlockSpec((tk, dv), lambda h, j, i: (j, h))],
        [jax.ShapeDtypeStruct((S, H * dqk), F32), jax.ShapeDtypeStruct((T, H * dqk), F32),
         jax.ShapeDtypeStruct((T, H * dv), F32)],
        [pltpu.VMEM((tk, dqk), F32), pltpu.VMEM((tk, dv), F32)], [q, k, v, o, lse, do], name=name)
    return outs if exch is None else (outs, received)


def _rope_tables(positions):
    half = MLA_ROPE // 2
    inv_freq = jnp.power(ROPE_THETA, -jnp.arange(half, dtype=F32) / half)
    ang = positions.astype(F32)[:, None] * inv_freq
    cos, sin = jnp.cos(ang), jnp.sin(ang)
    z = jnp.zeros_like(cos)
    tc = jnp.concatenate([cos, cos, z, z], axis=1)
    ta = jnp.concatenate([-sin, z, z, z], axis=1)
    tb = jnp.concatenate([z, sin, z, z], axis=1)
    return tc, ta, tb


def _rope_apply(v, tc, ta, tb):
    half = MLA_ROPE // 2
    return v * tc + pltpu.roll(v, LANES - half, 1) * ta + pltpu.roll(v, half, 1) * tb


def _rope_apply_t(d, tc, ta, tb):
    half = MLA_ROPE // 2
    return d * tc + pltpu.roll(d * ta, half, 1) + pltpu.roll(d * tb, LANES - half, 1)


def _local_step(x, mem, positions, loss_target, w_in_k, small, W, MW, DFF, comm=None):
    S, D = x.shape
    M = mem.shape[0]
    HW = HG_HEADS * HG_D
    QR = small["mla_g_cq"].shape[1]
    KR = small["mla_g_ckv"].shape[1]
    MHD = MW // MEM_HEADS
    QW = MLA_HEADS * 2 * LANES
    VW = MLA_HEADS * MLA_V
    c_hg, c_cq, c_ckv, c_qm, c_gate = 0, 5 * HW, 5 * HW + QR, 5 * HW + QR + KR, 5 * HW + QR + KR + MW
    c_kr = c_gate + N_BRANCH * D
    PW = c_kr + LANES
    assert comm is not None or w_in_k.shape == (D, PW)
    TR = 256
    row = lambda a: a.reshape(1, -1)
    ge, be = row(small["ln_emb_g"]), row(small["ln_emb_b"])
    g1, b1, g2, b2 = small["ln1_g"], small["ln1_b"], small["ln2_g"], small["ln2_b"]
    lb = small["lb"]
    tc, ta, tb = _rope_tables(positions)

    (h0,) = _rowwise(lambda z, g, b: _ln_stats(z)[0] * g + b, [_Rows(x, D)], [ge, be],
                     [_Rows(D, D)], [], R=S, tr=TR, name="ln_emb")
    if comm is None:
        P = _mm(h0, w_in_k, S, PW, D, tn=896, name="proj_in")
    else:
        half = D // 2
        P_top, got = _mm(h0, comm.w_in_top, S, PW, half, tn=896, exch=comm.gather_bottom, name="proj_in_top")
        w_in_bottom = comm.w_in_rows(got)
        P, got = _mm(h0, w_in_bottom, S, PW, half, a_off=(0, half), add=P_top, tn=896, exch=comm.gather_mix,
                     name="proj_in")
        W = comm.weights(comm.mix_names, got)
        w_in_k = jnp.concatenate([comm.w_in_top, w_in_bottom], axis=0)

    o_fw, st_fw, a_fw = _hgrn_scan(P, lb[0:1], S=S, rev=False, name="hgrn_fw")
    o_bw, st_bw, a_bw = _hgrn_scan(P, lb[1:2], S=S, rev=True, name="hgrn_bw")

    def hg_post(of, ob, gr, ng):
        o = of + ob
        sg = _sigmoid(gr)
        outs = []
        for h in range(HG_HEADS):
            sl = slice(h * HG_D, (h + 1) * HG_D)
            outs.append(_rms_fwd(o[:, sl], ng, RMS_EPS) * sg[:, sl])
        return jnp.concatenate(outs, axis=1)

    (y_hg,) = _rowwise(hg_post, [_Rows(o_fw, HW), _Rows(o_bw, HW), _Rows(P, HW, 4)], [small["hgrn_norm_g"]],
                       [_Rows(HW, HW)], [], R=S, tr=TR, name="hgrn_post")

    def mla_norm(cq, ckv, gq, gk):
        return _rms_fwd(cq, gq, RMS_EPS), _rms_fwd(ckv, gk, RMS_EPS)

    assert c_cq % QR == 0 and c_ckv % KR == 0
    cqn, ckvn = _rowwise(mla_norm, [_Rows(P, QR, c_cq // QR), _Rows(P, KR, c_ckv // KR)],
                         [small["mla_g_cq"], small["mla_g_ckv"]],
                         [_Rows(QR, QR), _Rows(KR, KR)], [], R=S, tr=TR, name="mla_norm")
    q_raw = _mm(cqn, W["mla_w_uq"], S, QW, QR, name="mla_uq")
    kv = _mm(ckvn, W["mla_w_ukv"], S, 2 * VW, KR, name="mla_ukv")

    def rope_fwd(qb, knb, vb_, krb, tcb, tab, tbb):
        kr = _rope_apply(krb, tcb, tab, tbb)
        qo, ko = [], []
        for h in range(MLA_HEADS):
            qo += [qb[:, 2 * h * LANES:(2 * h + 1) * LANES],
                   _rope_apply(qb[:, (2 * h + 1) * LANES:(2 * h + 2) * LANES], tcb, tab, tbb)]
            ko += [knb[:, h * LANES:(h + 1) * LANES], kr]
        return jnp.concatenate(qo, axis=1), jnp.concatenate(ko, axis=1), vb_

    qc, kc, vc = _rowwise(rope_fwd, [_Rows(q_raw, QW), _Rows(kv, VW), _Rows(kv, VW, 1), _Rows(P, LANES, c_kr // LANES),
                                     _Rows(tc, LANES), _Rows(ta, LANES), _Rows(tb, LANES)], [],
                          [_Rows(QW, QW, dtype=MXU), _Rows(QW, QW, dtype=MXU), _Rows(VW, VW, dtype=MXU)], [],
                          R=S, tr=TR, name="rope_fwd")
    mla_kw = dict(S=S, T=S, H=MLA_HEADS, dqk=2 * LANES, dv=MLA_V, q_col0=0, k_col0=0, v_col0=0,
                  scale=(MLA_NOPE + MLA_ROPE) ** -0.5, tq=2048, tk=2048)
    if comm is None:
        y_mla, lse_mla = _attn_fwd(qc, kc, vc, name="mla_attn", **mla_kw)
    else:
        (y_mla, lse_mla), got = _attn_fwd(qc, kc, vc, exch=comm.gather_ffn, name="mla_attn", **mla_kw)
        W = {**W, **comm.weights(comm.ffn_names, got)}

    kvm = _mm(mem, W["mem_w_kv"], M, 2 * MW, D, name="mem_kv")
    mem_kw = dict(S=S, T=M, H=MEM_HEADS, dqk=MHD, dv=MHD, q_col0=c_qm // MHD, k_col0=0, v_col0=MEM_HEADS,
                  scale=MHD ** -0.5, tq=1024, tk=M)
    assert c_qm % MHD == 0
    y_mem, lse_mem = _attn_fwd(P, kvm, kvm, name="mem_attn", **mem_kw)

    ys = (y_hg, y_mla, y_mem)
    us = [_mm(ys[b], W["w_branch"][b], S, D, HW, name=f"branch{b}") for b in range(N_BRANCH)]
    TCW = _tile(D, 1024, LANES)
    ncw = D // TCW

    def merge_fwd(g0, g1_, g2_, u0, u1, u2):
        return _sigmoid(g0) * u0 + _sigmoid(g1_) * u1 + _sigmoid(g2_) * u2

    gate_rows = [_Rows(P, TCW, (c_gate + b * D) // TCW) for b in range(N_BRANCH)]
    assert c_gate % TCW == 0
    (merged,) = _rowwise(merge_fwd, gate_rows + [_Rows(u, TCW) for u in us], [],
                         [_Rows(D, TCW)], [], R=S, tr=TR, ncol=ncw, name="merge_fwd")
    mix = _mm(merged, W["w_o"], S, D, D, name="out_proj")

    def ln_res(hp, addv, g, b):
        z = ALPHA * hp + addv
        return z, _ln_stats(z)[0] * g + b

    z1, h1 = _rowwise(ln_res, [_Rows(h0, D), _Rows(mix, D)], [g1, b1],
                      [_Rows(D, D), _Rows(D, D)], [], R=S, tr=TR, name="ln1")

    TF = _tile(DFF, 512, LANES)

    def swiglu(abv):
        a, b = abv[:, :TF], abv[:, TF:]
        return [abv, a * _sigmoid(a) * b]

    ab, cff = _mm(h1, W["w_ffn_gu"], S, 2 * DFF, D, tn=2 * TF, epilogue=(swiglu, [], [1, (1, 2)]), name="ffn_gu")
    ff = _mm(cff, W["w_ffn_down"], S, D, DFF, name="ffn_down")

    def loss_bwd(hp, addv, tgt, g, b):
        z = ALPHA * hp + addv
        xhat, rstd = _ln_stats(z)
        y = xhat * g + b
        err = y - tgt
        dy = err * (1.0 / D)
        dxh = dy * g
        m1 = jnp.mean(dxh, axis=-1, keepdims=True)
        m2 = jnp.mean(dxh * xhat, axis=-1, keepdims=True)
        dz = rstd * (dxh - m1 - xhat * m2)
        lrow = jnp.sum(_colsum(err * err), axis=-1, keepdims=True) * (0.5 / D)
        return dz, _colsum(dy * xhat), _colsum(dy), lrow

    dz2, dg2, db2, loss = _rowwise(loss_bwd, [_Rows(h1, D), _Rows(ff, D), _Rows(loss_target, D)], [g2, b2],
                                   [_Rows(D, D)], [D, D, 1], R=S, tr=TR, name="loss_ln2_bwd")
    def swiglu_bwd(dc, abv):
        a, b = abv[:, :TF], abv[:, TF:]
        sg = _sigmoid(a)
        return [jnp.concatenate([dc * b * sg * (1.0 + a * (1.0 - sg)), dc * a * sg], axis=1)]

    (dab,) = _mm(dz2, W["w_ffn_down"], S, DFF, D, tb=True, tn=TF, epilogue=(swiglu_bwd, [(ab, (2, 1))], [(2, 1)]),
                 name="ffn_down_dx")
    g_ffn_down = _mm(cff, dz2, DFF, D, S, ta=True, name="ffn_down_dw")
    dh1 = _mm(dab, W["w_ffn_gu"], S, D, 2 * DFF, tb=True, name="ffn_gu_dx")
    g_ffn_gu = _mm(h1, dab, D, 2 * DFF, S, ta=True, name="ffn_gu_dw")

    def ln1_bwd(z, dmm, dz2v, g):
        return _ln_bwd_core(z, g, ALPHA * dz2v + dmm)

    dz1, dg1, db1 = _rowwise(ln1_bwd, [_Rows(z1, D), _Rows(dh1, D), _Rows(dz2, D)], [g1],
                             [_Rows(D, D)], [D, D], R=S, tr=TR, name="ln1_bwd")
    dmerged = _mm(dz1, W["w_o"], S, D, D, tb=True, name="out_proj_dx")
    g_w_o = _mm(merged, dz1, D, D, S, ta=True, name="out_proj_dw")

    def merge_bwd(g0, g1_, g2_, u0, u1, u2, dm):
        res_g, res_u = [], []
        for gv, uv in ((g0, u0), (g1_, u1), (g2_, u2)):
            sg = _sigmoid(gv)
            res_g.append(dm * uv * sg * (1.0 - sg))
            res_u.append(dm * sg)
        return (*res_g, *res_u)

    mres = _rowwise(merge_bwd, gate_rows + [_Rows(u, TCW) for u in us] + [_Rows(dmerged, TCW)], [],
                    [_Rows(D, TCW)] * (2 * N_BRANCH), [], R=S, tr=TR, ncol=ncw, name="merge_bwd")
    dgates, dus = mres[:N_BRANCH], mres[N_BRANCH:]
    dys = [_mm(dus[b], W["w_branch"][b], S, HW, D, tb=True, name=f"branch{b}_dx") for b in range(N_BRANCH)]
    g_w_branch = [_mm(ys[b], dus[b], HW, D, S, ta=True, name=f"branch{b}_dw") for b in range(N_BRANCH)]

    dq_mem, dk_mem, dv_mem = _attn_bwd(P, kvm, kvm, y_mem, lse_mem, dys[2], name="mem_attn_bwd", **mem_kw)
    dkvm = jnp.concatenate([dk_mem, dv_mem], axis=1)
    g_mem_w_kv = _mm(mem, dkvm, D, 2 * MW, M, ta=True, name="mem_kv_dw")

    g_w_branch = jnp.stack(g_w_branch)
    delivered = {}
    if comm is None:
        dqc, dkc, dvv = _attn_bwd(qc, kc, vc, y_mla, lse_mla, dys[1], name="mla_attn_bwd", **mla_kw)
    else:
        exch = comm.scatter(dict(w_ffn_gu=g_ffn_gu, w_ffn_down=g_ffn_down, w_o=g_w_o, w_branch=g_w_branch,
                                 mem_w_kv=g_mem_w_kv))
        (dqc, dkc, dvv), got = _attn_bwd(qc, kc, vc, y_mla, lse_mla, dys[1], exch=exch, name="mla_attn_bwd",
                                         **mla_kw)
        delivered.update(zip(exch.names, got))

    def rope_bwd(dqb, dkb, tcb, tab, tbb):
        qo, kn = [], []
        dkr = jnp.zeros_like(tcb)
        for h in range(MLA_HEADS):
            qo += [dqb[:, 2 * h * LANES:(2 * h + 1) * LANES],
                   _rope_apply_t(dqb[:, (2 * h + 1) * LANES:(2 * h + 2) * LANES], tcb, tab, tbb)]
            kn.append(dkb[:, 2 * h * LANES:(2 * h + 1) * LANES])
            dkr = dkr + dkb[:, (2 * h + 1) * LANES:(2 * h + 2) * LANES]
        return jnp.concatenate(qo, axis=1), jnp.concatenate(kn, axis=1), _rope_apply_t(dkr, tcb, tab, tbb)

    dq_raw, dkn, dkr_raw = _rowwise(rope_bwd, [_Rows(dqc, QW), _Rows(dkc, QW), _Rows(tc, LANES),
                                               _Rows(ta, LANES), _Rows(tb, LANES)], [],
                                    [_Rows(QW, QW), _Rows(VW, VW), _Rows(LANES, LANES)], [],
                                    R=S, tr=TR, name="rope_bwd")
    dkv = jnp.concatenate([dkn, dvv], axis=1)
    dcqn = _mm(dq_raw, W["mla_w_uq"], S, QR, QW, tb=True, name="mla_uq_dx")
    g_mla_w_uq = _mm(cqn, dq_raw, QR, QW, S, ta=True, name="mla_uq_dw")
    dckvn = _mm(dkv, W["mla_w_ukv"], S, KR, 2 * VW, tb=True, name="mla_ukv_dx")
    g_mla_w_ukv = _mm(ckvn, dkv, KR, 2 * VW, S, ta=True, name="mla_ukv_dw")

    def mla_norm_bwd(cq, ckv, dq_, dk_, gq, gk):
        dcq, gq_rows = _rms_bwd(cq, gq, dq_, RMS_EPS)
        dck, gk_rows = _rms_bwd(ckv, gk, dk_, RMS_EPS)
        return dcq, dck, _colsum(gq_rows), _colsum(gk_rows)

    dcq, dckv, dg_cq, dg_ckv = _rowwise(
        mla_norm_bwd, [_Rows(P, QR, c_cq // QR), _Rows(P, KR, c_ckv // KR), _Rows(dcqn, QR), _Rows(dckvn, KR)],
        [small["mla_g_cq"], small["mla_g_ckv"]], [_Rows(QR, QR), _Rows(KR, KR)], [QR, KR],
        R=S, tr=TR, name="mla_norm_bwd")

    def hg_post_bwd(of, ob, gr, dy, ng):
        o = of + ob
        sg = _sigmoid(gr)
        do_, dgr = [], []
        dng = jnp.zeros((1, HG_D), F32)
        for h in range(HG_HEADS):
            sl = slice(h * HG_D, (h + 1) * HG_D)
            t = _rms_fwd(o[:, sl], ng, RMS_EPS)
            dgr.append(dy[:, sl] * t * sg[:, sl] * (1.0 - sg[:, sl]))
            dx, grow = _rms_bwd(o[:, sl], ng, dy[:, sl] * sg[:, sl], RMS_EPS)
            do_.append(dx)
            dng = dng + _colsum(grow)
        return jnp.concatenate(do_, axis=1), jnp.concatenate(dgr, axis=1), dng

    do_hg, dg_hg, dng = _rowwise(hg_post_bwd, [_Rows(o_fw, HW), _Rows(o_bw, HW), _Rows(P, HW, 4), _Rows(dys[0], HW)],
                                 [small["hgrn_norm_g"]], [_Rows(HW, HW), _Rows(HW, HW)], [HG_D],
                                 R=S, tr=TR, name="hgrn_post_bwd")
    dq_f, dv_f, dff_fw, dlb_f = _hgrn_scan_bwd(P, lb[0:1], st_fw, a_fw, do_hg, S=S, rev=False, name="hgrn_fw_bwd")
    dq_b, dv_b, dff_bw, dlb_b = _hgrn_scan_bwd(P, lb[1:2], st_bw, a_bw, do_hg, S=S, rev=True, name="hgrn_bw_bwd")
    THW = _tile(HW, 1024, LANES)
    dq_hg, dv_hg = _rowwise(lambda a, b, c, d: (a + b, c + d),
                            [_Rows(dq_f, THW), _Rows(dq_b, THW), _Rows(dv_f, THW), _Rows(dv_b, THW)], [],
                            [_Rows(HW, THW), _Rows(HW, THW)], [], R=S, tr=TR, ncol=HW // THW, name="hgrn_dir_sum")

    dP = jnp.concatenate([dq_hg, dv_hg, dff_fw, dff_bw, dg_hg, dcq, dckv, dq_mem, *dgates, dkr_raw], axis=1)
    g_w_in = _mm(h0, dP, D, PW, S, ta=True, tn=896, name="proj_in_dw")
    dx_kw = dict(tb=True, tk=_tile(PW, 640, LANES), name="proj_in_dx")
    if comm is None:
        dh0 = _mm(dP, w_in_k, S, D, PW, **dx_kw)
    else:
        exch = comm.scatter(dict(w_in=g_w_in, mla_w_uq=g_mla_w_uq, mla_w_ukv=g_mla_w_ukv))
        dh0, got = _mm(dP, w_in_k, S, D, PW, exch=exch, **dx_kw)
        delivered.update(zip(exch.names, got))

    def ln0_bwd(z, dmm, dz1v, g):
        return _ln_bwd_core(z, g, ALPHA * dz1v + dmm)

    grad_x, dge, dbe = _rowwise(ln0_bwd, [_Rows(x, D), _Rows(dh0, D), _Rows(dz1, D)], [ge],
                                [_Rows(D, D)], [D, D], R=S, tr=TR, name="ln_emb_bwd")

    big = dict(w_in=g_w_in, mla_w_uq=g_mla_w_uq, mla_w_ukv=g_mla_w_ukv, mem_w_kv=g_mem_w_kv,
               w_branch=g_w_branch, w_o=g_w_o, w_ffn_gu=g_ffn_gu, w_ffn_down=g_ffn_down)
    sm = dict(ln_emb_g=dge, ln_emb_b=dbe, dlb=jnp.concatenate([dlb_f, dlb_b], axis=0), hgrn_norm_g=dng,
              mla_g_cq=dg_cq, mla_g_ckv=dg_ckv, ln1_g=dg1, ln1_b=db1, ln2_g=dg2, ln2_b=db2)
    return loss, grad_x, big, sm, delivered


def _gather_exch(shards):
    copies = [(m, lambda x, y, c: 0, _chip) for m in _CHIP_MASKS]
    return _Exch([s[None] for s in shards], 4, copies, [(lambda x, y, c: 0, _chip)])


def _scatter_exch(pieces):
    copies = [(m, (lambda x, y, c, m=m: 2 * _chip(x ^ m[0], y ^ m[1], c) + (c ^ m[2])), _device) for m in _ALL_MASKS]
    local = [((lambda x, y, c: 2 * _chip(x, y, c) + c), _device)]
    return _Exch([p.reshape((8,) + p.shape[2:]) for p in pieces], 8, copies, local)


SHARE_BLOCK_BYTES = 4 << 20


def _sum_share(arr, *, name):
    n, rh, w = arr.shape
    tr = _tile(rh, max(16, SHARE_BLOCK_BYTES // (n * w * arr.dtype.itemsize) // 16 * 16), 16)
    nb = rh // tr

    def body(a_ref, o_ref, slots, send_sems, recv_sem, local_sems):
        i = pl.program_id(0)
        x, y, c = _coords()
        sibling = (x, y, 1 - c)

        def pushes(step, slot):
            rows = pl.ds(pl.multiple_of(c * rh + step * tr, SUBLANES), tr)
            return (pltpu.make_async_copy(slots.at[slot], o_ref.at[rows], local_sems.at[slot]),
                    pltpu.make_async_remote_copy(src_ref=slots.at[slot], dst_ref=o_ref.at[rows],
                                                 send_sem=send_sems.at[slot], recv_sem=recv_sem,
                                                 device_id=sibling, device_id_type=MESH))

        def drain(step, slot):
            loc, rem = pushes(step, slot)
            loc.wait()
            rem.wait_send()

        slot = i % 2

        @pl.when(i >= 2)
        def _():
            drain(i - 2, slot)

        acc = a_ref[0].astype(F32)
        for k in range(1, n):
            acc = acc + a_ref[k].astype(F32)
        slots[slot] = acc
        loc, rem = pushes(i, slot)
        loc.start()
        rem.start()

        @pl.when(i == nb - 1)
        def _():
            if nb >= 2:
                drain(i - 1, 1 - slot)
            drain(i, slot)
            other = o_ref.at[pl.ds(pl.multiple_of((1 - c) * rh, SUBLANES), rh)]
            pltpu.make_async_remote_copy(src_ref=other, dst_ref=other, send_sem=send_sems.at[0], recv_sem=recv_sem,
                                         device_id=sibling, device_id_type=MESH).wait_recv()

    return pl.pallas_call(
        body, name=name, grid=(nb,),
        in_specs=[pl.BlockSpec((n, tr, w), lambda i: (0, i, 0))],
        out_specs=pl.BlockSpec(memory_space=pl.ANY),
        out_shape=jax.ShapeDtypeStruct((2 * rh, w), F32),
        scratch_shapes=[pltpu.VMEM((2, tr, w), F32), pltpu.SemaphoreType.DMA((2,)), pltpu.SemaphoreType.DMA,
                        pltpu.SemaphoreType.DMA((2,))],
        compiler_params=pltpu.CompilerParams(dimension_semantics=("arbitrary",), has_side_effects=True,
                                             vmem_limit_bytes=VMEM_LIMIT),
    )(arr)


def _allreduce_small(v, *, name):
    r, w = v.shape

    def body(v_ref, o_ref, buf, send_sems, recv_sems):
        x, y, c = _coords()
        me = 4 * x + 2 * y + c
        buf[me] = v_ref[...]
        cps = []
        for k in range(7):
            m = ((k + 1) >> 2 & 1, (k + 1) >> 1 & 1, (k + 1) & 1)
            cp = pltpu.make_async_remote_copy(
                src_ref=v_ref, dst_ref=buf.at[me], send_sem=send_sems.at[k], recv_sem=recv_sems.at[k],
                device_id=(x ^ m[0], y ^ m[1], c ^ m[2]), device_id_type=MESH)
            cp.start()
            cps.append(cp)
        for cp in cps:
            cp.wait_recv()
        for cp in cps:
            cp.wait_send()
        acc = buf[0]
        for k in range(1, 8):
            acc = acc + buf[k]
        o_ref[...] = acc

    return pl.pallas_call(
        body, name=name,
        in_specs=[pl.BlockSpec(memory_space=pltpu.VMEM)],
        out_specs=pl.BlockSpec(memory_space=pltpu.VMEM),
        out_shape=jax.ShapeDtypeStruct((r, w), F32),
        scratch_shapes=[pltpu.VMEM((8, r, w), F32), pltpu.SemaphoreType.DMA((7,)), pltpu.SemaphoreType.DMA((7,))],
        compiler_params=pltpu.CompilerParams(has_side_effects=True),
    )(v)


_BIG = (("w_in", 1), ("mla_w_uq", 1), ("mla_w_ukv", 1), ("mem_w_kv", 0), ("w_branch", 1), ("w_o", 0),
        ("w_ffn_gate", 1), ("w_ffn_up", 1), ("w_ffn_down", 0))


def _assemble(gathered, ax):
    _, r, c = gathered.shape
    if ax == 0:
        return gathered.reshape(4 * r, c)
    return jnp.concatenate([gathered[j] for j in range(4)], axis=1)


def _split_pieces(g, ax):
    r, c = g.shape
    if ax == 0:
        return g.reshape(4, 2, r // 8, c).astype(BF16)
    rh, cs = r // 2, c // 4
    return jnp.stack([g[h * rh:(h + 1) * rh, j * cs:(j + 1) * cs].astype(BF16)
                      for j in range(4) for h in range(2)]).reshape(4, 2, rh, cs)


def _pad_cols(a, n):
    return jnp.pad(a, ((0, 0), (0, n - a.shape[1])))


def _to_kernel_layout(full, QR, KR):
    out = {}
    for n in ("mem_w_kv", "w_branch", "w_o", "w_ffn_down"):
        if n in full:
            out[n] = full[n]
    if "w_in" in full:
        w_in = full["w_in"]
        a = 5 * HG_HEADS * HG_D + QR + KR
        out["w_in"] = jnp.concatenate([w_in[:, :a], w_in[:, a + MLA_ROPE:], _pad_cols(w_in[:, a:a + MLA_ROPE], LANES)],
                                      axis=1)
    if "mla_w_uq" in full:
        uq = full["mla_w_uq"].reshape(QR, MLA_HEADS, MLA_NOPE + MLA_ROPE)
        out["mla_w_uq"] = jnp.pad(uq, ((0, 0), (0, 0), (0, 2 * LANES - MLA_NOPE - MLA_ROPE))).reshape(QR, -1)
    if "mla_w_ukv" in full:
        ukv = full["mla_w_ukv"].reshape(KR, MLA_HEADS, MLA_NOPE + MLA_V)
        out["mla_w_ukv"] = jnp.concatenate([ukv[:, :, :MLA_NOPE].reshape(KR, -1), ukv[:, :, MLA_NOPE:].reshape(KR, -1)],
                                           axis=1)
    if "w_ffn_gate" in full:
        gate, up = full["w_ffn_gate"], full["w_ffn_up"]
        DFF = gate.shape[1]
        TF = _tile(DFF, 512, LANES)
        blocks = []
        for j in range(DFF // TF):
            blocks += [gate[:, j * TF:(j + 1) * TF], up[:, j * TF:(j + 1) * TF]]
        out["w_ffn_gu"] = jnp.concatenate(blocks, axis=1)
    return out


def _from_kernel_layout(gk, QR, KR):
    out = {}
    for n in ("mem_w_kv", "w_o", "w_ffn_down"):
        if n in gk:
            out[n] = gk[n]
    if "w_branch" in gk:
        out["w_branch"] = gk["w_branch"].reshape(-1, gk["w_branch"].shape[-1])
    if "w_in" in gk:
        g = gk["w_in"]
        a = 5 * HG_HEADS * HG_D + QR + KR
        rest = g.shape[1] - LANES - a
        out["w_in"] = jnp.concatenate([g[:, :a], g[:, a + rest:a + rest + MLA_ROPE], g[:, a:a + rest]], axis=1)
    if "mla_w_uq" in gk:
        out["mla_w_uq"] = gk["mla_w_uq"].reshape(QR, MLA_HEADS, 2 * LANES)[:, :, :MLA_NOPE + MLA_ROPE].reshape(QR, -1)
    if "mla_w_ukv" in gk:
        VW = MLA_HEADS * MLA_V
        g = gk["mla_w_ukv"]
        out["mla_w_ukv"] = jnp.concatenate([g[:, :VW].reshape(KR, MLA_HEADS, MLA_NOPE),
                                            g[:, VW:].reshape(KR, MLA_HEADS, MLA_V)], axis=2).reshape(KR, -1)
    if "w_ffn_gu" in gk:
        g = gk["w_ffn_gu"]
        DFF = g.shape[1] // 2
        TF = _tile(DFF, 512, LANES)
        out["w_ffn_gate"] = jnp.concatenate([g[:, 2 * j * TF:(2 * j + 1) * TF] for j in range(DFF // TF)], axis=1)
        out["w_ffn_up"] = jnp.concatenate([g[:, (2 * j + 1) * TF:(2 * j + 2) * TF] for j in range(DFF // TF)], axis=1)
    return out


def _adamw(w, g, m, v, *, name):
    r, c = w.shape
    tr = max(SUBLANES, min(512, (1 << 20) // (4 * c)) // SUBLANES * SUBLANES)
    c1 = 1.0 / (1.0 - ADAM_B1 ** ADAM_STEP)
    c2 = 1.0 / (1.0 - ADAM_B2 ** ADAM_STEP)

    def fn(wv, gv, mv, vv):
        mn = ADAM_B1 * mv + (1.0 - ADAM_B1) * gv
        vn = ADAM_B2 * vv + (1.0 - ADAM_B2) * (gv * gv)
        delta = -ADAM_LR * ((mn * c1) / (jnp.sqrt(vn * c2) + ADAM_EPS) + ADAM_WD * wv)
        return delta, mn, vn

    return _rowwise(fn, [_Rows(a, c) for a in (w, g, m, v)], [], [_Rows(c, c)] * 3, [], R=r, tr=tr, name=name)


_SMALL = ("ln_emb_g", "ln_emb_b", "hgrn_lb_logits", "hgrn_norm_g", "mla_g_cq", "mla_g_ckv",
          "ln1_g", "ln1_b", "ln2_g", "ln2_b")


def _lb_from_logits(logits):
    return jnp.cumsum(jax.nn.softmax(logits, axis=1), axis=1)[:, 0]


def _small_rows(parts):
    flat = jnp.concatenate([p.reshape(-1) for p in parts])
    n = flat.shape[0]
    total = -(-n // (SUBLANES * LANES)) * SUBLANES * LANES
    return jnp.pad(flat, (0, total - n)).reshape(total // LANES, LANES)


def kernel(x, mem, positions, ln_emb_g, ln_emb_b, hgrn_lb_logits, w_in, hgrn_norm_g, mla_g_cq, mla_g_ckv, mla_w_uq, mla_w_ukv, mem_w_kv, w_branch, w_o, ln1_g, ln1_b, w_ffn_gate, w_ffn_up, w_ffn_down, ln2_g, ln2_b, loss_target, m_ln_emb_g, m_ln_emb_b, m_hgrn_lb_logits, m_w_in, m_hgrn_norm_g, m_mla_g_cq, m_mla_g_ckv, m_mla_w_uq, m_mla_w_ukv, m_mem_w_kv, m_w_branch, m_w_o, m_ln1_g, m_ln1_b, m_w_ffn_gate, m_w_ffn_up, m_w_ffn_down, m_ln2_g, m_ln2_b, v_ln_emb_g, v_ln_emb_b, v_hgrn_lb_logits, v_w_in, v_hgrn_norm_g, v_mla_g_cq, v_mla_g_ckv, v_mla_w_uq, v_mla_w_ukv, v_mem_w_kv, v_w_branch, v_w_o, v_ln1_g, v_ln1_b, v_w_ffn_gate, v_w_ffn_up, v_w_ffn_down, v_ln2_g, v_ln2_b):
    names = ["ln_emb_g", "ln_emb_b", "hgrn_lb_logits", "w_in", "hgrn_norm_g", "mla_g_cq", "mla_g_ckv", "mla_w_uq",
             "mla_w_ukv", "mem_w_kv", "w_branch", "w_o", "ln1_g", "ln1_b", "w_ffn_gate", "w_ffn_up", "w_ffn_down",
             "ln2_g", "ln2_b"]
    wts = dict(zip(names, [ln_emb_g, ln_emb_b, hgrn_lb_logits, w_in, hgrn_norm_g, mla_g_cq, mla_g_ckv, mla_w_uq,
                           mla_w_ukv, mem_w_kv, w_branch, w_o, ln1_g, ln1_b, w_ffn_gate, w_ffn_up, w_ffn_down,
                           ln2_g, ln2_b]))
    mom = dict(zip(names, [m_ln_emb_g, m_ln_emb_b, m_hgrn_lb_logits, m_w_in, m_hgrn_norm_g, m_mla_g_cq, m_mla_g_ckv,
                           m_mla_w_uq, m_mla_w_ukv, m_mem_w_kv, m_w_branch, m_w_o, m_ln1_g, m_ln1_b, m_w_ffn_gate,
                           m_w_ffn_up, m_w_ffn_down, m_ln2_g, m_ln2_b]))
    var = dict(zip(names, [v_ln_emb_g, v_ln_emb_b, v_hgrn_lb_logits, v_w_in, v_hgrn_norm_g, v_mla_g_cq, v_mla_g_ckv,
                           v_mla_w_uq, v_mla_w_ukv, v_mem_w_kv, v_w_branch, v_w_o, v_ln1_g, v_ln1_b, v_w_ffn_gate,
                           v_w_ffn_up, v_w_ffn_down, v_ln2_g, v_ln2_b]))
    xc, yc, cc = _coords()
    chip = _chip(xc, yc, cc)
    S, D = x.shape[1], x.shape[2]

    axis = dict(_BIG)
    shard = lambda n: wts[n].reshape(-1, wts[n].shape[-1]).astype(BF16)
    QR, KR = mla_w_uq.shape[1], mla_w_ukv.shape[1]
    MW, DFF = mem_w_kv.shape[2] // 2, 4 * w_ffn_gate.shape[2]
    rows_of_w_in = lambda got: _to_kernel_layout(dict(w_in=_assemble(got[0], axis["w_in"])), QR, KR)["w_in"]
    w_in_shard = shard("w_in")

    class _Comm:
        w_in_top = rows_of_w_in(_exchange(_gather_exch([w_in_shard[:D // 2]]), name="gather_w_in_top"))
        gather_bottom = _gather_exch([w_in_shard[D // 2:]])
        w_in_rows = staticmethod(rows_of_w_in)
        ffn_names = ["w_ffn_gate", "w_ffn_up", "w_ffn_down"]
        mix_names = [n for n, _ in _BIG if n != "w_in" and not n.startswith("w_ffn")]
        gather_mix = _gather_exch([shard(n) for n in mix_names])
        gather_ffn = _gather_exch([shard(n) for n in ffn_names])

        @staticmethod
        def weights(names, received):
            full = {n: _assemble(g, axis[n]) for n, g in zip(names, received)}
            if "w_branch" in full:
                full["w_branch"] = full["w_branch"].reshape(N_BRANCH, -1, D)
            return _to_kernel_layout(full, QR, KR)

        @staticmethod
        def scatter(gk_part):
            gpart = _from_kernel_layout(gk_part, QR, KR)
            exch = _scatter_exch([_split_pieces(gpart[n], axis[n]) for n in gpart])
            exch.names = list(gpart)
            return exch

    lsh = hgrn_lb_logits.shape
    HW = 4 * lsh[2]
    placed = lax.dynamic_update_slice(jnp.zeros((lsh[0], lsh[1], HW), F32), hgrn_lb_logits, (0, 0, chip * lsh[2]))
    placed = jnp.where(cc == 0, placed, 0.0)
    logits = _allreduce_small(_small_rows([placed]), name="gather_logits").reshape(-1)[:placed.size].reshape(placed.shape)
    lb, lb_vjp = jax.vjp(_lb_from_logits, logits)

    small = dict(ln_emb_g=ln_emb_g, ln_emb_b=ln_emb_b, lb=lb, hgrn_norm_g=hgrn_norm_g, mla_g_cq=mla_g_cq,
                 mla_g_ckv=mla_g_ckv, ln1_g=ln1_g, ln1_b=ln1_b, ln2_g=ln2_g, ln2_b=ln2_b)
    loss_l, grad_x, _, gs, delivered = _local_step(x[0], mem[0], positions[0], loss_target[0], None, small, None,
                                                   MW, DFF, comm=_Comm)

    (dlogits,) = lb_vjp(gs["dlb"])
    sm_parts = [loss_l, gs["ln_emb_g"], gs["ln_emb_b"], dlogits, gs["hgrn_norm_g"], gs["mla_g_cq"], gs["mla_g_ckv"],
                gs["ln1_g"], gs["ln1_b"], gs["ln2_g"], gs["ln2_b"]]
    red = _allreduce_small(_small_rows(sm_parts), name="allreduce_small").reshape(-1)
    sm_out, off = [], 0
    for p in sm_parts:
        sm_out.append(red[off:off + p.size].reshape(p.shape))
        off += p.size
    loss = sm_out[0].reshape(())
    g_small = dict(zip(_SMALL, sm_out[1:]))
    g_small["hgrn_lb_logits"] = lax.dynamic_slice(g_small["hgrn_lb_logits"], (0, 0, chip * lsh[2]), lsh)
    for n in _SMALL:
        g_small[n] = g_small[n].reshape(wts[n].shape)

    g_big = {n: _sum_share(delivered[n], name="rs_sum_" + n).reshape(wts[n].shape) for n, _ in _BIG}

    grads = {**g_small, **g_big}
    delta, new_m, new_v = {}, {}, {}
    for n, _ in _BIG:
        shp = wts[n].shape
        two_d = lambda a: a.reshape(-1, shp[-1])
        d_, m_, v_ = _adamw(two_d(wts[n]), two_d(grads[n]), two_d(mom[n]), two_d(var[n]), name="adamw_" + n)
        delta[n], new_m[n], new_v[n] = d_.reshape(shp), m_.reshape(shp), v_.reshape(shp)
    sw, sg_, sm_, sv_ = (_small_rows([d[n] for n in _SMALL]) for d in (wts, grads, mom, var))
    d_, m_, v_ = _adamw(sw, sg_, sm_, sv_, name="adamw_small")
    for res, packed_rows in ((delta, d_), (new_m, m_), (new_v, v_)):
        flat, off = packed_rows.reshape(-1), 0
        for n in _SMALL:
            res[n] = flat[off:off + wts[n].size].reshape(wts[n].shape)
            off += wts[n].size

    return (loss, grad_x[None], *[grads[n] for n in names], *[delta[n] for n in names],
            *[new_m[n] for n in names], *[new_v[n] for n in names])
```

```python
import jax
import jax.numpy as jnp
from jax import lax
from jax.experimental import pallas as pl
from jax.experimental.pallas import tpu as pltpu

F32 = jnp.float32
BF16 = jnp.bfloat16

HG_HEADS = 8
HG_D = 128
MLA_HEADS = 8
MLA_NOPE = 128
MLA_ROPE = 64
MLA_V = 128
MEM_HEADS = 4
N_BRANCH = 3
ROPE_THETA = 10000.0
DEPTH = 1
ALPHA = (2.0 * DEPTH) ** 0.25
LN_EPS = 1e-5
RMS_EPS = 1e-6
ADAM_LR = 0.001
ADAM_B1 = 0.9
ADAM_B2 = 0.999
ADAM_EPS = 1e-08
ADAM_WD = 0.01
ADAM_STEP = 10

LANES = 128
SUBLANES = 8
VMEM_LIMIT = 48 * 1024 * 1024

HG_CHUNK = 128
HG_SUB = 16
HG_PAIR = 2

MESH = pl.DeviceIdType.MESH
HI = lax.Precision.HIGHEST
HG_OFF_PREC = lax.Precision.DEFAULT


def _cparams(sem=None):
    if sem is None:
        return pltpu.CompilerParams(vmem_limit_bytes=VMEM_LIMIT)
    return pltpu.CompilerParams(dimension_semantics=sem, vmem_limit_bytes=VMEM_LIMIT)


def _tile(dim, pref, quantum):
    t = min(pref, dim) // quantum * quantum
    while t >= quantum:
        if dim % t == 0:
            return t
        t -= quantum
    return dim


def _sigmoid(x):
    return 1.0 / (1.0 + jnp.exp(-x))


def _coords():
    return lax.axis_index("x"), lax.axis_index("y"), lax.axis_index("c")


def _chip(x, y, c):
    return 2 * x + y


def _device(x, y, c):
    return 4 * x + 2 * y + c


_CHIP_MASKS = ((1, 0, 0), (0, 1, 0), (1, 1, 0))
_ALL_MASKS = tuple((k >> 2 & 1, k >> 1 & 1, k & 1) for k in range(1, 8))
_HBM = pl.BlockSpec(memory_space=pl.ANY)


class _Exch:
    def __init__(self, srcs, n_dst, copies, local_copies):
        self.srcs, self.n_dst, self.copies, self.local_copies = list(srcs), n_dst, copies, local_copies
        self.n = len(self.srcs)

    def out_shape(self):
        return [jax.ShapeDtypeStruct((self.n_dst,) + s.shape[1:], s.dtype) for s in self.srcs]

    def scratch(self):
        n_rc, n_lc = self.n * len(self.copies), self.n * len(self.local_copies)
        return [pltpu.SemaphoreType.DMA((n_rc,)), pltpu.SemaphoreType.DMA((n_rc,)),
                pltpu.SemaphoreType.DMA((max(n_lc, 1),))]

    def _descriptors(self, src_refs, dst_refs, sems):
        send_sems, recv_sems, local_sems = sems
        x, y, c = _coords()
        n_rc, n_lc = len(self.copies), len(self.local_copies)
        remote, local = [], []
        for a in range(self.n):
            for k, (mask, sidx, didx) in enumerate(self.copies):
                remote.append(pltpu.make_async_remote_copy(
                    src_ref=src_refs[a].at[sidx(x, y, c)], dst_ref=dst_refs[a].at[didx(x, y, c)],
                    send_sem=send_sems.at[a * n_rc + k], recv_sem=recv_sems.at[a * n_rc + k],
                    device_id=(x ^ mask[0], y ^ mask[1], c ^ mask[2]), device_id_type=MESH))
            for k, (sidx, didx) in enumerate(self.local_copies):
                local.append(pltpu.make_async_copy(src_refs[a].at[sidx(x, y, c)], dst_refs[a].at[didx(x, y, c)],
                                                   local_sems.at[a * n_lc + k]))
        return remote, local

    def start(self, src_refs, dst_refs, sems):
        remote, local = self._descriptors(src_refs, dst_refs, sems)
        for cp in remote + local:
            cp.start()

    def wait(self, src_refs, dst_refs, sems):
        remote, local = self._descriptors(src_refs, dst_refs, sems)
        for cp in remote:
            cp.wait_recv()
        for cp in remote:
            cp.wait_send()
        for cp in local:
            cp.wait()


def _exchange(exch, *, name):
    n = exch.n

    def body(*refs):
        src_refs, dst_refs, sems = refs[:n], refs[n:2 * n], refs[2 * n:]
        exch.start(src_refs, dst_refs, sems)
        exch.wait(src_refs, dst_refs, sems)

    return pl.pallas_call(
        body, name=name, in_specs=[_HBM] * n, out_specs=[_HBM] * n, out_shape=exch.out_shape(),
        scratch_shapes=exch.scratch(), compiler_params=pltpu.CompilerParams(has_side_effects=True),
    )(*exch.srcs)


def _carried(call, exch, grid, in_specs, out_specs, out_shape, scratch_shapes, args, *, name):
    n_in, n_out, n_scr = len(in_specs), len(out_specs), len(scratch_shapes)
    n = 0 if exch is None else exch.n

    def body(*refs):
        o0 = n_in + n
        s0 = o0 + n_out + n
        ins, srcs = refs[:n_in], refs[n_in:o0]
        outs, dsts = refs[o0:o0 + n_out], refs[o0 + n_out:s0]
        scr, sems = refs[s0:s0 + n_scr], refs[s0 + n_scr:]
        if exch is not None:
            ids = [pl.program_id(d) for d in range(len(grid))]
            first = _all([i == 0 for i in ids])
            last = _all([i == g - 1 for i, g in zip(ids, grid)])

            @pl.when(first)
            def _():
                exch.start(srcs, dsts, sems)

        call(*ins, *outs, *scr)
        if exch is not None:
            @pl.when(last)
            def _():
                exch.wait(srcs, dsts, sems)

    if exch is None:
        params = pltpu.CompilerParams(dimension_semantics=("arbitrary",) * len(grid), vmem_limit_bytes=VMEM_LIMIT)
        extra_in, extra_out, extra_shape, extra_scr, extra_args = [], [], [], [], []
    else:
        params = pltpu.CompilerParams(dimension_semantics=("arbitrary",) * len(grid), vmem_limit_bytes=VMEM_LIMIT,
                                      has_side_effects=True)
        extra_in, extra_out, extra_shape = [_HBM] * n, [_HBM] * n, exch.out_shape()
        extra_scr, extra_args = exch.scratch(), exch.srcs
    res = pl.pallas_call(
        body, name=name, grid=grid, in_specs=list(in_specs) + extra_in, out_specs=list(out_specs) + extra_out,
        out_shape=list(out_shape) + extra_shape, scratch_shapes=list(scratch_shapes) + extra_scr,
        compiler_params=params,
    )(*args, *extra_args)
    return res[:n_out], res[n_out:]


def _all(conds):
    out = conds[0]
    for c in conds[1:]:
        out = jnp.logical_and(out, c)
    return out


def _mm(a, b, M, N, K, *, ta=False, tb=False, a_off=(0, 0), b_off=(0, 0), add=None, exch=None, epilogue=None,
        tm=1024, tn=1024, tk=1024, name):
    tm = _tile(M, tm, LANES if ta else SUBLANES)
    tn = _tile(N, tn, LANES)
    tk = _tile(K, tk, LANES)
    nk = K // tk
    ar, ac = a_off
    br, bc = b_off

    if ta:
        assert ar % tk == 0 and ac % tm == 0
        a_spec = pl.BlockSpec((tk, tm), lambda i, j, k: (ar // tk + k, ac // tm + i))
    else:
        assert ar % tm == 0 and ac % tk == 0
        a_spec = pl.BlockSpec((tm, tk), lambda i, j, k: (ar // tm + i, ac // tk + k))
    if tb:
        assert br % tn == 0 and bc % tk == 0
        b_spec = pl.BlockSpec((tn, tk), lambda i, j, k: (br // tn + j, bc // tk + k))
    else:
        assert br % tk == 0 and bc % tn == 0
        b_spec = pl.BlockSpec((tk, tn), lambda i, j, k: (br // tk + k, bc // tn + j))
    o_spec = pl.BlockSpec((tm, tn), lambda i, j, k: (i, j))
    mixed = a.dtype != b.dtype

    epi_fn, epi_ins, epi_outs = (None, [], [1]) if epilogue is None else epilogue
    n_in = 2 + (add is not None) + len(epi_ins)

    def body(*refs):
        a_ref, b_ref = refs[:2]
        add_ref = refs[2] if add is not None else None
        epi_refs = refs[n_in - len(epi_ins):n_in]
        o_refs, acc = refs[n_in:n_in + len(epi_outs)], refs[-1]
        k = pl.program_id(2)
        av = a_ref[...]
        bv = b_ref[...]
        if ta:
            av = av.astype(F32).T
        if mixed:
            av = av.astype(BF16)
            bv = bv.astype(BF16)
        dims = (((1,), (1 if tb else 0,)), ((), ()))
        d = lax.dot_general(av, bv, dims, preferred_element_type=F32)

        def finish(total):
            if add is not None:
                total = total + add_ref[...]
            tiles = [total] if epi_fn is None else epi_fn(total, *[r[...] for r in epi_refs])
            for o_ref, t in zip(o_refs, tiles):
                o_ref[...] = t

        if nk == 1:
            finish(d)
        else:
            @pl.when(k == 0)
            def _():
                acc[...] = d

            @pl.when(jnp.logical_and(k > 0, k < nk - 1))
            def _():
                acc[...] += d

            @pl.when(k == nk - 1)
            def _():
                finish(acc[...] + d)

    in_specs = [a_spec, b_spec]
    args = [a, b]
    if add is not None:
        in_specs.append(o_spec)
        args.append(add)
    wide = lambda w: pl.BlockSpec((tm, tn * w[0] // w[1]), lambda i, j, k: (i, j))
    for arr, w in epi_ins:
        in_specs.append(wide(w))
        args.append(arr)
    out_specs = [o_spec if w == 1 else wide(w) for w in epi_outs]
    out_shape = [jax.ShapeDtypeStruct((M, N if w == 1 else N * w[0] // w[1]), F32) for w in epi_outs]
    outs, received = _carried(body, exch, (M // tm, N // tn, nk), in_specs, out_specs, out_shape,
                              [pltpu.VMEM((tm, tn), F32)], args, name=name)
    outs = outs[0] if epilogue is None else outs
    return outs if exch is None else (outs, received)


class _Rows:
    def __init__(self, arr, width, col0=0, lead=None, dtype=F32):
        self.arr, self.width, self.col0, self.lead, self.dtype = arr, width, col0, lead, dtype


def _rowwise(fn, rows, consts, outs, accs, *, R, tr, ncol=1, name):
    tr = _tile(R, tr, SUBLANES)
    nrow = R // tr

    def spec(r):
        if r.lead is None:
            return pl.BlockSpec((tr, r.width), lambda j, i, c0=r.col0: (i, c0 + j))
        return pl.BlockSpec((None, tr, r.width), lambda j, i, c0=r.col0, l=r.lead: (l, i, c0 + j))

    in_specs = [spec(r) for r in rows]
    for c in consts:
        in_specs.append(pl.BlockSpec(c.shape, lambda j, i, nd=c.ndim: (0,) * nd))
    out_specs = [spec(o) for o in outs]
    out_shape = [jax.ShapeDtypeStruct((R, o.arr), o.dtype) for o in outs]
    for w in accs:
        out_specs.append(pl.BlockSpec((1, w), lambda j, i: (0, j)))
        out_shape.append(jax.ShapeDtypeStruct((1, w * ncol), F32))
    n_in = len(rows) + len(consts)
    n_out = len(outs)

    def body(*refs):
        ins = [r[...] for r in refs[:n_in]]
        res = fn(*ins)
        if not isinstance(res, (tuple, list)):
            res = (res,)
        for k in range(n_out):
            refs[n_in + k][...] = res[k].astype(refs[n_in + k].dtype)
        i = pl.program_id(1)
        for k in range(len(accs)):
            a_ref = refs[n_in + n_out + k]

            @pl.when(i == 0)
            def _(a_ref=a_ref):
                a_ref[...] = jnp.zeros_like(a_ref)

            a_ref[...] += res[n_out + k]

    res = pl.pallas_call(
        body, name=name, grid=(ncol, nrow),
        in_specs=in_specs, out_specs=out_specs, out_shape=out_shape,
        compiler_params=_cparams(("parallel", "arbitrary")),
    )(*[r.arr for r in rows], *consts)
    return res


def _colsum(x):
    return jnp.sum(x, axis=0, keepdims=True)


def _ln_stats(z):
    mu = jnp.mean(z, axis=-1, keepdims=True)
    zc = z - mu
    var = jnp.mean(zc * zc, axis=-1, keepdims=True)
    rstd = lax.rsqrt(var + LN_EPS)
    return zc * rstd, rstd


def _ln_bwd_core(z, g, dy):
    xhat, rstd = _ln_stats(z)
    dxh = dy * g
    m1 = jnp.mean(dxh, axis=-1, keepdims=True)
    m2 = jnp.mean(dxh * xhat, axis=-1, keepdims=True)
    dz = rstd * (dxh - m1 - xhat * m2)
    return dz, _colsum(dy * xhat), _colsum(dy)


def _rms_fwd(x, g, eps):
    r = lax.rsqrt(jnp.mean(x * x, axis=-1, keepdims=True) + eps)
    return x * r * g


def _rms_bwd(x, g, dy, eps):
    r = lax.rsqrt(jnp.mean(x * x, axis=-1, keepdims=True) + eps)
    xr = x * r
    dyg = dy * g
    dx = r * (dyg - xr * jnp.mean(dyg * xr, axis=-1, keepdims=True))
    return dx, dy * xr


def _hg_gate(fr, lb):
    sig = _sigmoid(fr)
    f = lb + (1.0 - lb) * sig
    return sig, f


def _hg_masks(rev):
    C = HG_CHUNK
    t = lax.broadcasted_iota(jnp.int32, (C, C), 0)
    s = lax.broadcasted_iota(jnp.int32, (C, C), 1)
    tri = (s >= t) if rev else (s <= t)
    return tri


def _hg_offdiag(Q, K, b, i, rev):
    C, sb = HG_CHUNK, HG_SUB
    nb = C // sb
    if (not rev and i == 0) or (rev and i == nb - 1):
        return None
    ref = b[sb * i - 1:sb * i] if not rev else b[sb * (i + 1):sb * (i + 1) + 1]
    srow = lax.broadcasted_iota(jnp.int32, (C, 1), 0)
    smask = (srow < sb * i) if not rev else (srow >= sb * (i + 1))
    qscale = jnp.exp(jnp.minimum(b - ref, 0.0))
    kscale = jnp.where(smask, jnp.exp(jnp.minimum(ref - b, 0.0)), 0.0)
    return qscale, kscale


def _hg_att(Q, K, b, rev):
    C, sb = HG_CHUNK, HG_SUB
    lane = lax.broadcasted_iota(jnp.int32, (sb, C), 1)
    rloc = lax.broadcasted_iota(jnp.int32, (sb, 1), 0)
    rows = []
    for i in range(C // sb):
        sl = slice(sb * i, sb * i + sb)
        Qi, Ki, bi = Q[sl], K[sl], b[sl]
        od = _hg_offdiag(Q, K, b, i, rev)
        if od is None:
            acc = jnp.zeros((sb, C), F32)
        else:
            qs, ks = od
            acc = lax.dot_general(Qi * qs[sl], K * ks, (((1,), (1,)), ((), ())),
                                  precision=HG_OFF_PREC, preferred_element_type=F32)
        for j in range(sb):
            e = jnp.exp(jnp.minimum(bi - bi[j:j + 1], 0.0))
            col = jnp.sum(Qi * Ki[j:j + 1] * e, axis=-1, keepdims=True)
            vis = (rloc <= j) if rev else (rloc >= j)
            acc = jnp.where(lane == sb * i + j, jnp.where(vis, col, 0.0), acc)
        rows.append(acc)
    return jnp.concatenate(rows, axis=0)


def _hg_att_bwd(Q, K, b, dA, rev):
    C, sb = HG_CHUNK, HG_SUB
    rloc = lax.broadcasted_iota(jnp.int32, (sb, 1), 0)
    rrow = lax.broadcasted_iota(jnp.int32, (sb, HG_D), 0)
    trow = lax.broadcasted_iota(jnp.int32, (C, C), 1) // sb
    dAT = dA.T
    dQ_rows, dKd_rows = [], []
    dK = jnp.zeros((C, HG_D), F32)
    for i in range(C // sb):
        sl = slice(sb * i, sb * i + sb)
        Qi, Ki, bi, dAi = Q[sl], K[sl], b[sl], dA[sl]
        od = _hg_offdiag(Q, K, b, i, rev)
        if od is None:
            dQi = jnp.zeros((sb, HG_D), F32)
        else:
            qs, ks = od
            dQi = lax.dot_general(dAi, K * ks, (((1,), (0,)), ((), ())),
                                  precision=HI, preferred_element_type=F32) * qs[sl]
            zt = jnp.where(trow == i, dAT, 0.0)
            dK = dK + lax.dot_general(zt, Q * qs, (((1,), (0,)), ((), ())),
                                      precision=HI, preferred_element_type=F32) * ks
        dKd = jnp.zeros((sb, HG_D), F32)
        for j in range(sb):
            vis = (rloc <= j) if rev else (rloc >= j)
            e = jnp.where(vis, jnp.exp(jnp.minimum(bi - bi[j:j + 1], 0.0)), 0.0)
            dcol = dAi[:, sb * i + j:sb * i + j + 1]
            dQi = dQi + dcol * Ki[j:j + 1] * e
            krow = jnp.sum(dcol * Qi * e, axis=0, keepdims=True)
            dKd = jnp.where(rrow == j, krow, dKd)
        dQ_rows.append(dQi)
        dKd_rows.append(dKd)
    return jnp.concatenate(dQ_rows, axis=0), dK + jnp.concatenate(dKd_rows, axis=0)


def _hg_prep(qr, fr, lb, tri):
    sigq = _sigmoid(qr)
    Q = qr * sigq
    sig, f = _hg_gate(fr, lb)
    K = 1.0 - f
    logf = jnp.log(f)
    b = lax.dot_general(tri.astype(F32), logf, (((1,), (0,)), ((), ())),
                        precision=HI, preferred_element_type=F32)
    return sigq, Q, sig, f, K, b


def _hgrn_scan(P, lb, *, S, rev, name):
    C, H, D_ = HG_CHUNK, HG_HEADS, HG_D
    HP = HG_PAIR if H % HG_PAIR == 0 else 1
    NC = S // C
    fcol = (3 if rev else 2) * H

    def cidx(n):
        return NC - 1 - n if rev else n

    def body(q_ref, v_ref, f_ref, lb_ref, o_ref, st_ref, a_ref, state):
        n = pl.program_id(1)

        @pl.when(n == 0)
        def _():
            state[...] = jnp.zeros_like(state)

        tri = _hg_masks(rev)
        qa, va, fa, lba, sta = q_ref[...], v_ref[...], f_ref[...], lb_ref[...], state[...]
        st_ref[...] = sta
        outs, amats, states = [], [], []
        for hp in range(HP):
            sl = slice(hp * D_, (hp + 1) * D_)
            _, Q, _, _, K, b = _hg_prep(qa[:, sl], fa[:, sl], lba[:, sl], tri)
            V, ST0 = va[:, sl], sta[hp]
            e_b = jnp.exp(b)
            bE = b[0:1] if rev else b[C - 1:C]
            W = jnp.exp(bE - b)
            inter = lax.dot_general(Q * e_b, ST0, (((1,), (1,)), ((), ())), preferred_element_type=F32)
            A = _hg_att(Q, K, b, rev)
            amats.append(A)
            outs.append(inter + jnp.dot(A, V, preferred_element_type=F32))
            states.append(ST0 * jnp.exp(bE) + lax.dot_general(
                V, K * W, (((0,), (0,)), ((), ())), preferred_element_type=F32))
        a_ref[...] = jnp.stack(amats)
        o_ref[...] = jnp.concatenate(outs, axis=1)
        state[...] = jnp.stack(states)

    blk = lambda c0: pl.BlockSpec((C, HP * D_), lambda h, n, c0=c0: (cidx(n), c0 // HP + h))
    return pl.pallas_call(
        body, name=name, grid=(H // HP, NC),
        in_specs=[blk(0), blk(H), blk(fcol), pl.BlockSpec((1, HP * D_), lambda h, n: (0, h))],
        out_specs=[pl.BlockSpec((C, HP * D_), lambda h, n: (cidx(n), h)),
                   pl.BlockSpec((None, HP, D_, D_), lambda h, n: (cidx(n), h, 0, 0)),
                   pl.BlockSpec((None, HP, C, C), lambda h, n: (cidx(n), h, 0, 0))],
        out_shape=[jax.ShapeDtypeStruct((S, H * D_), F32),
                   jax.ShapeDtypeStruct((NC, H, D_, D_), F32),
                   jax.ShapeDtypeStruct((NC, H, C, C), F32)],
        scratch_shapes=[pltpu.VMEM((HP, D_, D_), F32)],
        compiler_params=_cparams(("parallel", "arbitrary")),
    )(P, P, P, lb)


def _hgrn_scan_bwd(P, lb, st, amat, do, *, S, rev, name):
    C, H, D_ = HG_CHUNK, HG_HEADS, HG_D
    HP = HG_PAIR if H % HG_PAIR == 0 else 1
    NC = S // C
    fcol = (3 if rev else 2) * H

    def cidx(n):
        return n if rev else NC - 1 - n

    def body(q_ref, v_ref, f_ref, lb_ref, st_ref, a_ref, do_ref, dq_ref, dv_ref, df_ref, dlb_ref, dstate):
        n = pl.program_id(1)

        @pl.when(n == 0)
        def _():
            dstate[...] = jnp.zeros_like(dstate)
            dlb_ref[...] = jnp.zeros_like(dlb_ref)

        tri = _hg_masks(rev)
        tri_t = _hg_masks(not rev).astype(F32)
        qa, va, fa, lba, doa = q_ref[...], v_ref[...], f_ref[...], lb_ref[...], do_ref[...]
        sta, ama, dsta = st_ref[...], a_ref[...], dstate[...]
        trow = lax.broadcasted_iota(jnp.int32, (C, 1), 0)
        dqs, dvs, dfs, dlbs, dstates = [], [], [], [], []
        for hp in range(HP):
            sl = slice(hp * D_, (hp + 1) * D_)
            lbv, qr = lba[:, sl], qa[:, sl]
            sigq, Q, sig, f, K, b = _hg_prep(qr, fa[:, sl], lbv, tri)
            V, ST0, A, dO, dST1 = va[:, sl], sta[hp], ama[hp], doa[:, sl], dsta[hp]
            e_b = jnp.exp(b)
            bE = b[0:1] if rev else b[C - 1:C]
            eE = jnp.exp(bE)
            W = jnp.exp(bE - b)
            Qe = Q * e_b
            KW = K * W
            dA = jnp.where(tri, lax.dot_general(dO, V, (((1,), (1,)), ((), ())), preferred_element_type=F32), 0.0)
            dV = (lax.dot_general(A, dO, (((0,), (0,)), ((), ())), preferred_element_type=F32)
                  + lax.dot_general(KW, dST1, (((1,), (1,)), ((), ())), preferred_element_type=F32))
            dQe = jnp.dot(dO, ST0, preferred_element_type=F32)
            dKW = jnp.dot(V, dST1, preferred_element_type=F32)
            dstates.append(dST1 * eE + lax.dot_general(dO, Qe, (((0,), (0,)), ((), ())), preferred_element_type=F32))
            dQa, dKa = _hg_att_bwd(Q, K, b, dA, rev)
            dQ = dQe * e_b + dQa
            dK = dKW * W + dKa
            extra = _colsum(KW * dKW) + eE * _colsum(ST0 * dST1)
            db = Q * dQ - K * dK + jnp.where(trow == (0 if rev else C - 1), extra, 0.0)
            dlogf = lax.dot_general(tri_t, db, (((1,), (0,)), ((), ())), precision=HI, preferred_element_type=F32)
            dfv = dlogf / f - dK
            dfs.append(dfv * (1.0 - lbv) * sig * (1.0 - sig))
            dlbs.append(_colsum(dfv * (1.0 - sig)))
            dqs.append(dQ * (sigq * (1.0 + qr * (1.0 - sigq))))
            dvs.append(dV)
        dstate[...] = jnp.stack(dstates)
        df_ref[...] = jnp.concatenate(dfs, axis=1)
        dlb_ref[...] += jnp.concatenate(dlbs, axis=1)
        dq_ref[...] = jnp.concatenate(dqs, axis=1)
        dv_ref[...] = jnp.concatenate(dvs, axis=1)

    blk = lambda c0: pl.BlockSpec((C, HP * D_), lambda h, n, c0=c0: (cidx(n), c0 // HP + h))
    oblk = pl.BlockSpec((C, HP * D_), lambda h, n: (cidx(n), h))
    return pl.pallas_call(
        body, name=name, grid=(H // HP, NC),
        in_specs=[blk(0), blk(H), blk(fcol), pl.BlockSpec((1, HP * D_), lambda h, n: (0, h)),
                  pl.BlockSpec((None, HP, D_, D_), lambda h, n: (cidx(n), h, 0, 0)),
                  pl.BlockSpec((None, HP, C, C), lambda h, n: (cidx(n), h, 0, 0)),
                  oblk],
        out_specs=[oblk, oblk, oblk, pl.BlockSpec((1, HP * D_), lambda h, n: (0, h))],
        out_shape=[jax.ShapeDtypeStruct((S, H * D_), F32)] * 3 + [jax.ShapeDtypeStruct((1, H * D_), F32)],
        scratch_shapes=[pltpu.VMEM((HP, D_, D_), F32)],
        compiler_params=_cparams(("parallel", "arbitrary")),
    )(P, P, P, lb, st, amat, do)


LOG2E = 1.4426950408889634
MXU = BF16
ATT_SUB = 512


def _mx(x):
    return x if x.dtype == MXU else x.astype(MXU)


def _attn_fwd(q, k, v, *, S, T, H, dqk, dv, q_col0, k_col0, v_col0, scale, tq, tk, exch=None, name):
    tq = _tile(S, tq, SUBLANES)
    tk = _tile(T, tk, LANES)
    nk = T // tk
    ts = _tile(tq, ATT_SUB, SUBLANES)

    def body(q_ref, k_ref, v_ref, o_ref, lse_ref, m_s, l_s, acc):
        j = pl.program_id(2)

        @pl.when(j == 0)
        def _():
            m_s[...] = jnp.full_like(m_s, -jnp.inf)
            l_s[...] = jnp.zeros_like(l_s)
            acc[...] = jnp.zeros_like(acc)

        kv, vv = _mx(k_ref[...]), _mx(v_ref[...])
        m_all, l_all, a_all = m_s[...], l_s[...], acc[...]
        ms, ls, accs = [], [], []
        for r0 in range(0, tq, ts):
            rows = slice(r0, r0 + ts)
            s = lax.dot_general(_mx(q_ref[rows, :]), kv, (((1,), (1,)), ((), ())),
                                preferred_element_type=F32) * (scale * LOG2E)
            m_old = m_all[rows]
            m_new = jnp.maximum(m_old, jnp.max(s, axis=1)[:, None])
            corr = jnp.exp2(m_old - m_new)
            p = jnp.exp2(s - jnp.tile(m_new, (1, tk // LANES)))
            ms.append(m_new)
            ls.append(corr * l_all[rows] + jnp.sum(p, axis=1)[:, None])
            accs.append(jnp.tile(corr, (1, dv // LANES)) * a_all[rows] + jnp.dot(_mx(p), vv, preferred_element_type=F32))
        m_s[...] = jnp.concatenate(ms, axis=0)
        l_s[...] = jnp.concatenate(ls, axis=0)
        acc[...] = jnp.concatenate(accs, axis=0)

        @pl.when(j == nk - 1)
        def _():
            o_ref[...] = acc[...] / jnp.tile(l_s[...], (1, dv // LANES))
            lse_ref[...] = ((m_s[...] + jnp.log2(l_s[...])) * (1.0 / LOG2E))[:, :1]

    outs, received = _carried(
        body, exch, (H, S // tq, nk),
        [pl.BlockSpec((tq, dqk), lambda h, i, j: (i, q_col0 + h)),
         pl.BlockSpec((tk, dqk), lambda h, i, j: (j, k_col0 + h)),
         pl.BlockSpec((tk, dv), lambda h, i, j: (j, v_col0 + h))],
        [pl.BlockSpec((tq, dv), lambda h, i, j: (i, h)),
         pl.BlockSpec((None, tq, 1), lambda h, i, j: (h, i, 0))],
        [jax.ShapeDtypeStruct((S, H * dv), F32), jax.ShapeDtypeStruct((H, S, 1), F32)],
        [pltpu.VMEM((tq, LANES), F32), pltpu.VMEM((tq, LANES), F32), pltpu.VMEM((tq, dv), F32)],
        [q, k, v], name=name)
    return outs if exch is None else (outs, received)


def _attn_bwd(q, k, v, o, lse, do, *, S, T, H, dqk, dv, q_col0, k_col0, v_col0, scale, tq, tk, exch=None, name):
    tq = _tile(S, tq, SUBLANES)
    tk = _tile(T, tk, LANES)
    nq = S // tq
    ts = _tile(tq, ATT_SUB, SUBLANES)

    def body(q_ref, k_ref, v_ref, o_ref, lse_ref, do_ref, dq_ref, dk_ref, dv_ref, dk_acc, dv_acc):
        j = pl.program_id(1)
        i = pl.program_id(2)

        @pl.when(jnp.logical_and(i == 0, j == 0))
        def _():
            dq_ref[...] = jnp.zeros_like(dq_ref)

        @pl.when(i == 0)
        def _():
            dk_acc[...] = jnp.zeros_like(dk_acc)
            dv_acc[...] = jnp.zeros_like(dv_acc)

        kv, vv = _mx(k_ref[...]), _mx(v_ref[...])
        dk_new, dv_new = dk_acc[...], dv_acc[...]
        lse2 = lse_ref[...] * LOG2E
        dqs = []
        for r0 in range(0, tq, ts):
            rows = slice(r0, r0 + ts)
            qv, dov = _mx(q_ref[rows, :]), do_ref[rows, :]
            s = lax.dot_general(qv, kv, (((1,), (1,)), ((), ())), preferred_element_type=F32) * (scale * LOG2E)
            p = jnp.exp2(s - lse2[rows])
            delta = jnp.sum(dov * o_ref[rows, :], axis=-1, keepdims=True)
            dob = _mx(dov)
            dp = lax.dot_general(dob, vv, (((1,), (1,)), ((), ())), preferred_element_type=F32)
            ds = _mx(p * (dp - delta) * scale)
            dv_new = dv_new + lax.dot_general(_mx(p), dob, (((0,), (0,)), ((), ())), preferred_element_type=F32)
            dk_new = dk_new + lax.dot_general(ds, qv, (((0,), (0,)), ((), ())), preferred_element_type=F32)
            dqs.append(jnp.dot(ds, kv, preferred_element_type=F32))
        dq_ref[pl.ds(pl.multiple_of(i * tq, tq), tq), :] += jnp.concatenate(dqs, axis=0)
        dk_acc[...] = dk_new
        dv_acc[...] = dv_new

        @pl.when(i == nq - 1)
        def _():
            dk_ref[...] = dk_new
            dv_ref[...] = dv_new

    outs, received = _carried(
        body, exch, (H, T // tk, nq),
        [pl.BlockSpec((tq, dqk), lambda h, j, i: (i, q_col0 + h)),
         pl.BlockSpec((tk, dqk), lambda h, j, i: (j, k_col0 + h)),
         pl.BlockSpec((tk, dv), lambda h, j, i: (j, v_col0 + h)),
         pl.BlockSpec((tq, dv), lambda h, j, i: (i, h)),
         pl.BlockSpec((None, tq, 1), lambda h, j, i: (h, i, 0)),
         pl.BlockSpec((tq, dv), lambda h, j, i: (i, h))],
        [pl.BlockSpec((S, dqk), lambda h, j, i: (0, h)),
         pl.BlockSpec((tk, dqk), lambda h, j, i: (j, h)),
         pl.BlockSpec((tk, dv), lambda h, j, i: (j, h))],
        [jax.ShapeDtypeStruct((S, H * dqk), F32), jax.ShapeDtypeStruct((T, H * dqk), F32),
         jax.ShapeDtypeStruct((T, H * dv), F32)],
        [pltpu.VMEM((tk, dqk), F32), pltpu.VMEM((tk, dv), F32)], [q, k, v, o, lse, do], name=name)
    return outs if exch is None else (outs, received)


def _rope_tables(positions):
    half = MLA_ROPE // 2
    inv_freq = jnp.power(ROPE_THETA, -jnp.arange(half, dtype=F32) / half)
    ang = positions.astype(F32)[:, None] * inv_freq
    cos, sin = jnp.cos(ang), jnp.sin(ang)
    z = jnp.zeros_like(cos)
    tc = jnp.concatenate([cos, cos, z, z], axis=1)
    ta = jnp.concatenate([-sin, z, z, z], axis=1)
    tb = jnp.concatenate([z, sin, z, z], axis=1)
    return tc, ta, tb


def _rope_apply(v, tc, ta, tb):
    half = MLA_ROPE // 2
    return v * tc + pltpu.roll(v, LANES - half, 1) * ta + pltpu.roll(v, half, 1) * tb


def _rope_apply_t(d, tc, ta, tb):
    half = MLA_ROPE // 2
    return d * tc + pltpu.roll(d * ta, half, 1) + pltpu.roll(d * tb, LANES - half, 1)


def _local_step(x, mem, positions, loss_target, w_in_k, small, W, MW, DFF, comm=None):
    S, D = x.shape
    M = mem.shape[0]
    HW = HG_HEADS * HG_D
    QR = small["mla_g_cq"].shape[1]
    KR = small["mla_g_ckv"].shape[1]
    MHD = MW // MEM_HEADS
    QW = MLA_HEADS * 2 * LANES
    VW = MLA_HEADS * MLA_V
    c_hg, c_cq, c_ckv, c_qm, c_gate = 0, 5 * HW, 5 * HW + QR, 5 * HW + QR + KR, 5 * HW + QR + KR + MW
    c_kr = c_gate + N_BRANCH * D
    PW = c_kr + LANES
    assert comm is not None or w_in_k.shape == (D, PW)
    TR = 256
    row = lambda a: a.reshape(1, -1)
    ge, be = row(small["ln_emb_g"]), row(small["ln_emb_b"])
    g1, b1, g2, b2 = small["ln1_g"], small["ln1_b"], small["ln2_g"], small["ln2_b"]
    lb = small["lb"]
    tc, ta, tb = _rope_tables(positions)

    (h0,) = _rowwise(lambda z, g, b: _ln_stats(z)[0] * g + b, [_Rows(x, D)], [ge, be],
                     [_Rows(D, D)], [], R=S, tr=TR, name="ln_emb")
    if comm is None:
        P = _mm(h0, w_in_k, S, PW, D, tn=896, name="proj_in")
    else:
        half = D // 2
        P_top, got = _mm(h0, comm.w_in_top, S, PW, half, tn=896, exch=comm.gather_bottom, name="proj_in_top")
        w_in_bottom = comm.w_in_rows(got)
        P, got = _mm(h0, w_in_bottom, S, PW, half, a_off=(0, half), add=P_top, tn=896, exch=comm.gather_mix,
                     name="proj_in")
        W = comm.weights(comm.mix_names, got)
        w_in_k = jnp.concatenate([comm.w_in_top, w_in_bottom], axis=0)

    o_fw, st_fw, a_fw = _hgrn_scan(P, lb[0:1], S=S, rev=False, name="hgrn_fw")
    o_bw, st_bw, a_bw = _hgrn_scan(P, lb[1:2], S=S, rev=True, name="hgrn_bw")

    def hg_post(of, ob, gr, ng):
        o = of + ob
        sg = _sigmoid(gr)
        outs = []
        for h in range(HG_HEADS):
            sl = slice(h * HG_D, (h + 1) * HG_D)
            outs.append(_rms_fwd(o[:, sl], ng, RMS_EPS) * sg[:, sl])
        return jnp.concatenate(outs, axis=1)

    (y_hg,) = _rowwise(hg_post, [_Rows(o_fw, HW), _Rows(o_bw, HW), _Rows(P, HW, 4)], [small["hgrn_norm_g"]],
                       [_Rows(HW, HW)], [], R=S, tr=TR, name="hgrn_post")

    def mla_norm(cq, ckv, gq, gk):
        return _rms_fwd(cq, gq, RMS_EPS), _rms_fwd(ckv, gk, RMS_EPS)

    assert c_cq % QR == 0 and c_ckv % KR == 0
    cqn, ckvn = _rowwise(mla_norm, [_Rows(P, QR, c_cq // QR), _Rows(P, KR, c_ckv // KR)],
                         [small["mla_g_cq"], small["mla_g_ckv"]],
                         [_Rows(QR, QR), _Rows(KR, KR)], [], R=S, tr=TR, name="mla_norm")
    q_raw = _mm(cqn, W["mla_w_uq"], S, QW, QR, name="mla_uq")
    kv = _mm(ckvn, W["mla_w_ukv"], S, 2 * VW, KR, name="mla_ukv")

    def rope_fwd(qb, knb, vb_, krb, tcb, tab, tbb):
        kr = _rope_apply(krb, tcb, tab, tbb)
        qo, ko = [], []
        for h in range(MLA_HEADS):
            qo += [qb[:, 2 * h * LANES:(2 * h + 1) * LANES],
                   _rope_apply(qb[:, (2 * h + 1) * LANES:(2 * h + 2) * LANES], tcb, tab, tbb)]
            ko += [knb[:, h * LANES:(h + 1) * LANES], kr]
        return jnp.concatenate(qo, axis=1), jnp.concatenate(ko, axis=1), vb_

    qc, kc, vc = _rowwise(rope_fwd, [_Rows(q_raw, QW), _Rows(kv, VW), _Rows(kv, VW, 1), _Rows(P, LANES, c_kr // LANES),
                                     _Rows(tc, LANES), _Rows(ta, LANES), _Rows(tb, LANES)], [],
                          [_Rows(QW, QW, dtype=MXU), _Rows(QW, QW, dtype=MXU), _Rows(VW, VW, dtype=MXU)], [],
                          R=S, tr=TR, name="rope_fwd")
    mla_kw = dict(S=S, T=S, H=MLA_HEADS, dqk=2 * LANES, dv=MLA_V, q_col0=0, k_col0=0, v_col0=0,
                  scale=(MLA_NOPE + MLA_ROPE) ** -0.5, tq=2048, tk=2048)
    if comm is None:
        y_mla, lse_mla = _attn_fwd(qc, kc, vc, name="mla_attn", **mla_kw)
    else:
        (y_mla, lse_mla), got = _attn_fwd(qc, kc, vc, exch=comm.gather_ffn, name="mla_attn", **mla_kw)
        W = {**W, **comm.weights(comm.ffn_names, got)}

    kvm = _mm(mem, W["mem_w_kv"], M, 2 * MW, D, name="mem_kv")
    mem_kw = dict(S=S, T=M, H=MEM_HEADS, dqk=MHD, dv=MHD, q_col0=c_qm // MHD, k_col0=0, v_col0=MEM_HEADS,
                  scale=MHD ** -0.5, tq=1024, tk=M)
    assert c_qm % MHD == 0
    y_mem, lse_mem = _attn_fwd(P, kvm, kvm, name="mem_attn", **mem_kw)

    ys = (y_hg, y_mla, y_mem)
    us = [_mm(ys[b], W["w_branch"][b], S, D, HW, name=f"branch{b}") for b in range(N_BRANCH)]
    TCW = _tile(D, 1024, LANES)
    ncw = D // TCW

    def merge_fwd(g0, g1_, g2_, u0, u1, u2):
        return _sigmoid(g0) * u0 + _sigmoid(g1_) * u1 + _sigmoid(g2_) * u2

    gate_rows = [_Rows(P, TCW, (c_gate + b * D) // TCW) for b in range(N_BRANCH)]
    assert c_gate % TCW == 0
    (merged,) = _rowwise(merge_fwd, gate_rows + [_Rows(u, TCW) for u in us], [],
                         [_Rows(D, TCW)], [], R=S, tr=TR, ncol=ncw, name="merge_fwd")
    mix = _mm(merged, W["w_o"], S, D, D, tk=D, name="out_proj")

    def ln_res(hp, addv, g, b):
        z = ALPHA * hp + addv
        return z, _ln_stats(z)[0] * g + b

    z1, h1 = _rowwise(ln_res, [_Rows(h0, D), _Rows(mix, D)], [g1, b1],
                      [_Rows(D, D), _Rows(D, D)], [], R=S, tr=TR, name="ln1")

    TF = _tile(DFF, 512, LANES)

    def swiglu(abv):
        a, b = abv[:, :TF], abv[:, TF:]
        return [abv, a * _sigmoid(a) * b]

    ab, cff = _mm(h1, W["w_ffn_gu"], S, 2 * DFF, D, tn=2 * TF, tk=D, epilogue=(swiglu, [], [1, (1, 2)]), name="ffn_gu")
    ff = _mm(cff, W["w_ffn_down"], S, D, DFF, name="ffn_down")

    def loss_bwd(hp, addv, tgt, g, b):
        z = ALPHA * hp + addv
        xhat, rstd = _ln_stats(z)
        y = xhat * g + b
        err = y - tgt
        dy = err * (1.0 / D)
        dxh = dy * g
        m1 = jnp.mean(dxh, axis=-1, keepdims=True)
        m2 = jnp.mean(dxh * xhat, axis=-1, keepdims=True)
        dz = rstd * (dxh - m1 - xhat * m2)
        lrow = jnp.sum(_colsum(err * err), axis=-1, keepdims=True) * (0.5 / D)
        return dz, _colsum(dy * xhat), _colsum(dy), lrow

    dz2, dg2, db2, loss = _rowwise(loss_bwd, [_Rows(h1, D), _Rows(ff, D), _Rows(loss_target, D)], [g2, b2],
                                   [_Rows(D, D)], [D, D, 1], R=S, tr=TR, name="loss_ln2_bwd")
    def swiglu_bwd(dc, abv):
        a, b = abv[:, :TF], abv[:, TF:]
        sg = _sigmoid(a)
        return [jnp.concatenate([dc * b * sg * (1.0 + a * (1.0 - sg)), dc * a * sg], axis=1)]

    (dab,) = _mm(dz2, W["w_ffn_down"], S, DFF, D, tb=True, tn=TF, epilogue=(swiglu_bwd, [(ab, (2, 1))], [(2, 1)]),
                 name="ffn_down_dx")
    g_ffn_down = _mm(cff, dz2, DFF, D, S, ta=True, name="ffn_down_dw")
    dh1 = _mm(dab, W["w_ffn_gu"], S, D, 2 * DFF, tb=True, name="ffn_gu_dx")
    g_ffn_gu = _mm(h1, dab, D, 2 * DFF, S, ta=True, name="ffn_gu_dw")

    def ln1_bwd(z, dmm, dz2v, g):
        return _ln_bwd_core(z, g, ALPHA * dz2v + dmm)

    dz1, dg1, db1 = _rowwise(ln1_bwd, [_Rows(z1, D), _Rows(dh1, D), _Rows(dz2, D)], [g1],
                             [_Rows(D, D)], [D, D], R=S, tr=TR, name="ln1_bwd")
    dmerged = _mm(dz1, W["w_o"], S, D, D, tb=True, name="out_proj_dx")
    g_w_o = _mm(merged, dz1, D, D, S, ta=True, name="out_proj_dw")

    def merge_bwd(g0, g1_, g2_, u0, u1, u2, dm):
        res_g, res_u = [], []
        for gv, uv in ((g0, u0), (g1_, u1), (g2_, u2)):
            sg = _sigmoid(gv)
            res_g.append(dm * uv * sg * (1.0 - sg))
            res_u.append(dm * sg)
        return (*res_g, *res_u)

    mres = _rowwise(merge_bwd, gate_rows + [_Rows(u, TCW) for u in us] + [_Rows(dmerged, TCW)], [],
                    [_Rows(D, TCW)] * (2 * N_BRANCH), [], R=S, tr=TR, ncol=ncw, name="merge_bwd")
    dgates, dus = mres[:N_BRANCH], mres[N_BRANCH:]
    dys = [_mm(dus[b], W["w_branch"][b], S, HW, D, tb=True, name=f"branch{b}_dx") for b in range(N_BRANCH)]
    g_w_branch = [_mm(ys[b], dus[b], HW, D, S, ta=True, name=f"branch{b}_dw") for b in range(N_BRANCH)]

    dq_mem, dk_mem, dv_mem = _attn_bwd(P, kvm, kvm, y_mem, lse_mem, dys[2], name="mem_attn_bwd", **mem_kw)
    dkvm = jnp.concatenate([dk_mem, dv_mem], axis=1)
    g_mem_w_kv = _mm(mem, dkvm, D, 2 * MW, M, ta=True, name="mem_kv_dw")

    g_w_branch = jnp.stack(g_w_branch)
    delivered = {}
    if comm is None:
        dqc, dkc, dvv = _attn_bwd(qc, kc, vc, y_mla, lse_mla, dys[1], name="mla_attn_bwd", **mla_kw)
    else:
        exch = comm.scatter(dict(w_ffn_gu=g_ffn_gu, w_ffn_down=g_ffn_down, w_o=g_w_o, w_branch=g_w_branch,
                                 mem_w_kv=g_mem_w_kv))
        (dqc, dkc, dvv), got = _attn_bwd(qc, kc, vc, y_mla, lse_mla, dys[1], exch=exch, name="mla_attn_bwd",
                                         **mla_kw)
        delivered.update(zip(exch.names, got))

    def rope_bwd(dqb, dkb, tcb, tab, tbb):
        qo, kn = [], []
        dkr = jnp.zeros_like(tcb)
        for h in range(MLA_HEADS):
            qo += [dqb[:, 2 * h * LANES:(2 * h + 1) * LANES],
                   _rope_apply_t(dqb[:, (2 * h + 1) * LANES:(2 * h + 2) * LANES], tcb, tab, tbb)]
            kn.append(dkb[:, 2 * h * LANES:(2 * h + 1) * LANES])
            dkr = dkr + dkb[:, (2 * h + 1) * LANES:(2 * h + 2) * LANES]
        return jnp.concatenate(qo, axis=1), jnp.concatenate(kn, axis=1), _rope_apply_t(dkr, tcb, tab, tbb)

    dq_raw, dkn, dkr_raw = _rowwise(rope_bwd, [_Rows(dqc, QW), _Rows(dkc, QW), _Rows(tc, LANES),
                                               _Rows(ta, LANES), _Rows(tb, LANES)], [],
                                    [_Rows(QW, QW), _Rows(VW, VW), _Rows(LANES, LANES)], [],
                                    R=S, tr=TR, name="rope_bwd")
    dkv = jnp.concatenate([dkn, dvv], axis=1)
    dcqn = _mm(dq_raw, W["mla_w_uq"], S, QR, QW, tb=True, name="mla_uq_dx")
    g_mla_w_uq = _mm(cqn, dq_raw, QR, QW, S, ta=True, name="mla_uq_dw")
    dckvn = _mm(dkv, W["mla_w_ukv"], S, KR, 2 * VW, tb=True, name="mla_ukv_dx")
    g_mla_w_ukv = _mm(ckvn, dkv, KR, 2 * VW, S, ta=True, name="mla_ukv_dw")

    def mla_norm_bwd(cq, ckv, dq_, dk_, gq, gk):
        dcq, gq_rows = _rms_bwd(cq, gq, dq_, RMS_EPS)
        dck, gk_rows = _rms_bwd(ckv, gk, dk_, RMS_EPS)
        return dcq, dck, _colsum(gq_rows), _colsum(gk_rows)

    dcq, dckv, dg_cq, dg_ckv = _rowwise(
        mla_norm_bwd, [_Rows(P, QR, c_cq // QR), _Rows(P, KR, c_ckv // KR), _Rows(dcqn, QR), _Rows(dckvn, KR)],
        [small["mla_g_cq"], small["mla_g_ckv"]], [_Rows(QR, QR), _Rows(KR, KR)], [QR, KR],
        R=S, tr=TR, name="mla_norm_bwd")

    def hg_post_bwd(of, ob, gr, dy, ng):
        o = of + ob
        sg = _sigmoid(gr)
        do_, dgr = [], []
        dng = jnp.zeros((1, HG_D), F32)
        for h in range(HG_HEADS):
            sl = slice(h * HG_D, (h + 1) * HG_D)
            t = _rms_fwd(o[:, sl], ng, RMS_EPS)
            dgr.append(dy[:, sl] * t * sg[:, sl] * (1.0 - sg[:, sl]))
            dx, grow = _rms_bwd(o[:, sl], ng, dy[:, sl] * sg[:, sl], RMS_EPS)
            do_.append(dx)
            dng = dng + _colsum(grow)
        return jnp.concatenate(do_, axis=1), jnp.concatenate(dgr, axis=1), dng

    do_hg, dg_hg, dng = _rowwise(hg_post_bwd, [_Rows(o_fw, HW), _Rows(o_bw, HW), _Rows(P, HW, 4), _Rows(dys[0], HW)],
                                 [small["hgrn_norm_g"]], [_Rows(HW, HW), _Rows(HW, HW)], [HG_D],
                                 R=S, tr=TR, name="hgrn_post_bwd")
    dq_f, dv_f, dff_fw, dlb_f = _hgrn_scan_bwd(P, lb[0:1], st_fw, a_fw, do_hg, S=S, rev=False, name="hgrn_fw_bwd")
    dq_b, dv_b, dff_bw, dlb_b = _hgrn_scan_bwd(P, lb[1:2], st_bw, a_bw, do_hg, S=S, rev=True, name="hgrn_bw_bwd")
    THW = _tile(HW, 1024, LANES)
    dq_hg, dv_hg = _rowwise(lambda a, b, c, d: (a + b, c + d),
                            [_Rows(dq_f, THW), _Rows(dq_b, THW), _Rows(dv_f, THW), _Rows(dv_b, THW)], [],
                            [_Rows(HW, THW), _Rows(HW, THW)], [], R=S, tr=TR, ncol=HW // THW, name="hgrn_dir_sum")

    dP = jnp.concatenate([dq_hg, dv_hg, dff_fw, dff_bw, dg_hg, dcq, dckv, dq_mem, *dgates, dkr_raw], axis=1)
    g_w_in = _mm(h0, dP, D, PW, S, ta=True, tn=896, name="proj_in_dw")
    dx_kw = dict(tb=True, tk=_tile(PW, 640, LANES), name="proj_in_dx")
    if comm is None:
        dh0 = _mm(dP, w_in_k, S, D, PW, **dx_kw)
    else:
        exch = comm.scatter(dict(w_in=g_w_in, mla_w_uq=g_mla_w_uq, mla_w_ukv=g_mla_w_ukv))
        dh0, got = _mm(dP, w_in_k, S, D, PW, exch=exch, **dx_kw)
        delivered.update(zip(exch.names, got))

    def ln0_bwd(z, dmm, dz1v, g):
        return _ln_bwd_core(z, g, ALPHA * dz1v + dmm)

    grad_x, dge, dbe = _rowwise(ln0_bwd, [_Rows(x, D), _Rows(dh0, D), _Rows(dz1, D)], [ge],
                                [_Rows(D, D)], [D, D], R=S, tr=TR, name="ln_emb_bwd")

    big = dict(w_in=g_w_in, mla_w_uq=g_mla_w_uq, mla_w_ukv=g_mla_w_ukv, mem_w_kv=g_mem_w_kv,
               w_branch=g_w_branch, w_o=g_w_o, w_ffn_gu=g_ffn_gu, w_ffn_down=g_ffn_down)
    sm = dict(ln_emb_g=dge, ln_emb_b=dbe, dlb=jnp.concatenate([dlb_f, dlb_b], axis=0), hgrn_norm_g=dng,
              mla_g_cq=dg_cq, mla_g_ckv=dg_ckv, ln1_g=dg1, ln1_b=db1, ln2_g=dg2, ln2_b=db2)
    return loss, grad_x, big, sm, delivered


def _gather_exch(shards):
    copies = [(m, lambda x, y, c: 0, _chip) for m in _CHIP_MASKS]
    return _Exch([s[None] for s in shards], 4, copies, [(lambda x, y, c: 0, _chip)])


def _scatter_exch(pieces):
    copies = [(m, (lambda x, y, c, m=m: 2 * _chip(x ^ m[0], y ^ m[1], c) + (c ^ m[2])), _device) for m in _ALL_MASKS]
    local = [((lambda x, y, c: 2 * _chip(x, y, c) + c), _device)]
    return _Exch([p.reshape((8,) + p.shape[2:]) for p in pieces], 8, copies, local)


SHARE_BLOCK_BYTES = 4 << 20


def _sum_share(arr, *, name):
    n, rh, w = arr.shape
    tr = _tile(rh, max(16, SHARE_BLOCK_BYTES // (n * w * arr.dtype.itemsize) // 16 * 16), 16)
    nb = rh // tr

    def body(a_ref, o_ref, slots, send_sems, recv_sem, local_sems):
        i = pl.program_id(0)
        x, y, c = _coords()
        sibling = (x, y, 1 - c)

        def pushes(step, slot):
            rows = pl.ds(pl.multiple_of(c * rh + step * tr, SUBLANES), tr)
            return (pltpu.make_async_copy(slots.at[slot], o_ref.at[rows], local_sems.at[slot]),
                    pltpu.make_async_remote_copy(src_ref=slots.at[slot], dst_ref=o_ref.at[rows],
                                                 send_sem=send_sems.at[slot], recv_sem=recv_sem,
                                                 device_id=sibling, device_id_type=MESH))

        def drain(step, slot):
            loc, rem = pushes(step, slot)
            loc.wait()
            rem.wait_send()

        slot = i % 2

        @pl.when(i >= 2)
        def _():
            drain(i - 2, slot)

        acc = a_ref[0].astype(F32)
        for k in range(1, n):
            acc = acc + a_ref[k].astype(F32)
        slots[slot] = acc
        loc, rem = pushes(i, slot)
        loc.start()
        rem.start()

        @pl.when(i == nb - 1)
        def _():
            if nb >= 2:
                drain(i - 1, 1 - slot)
            drain(i, slot)
            other = o_ref.at[pl.ds(pl.multiple_of((1 - c) * rh, SUBLANES), rh)]
            pltpu.make_async_remote_copy(src_ref=other, dst_ref=other, send_sem=send_sems.at[0], recv_sem=recv_sem,
                                         device_id=sibling, device_id_type=MESH).wait_recv()

    return pl.pallas_call(
        body, name=name, grid=(nb,),
        in_specs=[pl.BlockSpec((n, tr, w), lambda i: (0, i, 0))],
        out_specs=pl.BlockSpec(memory_space=pl.ANY),
        out_shape=jax.ShapeDtypeStruct((2 * rh, w), F32),
        scratch_shapes=[pltpu.VMEM((2, tr, w), F32), pltpu.SemaphoreType.DMA((2,)), pltpu.SemaphoreType.DMA,
                        pltpu.SemaphoreType.DMA((2,))],
        compiler_params=pltpu.CompilerParams(dimension_semantics=("arbitrary",), has_side_effects=True,
                                             vmem_limit_bytes=VMEM_LIMIT),
    )(arr)


def _allreduce_small(v, *, name):
    r, w = v.shape

    def body(v_ref, o_ref, buf, send_sems, recv_sems):
        x, y, c = _coords()
        me = 4 * x + 2 * y + c
        buf[me] = v_ref[...]
        cps = []
        for k in range(7):
            m = ((k + 1) >> 2 & 1, (k + 1) >> 1 & 1, (k + 1) & 1)
            cp = pltpu.make_async_remote_copy(
                src_ref=v_ref, dst_ref=buf.at[me], send_sem=send_sems.at[k], recv_sem=recv_sems.at[k],
                device_id=(x ^ m[0], y ^ m[1], c ^ m[2]), device_id_type=MESH)
            cp.start()
            cps.append(cp)
        for cp in cps:
            cp.wait_recv()
        for cp in cps:
            cp.wait_send()
        acc = buf[0]
        for k in range(1, 8):
            acc = acc + buf[k]
        o_ref[...] = acc

    return pl.pallas_call(
        body, name=name,
        in_specs=[pl.BlockSpec(memory_space=pltpu.VMEM)],
        out_specs=pl.BlockSpec(memory_space=pltpu.VMEM),
        out_shape=jax.ShapeDtypeStruct((r, w), F32),
        scratch_shapes=[pltpu.VMEM((8, r, w), F32), pltpu.SemaphoreType.DMA((7,)), pltpu.SemaphoreType.DMA((7,))],
        compiler_params=pltpu.CompilerParams(has_side_effects=True),
    )(v)


_BIG = (("w_in", 1), ("mla_w_uq", 1), ("mla_w_ukv", 1), ("mem_w_kv", 0), ("w_branch", 1), ("w_o", 0),
        ("w_ffn_gate", 1), ("w_ffn_up", 1), ("w_ffn_down", 0))


def _assemble(gathered, ax):
    _, r, c = gathered.shape
    if ax == 0:
        return gathered.reshape(4 * r, c)
    return jnp.concatenate([gathered[j] for j in range(4)], axis=1)


def _split_pieces(g, ax):
    r, c = g.shape
    if ax == 0:
        return g.reshape(4, 2, r // 8, c).astype(BF16)
    rh, cs = r // 2, c // 4
    return jnp.stack([g[h * rh:(h + 1) * rh, j * cs:(j + 1) * cs].astype(BF16)
                      for j in range(4) for h in range(2)]).reshape(4, 2, rh, cs)


def _pad_cols(a, n):
    return jnp.pad(a, ((0, 0), (0, n - a.shape[1])))


def _to_kernel_layout(full, QR, KR):
    out = {}
    for n in ("mem_w_kv", "w_branch", "w_o", "w_ffn_down"):
        if n in full:
            out[n] = full[n]
    if "w_in" in full:
        w_in = full["w_in"]
        a = 5 * HG_HEADS * HG_D + QR + KR
        out["w_in"] = jnp.concatenate([w_in[:, :a], w_in[:, a + MLA_ROPE:], _pad_cols(w_in[:, a:a + MLA_ROPE], LANES)],
                                      axis=1)
    if "mla_w_uq" in full:
        uq = full["mla_w_uq"].reshape(QR, MLA_HEADS, MLA_NOPE + MLA_ROPE)
        out["mla_w_uq"] = jnp.pad(uq, ((0, 0), (0, 0), (0, 2 * LANES - MLA_NOPE - MLA_ROPE))).reshape(QR, -1)
    if "mla_w_ukv" in full:
        ukv = full["mla_w_ukv"].reshape(KR, MLA_HEADS, MLA_NOPE + MLA_V)
        out["mla_w_ukv"] = jnp.concatenate([ukv[:, :, :MLA_NOPE].reshape(KR, -1), ukv[:, :, MLA_NOPE:].reshape(KR, -1)],
                                           axis=1)
    if "w_ffn_gate" in full:
        gate, up = full["w_ffn_gate"], full["w_ffn_up"]
        DFF = gate.shape[1]
        TF = _tile(DFF, 512, LANES)
        blocks = []
        for j in range(DFF // TF):
            blocks += [gate[:, j * TF:(j + 1) * TF], up[:, j * TF:(j + 1) * TF]]
        out["w_ffn_gu"] = jnp.concatenate(blocks, axis=1)
    return out


def _from_kernel_layout(gk, QR, KR):
    out = {}
    for n in ("mem_w_kv", "w_o", "w_ffn_down"):
        if n in gk:
            out[n] = gk[n]
    if "w_branch" in gk:
        out["w_branch"] = gk["w_branch"].reshape(-1, gk["w_branch"].shape[-1])
    if "w_in" in gk:
        g = gk["w_in"]
        a = 5 * HG_HEADS * HG_D + QR + KR
        rest = g.shape[1] - LANES - a
        out["w_in"] = jnp.concatenate([g[:, :a], g[:, a + rest:a + rest + MLA_ROPE], g[:, a:a + rest]], axis=1)
    if "mla_w_uq" in gk:
        out["mla_w_uq"] = gk["mla_w_uq"].reshape(QR, MLA_HEADS, 2 * LANES)[:, :, :MLA_NOPE + MLA_ROPE].reshape(QR, -1)
    if "mla_w_ukv" in gk:
        VW = MLA_HEADS * MLA_V
        g = gk["mla_w_ukv"]
        out["mla_w_ukv"] = jnp.concatenate([g[:, :VW].reshape(KR, MLA_HEADS, MLA_NOPE),
                                            g[:, VW:].reshape(KR, MLA_HEADS, MLA_V)], axis=2).reshape(KR, -1)
    if "w_ffn_gu" in gk:
        g = gk["w_ffn_gu"]
        DFF = g.shape[1] // 2
        TF = _tile(DFF, 512, LANES)
        out["w_ffn_gate"] = jnp.concatenate([g[:, 2 * j * TF:(2 * j + 1) * TF] for j in range(DFF // TF)], axis=1)
        out["w_ffn_up"] = jnp.concatenate([g[:, (2 * j + 1) * TF:(2 * j + 2) * TF] for j in range(DFF // TF)], axis=1)
    return out


def _adamw(w, g, m, v, *, name):
    r, c = w.shape
    tr = max(SUBLANES, min(512, (1 << 20) // (4 * c)) // SUBLANES * SUBLANES)
    c1 = 1.0 / (1.0 - ADAM_B1 ** ADAM_STEP)
    c2 = 1.0 / (1.0 - ADAM_B2 ** ADAM_STEP)

    def fn(wv, gv, mv, vv):
        mn = ADAM_B1 * mv + (1.0 - ADAM_B1) * gv
        vn = ADAM_B2 * vv + (1.0 - ADAM_B2) * (gv * gv)
        delta = -ADAM_LR * ((mn * c1) / (jnp.sqrt(vn * c2) + ADAM_EPS) + ADAM_WD * wv)
        return delta, mn, vn

    return _rowwise(fn, [_Rows(a, c) for a in (w, g, m, v)], [], [_Rows(c, c)] * 3, [], R=r, tr=tr, name=name)


_SMALL = ("ln_emb_g", "ln_emb_b", "hgrn_lb_logits", "hgrn_norm_g", "mla_g_cq", "mla_g_ckv",
          "ln1_g", "ln1_b", "ln2_g", "ln2_b")


def _lb_from_logits(logits):
    return jnp.cumsum(jax.nn.softmax(logits, axis=1), axis=1)[:, 0]


def _small_rows(parts):
    flat = jnp.concatenate([p.reshape(-1) for p in parts])
    n = flat.shape[0]
    total = -(-n // (SUBLANES * LANES)) * SUBLANES * LANES
    return jnp.pad(flat, (0, total - n)).reshape(total // LANES, LANES)


def kernel(x, mem, positions, ln_emb_g, ln_emb_b, hgrn_lb_logits, w_in, hgrn_norm_g, mla_g_cq, mla_g_ckv, mla_w_uq, mla_w_ukv, mem_w_kv, w_branch, w_o, ln1_g, ln1_b, w_ffn_gate, w_ffn_up, w_ffn_down, ln2_g, ln2_b, loss_target, m_ln_emb_g, m_ln_emb_b, m_hgrn_lb_logits, m_w_in, m_hgrn_norm_g, m_mla_g_cq, m_mla_g_ckv, m_mla_w_uq, m_mla_w_ukv, m_mem_w_kv, m_w_branch, m_w_o, m_ln1_g, m_ln1_b, m_w_ffn_gate, m_w_ffn_up, m_w_ffn_down, m_ln2_g, m_ln2_b, v_ln_emb_g, v_ln_emb_b, v_hgrn_lb_logits, v_w_in, v_hgrn_norm_g, v_mla_g_cq, v_mla_g_ckv, v_mla_w_uq, v_mla_w_ukv, v_mem_w_kv, v_w_branch, v_w_o, v_ln1_g, v_ln1_b, v_w_ffn_gate, v_w_ffn_up, v_w_ffn_down, v_ln2_g, v_ln2_b):
    names = ["ln_emb_g", "ln_emb_b", "hgrn_lb_logits", "w_in", "hgrn_norm_g", "mla_g_cq", "mla_g_ckv", "mla_w_uq",
             "mla_w_ukv", "mem_w_kv", "w_branch", "w_o", "ln1_g", "ln1_b", "w_ffn_gate", "w_ffn_up", "w_ffn_down",
             "ln2_g", "ln2_b"]
    wts = dict(zip(names, [ln_emb_g, ln_emb_b, hgrn_lb_logits, w_in, hgrn_norm_g, mla_g_cq, mla_g_ckv, mla_w_uq,
                           mla_w_ukv, mem_w_kv, w_branch, w_o, ln1_g, ln1_b, w_ffn_gate, w_ffn_up, w_ffn_down,
                           ln2_g, ln2_b]))
    mom = dict(zip(names, [m_ln_emb_g, m_ln_emb_b, m_hgrn_lb_logits, m_w_in, m_hgrn_norm_g, m_mla_g_cq, m_mla_g_ckv,
                           m_mla_w_uq, m_mla_w_ukv, m_mem_w_kv, m_w_branch, m_w_o, m_ln1_g, m_ln1_b, m_w_ffn_gate,
                           m_w_ffn_up, m_w_ffn_down, m_ln2_g, m_ln2_b]))
    var = dict(zip(names, [v_ln_emb_g, v_ln_emb_b, v_hgrn_lb_logits, v_w_in, v_hgrn_norm_g, v_mla_g_cq, v_mla_g_ckv,
                           v_mla_w_uq, v_mla_w_ukv, v_mem_w_kv, v_w_branch, v_w_o, v_ln1_g, v_ln1_b, v_w_ffn_gate,
                           v_w_ffn_up, v_w_ffn_down, v_ln2_g, v_ln2_b]))
    xc, yc, cc = _coords()
    chip = _chip(xc, yc, cc)
    S, D = x.shape[1], x.shape[2]

    axis = dict(_BIG)
    shard = lambda n: wts[n].reshape(-1, wts[n].shape[-1]).astype(BF16)
    QR, KR = mla_w_uq.shape[1], mla_w_ukv.shape[1]
    MW, DFF = mem_w_kv.shape[2] // 2, 4 * w_ffn_gate.shape[2]
    rows_of_w_in = lambda got: _to_kernel_layout(dict(w_in=_assemble(got[0], axis["w_in"])), QR, KR)["w_in"]
    w_in_shard = shard("w_in")

    class _Comm:
        w_in_top = rows_of_w_in(_exchange(_gather_exch([w_in_shard[:D // 2]]), name="gather_w_in_top"))
        gather_bottom = _gather_exch([w_in_shard[D // 2:]])
        w_in_rows = staticmethod(rows_of_w_in)
        ffn_names = ["w_ffn_gate", "w_ffn_up", "w_ffn_down"]
        mix_names = [n for n, _ in _BIG if n != "w_in" and not n.startswith("w_ffn")]
        gather_mix = _gather_exch([shard(n) for n in mix_names])
        gather_ffn = _gather_exch([shard(n) for n in ffn_names])

        @staticmethod
        def weights(names, received):
            full = {n: _assemble(g, axis[n]) for n, g in zip(names, received)}
            if "w_branch" in full:
                full["w_branch"] = full["w_branch"].reshape(N_BRANCH, -1, D)
            return _to_kernel_layout(full, QR, KR)

        @staticmethod
        def scatter(gk_part):
            gpart = _from_kernel_layout(gk_part, QR, KR)
            exch = _scatter_exch([_split_pieces(gpart[n], axis[n]) for n in gpart])
            exch.names = list(gpart)
            return exch

    lsh = hgrn_lb_logits.shape
    HW = 4 * lsh[2]
    placed = lax.dynamic_update_slice(jnp.zeros((lsh[0], lsh[1], HW), F32), hgrn_lb_logits, (0, 0, chip * lsh[2]))
    placed = jnp.where(cc == 0, placed, 0.0)
    logits = _allreduce_small(_small_rows([placed]), name="gather_logits").reshape(-1)[:placed.size].reshape(placed.shape)
    lb, lb_vjp = jax.vjp(_lb_from_logits, logits)

    small = dict(ln_emb_g=ln_emb_g, ln_emb_b=ln_emb_b, lb=lb, hgrn_norm_g=hgrn_norm_g, mla_g_cq=mla_g_cq,
                 mla_g_ckv=mla_g_ckv, ln1_g=ln1_g, ln1_b=ln1_b, ln2_g=ln2_g, ln2_b=ln2_b)
    loss_l, grad_x, _, gs, delivered = _local_step(x[0], mem[0], positions[0], loss_target[0], None, small, None,
                                                   MW, DFF, comm=_Comm)

    (dlogits,) = lb_vjp(gs["dlb"])
    sm_parts = [loss_l, gs["ln_emb_g"], gs["ln_emb_b"], dlogits, gs["hgrn_norm_g"], gs["mla_g_cq"], gs["mla_g_ckv"],
                gs["ln1_g"], gs["ln1_b"], gs["ln2_g"], gs["ln2_b"]]
    red = _allreduce_small(_small_rows(sm_parts), name="allreduce_small").reshape(-1)
    sm_out, off = [], 0
    for p in sm_parts:
        sm_out.append(red[off:off + p.size].reshape(p.shape))
        off += p.size
    loss = sm_out[0].reshape(())
    g_small = dict(zip(_SMALL, sm_out[1:]))
    g_small["hgrn_lb_logits"] = lax.dynamic_slice(g_small["hgrn_lb_logits"], (0, 0, chip * lsh[2]), lsh)
    for n in _SMALL:
        g_small[n] = g_small[n].reshape(wts[n].shape)

    g_big = {n: _sum_share(delivered[n], name="rs_sum_" + n).reshape(wts[n].shape) for n, _ in _BIG}

    grads = {**g_small, **g_big}
    delta, new_m, new_v = {}, {}, {}
    for n, _ in _BIG:
        shp = wts[n].shape
        two_d = lambda a: a.reshape(-1, shp[-1])
        d_, m_, v_ = _adamw(two_d(wts[n]), two_d(grads[n]), two_d(mom[n]), two_d(var[n]), name="adamw_" + n)
        delta[n], new_m[n], new_v[n] = d_.reshape(shp), m_.reshape(shp), v_.reshape(shp)
    sw, sg_, sm_, sv_ = (_small_rows([d[n] for n in _SMALL]) for d in (wts, grads, mom, var))
    d_, m_, v_ = _adamw(sw, sg_, sm_, sv_, name="adamw_small")
    for res, packed_rows in ((delta, d_), (new_m, m_), (new_v, v_)):
        flat, off = packed_rows.reshape(-1), 0
        for n in _SMALL:
            res[n] = flat[off:off + wts[n].size].reshape(wts[n].shape)
            off += wts[n].size

    return (loss, grad_x[None], *[grads[n] for n in names], *[delta[n] for n in names],
            *[new_m[n] for n in names], *[new_v[n] for n in names])
```

```python
import jax
import jax.numpy as jnp
from jax import lax
from jax.experimental import pallas as pl
from jax.experimental.pallas import tpu as pltpu

F32 = jnp.float32
BF16 = jnp.bfloat16

HG_HEADS = 8
HG_D = 128
MLA_HEADS = 8
MLA_NOPE = 128
MLA_ROPE = 64
MLA_V = 128
MEM_HEADS = 4
N_BRANCH = 3
ROPE_THETA = 10000.0
DEPTH = 1
ALPHA = (2.0 * DEPTH) ** 0.25
LN_EPS = 1e-5
RMS_EPS = 1e-6
ADAM_LR = 0.001
ADAM_B1 = 0.9
ADAM_B2 = 0.999
ADAM_EPS = 1e-08
ADAM_WD = 0.01
ADAM_STEP = 10

LANES = 128
SUBLANES = 8
VMEM_LIMIT = 48 * 1024 * 1024

HG_CHUNK = 128
HG_SUB = 16
HG_PAIR = 2

MESH = pl.DeviceIdType.MESH
HI = lax.Precision.HIGHEST
HG_OFF_PREC = lax.Precision.DEFAULT


def _cparams(sem=None):
    if sem is None:
        return pltpu.CompilerParams(vmem_limit_bytes=VMEM_LIMIT)
    return pltpu.CompilerParams(dimension_semantics=sem, vmem_limit_bytes=VMEM_LIMIT)


def _tile(dim, pref, quantum):
    t = min(pref, dim) // quantum * quantum
    while t >= quantum:
        if dim % t == 0:
            return t
        t -= quantum
    return dim


def _sigmoid(x):
    return 1.0 / (1.0 + jnp.exp(-x))


def _coords():
    return lax.axis_index("x"), lax.axis_index("y"), lax.axis_index("c")


def _chip(x, y, c):
    return 2 * x + y


def _device(x, y, c):
    return 4 * x + 2 * y + c


_CHIP_MASKS = ((1, 0, 0), (0, 1, 0), (1, 1, 0))
_ALL_MASKS = tuple((k >> 2 & 1, k >> 1 & 1, k & 1) for k in range(1, 8))
_HBM = pl.BlockSpec(memory_space=pl.ANY)


class _Exch:
    def __init__(self, srcs, n_dst, copies, local_copies):
        self.srcs, self.n_dst, self.copies, self.local_copies = list(srcs), n_dst, copies, local_copies
        self.n = len(self.srcs)

    def out_shape(self):
        return [jax.ShapeDtypeStruct((self.n_dst,) + s.shape[1:], s.dtype) for s in self.srcs]

    def scratch(self):
        n_rc, n_lc = self.n * len(self.copies), self.n * len(self.local_copies)
        return [pltpu.SemaphoreType.DMA((n_rc,)), pltpu.SemaphoreType.DMA((n_rc,)),
                pltpu.SemaphoreType.DMA((max(n_lc, 1),))]

    def _descriptors(self, src_refs, dst_refs, sems):
        send_sems, recv_sems, local_sems = sems
        x, y, c = _coords()
        n_rc, n_lc = len(self.copies), len(self.local_copies)
        remote, local = [], []
        for a in range(self.n):
            for k, (mask, sidx, didx) in enumerate(self.copies):
                remote.append(pltpu.make_async_remote_copy(
                    src_ref=src_refs[a].at[sidx(x, y, c)], dst_ref=dst_refs[a].at[didx(x, y, c)],
                    send_sem=send_sems.at[a * n_rc + k], recv_sem=recv_sems.at[a * n_rc + k],
                    device_id=(x ^ mask[0], y ^ mask[1], c ^ mask[2]), device_id_type=MESH))
            for k, (sidx, didx) in enumerate(self.local_copies):
                local.append(pltpu.make_async_copy(src_refs[a].at[sidx(x, y, c)], dst_refs[a].at[didx(x, y, c)],
                                                   local_sems.at[a * n_lc + k]))
        return remote, local

    def start(self, src_refs, dst_refs, sems):
        remote, local = self._descriptors(src_refs, dst_refs, sems)
        for cp in remote + local:
            cp.start()

    def wait(self, src_refs, dst_refs, sems):
        remote, local = self._descriptors(src_refs, dst_refs, sems)
        for cp in remote:
            cp.wait_recv()
        for cp in remote:
            cp.wait_send()
        for cp in local:
            cp.wait()


def _exchange(exch, *, name):
    n = exch.n

    def body(*refs):
        src_refs, dst_refs, sems = refs[:n], refs[n:2 * n], refs[2 * n:]
        exch.start(src_refs, dst_refs, sems)
        exch.wait(src_refs, dst_refs, sems)

    return pl.pallas_call(
        body, name=name, in_specs=[_HBM] * n, out_specs=[_HBM] * n, out_shape=exch.out_shape(),
        scratch_shapes=exch.scratch(), compiler_params=pltpu.CompilerParams(has_side_effects=True),
    )(*exch.srcs)


def _carried(call, exch, grid, in_specs, out_specs, out_shape, scratch_shapes, args, *, name):
    n_in, n_out, n_scr = len(in_specs), len(out_specs), len(scratch_shapes)
    n = 0 if exch is None else exch.n

    def body(*refs):
        o0 = n_in + n
        s0 = o0 + n_out + n
        ins, srcs = refs[:n_in], refs[n_in:o0]
        outs, dsts = refs[o0:o0 + n_out], refs[o0 + n_out:s0]
        scr, sems = refs[s0:s0 + n_scr], refs[s0 + n_scr:]
        if exch is not None:
            ids = [pl.program_id(d) for d in range(len(grid))]
            first = _all([i == 0 for i in ids])
            last = _all([i == g - 1 for i, g in zip(ids, grid)])

            @pl.when(first)
            def _():
                exch.start(srcs, dsts, sems)

        call(*ins, *outs, *scr)
        if exch is not None:
            @pl.when(last)
            def _():
                exch.wait(srcs, dsts, sems)

    if exch is None:
        params = pltpu.CompilerParams(dimension_semantics=("arbitrary",) * len(grid), vmem_limit_bytes=VMEM_LIMIT)
        extra_in, extra_out, extra_shape, extra_scr, extra_args = [], [], [], [], []
    else:
        params = pltpu.CompilerParams(dimension_semantics=("arbitrary",) * len(grid), vmem_limit_bytes=VMEM_LIMIT,
                                      has_side_effects=True)
        extra_in, extra_out, extra_shape = [_HBM] * n, [_HBM] * n, exch.out_shape()
        extra_scr, extra_args = exch.scratch(), exch.srcs
    res = pl.pallas_call(
        body, name=name, grid=grid, in_specs=list(in_specs) + extra_in, out_specs=list(out_specs) + extra_out,
        out_shape=list(out_shape) + extra_shape, scratch_shapes=list(scratch_shapes) + extra_scr,
        compiler_params=params,
    )(*args, *extra_args)
    return res[:n_out], res[n_out:]


def _all(conds):
    out = conds[0]
    for c in conds[1:]:
        out = jnp.logical_and(out, c)
    return out


def _mm(a, b, M, N, K, *, ta=False, tb=False, a_off=(0, 0), b_off=(0, 0), add=None, exch=None, epilogue=None,
        tm=1024, tn=1024, tk=1024, name):
    tm = _tile(M, tm, LANES if ta else SUBLANES)
    tn = _tile(N, tn, LANES)
    tk = _tile(K, tk, LANES)
    nk = K // tk
    ar, ac = a_off
    br, bc = b_off

    if ta:
        assert ar % tk == 0 and ac % tm == 0
        a_spec = pl.BlockSpec((tk, tm), lambda i, j, k: (ar // tk + k, ac // tm + i))
    else:
        assert ar % tm == 0 and ac % tk == 0
        a_spec = pl.BlockSpec((tm, tk), lambda i, j, k: (ar // tm + i, ac // tk + k))
    if tb:
        assert br % tn == 0 and bc % tk == 0
        b_spec = pl.BlockSpec((tn, tk), lambda i, j, k: (br // tn + j, bc // tk + k))
    else:
        assert br % tk == 0 and bc % tn == 0
        b_spec = pl.BlockSpec((tk, tn), lambda i, j, k: (br // tk + k, bc // tn + j))
    o_spec = pl.BlockSpec((tm, tn), lambda i, j, k: (i, j))
    mixed = a.dtype != b.dtype

    epi_fn, epi_ins, epi_outs = (None, [], [1]) if epilogue is None else epilogue
    n_in = 2 + (add is not None) + len(epi_ins)

    def body(*refs):
        a_ref, b_ref = refs[:2]
        add_ref = refs[2] if add is not None else None
        epi_refs = refs[n_in - len(epi_ins):n_in]
        o_refs, acc = refs[n_in:n_in + len(epi_outs)], refs[-1]
        k = pl.program_id(2)
        av = a_ref[...]
        bv = b_ref[...]
        if ta:
            av = av.astype(F32).T
        if mixed:
            av = av.astype(BF16)
            bv = bv.astype(BF16)
        dims = (((1,), (1 if tb else 0,)), ((), ()))
        d = lax.dot_general(av, bv, dims, preferred_element_type=F32)

        def finish(total):
            if add is not None:
                total = total + add_ref[...]
            tiles = [total] if epi_fn is None else epi_fn(total, *[r[...] for r in epi_refs])
            for o_ref, t in zip(o_refs, tiles):
                o_ref[...] = t

        if nk == 1:
            finish(d)
        else:
            @pl.when(k == 0)
            def _():
                acc[...] = d

            @pl.when(jnp.logical_and(k > 0, k < nk - 1))
            def _():
                acc[...] += d

            @pl.when(k == nk - 1)
            def _():
                finish(acc[...] + d)

    in_specs = [a_spec, b_spec]
    args = [a, b]
    if add is not None:
        in_specs.append(o_spec)
        args.append(add)
    wide = lambda w: pl.BlockSpec((tm, tn * w[0] // w[1]), lambda i, j, k: (i, j))
    for arr, w in epi_ins:
        in_specs.append(wide(w))
        args.append(arr)
    out_specs = [o_spec if w == 1 else wide(w) for w in epi_outs]
    out_shape = [jax.ShapeDtypeStruct((M, N if w == 1 else N * w[0] // w[1]), F32) for w in epi_outs]
    outs, received = _carried(body, exch, (M // tm, N // tn, nk), in_specs, out_specs, out_shape,
                              [pltpu.VMEM((tm, tn), F32)], args, name=name)
    outs = outs[0] if epilogue is None else outs
    return outs if exch is None else (outs, received)


class _Rows:
    def __init__(self, arr, width, col0=0, lead=None, dtype=F32):
        self.arr, self.width, self.col0, self.lead, self.dtype = arr, width, col0, lead, dtype


def _rowwise(fn, rows, consts, outs, accs, *, R, tr, ncol=1, name):
    tr = _tile(R, tr, SUBLANES)
    nrow = R // tr

    def spec(r):
        if r.lead is None:
            return pl.BlockSpec((tr, r.width), lambda j, i, c0=r.col0: (i, c0 + j))
        return pl.BlockSpec((None, tr, r.width), lambda j, i, c0=r.col0, l=r.lead: (l, i, c0 + j))

    in_specs = [spec(r) for r in rows]
    for c in consts:
        in_specs.append(pl.BlockSpec(c.shape, lambda j, i, nd=c.ndim: (0,) * nd))
    out_specs = [spec(o) for o in outs]
    out_shape = [jax.ShapeDtypeStruct((R, o.arr), o.dtype) for o in outs]
    for w in accs:
        out_specs.append(pl.BlockSpec((1, w), lambda j, i: (0, j)))
        out_shape.append(jax.ShapeDtypeStruct((1, w * ncol), F32))
    n_in = len(rows) + len(consts)
    n_out = len(outs)

    def body(*refs):
        ins = [r[...] for r in refs[:n_in]]
        res = fn(*ins)
        if not isinstance(res, (tuple, list)):
            res = (res,)
        for k in range(n_out):
            refs[n_in + k][...] = res[k].astype(refs[n_in + k].dtype)
        i = pl.program_id(1)
        for k in range(len(accs)):
            a_ref = refs[n_in + n_out + k]

            @pl.when(i == 0)
            def _(a_ref=a_ref):
                a_ref[...] = jnp.zeros_like(a_ref)

            a_ref[...] += res[n_out + k]

    res = pl.pallas_call(
        body, name=name, grid=(ncol, nrow),
        in_specs=in_specs, out_specs=out_specs, out_shape=out_shape,
        compiler_params=_cparams(("parallel", "arbitrary")),
    )(*[r.arr for r in rows], *consts)
    return res


def _colsum(x):
    return jnp.sum(x, axis=0, keepdims=True)


def _ln_stats(z):
    mu = jnp.mean(z, axis=-1, keepdims=True)
    zc = z - mu
    var = jnp.mean(zc * zc, axis=-1, keepdims=True)
    rstd = lax.rsqrt(var + LN_EPS)
    return zc * rstd, rstd


def _ln_bwd_core(z, g, dy):
    xhat, rstd = _ln_stats(z)
    dxh = dy * g
    m1 = jnp.mean(dxh, axis=-1, keepdims=True)
    m2 = jnp.mean(dxh * xhat, axis=-1, keepdims=True)
    dz = rstd * (dxh - m1 - xhat * m2)
    return dz, _colsum(dy * xhat), _colsum(dy)


def _rms_fwd(x, g, eps):
    r = lax.rsqrt(jnp.mean(x * x, axis=-1, keepdims=True) + eps)
    return x * r * g


def _rms_bwd(x, g, dy, eps):
    r = lax.rsqrt(jnp.mean(x * x, axis=-1, keepdims=True) + eps)
    xr = x * r
    dyg = dy * g
    dx = r * (dyg - xr * jnp.mean(dyg * xr, axis=-1, keepdims=True))
    return dx, dy * xr


def _hg_gate(fr, lb):
    sig = _sigmoid(fr)
    f = lb + (1.0 - lb) * sig
    return sig, f


def _hg_masks(rev):
    C = HG_CHUNK
    t = lax.broadcasted_iota(jnp.int32, (C, C), 0)
    s = lax.broadcasted_iota(jnp.int32, (C, C), 1)
    tri = (s >= t) if rev else (s <= t)
    return tri


def _hg_offdiag(Q, K, b, i, rev):
    C, sb = HG_CHUNK, HG_SUB
    nb = C // sb
    if (not rev and i == 0) or (rev and i == nb - 1):
        return None
    ref = b[sb * i - 1:sb * i] if not rev else b[sb * (i + 1):sb * (i + 1) + 1]
    srow = lax.broadcasted_iota(jnp.int32, (C, 1), 0)
    smask = (srow < sb * i) if not rev else (srow >= sb * (i + 1))
    qscale = jnp.exp(jnp.minimum(b - ref, 0.0))
    kscale = jnp.where(smask, jnp.exp(jnp.minimum(ref - b, 0.0)), 0.0)
    return qscale, kscale


def _hg_att(Q, K, b, rev):
    C, sb = HG_CHUNK, HG_SUB
    lane = lax.broadcasted_iota(jnp.int32, (sb, C), 1)
    rloc = lax.broadcasted_iota(jnp.int32, (sb, 1), 0)
    rows = []
    for i in range(C // sb):
        sl = slice(sb * i, sb * i + sb)
        Qi, Ki, bi = Q[sl], K[sl], b[sl]
        od = _hg_offdiag(Q, K, b, i, rev)
        if od is None:
            acc = jnp.zeros((sb, C), F32)
        else:
            qs, ks = od
            acc = lax.dot_general(Qi * qs[sl], K * ks, (((1,), (1,)), ((), ())),
                                  precision=HG_OFF_PREC, preferred_element_type=F32)
        for j in range(sb):
            e = jnp.exp(jnp.minimum(bi - bi[j:j + 1], 0.0))
            col = jnp.sum(Qi * Ki[j:j + 1] * e, axis=-1, keepdims=True)
            vis = (rloc <= j) if rev else (rloc >= j)
            acc = jnp.where(lane == sb * i + j, jnp.where(vis, col, 0.0), acc)
        rows.append(acc)
    return jnp.concatenate(rows, axis=0)


def _hg_att_bwd(Q, K, b, dA, rev):
    C, sb = HG_CHUNK, HG_SUB
    rloc = lax.broadcasted_iota(jnp.int32, (sb, 1), 0)
    rrow = lax.broadcasted_iota(jnp.int32, (sb, HG_D), 0)
    trow = lax.broadcasted_iota(jnp.int32, (C, C), 1) // sb
    dAT = dA.T
    dQ_rows, dKd_rows = [], []
    dK = jnp.zeros((C, HG_D), F32)
    for i in range(C // sb):
        sl = slice(sb * i, sb * i + sb)
        Qi, Ki, bi, dAi = Q[sl], K[sl], b[sl], dA[sl]
        od = _hg_offdiag(Q, K, b, i, rev)
        if od is None:
            dQi = jnp.zeros((sb, HG_D), F32)
        else:
            qs, ks = od
            dQi = lax.dot_general(dAi, K * ks, (((1,), (0,)), ((), ())),
                                  precision=HI, preferred_element_type=F32) * qs[sl]
            zt = jnp.where(trow == i, dAT, 0.0)
            dK = dK + lax.dot_general(zt, Q * qs, (((1,), (0,)), ((), ())),
                                      precision=HI, preferred_element_type=F32) * ks
        dKd = jnp.zeros((sb, HG_D), F32)
        for j in range(sb):
            vis = (rloc <= j) if rev else (rloc >= j)
            e = jnp.where(vis, jnp.exp(jnp.minimum(bi - bi[j:j + 1], 0.0)), 0.0)
            dcol = dAi[:, sb * i + j:sb * i + j + 1]
            dQi = dQi + dcol * Ki[j:j + 1] * e
            krow = jnp.sum(dcol * Qi * e, axis=0, keepdims=True)
            dKd = jnp.where(rrow == j, krow, dKd)
        dQ_rows.append(dQi)
        dKd_rows.append(dKd)
    return jnp.concatenate(dQ_rows, axis=0), dK + jnp.concatenate(dKd_rows, axis=0)


def _hg_prep(qr, fr, lb, tri):
    sigq = _sigmoid(qr)
    Q = qr * sigq
    sig, f = _hg_gate(fr, lb)
    K = 1.0 - f
    logf = jnp.log(f)
    b = lax.dot_general(tri.astype(F32), logf, (((1,), (0,)), ((), ())),
                        precision=HI, preferred_element_type=F32)
    return sigq, Q, sig, f, K, b


def _hgrn_scan(P, lb, *, S, rev, name):
    C, H, D_ = HG_CHUNK, HG_HEADS, HG_D
    HP = HG_PAIR if H % HG_PAIR == 0 else 1
    NC = S // C
    fcol = (3 if rev else 2) * H

    def cidx(n):
        return NC - 1 - n if rev else n

    def body(q_ref, v_ref, f_ref, lb_ref, o_ref, st_ref, a_ref, state):
        n = pl.program_id(1)

        @pl.when(n == 0)
        def _():
            state[...] = jnp.zeros_like(state)

        tri = _hg_masks(rev)
        qa, va, fa, lba, sta = q_ref[...], v_ref[...], f_ref[...], lb_ref[...], state[...]
        st_ref[...] = sta
        outs, amats, states = [], [], []
        for hp in range(HP):
            sl = slice(hp * D_, (hp + 1) * D_)
            _, Q, _, _, K, b = _hg_prep(qa[:, sl], fa[:, sl], lba[:, sl], tri)
            V, ST0 = va[:, sl], sta[hp]
            e_b = jnp.exp(b)
            bE = b[0:1] if rev else b[C - 1:C]
            W = jnp.exp(bE - b)
            inter = lax.dot_general(Q * e_b, ST0, (((1,), (1,)), ((), ())), preferred_element_type=F32)
            A = _hg_att(Q, K, b, rev)
            amats.append(A)
            outs.append(inter + jnp.dot(A, V, preferred_element_type=F32))
            states.append(ST0 * jnp.exp(bE) + lax.dot_general(
                V, K * W, (((0,), (0,)), ((), ())), preferred_element_type=F32))
        a_ref[...] = jnp.stack(amats)
        o_ref[...] = jnp.concatenate(outs, axis=1)
        state[...] = jnp.stack(states)

    blk = lambda c0: pl.BlockSpec((C, HP * D_), lambda h, n, c0=c0: (cidx(n), c0 // HP + h))
    return pl.pallas_call(
        body, name=name, grid=(H // HP, NC),
        in_specs=[blk(0), blk(H), blk(fcol), pl.BlockSpec((1, HP * D_), lambda h, n: (0, h))],
        out_specs=[pl.BlockSpec((C, HP * D_), lambda h, n: (cidx(n), h)),
                   pl.BlockSpec((None, HP, D_, D_), lambda h, n: (cidx(n), h, 0, 0)),
                   pl.BlockSpec((None, HP, C, C), lambda h, n: (cidx(n), h, 0, 0))],
        out_shape=[jax.ShapeDtypeStruct((S, H * D_), F32),
                   jax.ShapeDtypeStruct((NC, H, D_, D_), F32),
                   jax.ShapeDtypeStruct((NC, H, C, C), F32)],
        scratch_shapes=[pltpu.VMEM((HP, D_, D_), F32)],
        compiler_params=_cparams(("parallel", "arbitrary")),
    )(P, P, P, lb)


def _hgrn_scan_bwd(P, lb, st, amat, do, *, S, rev, name):
    C, H, D_ = HG_CHUNK, HG_HEADS, HG_D
    HP = HG_PAIR if H % HG_PAIR == 0 else 1
    NC = S // C
    fcol = (3 if rev else 2) * H

    def cidx(n):
        return n if rev else NC - 1 - n

    def body(q_ref, v_ref, f_ref, lb_ref, st_ref, a_ref, do_ref, dq_ref, dv_ref, df_ref, dlb_ref, dstate):
        n = pl.program_id(1)

        @pl.when(n == 0)
        def _():
            dstate[...] = jnp.zeros_like(dstate)
            dlb_ref[...] = jnp.zeros_like(dlb_ref)

        tri = _hg_masks(rev)
        tri_t = _hg_masks(not rev).astype(F32)
        qa, va, fa, lba, doa = q_ref[...], v_ref[...], f_ref[...], lb_ref[...], do_ref[...]
        sta, ama, dsta = st_ref[...], a_ref[...], dstate[...]
        trow = lax.broadcasted_iota(jnp.int32, (C, 1), 0)
        dqs, dvs, dfs, dlbs, dstates = [], [], [], [], []
        for hp in range(HP):
            sl = slice(hp * D_, (hp + 1) * D_)
            lbv, qr = lba[:, sl], qa[:, sl]
            sigq, Q, sig, f, K, b = _hg_prep(qr, fa[:, sl], lbv, tri)
            V, ST0, A, dO, dST1 = va[:, sl], sta[hp], ama[hp], doa[:, sl], dsta[hp]
            e_b = jnp.exp(b)
            bE = b[0:1] if rev else b[C - 1:C]
            eE = jnp.exp(bE)
            W = jnp.exp(bE - b)
            Qe = Q * e_b
            KW = K * W
            dA = jnp.where(tri, lax.dot_general(dO, V, (((1,), (1,)), ((), ())), preferred_element_type=F32), 0.0)
            dV = (lax.dot_general(A, dO, (((0,), (0,)), ((), ())), preferred_element_type=F32)
                  + lax.dot_general(KW, dST1, (((1,), (1,)), ((), ())), preferred_element_type=F32))
            dQe = jnp.dot(dO, ST0, preferred_element_type=F32)
            dKW = jnp.dot(V, dST1, preferred_element_type=F32)
            dstates.append(dST1 * eE + lax.dot_general(dO, Qe, (((0,), (0,)), ((), ())), preferred_element_type=F32))
            dQa, dKa = _hg_att_bwd(Q, K, b, dA, rev)
            dQ = dQe * e_b + dQa
            dK = dKW * W + dKa
            extra = _colsum(KW * dKW) + eE * _colsum(ST0 * dST1)
            db = Q * dQ - K * dK + jnp.where(trow == (0 if rev else C - 1), extra, 0.0)
            dlogf = lax.dot_general(tri_t, db, (((1,), (0,)), ((), ())), precision=HI, preferred_element_type=F32)
            dfv = dlogf / f - dK
            dfs.append(dfv * (1.0 - lbv) * sig * (1.0 - sig))
            dlbs.append(_colsum(dfv * (1.0 - sig)))
            dqs.append(dQ * (sigq * (1.0 + qr * (1.0 - sigq))))
            dvs.append(dV)
        dstate[...] = jnp.stack(dstates)
        df_ref[...] = jnp.concatenate(dfs, axis=1)
        dlb_ref[...] += jnp.concatenate(dlbs, axis=1)
        dq_ref[...] = jnp.concatenate(dqs, axis=1)
        dv_ref[...] = jnp.concatenate(dvs, axis=1)

    blk = lambda c0: pl.BlockSpec((C, HP * D_), lambda h, n, c0=c0: (cidx(n), c0 // HP + h))
    oblk = pl.BlockSpec((C, HP * D_), lambda h, n: (cidx(n), h))
    return pl.pallas_call(
        body, name=name, grid=(H // HP, NC),
        in_specs=[blk(0), blk(H), blk(fcol), pl.BlockSpec((1, HP * D_), lambda h, n: (0, h)),
                  pl.BlockSpec((None, HP, D_, D_), lambda h, n: (cidx(n), h, 0, 0)),
                  pl.BlockSpec((None, HP, C, C), lambda h, n: (cidx(n), h, 0, 0)),
                  oblk],
        out_specs=[oblk, oblk, oblk, pl.BlockSpec((1, HP * D_), lambda h, n: (0, h))],
        out_shape=[jax.ShapeDtypeStruct((S, H * D_), F32)] * 3 + [jax.ShapeDtypeStruct((1, H * D_), F32)],
        scratch_shapes=[pltpu.VMEM((HP, D_, D_), F32)],
        compiler_params=_cparams(("parallel", "arbitrary")),
    )(P, P, P, lb, st, amat, do)


LOG2E = 1.4426950408889634
MXU = BF16
ATT_SUB = 512


def _mx(x):
    return x if x.dtype == MXU else x.astype(MXU)


def _attn_fwd(q, k, v, *, S, T, H, dqk, dv, q_col0, k_col0, v_col0, scale, tq, tk, exch=None, name):
    tq = _tile(S, tq, SUBLANES)
    tk = _tile(T, tk, LANES)
    nk = T // tk
    ts = _tile(tq, ATT_SUB, SUBLANES)

    def body(q_ref, k_ref, v_ref, o_ref, lse_ref, m_s, l_s, acc):
        j = pl.program_id(2)

        @pl.when(j == 0)
        def _():
            m_s[...] = jnp.full_like(m_s, -jnp.inf)
            l_s[...] = jnp.zeros_like(l_s)
            acc[...] = jnp.zeros_like(acc)

        kv, vv = _mx(k_ref[...]), _mx(v_ref[...])
        m_all, l_all, a_all = m_s[...], l_s[...], acc[...]
        ms, ls, accs = [], [], []
        for r0 in range(0, tq, ts):
            rows = slice(r0, r0 + ts)
            s = lax.dot_general(_mx(q_ref[rows, :]), kv, (((1,), (1,)), ((), ())),
                                preferred_element_type=F32) * (scale * LOG2E)
            m_old = m_all[rows]
            m_new = jnp.maximum(m_old, jnp.max(s, axis=1)[:, None])
            corr = jnp.exp2(m_old - m_new)
            p = jnp.exp2(s - jnp.tile(m_new, (1, tk // LANES)))
            ms.append(m_new)
            ls.append(corr * l_all[rows] + jnp.sum(p, axis=1)[:, None])
            accs.append(jnp.tile(corr, (1, dv // LANES)) * a_all[rows] + jnp.dot(_mx(p), vv, preferred_element_type=F32))
        m_s[...] = jnp.concatenate(ms, axis=0)
        l_s[...] = jnp.concatenate(ls, axis=0)
        acc[...] = jnp.concatenate(accs, axis=0)

        @pl.when(j == nk - 1)
        def _():
            o_ref[...] = acc[...] / jnp.tile(l_s[...], (1, dv // LANES))
            lse_ref[...] = ((m_s[...] + jnp.log2(l_s[...])) * (1.0 / LOG2E))[:, :1]

    outs, received = _carried(
        body, exch, (H, S // tq, nk),
        [pl.BlockSpec((tq, dqk), lambda h, i, j: (i, q_col0 + h)),
         pl.BlockSpec((tk, dqk), lambda h, i, j: (j, k_col0 + h)),
         pl.BlockSpec((tk, dv), lambda h, i, j: (j, v_col0 + h))],
        [pl.BlockSpec((tq, dv), lambda h, i, j: (i, h)),
         pl.BlockSpec((None, tq, 1), lambda h, i, j: (h, i, 0))],
        [jax.ShapeDtypeStruct((S, H * dv), F32), jax.ShapeDtypeStruct((H, S, 1), F32)],
        [pltpu.VMEM((tq, LANES), F32), pltpu.VMEM((tq, LANES), F32), pltpu.VMEM((tq, dv), F32)],
        [q, k, v], name=name)
    return outs if exch is None else (outs, received)


def _attn_bwd(q, k, v, o, lse, do, *, S, T, H, dqk, dv, q_col0, k_col0, v_col0, scale, tq, tk, exch=None, name):
    tq = _tile(S, tq, SUBLANES)
    tk = _tile(T, tk, LANES)
    nq = S // tq
    ts = _tile(tq, ATT_SUB, SUBLANES)

    def body(q_ref, k_ref, v_ref, o_ref, lse_ref, do_ref, dq_ref, dk_ref, dv_ref, dk_acc, dv_acc):
        j = pl.program_id(1)
        i = pl.program_id(2)

        @pl.when(jnp.logical_and(i == 0, j == 0))
        def _():
            dq_ref[...] = jnp.zeros_like(dq_ref)

        @pl.when(i == 0)
        def _():
            dk_acc[...] = jnp.zeros_like(dk_acc)
            dv_acc[...] = jnp.zeros_like(dv_acc)

        kv, vv = _mx(k_ref[...]), _mx(v_ref[...])
        dk_new, dv_new = dk_acc[...], dv_acc[...]
        lse2 = lse_ref[...] * LOG2E
        dqs = []
        for r0 in range(0, tq, ts):
            rows = slice(r0, r0 + ts)
            qv, dov = _mx(q_ref[rows, :]), do_ref[rows, :]
            s = lax.dot_general(qv, kv, (((1,), (1,)), ((), ())), preferred_element_type=F32) * (scale * LOG2E)
            p = jnp.exp2(s - lse2[rows])
            delta = jnp.sum(dov * o_ref[rows, :], axis=-1, keepdims=True)
            dob = _mx(dov)
            dp = lax.dot_general(dob, vv, (((1,), (1,)), ((), ())), preferred_element_type=F32)
            ds = _mx(p * (dp - delta) * scale)
            dv_new = dv_new + lax.dot_general(_mx(p), dob, (((0,), (0,)), ((), ())), preferred_element_type=F32)
            dk_new = dk_new + lax.dot_general(ds, qv, (((0,), (0,)), ((), ())), preferred_element_type=F32)
            dqs.append(jnp.dot(ds, kv, preferred_element_type=F32))
        dq_ref[pl.ds(pl.multiple_of(i * tq, tq), tq), :] += jnp.concatenate(dqs, axis=0)
        dk_acc[...] = dk_new
        dv_acc[...] = dv_new

        @pl.when(i == nq - 1)
        def _():
            dk_ref[...] = dk_new
            dv_ref[...] = dv_new

    outs, received = _carried(
        body, exch, (H, T // tk, nq),
        [pl.BlockSpec((tq, dqk), lambda h, j, i: (i, q_col0 + h)),
         pl.BlockSpec((tk, dqk), lambda h, j, i: (j, k_col0 + h)),
         pl.BlockSpec((tk, dv), lambda h, j, i: (j, v_col0 + h)),
         pl.BlockSpec((tq, dv), lambda h, j, i: (i, h)),
         pl.BlockSpec((None, tq, 1), lambda h, j, i: (h, i, 0)),
         pl.BlockSpec((tq, dv), lambda h, j, i: (i, h))],
        [pl.BlockSpec((S, dqk), lambda h, j, i: (0, h)),
         pl.BlockSpec((tk, dqk), lambda h, j, i: (j, h)),
         pl.BlockSpec((tk, dv), lambda h, j, i: (j, h))],
        [jax.ShapeDtypeStruct((S, H * dqk), F32), jax.ShapeDtypeStruct((T, H * dqk), F32),
         jax.ShapeDtypeStruct((T, H * dv), F32)],
        [pltpu.VMEM((tk, dqk), F32), pltpu.VMEM((tk, dv), F32)], [q, k, v, o, lse, do], name=name)
    return outs if exch is None else (outs, received)


def _rope_tables(positions):
    half = MLA_ROPE // 2
    inv_freq = jnp.power(ROPE_THETA, -jnp.arange(half, dtype=F32) / half)
    ang = positions.astype(F32)[:, None] * inv_freq
    cos, sin = jnp.cos(ang), jnp.sin(ang)
    z = jnp.zeros_like(cos)
    tc = jnp.concatenate([cos, cos, z, z], axis=1)
    ta = jnp.concatenate([-sin, z, z, z], axis=1)
    tb = jnp.concatenate([z, sin, z, z], axis=1)
    return tc, ta, tb


def _rope_apply(v, tc, ta, tb):
    half = MLA_ROPE // 2
    return v * tc + pltpu.roll(v, LANES - half, 1) * ta + pltpu.roll(v, half, 1) * tb


def _rope_apply_t(d, tc, ta, tb):
    half = MLA_ROPE // 2
    return d * tc + pltpu.roll(d * ta, half, 1) + pltpu.roll(d * tb, LANES - half, 1)


def _local_step(x, mem, positions, loss_target, w_in_k, small, W, MW, DFF, comm=None):
    S, D = x.shape
    M = mem.shape[0]
    HW = HG_HEADS * HG_D
    QR = small["mla_g_cq"].shape[1]
    KR = small["mla_g_ckv"].shape[1]
    MHD = MW // MEM_HEADS
    QW = MLA_HEADS * 2 * LANES
    VW = MLA_HEADS * MLA_V
    c_hg, c_cq, c_ckv, c_qm, c_gate = 0, 5 * HW, 5 * HW + QR, 5 * HW + QR + KR, 5 * HW + QR + KR + MW
    c_kr = c_gate + N_BRANCH * D
    PW = c_kr + LANES
    assert comm is not None or w_in_k.shape == (D, PW)
    TR = 256
    row = lambda a: a.reshape(1, -1)
    ge, be = row(small["ln_emb_g"]), row(small["ln_emb_b"])
    g1, b1, g2, b2 = small["ln1_g"], small["ln1_b"], small["ln2_g"], small["ln2_b"]
    lb = small["lb"]
    tc, ta, tb = _rope_tables(positions)

    (h0,) = _rowwise(lambda z, g, b: _ln_stats(z)[0] * g + b, [_Rows(x, D)], [ge, be],
                     [_Rows(D, D)], [], R=S, tr=TR, name="ln_emb")
    if comm is None:
        P = _mm(h0, w_in_k, S, PW, D, tn=896, name="proj_in")
    else:
        half = D // 2
        P_top, got = _mm(h0, comm.w_in_top, S, PW, half, tn=896, exch=comm.gather_bottom, name="proj_in_top")
        w_in_bottom = comm.w_in_rows(got)
        P, got = _mm(h0, w_in_bottom, S, PW, half, a_off=(0, half), add=P_top, tn=896, exch=comm.gather_mix,
                     name="proj_in")
        W = comm.weights(comm.mix_names, got)
        w_in_k = jnp.concatenate([comm.w_in_top, w_in_bottom], axis=0)

    o_fw, st_fw, a_fw = _hgrn_scan(P, lb[0:1], S=S, rev=False, name="hgrn_fw")
    o_bw, st_bw, a_bw = _hgrn_scan(P, lb[1:2], S=S, rev=True, name="hgrn_bw")

    def hg_post(of, ob, gr, ng):
        o = of + ob
        sg = _sigmoid(gr)
        outs = []
        for h in range(HG_HEADS):
            sl = slice(h * HG_D, (h + 1) * HG_D)
            outs.append(_rms_fwd(o[:, sl], ng, RMS_EPS) * sg[:, sl])
        return jnp.concatenate(outs, axis=1)

    (y_hg,) = _rowwise(hg_post, [_Rows(o_fw, HW), _Rows(o_bw, HW), _Rows(P, HW, 4)], [small["hgrn_norm_g"]],
                       [_Rows(HW, HW)], [], R=S, tr=TR, name="hgrn_post")

    def mla_norm(cq, ckv, gq, gk):
        return _rms_fwd(cq, gq, RMS_EPS), _rms_fwd(ckv, gk, RMS_EPS)

    assert c_cq % QR == 0 and c_ckv % KR == 0
    cqn, ckvn = _rowwise(mla_norm, [_Rows(P, QR, c_cq // QR), _Rows(P, KR, c_ckv // KR)],
                         [small["mla_g_cq"], small["mla_g_ckv"]],
                         [_Rows(QR, QR), _Rows(KR, KR)], [], R=S, tr=TR, name="mla_norm")
    q_raw = _mm(cqn, W["mla_w_uq"], S, QW, QR, name="mla_uq")
    kv = _mm(ckvn, W["mla_w_ukv"], S, 2 * VW, KR, name="mla_ukv")

    def rope_fwd(qb, knb, vb_, krb, tcb, tab, tbb):
        kr = _rope_apply(krb, tcb, tab, tbb)
        qo, ko = [], []
        for h in range(MLA_HEADS):
            qo += [qb[:, 2 * h * LANES:(2 * h + 1) * LANES],
                   _rope_apply(qb[:, (2 * h + 1) * LANES:(2 * h + 2) * LANES], tcb, tab, tbb)]
            ko += [knb[:, h * LANES:(h + 1) * LANES], kr]
        return jnp.concatenate(qo, axis=1), jnp.concatenate(ko, axis=1), vb_

    qc, kc, vc = _rowwise(rope_fwd, [_Rows(q_raw, QW), _Rows(kv, VW), _Rows(kv, VW, 1), _Rows(P, LANES, c_kr // LANES),
                                     _Rows(tc, LANES), _Rows(ta, LANES), _Rows(tb, LANES)], [],
                          [_Rows(QW, QW, dtype=MXU), _Rows(QW, QW, dtype=MXU), _Rows(VW, VW, dtype=MXU)], [],
                          R=S, tr=TR, name="rope_fwd")
    mla_kw = dict(S=S, T=S, H=MLA_HEADS, dqk=2 * LANES, dv=MLA_V, q_col0=0, k_col0=0, v_col0=0,
                  scale=(MLA_NOPE + MLA_ROPE) ** -0.5, tq=2048, tk=2048)
    if comm is None:
        y_mla, lse_mla = _attn_fwd(qc, kc, vc, name="mla_attn", **mla_kw)
    else:
        (y_mla, lse_mla), got = _attn_fwd(qc, kc, vc, exch=comm.gather_ffn, name="mla_attn", **mla_kw)
        W = {**W, **comm.weights(comm.ffn_names, got)}

    kvm = _mm(mem, W["mem_w_kv"], M, 2 * MW, D, name="mem_kv")
    mem_kw = dict(S=S, T=M, H=MEM_HEADS, dqk=MHD, dv=MHD, q_col0=c_qm // MHD, k_col0=0, v_col0=MEM_HEADS,
                  scale=MHD ** -0.5, tq=1024, tk=M)
    assert c_qm % MHD == 0
    y_mem, lse_mem = _attn_fwd(P, kvm, kvm, name="mem_attn", **mem_kw)

    ys = (y_hg, y_mla, y_mem)
    us = [_mm(ys[b], W["w_branch"][b], S, D, HW, name=f"branch{b}") for b in range(N_BRANCH)]
    TCW = _tile(D, 1024, LANES)
    ncw = D // TCW

    def merge_fwd(g0, g1_, g2_, u0, u1, u2):
        return _sigmoid(g0) * u0 + _sigmoid(g1_) * u1 + _sigmoid(g2_) * u2

    gate_rows = [_Rows(P, TCW, (c_gate + b * D) // TCW) for b in range(N_BRANCH)]
    assert c_gate % TCW == 0
    (merged,) = _rowwise(merge_fwd, gate_rows + [_Rows(u, TCW) for u in us], [],
                         [_Rows(D, TCW)], [], R=S, tr=TR, ncol=ncw, name="merge_fwd")
    mix = _mm(merged, W["w_o"], S, D, D, tk=D, name="out_proj")

    def ln_res(hp, addv, g, b):
        z = ALPHA * hp + addv
        return z, _ln_stats(z)[0] * g + b

    z1, h1 = _rowwise(ln_res, [_Rows(h0, D), _Rows(mix, D)], [g1, b1],
                      [_Rows(D, D), _Rows(D, D)], [], R=S, tr=TR, name="ln1")

    TF = _tile(DFF, 512, LANES)

    def swiglu(abv):
        a, b = abv[:, :TF], abv[:, TF:]
        return [abv, a * _sigmoid(a) * b]

    ab, cff = _mm(h1, W["w_ffn_gu"], S, 2 * DFF, D, tn=2 * TF, tk=D, epilogue=(swiglu, [], [1, (1, 2)]), name="ffn_gu")
    ff = _mm(cff, W["w_ffn_down"], S, D, DFF, tk=_tile(DFF, 1408, LANES), name="ffn_down")

    def loss_bwd(hp, addv, tgt, g, b):
        z = ALPHA * hp + addv
        xhat, rstd = _ln_stats(z)
        y = xhat * g + b
        err = y - tgt
        dy = err * (1.0 / D)
        dxh = dy * g
        m1 = jnp.mean(dxh, axis=-1, keepdims=True)
        m2 = jnp.mean(dxh * xhat, axis=-1, keepdims=True)
        dz = rstd * (dxh - m1 - xhat * m2)
        lrow = jnp.sum(_colsum(err * err), axis=-1, keepdims=True) * (0.5 / D)
        return dz, _colsum(dy * xhat), _colsum(dy), lrow

    dz2, dg2, db2, loss = _rowwise(loss_bwd, [_Rows(h1, D), _Rows(ff, D), _Rows(loss_target, D)], [g2, b2],
                                   [_Rows(D, D)], [D, D, 1], R=S, tr=TR, name="loss_ln2_bwd")
    def swiglu_bwd(dc, abv):
        a, b = abv[:, :TF], abv[:, TF:]
        sg = _sigmoid(a)
        return [jnp.concatenate([dc * b * sg * (1.0 + a * (1.0 - sg)), dc * a * sg], axis=1)]

    (dab,) = _mm(dz2, W["w_ffn_down"], S, DFF, D, tb=True, tn=TF, epilogue=(swiglu_bwd, [(ab, (2, 1))], [(2, 1)]),
                 name="ffn_down_dx")
    g_ffn_down = _mm(cff, dz2, DFF, D, S, ta=True, name="ffn_down_dw")
    dh1 = _mm(dab, W["w_ffn_gu"], S, D, 2 * DFF, tb=True, name="ffn_gu_dx")
    g_ffn_gu = _mm(h1, dab, D, 2 * DFF, S, ta=True, name="ffn_gu_dw")

    def ln1_bwd(z, dmm, dz2v, g):
        return _ln_bwd_core(z, g, ALPHA * dz2v + dmm)

    dz1, dg1, db1 = _rowwise(ln1_bwd, [_Rows(z1, D), _Rows(dh1, D), _Rows(dz2, D)], [g1],
                             [_Rows(D, D)], [D, D], R=S, tr=TR, name="ln1_bwd")
    dmerged = _mm(dz1, W["w_o"], S, D, D, tb=True, name="out_proj_dx")
    g_w_o = _mm(merged, dz1, D, D, S, ta=True, name="out_proj_dw")

    def merge_bwd(g0, g1_, g2_, u0, u1, u2, dm):
        res_g, res_u = [], []
        for gv, uv in ((g0, u0), (g1_, u1), (g2_, u2)):
            sg = _sigmoid(gv)
            res_g.append(dm * uv * sg * (1.0 - sg))
            res_u.append(dm * sg)
        return (*res_g, *res_u)

    mres = _rowwise(merge_bwd, gate_rows + [_Rows(u, TCW) for u in us] + [_Rows(dmerged, TCW)], [],
                    [_Rows(D, TCW)] * (2 * N_BRANCH), [], R=S, tr=TR, ncol=ncw, name="merge_bwd")
    dgates, dus = mres[:N_BRANCH], mres[N_BRANCH:]
    dys = [_mm(dus[b], W["w_branch"][b], S, HW, D, tb=True, name=f"branch{b}_dx") for b in range(N_BRANCH)]
    g_w_branch = [_mm(ys[b], dus[b], HW, D, S, ta=True, name=f"branch{b}_dw") for b in range(N_BRANCH)]

    dq_mem, dk_mem, dv_mem = _attn_bwd(P, kvm, kvm, y_mem, lse_mem, dys[2], name="mem_attn_bwd", **mem_kw)
    dkvm = jnp.concatenate([dk_mem, dv_mem], axis=1)
    g_mem_w_kv = _mm(mem, dkvm, D, 2 * MW, M, ta=True, name="mem_kv_dw")

    g_w_branch = jnp.stack(g_w_branch)
    delivered = {}
    if comm is None:
        dqc, dkc, dvv = _attn_bwd(qc, kc, vc, y_mla, lse_mla, dys[1], name="mla_attn_bwd", **mla_kw)
    else:
        exch = comm.scatter(dict(w_ffn_gu=g_ffn_gu, w_ffn_down=g_ffn_down, w_o=g_w_o, w_branch=g_w_branch,
                                 mem_w_kv=g_mem_w_kv))
        (dqc, dkc, dvv), got = _attn_bwd(qc, kc, vc, y_mla, lse_mla, dys[1], exch=exch, name="mla_attn_bwd",
                                         **mla_kw)
        delivered.update(zip(exch.names, got))

    def rope_bwd(dqb, dkb, tcb, tab, tbb):
        qo, kn = [], []
        dkr = jnp.zeros_like(tcb)
        for h in range(MLA_HEADS):
            qo += [dqb[:, 2 * h * LANES:(2 * h + 1) * LANES],
                   _rope_apply_t(dqb[:, (2 * h + 1) * LANES:(2 * h + 2) * LANES], tcb, tab, tbb)]
            kn.append(dkb[:, 2 * h * LANES:(2 * h + 1) * LANES])
            dkr = dkr + dkb[:, (2 * h + 1) * LANES:(2 * h + 2) * LANES]
        return jnp.concatenate(qo, axis=1), jnp.concatenate(kn, axis=1), _rope_apply_t(dkr, tcb, tab, tbb)

    dq_raw, dkn, dkr_raw = _rowwise(rope_bwd, [_Rows(dqc, QW), _Rows(dkc, QW), _Rows(tc, LANES),
                                               _Rows(ta, LANES), _Rows(tb, LANES)], [],
                                    [_Rows(QW, QW), _Rows(VW, VW), _Rows(LANES, LANES)], [],
                                    R=S, tr=TR, name="rope_bwd")
    dkv = jnp.concatenate([dkn, dvv], axis=1)
    dcqn = _mm(dq_raw, W["mla_w_uq"], S, QR, QW, tb=True, name="mla_uq_dx")
    g_mla_w_uq = _mm(cqn, dq_raw, QR, QW, S, ta=True, name="mla_uq_dw")
    dckvn = _mm(dkv, W["mla_w_ukv"], S, KR, 2 * VW, tb=True, name="mla_ukv_dx")
    g_mla_w_ukv = _mm(ckvn, dkv, KR, 2 * VW, S, ta=True, name="mla_ukv_dw")

    def mla_norm_bwd(cq, ckv, dq_, dk_, gq, gk):
        dcq, gq_rows = _rms_bwd(cq, gq, dq_, RMS_EPS)
        dck, gk_rows = _rms_bwd(ckv, gk, dk_, RMS_EPS)
        return dcq, dck, _colsum(gq_rows), _colsum(gk_rows)

    dcq, dckv, dg_cq, dg_ckv = _rowwise(
        mla_norm_bwd, [_Rows(P, QR, c_cq // QR), _Rows(P, KR, c_ckv // KR), _Rows(dcqn, QR), _Rows(dckvn, KR)],
        [small["mla_g_cq"], small["mla_g_ckv"]], [_Rows(QR, QR), _Rows(KR, KR)], [QR, KR],
        R=S, tr=TR, name="mla_norm_bwd")

    def hg_post_bwd(of, ob, gr, dy, ng):
        o = of + ob
        sg = _sigmoid(gr)
        do_, dgr = [], []
        dng = jnp.zeros((1, HG_D), F32)
        for h in range(HG_HEADS):
            sl = slice(h * HG_D, (h + 1) * HG_D)
            t = _rms_fwd(o[:, sl], ng, RMS_EPS)
            dgr.append(dy[:, sl] * t * sg[:, sl] * (1.0 - sg[:, sl]))
            dx, grow = _rms_bwd(o[:, sl], ng, dy[:, sl] * sg[:, sl], RMS_EPS)
            do_.append(dx)
            dng = dng + _colsum(grow)
        return jnp.concatenate(do_, axis=1), jnp.concatenate(dgr, axis=1), dng

    do_hg, dg_hg, dng = _rowwise(hg_post_bwd, [_Rows(o_fw, HW), _Rows(o_bw, HW), _Rows(P, HW, 4), _Rows(dys[0], HW)],
                                 [small["hgrn_norm_g"]], [_Rows(HW, HW), _Rows(HW, HW)], [HG_D],
                                 R=S, tr=TR, name="hgrn_post_bwd")
    dq_f, dv_f, dff_fw, dlb_f = _hgrn_scan_bwd(P, lb[0:1], st_fw, a_fw, do_hg, S=S, rev=False, name="hgrn_fw_bwd")
    dq_b, dv_b, dff_bw, dlb_b = _hgrn_scan_bwd(P, lb[1:2], st_bw, a_bw, do_hg, S=S, rev=True, name="hgrn_bw_bwd")
    THW = _tile(HW, 1024, LANES)
    dq_hg, dv_hg = _rowwise(lambda a, b, c, d: (a + b, c + d),
                            [_Rows(dq_f, THW), _Rows(dq_b, THW), _Rows(dv_f, THW), _Rows(dv_b, THW)], [],
                            [_Rows(HW, THW), _Rows(HW, THW)], [], R=S, tr=TR, ncol=HW // THW, name="hgrn_dir_sum")

    dP = jnp.concatenate([dq_hg, dv_hg, dff_fw, dff_bw, dg_hg, dcq, dckv, dq_mem, *dgates, dkr_raw], axis=1)
    g_w_in = _mm(h0, dP, D, PW, S, ta=True, tn=896, name="proj_in_dw")
    dx_kw = dict(tb=True, tk=_tile(PW, 1920, LANES), name="proj_in_dx")
    if comm is None:
        dh0 = _mm(dP, w_in_k, S, D, PW, **dx_kw)
    else:
        exch = comm.scatter(dict(w_in=g_w_in, mla_w_uq=g_mla_w_uq, mla_w_ukv=g_mla_w_ukv))
        dh0, got = _mm(dP, w_in_k, S, D, PW, exch=exch, **dx_kw)
        delivered.update(zip(exch.names, got))

    def ln0_bwd(z, dmm, dz1v, g):
        return _ln_bwd_core(z, g, ALPHA * dz1v + dmm)

    grad_x, dge, dbe = _rowwise(ln0_bwd, [_Rows(x, D), _Rows(dh0, D), _Rows(dz1, D)], [ge],
                                [_Rows(D, D)], [D, D], R=S, tr=TR, name="ln_emb_bwd")

    big = dict(w_in=g_w_in, mla_w_uq=g_mla_w_uq, mla_w_ukv=g_mla_w_ukv, mem_w_kv=g_mem_w_kv,
               w_branch=g_w_branch, w_o=g_w_o, w_ffn_gu=g_ffn_gu, w_ffn_down=g_ffn_down)
    sm = dict(ln_emb_g=dge, ln_emb_b=dbe, dlb=jnp.concatenate([dlb_f, dlb_b], axis=0), hgrn_norm_g=dng,
              mla_g_cq=dg_cq, mla_g_ckv=dg_ckv, ln1_g=dg1, ln1_b=db1, ln2_g=dg2, ln2_b=db2)
    return loss, grad_x, big, sm, delivered


def _gather_exch(shards):
    copies = [(m, lambda x, y, c: 0, _chip) for m in _CHIP_MASKS]
    return _Exch([s[None] for s in shards], 4, copies, [(lambda x, y, c: 0, _chip)])


def _scatter_exch(pieces):
    copies = [(m, (lambda x, y, c, m=m: 2 * _chip(x ^ m[0], y ^ m[1], c) + (c ^ m[2])), _device) for m in _ALL_MASKS]
    local = [((lambda x, y, c: 2 * _chip(x, y, c) + c), _device)]
    return _Exch([p.reshape((8,) + p.shape[2:]) for p in pieces], 8, copies, local)


SHARE_BLOCK_BYTES = 4 << 20


def _sum_share(arr, *, name):
    n, rh, w = arr.shape
    tr = _tile(rh, max(16, SHARE_BLOCK_BYTES // (n * w * arr.dtype.itemsize) // 16 * 16), 16)
    nb = rh // tr

    def body(a_ref, o_ref, slots, send_sems, recv_sem, local_sems):
        i = pl.program_id(0)
        x, y, c = _coords()
        sibling = (x, y, 1 - c)

        def pushes(step, slot):
            rows = pl.ds(pl.multiple_of(c * rh + step * tr, SUBLANES), tr)
            return (pltpu.make_async_copy(slots.at[slot], o_ref.at[rows], local_sems.at[slot]),
                    pltpu.make_async_remote_copy(src_ref=slots.at[slot], dst_ref=o_ref.at[rows],
                                                 send_sem=send_sems.at[slot], recv_sem=recv_sem,
                                                 device_id=sibling, device_id_type=MESH))

        def drain(step, slot):
            loc, rem = pushes(step, slot)
            loc.wait()
            rem.wait_send()

        slot = i % 2

        @pl.when(i >= 2)
        def _():
            drain(i - 2, slot)

        acc = a_ref[0].astype(F32)
        for k in range(1, n):
            acc = acc + a_ref[k].astype(F32)
        slots[slot] = acc
        loc, rem = pushes(i, slot)
        loc.start()
        rem.start()

        @pl.when(i == nb - 1)
        def _():
            if nb >= 2:
                drain(i - 1, 1 - slot)
            drain(i, slot)
            other = o_ref.at[pl.ds(pl.multiple_of((1 - c) * rh, SUBLANES), rh)]
            pltpu.make_async_remote_copy(src_ref=other, dst_ref=other, send_sem=send_sems.at[0], recv_sem=recv_sem,
                                         device_id=sibling, device_id_type=MESH).wait_recv()

    return pl.pallas_call(
        body, name=name, grid=(nb,),
        in_specs=[pl.BlockSpec((n, tr, w), lambda i: (0, i, 0))],
        out_specs=pl.BlockSpec(memory_space=pl.ANY),
        out_shape=jax.ShapeDtypeStruct((2 * rh, w), F32),
        scratch_shapes=[pltpu.VMEM((2, tr, w), F32), pltpu.SemaphoreType.DMA((2,)), pltpu.SemaphoreType.DMA,
                        pltpu.SemaphoreType.DMA((2,))],
        compiler_params=pltpu.CompilerParams(dimension_semantics=("arbitrary",), has_side_effects=True,
                                             vmem_limit_bytes=VMEM_LIMIT),
    )(arr)


def _allreduce_small(v, *, name):
    r, w = v.shape

    def body(v_ref, o_ref, buf, send_sems, recv_sems):
        x, y, c = _coords()
        me = 4 * x + 2 * y + c
        buf[me] = v_ref[...]
        cps = []
        for k in range(7):
            m = ((k + 1) >> 2 & 1, (k + 1) >> 1 & 1, (k + 1) & 1)
            cp = pltpu.make_async_remote_copy(
                src_ref=v_ref, dst_ref=buf.at[me], send_sem=send_sems.at[k], recv_sem=recv_sems.at[k],
                device_id=(x ^ m[0], y ^ m[1], c ^ m[2]), device_id_type=MESH)
            cp.start()
            cps.append(cp)
        for cp in cps:
            cp.wait_recv()
        for cp in cps:
            cp.wait_send()
        acc = buf[0]
        for k in range(1, 8):
            acc = acc + buf[k]
        o_ref[...] = acc

    return pl.pallas_call(
        body, name=name,
        in_specs=[pl.BlockSpec(memory_space=pltpu.VMEM)],
        out_specs=pl.BlockSpec(memory_space=pltpu.VMEM),
        out_shape=jax.ShapeDtypeStruct((r, w), F32),
        scratch_shapes=[pltpu.VMEM((8, r, w), F32), pltpu.SemaphoreType.DMA((7,)), pltpu.SemaphoreType.DMA((7,))],
        compiler_params=pltpu.CompilerParams(has_side_effects=True),
    )(v)


_BIG = (("w_in", 1), ("mla_w_uq", 1), ("mla_w_ukv", 1), ("mem_w_kv", 0), ("w_branch", 1), ("w_o", 0),
        ("w_ffn_gate", 1), ("w_ffn_up", 1), ("w_ffn_down", 0))


def _assemble(gathered, ax):
    _, r, c = gathered.shape
    if ax == 0:
        return gathered.reshape(4 * r, c)
    return jnp.concatenate([gathered[j] for j in range(4)], axis=1)


def _split_pieces(g, ax):
    r, c = g.shape
    if ax == 0:
        return g.reshape(4, 2, r // 8, c).astype(BF16)
    rh, cs = r // 2, c // 4
    return jnp.stack([g[h * rh:(h + 1) * rh, j * cs:(j + 1) * cs].astype(BF16)
                      for j in range(4) for h in range(2)]).reshape(4, 2, rh, cs)


def _pad_cols(a, n):
    return jnp.pad(a, ((0, 0), (0, n - a.shape[1])))


def _to_kernel_layout(full, QR, KR):
    out = {}
    for n in ("mem_w_kv", "w_branch", "w_o", "w_ffn_down"):
        if n in full:
            out[n] = full[n]
    if "w_in" in full:
        w_in = full["w_in"]
        a = 5 * HG_HEADS * HG_D + QR + KR
        out["w_in"] = jnp.concatenate([w_in[:, :a], w_in[:, a + MLA_ROPE:], _pad_cols(w_in[:, a:a + MLA_ROPE], LANES)],
                                      axis=1)
    if "mla_w_uq" in full:
        uq = full["mla_w_uq"].reshape(QR, MLA_HEADS, MLA_NOPE + MLA_ROPE)
        out["mla_w_uq"] = jnp.pad(uq, ((0, 0), (0, 0), (0, 2 * LANES - MLA_NOPE - MLA_ROPE))).reshape(QR, -1)
    if "mla_w_ukv" in full:
        ukv = full["mla_w_ukv"].reshape(KR, MLA_HEADS, MLA_NOPE + MLA_V)
        out["mla_w_ukv"] = jnp.concatenate([ukv[:, :, :MLA_NOPE].reshape(KR, -1), ukv[:, :, MLA_NOPE:].reshape(KR, -1)],
                                           axis=1)
    if "w_ffn_gate" in full:
        gate, up = full["w_ffn_gate"], full["w_ffn_up"]
        DFF = gate.shape[1]
        TF = _tile(DFF, 512, LANES)
        blocks = []
        for j in range(DFF // TF):
            blocks += [gate[:, j * TF:(j + 1) * TF], up[:, j * TF:(j + 1) * TF]]
        out["w_ffn_gu"] = jnp.concatenate(blocks, axis=1)
    return out


def _from_kernel_layout(gk, QR, KR):
    out = {}
    for n in ("mem_w_kv", "w_o", "w_ffn_down"):
        if n in gk:
            out[n] = gk[n]
    if "w_branch" in gk:
        out["w_branch"] = gk["w_branch"].reshape(-1, gk["w_branch"].shape[-1])
    if "w_in" in gk:
        g = gk["w_in"]
        a = 5 * HG_HEADS * HG_D + QR + KR
        rest = g.shape[1] - LANES - a
        out["w_in"] = jnp.concatenate([g[:, :a], g[:, a + rest:a + rest + MLA_ROPE], g[:, a:a + rest]], axis=1)
    if "mla_w_uq" in gk:
        out["mla_w_uq"] = gk["mla_w_uq"].reshape(QR, MLA_HEADS, 2 * LANES)[:, :, :MLA_NOPE + MLA_ROPE].reshape(QR, -1)
    if "mla_w_ukv" in gk:
        VW = MLA_HEADS * MLA_V
        g = gk["mla_w_ukv"]
        out["mla_w_ukv"] = jnp.concatenate([g[:, :VW].reshape(KR, MLA_HEADS, MLA_NOPE),
                                            g[:, VW:].reshape(KR, MLA_HEADS, MLA_V)], axis=2).reshape(KR, -1)
    if "w_ffn_gu" in gk:
        g = gk["w_ffn_gu"]
        DFF = g.shape[1] // 2
        TF = _tile(DFF, 512, LANES)
        out["w_ffn_gate"] = jnp.concatenate([g[:, 2 * j * TF:(2 * j + 1) * TF] for j in range(DFF // TF)], axis=1)
        out["w_ffn_up"] = jnp.concatenate([g[:, (2 * j + 1) * TF:(2 * j + 2) * TF] for j in range(DFF // TF)], axis=1)
    return out


def _adamw(w, g, m, v, *, name):
    r, c = w.shape
    tr = max(SUBLANES, min(512, (1 << 20) // (4 * c)) // SUBLANES * SUBLANES)
    c1 = 1.0 / (1.0 - ADAM_B1 ** ADAM_STEP)
    c2 = 1.0 / (1.0 - ADAM_B2 ** ADAM_STEP)

    def fn(wv, gv, mv, vv):
        mn = ADAM_B1 * mv + (1.0 - ADAM_B1) * gv
        vn = ADAM_B2 * vv + (1.0 - ADAM_B2) * (gv * gv)
        delta = -ADAM_LR * ((mn * c1) / (jnp.sqrt(vn * c2) + ADAM_EPS) + ADAM_WD * wv)
        return delta, mn, vn

    return _rowwise(fn, [_Rows(a, c) for a in (w, g, m, v)], [], [_Rows(c, c)] * 3, [], R=r, tr=tr, name=name)


_SMALL = ("ln_emb_g", "ln_emb_b", "hgrn_lb_logits", "hgrn_norm_g", "mla_g_cq", "mla_g_ckv",
          "ln1_g", "ln1_b", "ln2_g", "ln2_b")


def _lb_from_logits(logits):
    return jnp.cumsum(jax.nn.softmax(logits, axis=1), axis=1)[:, 0]


def _small_rows(parts):
    flat = jnp.concatenate([p.reshape(-1) for p in parts])
    n = flat.shape[0]
    total = -(-n // (SUBLANES * LANES)) * SUBLANES * LANES
    return jnp.pad(flat, (0, total - n)).reshape(total // LANES, LANES)


def kernel(x, mem, positions, ln_emb_g, ln_emb_b, hgrn_lb_logits, w_in, hgrn_norm_g, mla_g_cq, mla_g_ckv, mla_w_uq, mla_w_ukv, mem_w_kv, w_branch, w_o, ln1_g, ln1_b, w_ffn_gate, w_ffn_up, w_ffn_down, ln2_g, ln2_b, loss_target, m_ln_emb_g, m_ln_emb_b, m_hgrn_lb_logits, m_w_in, m_hgrn_norm_g, m_mla_g_cq, m_mla_g_ckv, m_mla_w_uq, m_mla_w_ukv, m_mem_w_kv, m_w_branch, m_w_o, m_ln1_g, m_ln1_b, m_w_ffn_gate, m_w_ffn_up, m_w_ffn_down, m_ln2_g, m_ln2_b, v_ln_emb_g, v_ln_emb_b, v_hgrn_lb_logits, v_w_in, v_hgrn_norm_g, v_mla_g_cq, v_mla_g_ckv, v_mla_w_uq, v_mla_w_ukv, v_mem_w_kv, v_w_branch, v_w_o, v_ln1_g, v_ln1_b, v_w_ffn_gate, v_w_ffn_up, v_w_ffn_down, v_ln2_g, v_ln2_b):
    names = ["ln_emb_g", "ln_emb_b", "hgrn_lb_logits", "w_in", "hgrn_norm_g", "mla_g_cq", "mla_g_ckv", "mla_w_uq",
             "mla_w_ukv", "mem_w_kv", "w_branch", "w_o", "ln1_g", "ln1_b", "w_ffn_gate", "w_ffn_up", "w_ffn_down",
             "ln2_g", "ln2_b"]
    wts = dict(zip(names, [ln_emb_g, ln_emb_b, hgrn_lb_logits, w_in, hgrn_norm_g, mla_g_cq, mla_g_ckv, mla_w_uq,
                           mla_w_ukv, mem_w_kv, w_branch, w_o, ln1_g, ln1_b, w_ffn_gate, w_ffn_up, w_ffn_down,
                           ln2_g, ln2_b]))
    mom = dict(zip(names, [m_ln_emb_g, m_ln_emb_b, m_hgrn_lb_logits, m_w_in, m_hgrn_norm_g, m_mla_g_cq, m_mla_g_ckv,
                           m_mla_w_uq, m_mla_w_ukv, m_mem_w_kv, m_w_branch, m_w_o, m_ln1_g, m_ln1_b, m_w_ffn_gate,
                           m_w_ffn_up, m_w_ffn_down, m_ln2_g, m_ln2_b]))
    var = dict(zip(names, [v_ln_emb_g, v_ln_emb_b, v_hgrn_lb_logits, v_w_in, v_hgrn_norm_g, v_mla_g_cq, v_mla_g_ckv,
                           v_mla_w_uq, v_mla_w_ukv, v_mem_w_kv, v_w_branch, v_w_o, v_ln1_g, v_ln1_b, v_w_ffn_gate,
                           v_w_ffn_up, v_w_ffn_down, v_ln2_g, v_ln2_b]))
    xc, yc, cc = _coords()
    chip = _chip(xc, yc, cc)
    S, D = x.shape[1], x.shape[2]

    axis = dict(_BIG)
    shard = lambda n: wts[n].reshape(-1, wts[n].shape[-1]).astype(BF16)
    QR, KR = mla_w_uq.shape[1], mla_w_ukv.shape[1]
    MW, DFF = mem_w_kv.shape[2] // 2, 4 * w_ffn_gate.shape[2]
    rows_of_w_in = lambda got: _to_kernel_layout(dict(w_in=_assemble(got[0], axis["w_in"])), QR, KR)["w_in"]
    w_in_shard = shard("w_in")

    class _Comm:
        w_in_top = rows_of_w_in(_exchange(_gather_exch([w_in_shard[:D // 2]]), name="gather_w_in_top"))
        gather_bottom = _gather_exch([w_in_shard[D // 2:]])
        w_in_rows = staticmethod(rows_of_w_in)
        ffn_names = ["w_ffn_gate", "w_ffn_up", "w_ffn_down"]
        mix_names = [n for n, _ in _BIG if n != "w_in" and not n.startswith("w_ffn")]
        gather_mix = _gather_exch([shard(n) for n in mix_names])
        gather_ffn = _gather_exch([shard(n) for n in ffn_names])

        @staticmethod
        def weights(names, received):
            full = {n: _assemble(g, axis[n]) for n, g in zip(names, received)}
            if "w_branch" in full:
                full["w_branch"] = full["w_branch"].reshape(N_BRANCH, -1, D)
            return _to_kernel_layout(full, QR, KR)

        @staticmethod
        def scatter(gk_part):
            gpart = _from_kernel_layout(gk_part, QR, KR)
            exch = _scatter_exch([_split_pieces(gpart[n], axis[n]) for n in gpart])
            exch.names = list(gpart)
            return exch

    lsh = hgrn_lb_logits.shape
    HW = 4 * lsh[2]
    placed = lax.dynamic_update_slice(jnp.zeros((lsh[0], lsh[1], HW), F32), hgrn_lb_logits, (0, 0, chip * lsh[2]))
    placed = jnp.where(cc == 0, placed, 0.0)
    logits = _allreduce_small(_small_rows([placed]), name="gather_logits").reshape(-1)[:placed.size].reshape(placed.shape)
    lb, lb_vjp = jax.vjp(_lb_from_logits, logits)

    small = dict(ln_emb_g=ln_emb_g, ln_emb_b=ln_emb_b, lb=lb, hgrn_norm_g=hgrn_norm_g, mla_g_cq=mla_g_cq,
                 mla_g_ckv=mla_g_ckv, ln1_g=ln1_g, ln1_b=ln1_b, ln2_g=ln2_g, ln2_b=ln2_b)
    loss_l, grad_x, _, gs, delivered = _local_step(x[0], mem[0], positions[0], loss_target[0], None, small, None,
                                                   MW, DFF, comm=_Comm)

    (dlogits,) = lb_vjp(gs["dlb"])
    sm_parts = [loss_l, gs["ln_emb_g"], gs["ln_emb_b"], dlogits, gs["hgrn_norm_g"], gs["mla_g_cq"], gs["mla_g_ckv"],
                gs["ln1_g"], gs["ln1_b"], gs["ln2_g"], gs["ln2_b"]]
    red = _allreduce_small(_small_rows(sm_parts), name="allreduce_small").reshape(-1)
    sm_out, off = [], 0
    for p in sm_parts:
        sm_out.append(red[off:off + p.size].reshape(p.shape))
        off += p.size
    loss = sm_out[0].reshape(())
    g_small = dict(zip(_SMALL, sm_out[1:]))
    g_small["hgrn_lb_logits"] = lax.dynamic_slice(g_small["hgrn_lb_logits"], (0, 0, chip * lsh[2]), lsh)
    for n in _SMALL:
        g_small[n] = g_small[n].reshape(wts[n].shape)

    g_big = {n: _sum_share(delivered[n], name="rs_sum_" + n).reshape(wts[n].shape) for n, _ in _BIG}

    grads = {**g_small, **g_big}
    delta, new_m, new_v = {}, {}, {}
    for n, _ in _BIG:
        shp = wts[n].shape
        two_d = lambda a: a.reshape(-1, shp[-1])
        d_, m_, v_ = _adamw(two_d(wts[n]), two_d(grads[n]), two_d(mom[n]), two_d(var[n]), name="adamw_" + n)
        delta[n], new_m[n], new_v[n] = d_.reshape(shp), m_.reshape(shp), v_.reshape(shp)
    sw, sg_, sm_, sv_ = (_small_rows([d[n] for n in _SMALL]) for d in (wts, grads, mom, var))
    d_, m_, v_ = _adamw(sw, sg_, sm_, sv_, name="adamw_small")
    for res, packed_rows in ((delta, d_), (new_m, m_), (new_v, v_)):
        flat, off = packed_rows.reshape(-1), 0
        for n in _SMALL:
            res[n] = flat[off:off + wts[n].size].reshape(wts[n].shape)
            off += wts[n].size

    return (loss, grad_x[None], *[grads[n] for n in names], *[delta[n] for n in names],
            *[new_m[n] for n in names], *[new_v[n] for n in names])
```
